```python
import jax, jax.numpy as jnp
from jax import lax
import numpy as np


D_MODEL = 1024
BATCH = 8
SEQ = 4096
DEPTH = 4

D_MIX = D_MODEL
D_CONV = D_MIX // 4
D_CONF = D_MIX // 4
D_DN = D_MIX // 2
N_CONV_GROUPS = 4
N_CONF_GROUPS = 4
DN_HEADS = 4
DN_HEAD_DIM = D_DN // DN_HEADS
SHORT_CONV_W = 3
CONF_CONV_W = 31
DN_CONV_W = 4
DN_CHUNK = 64
D_FF = ((8 * D_MODEL + 3 * 256 - 1) // (3 * 256)) * 256
IN_COLS = 3 * D_CONV + 2 * D_CONF + 4 * D_DN + 2 * DN_HEADS
N_MOD = 6
EPS = 1e-6

kernel_name = 'hymba_conv_conformer_gdn_adaln_trunk'


def rmsnorm(x, g):
    xf = x.astype(jnp.float32)
    y = xf * lax.rsqrt(jnp.mean(xf * xf, axis=-1, keepdims=True) + EPS)
    return y.astype(x.dtype) * g


def layernorm(x, g, b):
    xf = x.astype(jnp.float32)
    mu = jnp.mean(xf, axis=-1, keepdims=True)
    var = jnp.mean(jnp.square(xf - mu), axis=-1, keepdims=True)
    return ((xf - mu) * lax.rsqrt(var + 1e-5)).astype(x.dtype) * g + b


def causal_dwconv(x, w):
    K, C = w.shape
    xp = jnp.pad(x, ((0, 0), (K - 1, 0), (0, 0)))
    return lax.conv_general_dilated(xp, w[:, None, :].astype(x.dtype), window_strides=(1,), padding='VALID',
                                    dimension_numbers=('NWC', 'WIO', 'NWC'), feature_group_count=C)


def l2norm(x):
    return x * lax.rsqrt(jnp.sum(x * x, axis=-1, keepdims=True) + EPS)


def chunk_gated_delta_rule(q, k, v, g, beta):
    Bsz, T, H, Dk = q.shape
    C = DN_CHUNK
    N = T // C
    def to_chunks(t):
        return t.reshape(Bsz, N, C, H, -1).transpose(0, 3, 1, 2, 4)
    q = to_chunks(q) * (Dk ** -0.5)
    k = to_chunks(k)
    v = to_chunks(v)
    beta = beta.reshape(Bsz, N, C, H).transpose(0, 3, 1, 2)
    g = jnp.cumsum(g.reshape(Bsz, N, C, H).transpose(0, 3, 1, 2), axis=-1)
    causal = jnp.tril(jnp.ones((C, C), dtype=bool))
    strict = jnp.tril(jnp.ones((C, C), dtype=bool), -1)
    diff = g[..., :, None] - g[..., None, :]
    decay = jnp.where(causal, jnp.exp(jnp.where(causal, diff, 0.0)), 0.0)
    k_beta = k * beta[..., None]
    v_beta = v * beta[..., None]
    Lm = jnp.where(strict, jnp.einsum('bhncd,bhnsd->bhncs', k_beta, k) * decay, 0.0)
    eye = jnp.eye(C, dtype=q.dtype)
    Tm = lax.linalg.triangular_solve(eye + Lm, jnp.broadcast_to(eye, Lm.shape), left_side=True,
                                     lower=True, unit_diagonal=True)
    u = jnp.einsum('bhncs,bhnse->bhnce', Tm, v_beta)
    w = jnp.einsum('bhncs,bhnsd->bhncd', Tm, k_beta * jnp.exp(g)[..., None])
    qk = jnp.where(causal, jnp.einsum('bhncd,bhnsd->bhncs', q, k) * decay, 0.0)

    def step(S, inp):
        q_i, k_i, u_i, w_i, g_i, qk_i = inp
        v_new = u_i - jnp.einsum('bhcd,bhde->bhce', w_i, S)
        o = (jnp.einsum('bhcd,bhde->bhce', q_i * jnp.exp(g_i)[..., None], S)
             + jnp.einsum('bhcs,bhse->bhce', qk_i, v_new))
        g_last = g_i[..., -1]
        S = (S * jnp.exp(g_last)[..., None, None]
             + jnp.einsum('bhcd,bhce->bhde', k_i * jnp.exp(g_last[..., None] - g_i)[..., None], v_new))
        return S, o

    xs = tuple(jnp.moveaxis(t, 2, 0) for t in (q, k, u, w, g, qk))
    S0 = jnp.zeros((Bsz, H, Dk, v.shape[-1]), jnp.float32)
    _, o = lax.scan(step, S0, xs)
    return o.transpose(1, 0, 3, 2, 4).reshape(Bsz, T, H, -1)


def _fwd_setup_inputs(seed: int = 0) -> dict:
    key = jax.random.key(seed)
    ks = jax.random.split(key, 24)
    f32 = jnp.float32
    def nrm(k, shape, scale):
        return jax.random.normal(k, shape, f32) * scale
    L, D = DEPTH, D_MODEL
    dt = jax.random.uniform(ks[13], (L, DN_HEADS), f32, minval=1e-3, maxval=0.1)
    return {
        'x': nrm(ks[0], (BATCH, SEQ, D), 1.0),
        'c': nrm(ks[1], (BATCH, D), 1.0),
        'w_ada': nrm(ks[2], (L, D, N_MOD * D), 0.5 * D ** -0.5),
        'b_ada': nrm(ks[3], (L, N_MOD * D), 0.02),
        'norm_mix_g': 1.0 + nrm(ks[4], (L, D), 0.02),
        'norm_ffn_g': 1.0 + nrm(ks[5], (L, D), 0.02),
        'w_in': nrm(ks[6], (L, D, IN_COLS), D ** -0.5),
        'conv_a_w': nrm(ks[7], (L, SHORT_CONV_W, D_CONV), SHORT_CONV_W ** -0.5),
        'conf_dw_w': nrm(ks[8], (L, CONF_CONV_W, D_CONF), CONF_CONV_W ** -0.5),
        'conf_dw_b': nrm(ks[9], (L, D_CONF), 0.02),
        'conf_ln_g': 1.0 + nrm(ks[10], (L, D_CONF), 0.02),
        'conf_ln_b': nrm(ks[11], (L, D_CONF), 0.02),
        'dn_conv_w': nrm(ks[12], (L, DN_CONV_W, 3 * D_DN), DN_CONV_W ** -0.5),
        'dn_a_log': jnp.log(jax.random.uniform(ks[14], (L, DN_HEADS), f32, minval=1.0, maxval=16.0)),
        'dn_dt_bias': dt + jnp.log(-jnp.expm1(-dt)),
        'dn_norm_g': 1.0 + nrm(ks[15], (L, DN_HEAD_DIM), 0.02),
        'w_out': nrm(ks[16], (L, D_MIX, D), D_MIX ** -0.5),
        'w_ffn_in': nrm(ks[17], (L, D, 2 * D_FF), D ** -0.5),
        'w_ffn_out': nrm(ks[18], (L, D_FF, D), D_FF ** -0.5),
        'final_norm_g': 1.0 + nrm(ks[19], (D,), 0.02),
    }


def _fwd_reference(x, c, w_ada, b_ada, norm_mix_g, norm_ffn_g, w_in, conv_a_w, conf_dw_w, conf_dw_b,
              conf_ln_g, conf_ln_b, dn_conv_w, dn_a_log, dn_dt_bias, dn_norm_g, w_out,
              w_ffn_in, w_ffn_out, final_norm_g):
    Bsz, T, _ = x.shape
    c_act = jax.nn.silu(c)
    sizes = [D_CONV] * 3 + [D_CONF] * 2 + [D_DN] * 4 + [DN_HEADS] * 2
    split_idx = np.cumsum(sizes)[:-1].tolist()
    for l in range(DEPTH):
        mod = (c_act @ w_ada[l] + b_ada[l])[:, None, :]
        sh1, sc1, g1, sh2, sc2, g2 = jnp.split(mod, N_MOD, axis=-1)

        h = rmsnorm(x, norm_mix_g[l]) * (1.0 + sc1) + sh1
        proj = h @ w_in[l]
        (a_b, a_c, a_v, b_a, b_g, c_q, c_k, c_v, c_z, c_alpha, c_beta) = jnp.split(proj, split_idx, axis=-1)

        y_a = a_b * causal_dwconv(a_c * a_v, conv_a_w[l])

        u = b_a * jax.nn.sigmoid(b_g)
        u = causal_dwconv(u, conf_dw_w[l]) + conf_dw_b[l]
        y_b = jax.nn.silu(layernorm(u, conf_ln_g[l], conf_ln_b[l]))

        qkv = jax.nn.silu(causal_dwconv(jnp.concatenate([c_q, c_k, c_v], axis=-1), dn_conv_w[l]))
        q, k, v = jnp.split(qkv.astype(jnp.float32), 3, axis=-1)
        q = l2norm(q.reshape(Bsz, T, DN_HEADS, DN_HEAD_DIM))
        k = l2norm(k.reshape(Bsz, T, DN_HEADS, DN_HEAD_DIM))
        v = v.reshape(Bsz, T, DN_HEADS, DN_HEAD_DIM)
        gdec = -jnp.exp(dn_a_log[l].astype(jnp.float32)) * jax.nn.softplus(
            c_alpha.astype(jnp.float32) + dn_dt_bias[l].astype(jnp.float32))
        beta = jax.nn.sigmoid(c_beta.astype(jnp.float32))
        o = chunk_gated_delta_rule(q, k, v, gdec, beta).astype(x.dtype)
        z = c_z.reshape(Bsz, T, DN_HEADS, DN_HEAD_DIM)
        y_c = (rmsnorm(o, dn_norm_g[l]) * jax.nn.silu(z)).reshape(Bsz, T, D_DN)

        mix = jnp.concatenate([y_a, y_b, y_c], axis=-1) @ w_out[l]
        x = x + g1 * mix

        h = rmsnorm(x, norm_ffn_g[l]) * (1.0 + sc2) + sh2
        gate, up = jnp.split(h @ w_ffn_in[l], 2, axis=-1)
        x = x + g2 * ((jax.nn.silu(gate) * up) @ w_ffn_out[l])
    return rmsnorm(x, final_norm_g)


import jax as _jax
import jax.numpy as _jnp

TWIN_FORMAT = 'train_step'
FWD_PARAMS = ['x', 'c', 'w_ada', 'b_ada', 'norm_mix_g', 'norm_ffn_g', 'w_in', 'conv_a_w', 'conf_dw_w', 'conf_dw_b', 'conf_ln_g', 'conf_ln_b', 'dn_conv_w', 'dn_a_log', 'dn_dt_bias', 'dn_norm_g', 'w_out', 'w_ffn_in', 'w_ffn_out', 'final_norm_g']
TWIN_WEIGHTS = ['w_ada', 'b_ada', 'norm_mix_g', 'norm_ffn_g', 'w_in', 'conv_a_w', 'conf_dw_w', 'conf_dw_b', 'conf_ln_g', 'conf_ln_b', 'dn_conv_w', 'dn_a_log', 'dn_dt_bias', 'dn_norm_g', 'w_out', 'w_ffn_in', 'w_ffn_out', 'final_norm_g']
TWIN_DIFF_INPUT = 'x'
TWIN_INPUTS = ['x', 'c', 'w_ada', 'b_ada', 'norm_mix_g', 'norm_ffn_g', 'w_in', 'conv_a_w', 'conf_dw_w', 'conf_dw_b', 'conf_ln_g', 'conf_ln_b', 'dn_conv_w', 'dn_a_log', 'dn_dt_bias', 'dn_norm_g', 'w_out', 'w_ffn_in', 'w_ffn_out', 'final_norm_g', 'loss_target', 'm_w_ada', 'm_b_ada', 'm_norm_mix_g', 'm_norm_ffn_g', 'm_w_in', 'm_conv_a_w', 'm_conf_dw_w', 'm_conf_dw_b', 'm_conf_ln_g', 'm_conf_ln_b', 'm_dn_conv_w', 'm_dn_a_log', 'm_dn_dt_bias', 'm_dn_norm_g', 'm_w_out', 'm_w_ffn_in', 'm_w_ffn_out', 'm_final_norm_g', 'v_w_ada', 'v_b_ada', 'v_norm_mix_g', 'v_norm_ffn_g', 'v_w_in', 'v_conv_a_w', 'v_conf_dw_w', 'v_conf_dw_b', 'v_conf_ln_g', 'v_conf_ln_b', 'v_dn_conv_w', 'v_dn_a_log', 'v_dn_dt_bias', 'v_dn_norm_g', 'v_w_out', 'v_w_ffn_in', 'v_w_ffn_out', 'v_final_norm_g']
TWIN_OUTPUTS = ['loss', 'grad_x', 'grad_w_ada', 'grad_b_ada', 'grad_norm_mix_g', 'grad_norm_ffn_g', 'grad_w_in', 'grad_conv_a_w', 'grad_conf_dw_w', 'grad_conf_dw_b', 'grad_conf_ln_g', 'grad_conf_ln_b', 'grad_dn_conv_w', 'grad_dn_a_log', 'grad_dn_dt_bias', 'grad_dn_norm_g', 'grad_w_out', 'grad_w_ffn_in', 'grad_w_ffn_out', 'grad_final_norm_g', 'delta_w_ada', 'delta_b_ada', 'delta_norm_mix_g', 'delta_norm_ffn_g', 'delta_w_in', 'delta_conv_a_w', 'delta_conf_dw_w', 'delta_conf_dw_b', 'delta_conf_ln_g', 'delta_conf_ln_b', 'delta_dn_conv_w', 'delta_dn_a_log', 'delta_dn_dt_bias', 'delta_dn_norm_g', 'delta_w_out', 'delta_w_ffn_in', 'delta_w_ffn_out', 'delta_final_norm_g', 'new_m_w_ada', 'new_m_b_ada', 'new_m_norm_mix_g', 'new_m_norm_ffn_g', 'new_m_w_in', 'new_m_conv_a_w', 'new_m_conf_dw_w', 'new_m_conf_dw_b', 'new_m_conf_ln_g', 'new_m_conf_ln_b', 'new_m_dn_conv_w', 'new_m_dn_a_log', 'new_m_dn_dt_bias', 'new_m_dn_norm_g', 'new_m_w_out', 'new_m_w_ffn_in', 'new_m_w_ffn_out', 'new_m_final_norm_g', 'new_v_w_ada', 'new_v_b_ada', 'new_v_norm_mix_g', 'new_v_norm_ffn_g', 'new_v_w_in', 'new_v_conv_a_w', 'new_v_conf_dw_w', 'new_v_conf_dw_b', 'new_v_conf_ln_g', 'new_v_conf_ln_b', 'new_v_dn_conv_w', 'new_v_dn_a_log', 'new_v_dn_dt_bias', 'new_v_dn_norm_g', 'new_v_w_out', 'new_v_w_ffn_in', 'new_v_w_ffn_out', 'new_v_final_norm_g']
TWIN_LEAF_KINDS = {'loss': 'loss', 'grad_x': 'grad_x', 'grad_w_ada': 'grad_w', 'grad_b_ada': 'grad_w', 'grad_norm_mix_g': 'grad_w', 'grad_norm_ffn_g': 'grad_w', 'grad_w_in': 'grad_w', 'grad_conv_a_w': 'grad_w', 'grad_conf_dw_w': 'grad_w', 'grad_conf_dw_b': 'grad_w', 'grad_conf_ln_g': 'grad_w', 'grad_conf_ln_b': 'grad_w', 'grad_dn_conv_w': 'grad_w', 'grad_dn_a_log': 'grad_w', 'grad_dn_dt_bias': 'grad_w', 'grad_dn_norm_g': 'grad_w', 'grad_w_out': 'grad_w', 'grad_w_ffn_in': 'grad_w', 'grad_w_ffn_out': 'grad_w', 'grad_final_norm_g': 'grad_w', 'delta_w_ada': 'delta_w', 'delta_b_ada': 'delta_w', 'delta_norm_mix_g': 'delta_w', 'delta_norm_ffn_g': 'delta_w', 'delta_w_in': 'delta_w', 'delta_conv_a_w': 'delta_w', 'delta_conf_dw_w': 'delta_w', 'delta_conf_dw_b': 'delta_w', 'delta_conf_ln_g': 'delta_w', 'delta_conf_ln_b': 'delta_w', 'delta_dn_conv_w': 'delta_w', 'delta_dn_a_log': 'delta_w', 'delta_dn_dt_bias': 'delta_w', 'delta_dn_norm_g': 'delta_w', 'delta_w_out': 'delta_w', 'delta_w_ffn_in': 'delta_w', 'delta_w_ffn_out': 'delta_w', 'delta_final_norm_g': 'delta_w', 'new_m_w_ada': 'new_m', 'new_m_b_ada': 'new_m', 'new_m_norm_mix_g': 'new_m', 'new_m_norm_ffn_g': 'new_m', 'new_m_w_in': 'new_m', 'new_m_conv_a_w': 'new_m', 'new_m_conf_dw_w': 'new_m', 'new_m_conf_dw_b': 'new_m', 'new_m_conf_ln_g': 'new_m', 'new_m_conf_ln_b': 'new_m', 'new_m_dn_conv_w': 'new_m', 'new_m_dn_a_log': 'new_m', 'new_m_dn_dt_bias': 'new_m', 'new_m_dn_norm_g': 'new_m', 'new_m_w_out': 'new_m', 'new_m_w_ffn_in': 'new_m', 'new_m_w_ffn_out': 'new_m', 'new_m_final_norm_g': 'new_m', 'new_v_w_ada': 'new_v', 'new_v_b_ada': 'new_v', 'new_v_norm_mix_g': 'new_v', 'new_v_norm_ffn_g': 'new_v', 'new_v_w_in': 'new_v', 'new_v_conv_a_w': 'new_v', 'new_v_conf_dw_w': 'new_v', 'new_v_conf_dw_b': 'new_v', 'new_v_conf_ln_g': 'new_v', 'new_v_conf_ln_b': 'new_v', 'new_v_dn_conv_w': 'new_v', 'new_v_dn_a_log': 'new_v', 'new_v_dn_dt_bias': 'new_v', 'new_v_dn_norm_g': 'new_v', 'new_v_w_out': 'new_v', 'new_v_w_ffn_in': 'new_v', 'new_v_w_ffn_out': 'new_v', 'new_v_final_norm_g': 'new_v'}


def _forward(args):
    return _fwd_reference(*[args[k] for k in FWD_PARAMS])


def _output_shape():
    def fwd():
        inp = _fwd_setup_inputs(0)
        return _fwd_reference(*[inp[k] for k in FWD_PARAMS])
    out = _jax.eval_shape(fwd)
    return out.shape, out.dtype

N_MICROBATCH = 1
ADAM_LR = 0.001
ADAM_B1 = 0.9
ADAM_B2 = 0.999
ADAM_EPS = 1e-08
ADAM_WD = 0.01
ADAM_STEP = 10
PER_EXAMPLE_BATCH_AXIS = {'x': 0, 'c': 0, 'loss_target': 0}
SHARED_INPUTS = []
_WEIGHT_DTYPES = {'w_ada': _jnp.float32, 'b_ada': _jnp.float32, 'norm_mix_g': _jnp.float32, 'norm_ffn_g': _jnp.float32, 'w_in': _jnp.float32, 'conv_a_w': _jnp.float32, 'conf_dw_w': _jnp.float32, 'conf_dw_b': _jnp.float32, 'conf_ln_g': _jnp.float32, 'conf_ln_b': _jnp.float32, 'dn_conv_w': _jnp.float32, 'dn_a_log': _jnp.float32, 'dn_dt_bias': _jnp.float32, 'dn_norm_g': _jnp.float32, 'w_out': _jnp.float32, 'w_ffn_in': _jnp.float32, 'w_ffn_out': _jnp.float32, 'final_norm_g': _jnp.float32}
MOMENT_SCALE = {'w_ada': 5.771345e-02, 'b_ada': 9.879317e-02, 'norm_mix_g': 7.121291e-02, 'norm_ffn_g': 5.149051e-02, 'w_in': 4.126328e-02, 'conv_a_w': 7.135889e-02, 'conf_dw_w': 3.357785e-02, 'conf_dw_b': 6.400623e-02, 'conf_ln_g': 3.890024e-02, 'conf_ln_b': 3.562829e-02, 'dn_conv_w': 2.548478e-02, 'dn_a_log': 1.647123e-01, 'dn_dt_bias': 1.501723e-01, 'dn_norm_g': 7.441298e-02, 'w_out': 4.518446e-02, 'w_ffn_in': 2.295249e-02, 'w_ffn_out': 3.743371e-02, 'final_norm_g': 3.205174e+01}


def _to_microbatches(a, axis):
    t = _jnp.moveaxis(a, axis, 0)
    t = t.reshape((N_MICROBATCH, t.shape[0] // N_MICROBATCH) + t.shape[1:])
    return _jnp.moveaxis(t, 1, axis + 1)


def setup_inputs(seed: int = 0) -> dict:
    inp = _fwd_setup_inputs(seed)
    key = _jax.random.fold_in(_jax.random.key(seed), 7919)
    shape, _ = _output_shape()
    out = dict(inp)
    out["loss_target"] = _jax.random.normal(_jax.random.fold_in(key, 0), shape, _jnp.float32)
    for i, name in enumerate(TWIN_WEIGHTS):
        w = inp[name].astype(_jnp.float32)
        if MOMENT_SCALE is None:
            s = _jnp.sqrt(_jnp.mean(_jnp.square(w)) + 1e-30)
        else:
            s = MOMENT_SCALE[name]
        km, kv = _jax.random.split(_jax.random.fold_in(key, i + 1))
        out[name] = w
        out["m_" + name] = s * _jax.random.normal(km, w.shape, _jnp.float32)
        out["v_" + name] = (s * s) * _jax.random.uniform(kv, w.shape, _jnp.float32, 0.5, 1.5)
    if N_MICROBATCH > 1:
        for name, axis in PER_EXAMPLE_BATCH_AXIS.items():
            out[name] = _to_microbatches(out[name], axis)
    return {'x': out['x'], 'c': out['c'], 'w_ada': out['w_ada'], 'b_ada': out['b_ada'], 'norm_mix_g': out['norm_mix_g'], 'norm_ffn_g': out['norm_ffn_g'], 'w_in': out['w_in'], 'conv_a_w': out['conv_a_w'], 'conf_dw_w': out['conf_dw_w'], 'conf_dw_b': out['conf_dw_b'], 'conf_ln_g': out['conf_ln_g'], 'conf_ln_b': out['conf_ln_b'], 'dn_conv_w': out['dn_conv_w'], 'dn_a_log': out['dn_a_log'], 'dn_dt_bias': out['dn_dt_bias'], 'dn_norm_g': out['dn_norm_g'], 'w_out': out['w_out'], 'w_ffn_in': out['w_ffn_in'], 'w_ffn_out': out['w_ffn_out'], 'final_norm_g': out['final_norm_g'], 'loss_target': out['loss_target'], 'm_w_ada': out['m_w_ada'], 'm_b_ada': out['m_b_ada'], 'm_norm_mix_g': out['m_norm_mix_g'], 'm_norm_ffn_g': out['m_norm_ffn_g'], 'm_w_in': out['m_w_in'], 'm_conv_a_w': out['m_conv_a_w'], 'm_conf_dw_w': out['m_conf_dw_w'], 'm_conf_dw_b': out['m_conf_dw_b'], 'm_conf_ln_g': out['m_conf_ln_g'], 'm_conf_ln_b': out['m_conf_ln_b'], 'm_dn_conv_w': out['m_dn_conv_w'], 'm_dn_a_log': out['m_dn_a_log'], 'm_dn_dt_bias': out['m_dn_dt_bias'], 'm_dn_norm_g': out['m_dn_norm_g'], 'm_w_out': out['m_w_out'], 'm_w_ffn_in': out['m_w_ffn_in'], 'm_w_ffn_out': out['m_w_ffn_out'], 'm_final_norm_g': out['m_final_norm_g'], 'v_w_ada': out['v_w_ada'], 'v_b_ada': out['v_b_ada'], 'v_norm_mix_g': out['v_norm_mix_g'], 'v_norm_ffn_g': out['v_norm_ffn_g'], 'v_w_in': out['v_w_in'], 'v_conv_a_w': out['v_conv_a_w'], 'v_conf_dw_w': out['v_conf_dw_w'], 'v_conf_dw_b': out['v_conf_dw_b'], 'v_conf_ln_g': out['v_conf_ln_g'], 'v_conf_ln_b': out['v_conf_ln_b'], 'v_dn_conv_w': out['v_dn_conv_w'], 'v_dn_a_log': out['v_dn_a_log'], 'v_dn_dt_bias': out['v_dn_dt_bias'], 'v_dn_norm_g': out['v_dn_norm_g'], 'v_w_out': out['v_w_out'], 'v_w_ffn_in': out['v_w_ffn_in'], 'v_w_ffn_out': out['v_w_ffn_out'], 'v_final_norm_g': out['v_final_norm_g']}


def _loss(weights, diff, rest, loss_target):
    with _jax.named_scope("forward"):
        args = {**rest, TWIN_DIFF_INPUT: diff, **{k: w.astype(_WEIGHT_DTYPES[k]) for k, w in weights.items()}}
        y = _forward(args)
    with _jax.named_scope("loss_head"):
        err = _jnp.square(y.astype(_jnp.float32) - loss_target)
        return 0.5 * _jnp.sum(_jnp.mean(err, axis=-1)) if err.ndim else 0.5 * err


def _adamw(w, g, m, v):
    m = ADAM_B1 * m + (1.0 - ADAM_B1) * g
    v = ADAM_B2 * v + (1.0 - ADAM_B2) * _jnp.square(g)
    m_hat = m / (1.0 - ADAM_B1 ** ADAM_STEP)
    v_hat = v / (1.0 - ADAM_B2 ** ADAM_STEP)
    delta = -ADAM_LR * (m_hat / (_jnp.sqrt(v_hat) + ADAM_EPS) + ADAM_WD * w)
    return delta, m, v


def reference(x, c, w_ada, b_ada, norm_mix_g, norm_ffn_g, w_in, conv_a_w, conf_dw_w, conf_dw_b, conf_ln_g, conf_ln_b, dn_conv_w, dn_a_log, dn_dt_bias, dn_norm_g, w_out, w_ffn_in, w_ffn_out, final_norm_g, loss_target, m_w_ada, m_b_ada, m_norm_mix_g, m_norm_ffn_g, m_w_in, m_conv_a_w, m_conf_dw_w, m_conf_dw_b, m_conf_ln_g, m_conf_ln_b, m_dn_conv_w, m_dn_a_log, m_dn_dt_bias, m_dn_norm_g, m_w_out, m_w_ffn_in, m_w_ffn_out, m_final_norm_g, v_w_ada, v_b_ada, v_norm_mix_g, v_norm_ffn_g, v_w_in, v_conv_a_w, v_conf_dw_w, v_conf_dw_b, v_conf_ln_g, v_conf_ln_b, v_dn_conv_w, v_dn_a_log, v_dn_dt_bias, v_dn_norm_g, v_w_out, v_w_ffn_in, v_w_ffn_out, v_final_norm_g):
    given = dict(x=x, c=c, w_ada=w_ada, b_ada=b_ada, norm_mix_g=norm_mix_g, norm_ffn_g=norm_ffn_g, w_in=w_in, conv_a_w=conv_a_w, conf_dw_w=conf_dw_w, conf_dw_b=conf_dw_b, conf_ln_g=conf_ln_g, conf_ln_b=conf_ln_b, dn_conv_w=dn_conv_w, dn_a_log=dn_a_log, dn_dt_bias=dn_dt_bias, dn_norm_g=dn_norm_g, w_out=w_out, w_ffn_in=w_ffn_in, w_ffn_out=w_ffn_out, final_norm_g=final_norm_g, loss_target=loss_target, m_w_ada=m_w_ada, m_b_ada=m_b_ada, m_norm_mix_g=m_norm_mix_g, m_norm_ffn_g=m_norm_ffn_g, m_w_in=m_w_in, m_conv_a_w=m_conv_a_w, m_conf_dw_w=m_conf_dw_w, m_conf_dw_b=m_conf_dw_b, m_conf_ln_g=m_conf_ln_g, m_conf_ln_b=m_conf_ln_b, m_dn_conv_w=m_dn_conv_w, m_dn_a_log=m_dn_a_log, m_dn_dt_bias=m_dn_dt_bias, m_dn_norm_g=m_dn_norm_g, m_w_out=m_w_out, m_w_ffn_in=m_w_ffn_in, m_w_ffn_out=m_w_ffn_out, m_final_norm_g=m_final_norm_g, v_w_ada=v_w_ada, v_b_ada=v_b_ada, v_norm_mix_g=v_norm_mix_g, v_norm_ffn_g=v_norm_ffn_g, v_w_in=v_w_in, v_conv_a_w=v_conv_a_w, v_conf_dw_w=v_conf_dw_w, v_conf_dw_b=v_conf_dw_b, v_conf_ln_g=v_conf_ln_g, v_conf_ln_b=v_conf_ln_b, v_dn_conv_w=v_dn_conv_w, v_dn_a_log=v_dn_a_log, v_dn_dt_bias=v_dn_dt_bias, v_dn_norm_g=v_dn_norm_g, v_w_out=v_w_out, v_w_ffn_in=v_w_ffn_in, v_w_ffn_out=v_w_ffn_out, v_final_norm_g=v_final_norm_g)
    weights = {n: given[n] for n in TWIN_WEIGHTS}
    shared = {n: given[n] for n in SHARED_INPUTS}
    per_example = {n: given[n] for n in ['x', 'c']}
    grad_fn = _jax.value_and_grad(_loss, argnums=(0, 1))

    def one_microbatch(ex, loss_target):
        ex = dict(ex)
        diff = ex.pop(TWIN_DIFF_INPUT)
        return grad_fn(weights, diff, {**shared, **ex}, loss_target)

    if N_MICROBATCH == 1:
        loss, (grad_w, grad_x) = one_microbatch(per_example, given["loss_target"])
    else:
        def body(carry, xs):
            loss_sum, grad_sum = carry
            l_k, (gw_k, gx_k) = one_microbatch(xs[0], xs[1])
            with _jax.named_scope("update"):
                return (loss_sum + l_k, _jax.tree.map(_jnp.add, grad_sum, gw_k)), gx_k

        init = (_jnp.zeros((), _jnp.float32), _jax.tree.map(_jnp.zeros_like, weights))
        (loss, grad_w), grad_x = _jax.lax.scan(body, init, (per_example, given["loss_target"]))
    with _jax.named_scope("update"):
        delta_w, new_m, new_v = {}, {}, {}
        for n in TWIN_WEIGHTS:
            delta_w[n], new_m[n], new_v[n] = _adamw(weights[n], grad_w[n], given["m_" + n], given["v_" + n])
    return (loss, grad_x, *[grad_w[n] for n in TWIN_WEIGHTS], *[delta_w[n] for n in TWIN_WEIGHTS],
            *[new_m[n] for n in TWIN_WEIGHTS], *[new_v[n] for n in TWIN_WEIGHTS])
```

```python
import functools

import jax
import jax.numpy as jnp
from jax import lax
from jax.experimental import pallas as pl
from jax.experimental.pallas import tpu as pltpu

F32 = jnp.float32
BF16 = jnp.bfloat16
HI = lax.Precision.HIGHEST
MESH = pl.DeviceIdType.MESH

D = 1024
DEPTH = 4
DC = 256
DDN = 512
NH = 4
HD = 128
CH = 64
DFF = 2816
IN_COLS = 3336
NP = 3456
KA, KB, KC = 3, 31, 4
HALO = 32
EPS = 1e-6
O_AB, O_AC, O_AV, O_BA, O_BG, O_Q, O_K, O_V, O_Z, O_GB = 0, 256, 512, 768, 1024, 1280, 1792, 2304, 2816, 3328

ADAM_LR, ADAM_B1, ADAM_B2, ADAM_EPS, ADAM_WD, ADAM_STEP = 0.001, 0.9, 0.999, 1e-08, 0.01, 10

VMEM_LIMIT = 56 * 1024 * 1024


def _cp(n_grid):
    return pltpu.CompilerParams(dimension_semantics=("arbitrary",) * n_grid, vmem_limit_bytes=VMEM_LIMIT)


def _sds(shape, dtype=F32):
    return jax.ShapeDtypeStruct(tuple(shape), dtype)


def _dot(a, b, prec=None):
    return jnp.dot(a, b, preferred_element_type=F32, precision=prec)


def _dot_nt(a, b, prec=None):
    return lax.dot_general(a, b, (((1,), (1,)), ((), ())), preferred_element_type=F32, precision=prec)


def _dot_tn(a, b, prec=None):
    return lax.dot_general(a, b, (((0,), (0,)), ((), ())), preferred_element_type=F32, precision=prec)


def _silu(x):
    return x * jax.nn.sigmoid(x)


def _colsum(x):
    return jnp.sum(x, axis=0, keepdims=True)


TILE_CAP = 512


def _tile(T, want):
    t = min(T, want, TILE_CAP)
    assert T % t == 0
    return t


def _normmod(x, gn, sc, sh):
    r = lax.rsqrt(jnp.mean(x * x, axis=-1, keepdims=True) + EPS)
    return ((x * r) * gn) * (1.0 + sc) + sh


def _rms(x, g):
    return (x * lax.rsqrt(jnp.mean(x * x, axis=-1, keepdims=True) + EPS)) * g


def _mix_b_post(u, ln_g, ln_b):
    mu = jnp.mean(u, axis=-1, keepdims=True)
    var = jnp.mean(jnp.square(u - mu), axis=-1, keepdims=True)
    return _silu(((u - mu) * lax.rsqrt(var + 1e-5)) * ln_g + ln_b)


def _softplus(z):
    return jnp.where(z > 0, z, 0.0) + jnp.log(1.0 + jnp.exp(-jnp.where(z > 0, z, -z)))


def _chunk_tril(tt):
    r = lax.broadcasted_iota(jnp.int32, (tt, tt), 0)
    c = lax.broadcasted_iota(jnp.int32, (tt, tt), 1)
    return ((r // CH == c // CH) & (c <= r)).astype(F32)


def _eye8():
    return (lax.broadcasted_iota(jnp.int32, (8, HD), 0) == lax.broadcasted_iota(jnp.int32, (8, HD), 1)).astype(F32)


def _dn_post(pre_q, pre_k, pre_v, blk, alog_row, dt_row):
    q = [s * lax.rsqrt(jnp.sum(s * s, -1, keepdims=True) + EPS) * (HD ** -0.5) for s in map(_silu, pre_q)]
    k = [s * lax.rsqrt(jnp.sum(s * s, -1, keepdims=True) + EPS) for s in map(_silu, pre_k)]
    v = [_silu(p) for p in pre_v]
    lane = lax.broadcasted_iota(jnp.int32, (1, HD), 1)
    g = -jnp.exp(alog_row) * _softplus(blk + dt_row)
    beta = jax.nn.sigmoid(blk)
    gc = _dot(_chunk_tril(blk.shape[0]), jnp.where(lane < NH, g, 0.0), HI)
    gb = jnp.where(lane < NH, gc, jnp.where(lane < 2 * NH, beta, 0.0))
    grow = _dot_nt(_eye8(), gc, HI)
    return q, k, v, gb, grow


def _dn_intra(q, k, v, beta, gcol, grow):
    r = lax.broadcasted_iota(jnp.int32, (CH, CH), 0)
    c = lax.broadcasted_iota(jnp.int32, (CH, CH), 1)
    causal, strict = c <= r, c < r
    decay = jnp.where(causal, jnp.exp(jnp.where(causal, gcol - grow, 0.0)), 0.0)
    kb = k * beta
    x = -jnp.where(strict, _dot_nt(kb, k, HI) * decay, 0.0)
    p = (r == c).astype(F32) + x
    y = x
    for _ in range(5):
        y = _dot(y, y, HI)
        p = p + _dot(p, y, HI)
    u = _dot(p, v * beta, HI)
    w = _dot(p, kb * jnp.exp(gcol), HI)
    qk = jnp.where(causal, _dot_nt(q, k, HI) * decay, 0.0)
    return u, w, qk


def _dn_inter(s, q, k, u, w, gcol, qk):
    last = (lax.broadcasted_iota(jnp.int32, (CH, 1), 0) == CH - 1).astype(F32)
    g_last = jnp.sum(gcol * last, axis=0, keepdims=True)
    v_new = u - _dot(w, s, HI)
    o = _dot(q * jnp.exp(gcol), s, HI) + _dot(qk, v_new, HI)
    s_new = s * jnp.exp(g_last) + _dot_tn(k * jnp.exp(g_last - gcol), v_new, HI)
    return s_new, o


def _yc(o, z, gdn):
    return _rms(o, gdn) * _silu(z)


def nm_fwd(x, gn, sc, sh, w, *, tn, name):
    T, N = x.shape[0], w.shape[1]
    tt = _tile(T, 256)

    def body(x_ref, gn_ref, sc_ref, sh_ref, w_ref, h_ref, o_ref):
        @pl.when(pl.program_id(1) == 0)
        def _():
            h_ref[...] = _normmod(x_ref[...], gn_ref[...], sc_ref[...], sh_ref[...]).astype(BF16)
        o_ref[...] = _dot(h_ref[...], w_ref[...])

    row = pl.BlockSpec((1, D), lambda i, j: (0, 0))
    return pl.pallas_call(
        body, grid=(T // tt, N // tn), name=name,
        in_specs=[pl.BlockSpec((tt, D), lambda i, j: (i, 0)), row, row, row, pl.BlockSpec((D, tn), lambda i, j: (0, j))],
        out_specs=[pl.BlockSpec((tt, D), lambda i, j: (i, 0)), pl.BlockSpec((tt, tn), lambda i, j: (i, j))],
        out_shape=[_sds((T, D), BF16), _sds((T, N))], compiler_params=_cp(2))(x, gn, sc, sh, w)


def nm_bwd(dys, w, x, gn, sc, sh, dres, *, name):
    T = x.shape[0]
    N = w.shape[1]
    tt = _tile(T, 256)
    widths = [a.shape[1] for a in dys]
    assert sum(widths) == N
    n = len(dys)

    def body(*refs):
        dy_refs, (w_ref, x_ref, gn_ref, sc_ref, sh_ref, dres_ref) = refs[:n], refs[n:n + 6]
        dx_ref, dgn_ref, dsc_ref, dsh_ref, dyb_ref = refs[n + 6:]
        i = pl.program_id(0)
        dyb = jnp.concatenate([r[...].astype(BF16) for r in dy_refs], axis=1) if n > 1 else dy_refs[0][...].astype(BF16)
        dyb_ref[...] = dyb
        dh = _dot_nt(dyb, w_ref[...])
        _, vjp = jax.vjp(_normmod, x_ref[...], gn_ref[...], sc_ref[...], sh_ref[...])
        dx, dgn, dsc, dsh = vjp(dh)
        dx_ref[...] = dres_ref[...] + dx

        @pl.when(i == 0)
        def _():
            dgn_ref[...] = jnp.zeros_like(dgn_ref)
            dsc_ref[...] = jnp.zeros_like(dsc_ref)
            dsh_ref[...] = jnp.zeros_like(dsh_ref)
        dgn_ref[...] += dgn
        dsc_ref[...] += dsc
        dsh_ref[...] += dsh

    row = pl.BlockSpec((1, D), lambda i: (0, 0))
    tile = pl.BlockSpec((tt, D), lambda i: (i, 0))
    return pl.pallas_call(
        body, grid=(T // tt,), name=name,
        in_specs=[pl.BlockSpec((tt, wd), lambda i: (i, 0)) for wd in widths]
        + [pl.BlockSpec((D, N), lambda i: (0, 0)), tile, row, row, row, tile],
        out_specs=[tile, row, row, row, pl.BlockSpec((tt, N), lambda i: (i, 0))],
        out_shape=[_sds((T, D)), _sds((1, D)), _sds((1, D)), _sds((1, D)), _sds((T, N), BF16)],
        compiler_params=_cp(1))(*dys, w, x, gn, sc, sh, dres)


def mm_tn(a, b, *, tn, name):
    T, K = a.shape
    N = b.shape[1]
    tt = _tile(T, 512)

    def body(a_ref, b_ref, o_ref):
        @pl.when(pl.program_id(1) == 0)
        def _():
            o_ref[...] = jnp.zeros_like(o_ref)
        o_ref[...] += _dot_tn(a_ref[...], b_ref[...])

    return pl.pallas_call(
        body, grid=(N // tn, T // tt), name=name,
        in_specs=[pl.BlockSpec((tt, K), lambda j, t: (t, 0)), pl.BlockSpec((tt, tn), lambda j, t: (t, j))],
        out_specs=pl.BlockSpec((K, tn), lambda j, t: (0, j)), out_shape=_sds((K, N)), compiler_params=_cp(2))(a, b)


def _fill_pad(pad_ref, prev, cur, first):
    pad_ref[pl.ds(0, HALO), :] = jnp.where(first, 0.0, prev)
    pad_ref[pl.ds(HALO, cur.shape[0]), :] = cur


def _causal_conv(pad_ref, w, K, tt):
    acc = None
    for k in range(K):
        term = pad_ref[pl.ds(HALO - (K - 1) + k, tt), :] * w[k:k + 1, :]
        acc = term if acc is None else acc + term
    return acc


def _conv_inputs(pc, pp, first, pad_a, pad_b, pad_c):
    def s_of(p):
        return p[:, O_AC:O_AV] * p[:, O_AV:O_BA]

    def u0_of(p):
        return p[:, O_BA:O_BG] * jax.nn.sigmoid(p[:, O_BG:O_Q])

    _fill_pad(pad_a, s_of(pp), s_of(pc), first)
    _fill_pad(pad_b, u0_of(pp), u0_of(pc), first)
    _fill_pad(pad_c, pp[:, O_Q:O_Z], pc[:, O_Q:O_Z], first)


def _mix_specs(T, tt):
    cur = pl.BlockSpec((tt, O_Z), lambda i: (i, 0))
    prev = pl.BlockSpec((HALO, O_Z), lambda i: (jnp.maximum(i * (tt // HALO) - 1, 0), 0))
    gbb = pl.BlockSpec((tt, HD), lambda i: (i, O_GB // HD))
    return cur, prev, gbb


def _full(shape):
    return pl.BlockSpec(shape, lambda i: (0,) * len(shape))


def mix_fwd(proj, wa, wb, bb, ln_g, ln_b, wc, alog_row, dt_row, *, name):
    T = proj.shape[0]
    tt = _tile(T, 256)

    def body(pc_ref, pp_ref, blk_ref, wa_ref, wb_ref, bb_ref, lg_ref, lb_ref, wc_ref, al_ref, dt_ref,
             yab_ref, q_ref, k_ref, v_ref, gb_ref, grow_ref, pad_a, pad_b, pad_c):
        first = pl.program_id(0) == 0
        pc = pc_ref[...]
        _conv_inputs(pc, pp_ref[...], first, pad_a, pad_b, pad_c)
        ya = pc[:, O_AB:O_AC] * _causal_conv(pad_a, wa_ref[...], KA, tt)
        yb = _mix_b_post(_causal_conv(pad_b, wb_ref[...], KB, tt) + bb_ref[...], lg_ref[...], lb_ref[...])
        yab_ref[...] = jnp.concatenate([ya, yb], axis=1)
        pre = _causal_conv(pad_c, wc_ref[...], KC, tt)
        blocks = [pre[:, j * HD:(j + 1) * HD] for j in range(3 * NH)]
        q, k, v, gb, grow = _dn_post(blocks[:NH], blocks[NH:2 * NH], blocks[2 * NH:], blk_ref[...], al_ref[...], dt_ref[...])
        q_ref[...] = jnp.concatenate(q, axis=1)
        k_ref[...] = jnp.concatenate(k, axis=1)
        v_ref[...] = jnp.concatenate(v, axis=1)
        gb_ref[...] = gb
        grow_ref[...] = grow

    cur, prev, gbb = _mix_specs(T, tt)
    t512 = pl.BlockSpec((tt, DDN), lambda i: (i, 0))
    return pl.pallas_call(
        body, grid=(T // tt,), name=name,
        in_specs=[cur, prev, gbb, _full((8, DC)), _full((32, DC)), _full((1, DC)), _full((1, DC)), _full((1, DC)),
                  _full((8, 3 * DDN)), _full((1, HD)), _full((1, HD))],
        out_specs=[t512, t512, t512, t512, pl.BlockSpec((tt, HD), lambda i: (i, 0)), pl.BlockSpec((8, tt), lambda i: (0, i))],
        out_shape=[_sds((T, 2 * DC)), _sds((T, DDN)), _sds((T, DDN)), _sds((T, DDN)), _sds((T, HD)), _sds((8, T))],
        scratch_shapes=[pltpu.VMEM((HALO + tt, DC), F32), pltpu.VMEM((HALO + tt, DC), F32), pltpu.VMEM((HALO + tt, 3 * DDN), F32)],
        compiler_params=_cp(1))(proj, proj, proj, wa, wb, bb, ln_g, ln_b, wc, alog_row, dt_row)


def mix_bwd_point(proj, dyab, dq, dk, dv, dgb, dgrow, wa, wb, bb, ln_g, ln_b, wc, alog_row, dt_row, *, name):
    T = proj.shape[0]
    tt = _tile(T, 256)
    CW = 2 * DC + 3 * DDN

    def body(pc_ref, pp_ref, blk_ref, dyab_ref, dq_ref, dk_ref, dv_ref, dgb_ref, dgrow_ref,
             wa_ref, wb_ref, bb_ref, lg_ref, lb_ref, wc_ref, al_ref, dt_ref,
             dab_ref, dconv_ref, dblk_ref, dbb_ref, dlg_ref, dlb_ref, dal_ref, ddt_ref, pad_a, pad_b, pad_c):
        i = pl.program_id(0)
        pc = pc_ref[...]
        _conv_inputs(pc, pp_ref[...], i == 0, pad_a, pad_b, pad_c)
        ca = _causal_conv(pad_a, wa_ref[...], KA, tt)
        u = _causal_conv(pad_b, wb_ref[...], KB, tt) + bb_ref[...]
        pre = _causal_conv(pad_c, wc_ref[...], KC, tt)
        dyab_v = dyab_ref[...]
        dya, dyb = dyab_v[:, :DC], dyab_v[:, DC:]
        dab_ref[...] = dya * ca
        dca = dya * pc[:, O_AB:O_AC]
        _, vjp_b = jax.vjp(_mix_b_post, u, lg_ref[...], lb_ref[...])
        du, dlg, dlb = vjp_b(dyb)
        blocks = [pre[:, j * HD:(j + 1) * HD] for j in range(3 * NH)]
        _, vjp_c = jax.vjp(_dn_post, blocks[:NH], blocks[NH:2 * NH], blocks[2 * NH:], blk_ref[...], al_ref[...], dt_ref[...])

        def heads(r):
            vv = r[...]
            return [vv[:, h * HD:(h + 1) * HD] for h in range(NH)]
        dpq, dpk, dpv, dblk, dal, ddt = vjp_c((heads(dq_ref), heads(dk_ref), heads(dv_ref), dgb_ref[...], dgrow_ref[...]))
        dconv_ref[...] = jnp.concatenate([dca, du] + dpq + dpk + dpv, axis=1)
        dblk_ref[...] = dblk

        @pl.when(i == 0)
        def _():
            for r in (dbb_ref, dlg_ref, dlb_ref, dal_ref, ddt_ref):
                r[...] = jnp.zeros_like(r)
        dbb_ref[...] += _colsum(du)
        dlg_ref[...] += dlg
        dlb_ref[...] += dlb
        dal_ref[...] += dal
        ddt_ref[...] += ddt

    cur, prev, gbb = _mix_specs(T, tt)
    t512 = pl.BlockSpec((tt, DDN), lambda i: (i, 0))
    t128 = pl.BlockSpec((tt, HD), lambda i: (i, 0))
    return pl.pallas_call(
        body, grid=(T // tt,), name=name,
        in_specs=[cur, prev, gbb, t512, t512, t512, t512, t128, pl.BlockSpec((8, tt), lambda i: (0, i)),
                  _full((8, DC)), _full((32, DC)), _full((1, DC)), _full((1, DC)), _full((1, DC)),
                  _full((8, 3 * DDN)), _full((1, HD)), _full((1, HD))],
        out_specs=[pl.BlockSpec((tt, DC), lambda i: (i, 0)), pl.BlockSpec((tt, CW), lambda i: (i, 0)), t128,
                   _full((1, DC)), _full((1, DC)), _full((1, DC)), _full((1, HD)), _full((1, HD))],
        out_shape=[_sds((T, DC)), _sds((T, CW)), _sds((T, HD)), _sds((1, DC)), _sds((1, DC)), _sds((1, DC)),
                   _sds((1, HD)), _sds((1, HD))],
        scratch_shapes=[pltpu.VMEM((HALO + tt, DC), F32), pltpu.VMEM((HALO + tt, DC), F32), pltpu.VMEM((HALO + tt, 3 * DDN), F32)],
        compiler_params=_cp(1))(proj, proj, proj, dyab, dq, dk, dv, dgb, dgrow, wa, wb, bb, ln_g, ln_b, wc, alog_row, dt_row)


def mix_bwd_conv(proj, dconv, wa, wb, wc, *, name):
    T = proj.shape[0]
    tt = _tile(T, 256)
    CW = 2 * DC + 3 * DDN
    nblk = T // HALO

    def body(pc_ref, pp_ref, dc_ref, dn_ref, wa_ref, wb_ref, wc_ref, dp_ref, dwa_ref, dwb_ref, dwc_ref,
             pad_a, pad_b, pad_c, dpad):
        i = pl.program_id(0)
        last = i == pl.num_programs(0) - 1
        pc = pc_ref[...]
        _conv_inputs(pc, pp_ref[...], i == 0, pad_a, pad_b, pad_c)
        dcur = dc_ref[...]
        dpad[pl.ds(0, tt), :] = dcur
        dpad[pl.ds(tt, HALO), :] = jnp.where(last, 0.0, dn_ref[...])

        @pl.when(i == 0)
        def _():
            for r in (dwa_ref, dwb_ref, dwc_ref):
                r[...] = jnp.zeros_like(r)

        def tconv(lo, hi, w, K, pad_ref, dw_ref):
            dy = dcur[:, lo:hi]
            acc = None
            for k in range(K):
                term = dpad[pl.ds(K - 1 - k, tt), lo:hi] * w[k:k + 1, :]
                acc = term if acc is None else acc + term
                dw_ref[pl.ds(k, 1), :] += _colsum(dy * pad_ref[pl.ds(HALO - (K - 1) + k, tt), :])
            return acc

        ds = tconv(0, DC, wa_ref[...], KA, pad_a, dwa_ref)
        du0 = tconv(DC, 2 * DC, wb_ref[...], KB, pad_b, dwb_ref)
        dqkv = tconv(2 * DC, CW, wc_ref[...], KC, pad_c, dwc_ref)
        a_c, a_v, b_a, b_g = pc[:, O_AC:O_AV], pc[:, O_AV:O_BA], pc[:, O_BA:O_BG], pc[:, O_BG:O_Q]
        sg = jax.nn.sigmoid(b_g)
        dp_ref[...] = jnp.concatenate([ds * a_v, ds * a_c, du0 * sg, du0 * b_a * sg * (1.0 - sg), dqkv], axis=1)

    cur, prev, _ = _mix_specs(T, tt)
    return pl.pallas_call(
        body, grid=(T // tt,), name=name,
        in_specs=[cur, prev, pl.BlockSpec((tt, CW), lambda i: (i, 0)),
                  pl.BlockSpec((HALO, CW), lambda i: (jnp.minimum((i + 1) * (tt // HALO), nblk - 1), 0)),
                  _full((8, DC)), _full((32, DC)), _full((8, 3 * DDN))],
        out_specs=[pl.BlockSpec((tt, O_Z - O_AC), lambda i: (i, 0)), _full((8, DC)), _full((32, DC)), _full((8, 3 * DDN))],
        out_shape=[_sds((T, O_Z - O_AC)), _sds((8, DC)), _sds((32, DC)), _sds((8, 3 * DDN))],
        scratch_shapes=[pltpu.VMEM((HALO + tt, DC), F32), pltpu.VMEM((HALO + tt, DC), F32), pltpu.VMEM((HALO + tt, 3 * DDN), F32),
                        pltpu.VMEM((tt + HALO, CW), F32)],
        compiler_params=_cp(1))(proj, proj, dconv, dconv, wa, wb, wc)


def _head(v, h):
    return v[:, h * HD:(h + 1) * HD]


def dn_intra_fwd(q, k, v, gb, grow3, *, name):
    T = q.shape[0]
    N = T // CH
    cb = 4 if N % 4 == 0 else 1

    def body(q_ref, k_ref, v_ref, gb_ref, gr_ref, u_ref, w_ref, qk_ref):
        for c in range(cb):
            rows = pl.ds(c * CH, CH)
            qv, kv, vv, gbv = q_ref[rows, :], k_ref[rows, :], v_ref[rows, :], gb_ref[rows, :]
            us, ws = [], []
            for h in range(NH):
                u, w, qk = _dn_intra(_head(qv, h), _head(kv, h), _head(vv, h), gbv[:, NH + h:NH + h + 1], gbv[:, h:h + 1],
                                     gr_ref[c, h:h + 1, :])
                us.append(u)
                ws.append(w)
                qk_ref[c, h] = qk
            u_ref[rows, :] = jnp.concatenate(us, axis=1)
            w_ref[rows, :] = jnp.concatenate(ws, axis=1)

    t512 = pl.BlockSpec((cb * CH, DDN), lambda i: (i, 0))
    return pl.pallas_call(
        body, grid=(N // cb,), name=name,
        in_specs=[t512, t512, t512, pl.BlockSpec((cb * CH, HD), lambda i: (i, 0)), pl.BlockSpec((cb, 8, CH), lambda i: (i, 0, 0))],
        out_specs=[t512, t512, pl.BlockSpec((cb, NH, CH, CH), lambda i: (i, 0, 0, 0))],
        out_shape=[_sds((T, DDN)), _sds((T, DDN)), _sds((N, NH, CH, CH))], compiler_params=_cp(1))(q, k, v, gb, grow3)


def dn_inter_fwd(q, k, u, w, gb, qk, *, name):
    T = q.shape[0]
    N = T // CH
    cb = 4 if N % 4 == 0 else 1

    def body(q_ref, k_ref, u_ref, w_ref, gb_ref, qk_ref, o_ref, s_ref, state):
        @pl.when(pl.program_id(0) == 0)
        def _():
            state[...] = jnp.zeros_like(state)
        for c in range(cb):
            rows = pl.ds(c * CH, CH)
            qv, kv, uv, wv, gbv = q_ref[rows, :], k_ref[rows, :], u_ref[rows, :], w_ref[rows, :], gb_ref[rows, :]
            os_ = []
            for h in range(NH):
                s = state[h]
                s_ref[c, h] = s
                s_new, o = _dn_inter(s, _head(qv, h), _head(kv, h), _head(uv, h), _head(wv, h), gbv[:, h:h + 1], qk_ref[c, h])
                state[h] = s_new
                os_.append(o)
            o_ref[rows, :] = jnp.concatenate(os_, axis=1)

    t512 = pl.BlockSpec((cb * CH, DDN), lambda i: (i, 0))
    return pl.pallas_call(
        body, grid=(N // cb,), name=name,
        in_specs=[t512, t512, t512, t512, pl.BlockSpec((cb * CH, HD), lambda i: (i, 0)),
                  pl.BlockSpec((cb, NH, CH, CH), lambda i: (i, 0, 0, 0))],
        out_specs=[t512, pl.BlockSpec((cb, NH, HD, HD), lambda i: (i, 0, 0, 0))],
        out_shape=[_sds((T, DDN)), _sds((N, NH, HD, HD))],
        scratch_shapes=[pltpu.VMEM((NH, HD, HD), F32)], compiler_params=_cp(1))(q, k, u, w, gb, qk)


def _lane_onehot(h):
    return (lax.broadcasted_iota(jnp.int32, (1, HD), 1) == h).astype(F32)


def dn_inter_bwd(do, q, k, u, w, gb, qk, s_all, *, name):
    T = q.shape[0]
    N = T // CH
    cb = 2 if N % 2 == 0 else 1
    G = N // cb

    def body(do_ref, q_ref, k_ref, u_ref, w_ref, gb_ref, qk_ref, s_ref, dq_ref, dk_ref, du_ref, dw_ref, dg_ref, dqk_ref, dstate):
        @pl.when(pl.program_id(0) == 0)
        def _():
            dstate[...] = jnp.zeros_like(dstate)
        for c in reversed(range(cb)):
            rows = pl.ds(c * CH, CH)
            dov, qv, kv, uv, wv, gbv = do_ref[rows, :], q_ref[rows, :], k_ref[rows, :], u_ref[rows, :], w_ref[rows, :], gb_ref[rows, :]
            dqs, dks, dus, dws = [], [], [], []
            dg = jnp.zeros((CH, HD), F32)
            for h in range(NH):
                _, vjp = jax.vjp(_dn_inter, s_ref[c, h], _head(qv, h), _head(kv, h), _head(uv, h), _head(wv, h),
                                 gbv[:, h:h + 1], qk_ref[c, h])
                ds, dq, dk, du, dw, dgc, dqk = vjp((dstate[h], _head(dov, h)))
                dstate[h] = ds
                dqs.append(dq)
                dks.append(dk)
                dus.append(du)
                dws.append(dw)
                dg = dg + dgc * _lane_onehot(h)
                dqk_ref[c, h] = dqk
            dq_ref[rows, :] = jnp.concatenate(dqs, axis=1)
            dk_ref[rows, :] = jnp.concatenate(dks, axis=1)
            du_ref[rows, :] = jnp.concatenate(dus, axis=1)
            dw_ref[rows, :] = jnp.concatenate(dws, axis=1)
            dg_ref[rows, :] = dg

    t512 = pl.BlockSpec((cb * CH, DDN), lambda i: (G - 1 - i, 0))
    t128 = pl.BlockSpec((cb * CH, HD), lambda i: (G - 1 - i, 0))
    qkb = pl.BlockSpec((cb, NH, CH, CH), lambda i: (G - 1 - i, 0, 0, 0))
    return pl.pallas_call(
        body, grid=(G,), name=name,
        in_specs=[t512, t512, t512, t512, t512, t128, qkb, pl.BlockSpec((cb, NH, HD, HD), lambda i: (G - 1 - i, 0, 0, 0))],
        out_specs=[t512, t512, t512, t512, t128, qkb],
        out_shape=[_sds((T, DDN))] * 4 + [_sds((T, HD)), _sds((N, NH, CH, CH))],
        scratch_shapes=[pltpu.VMEM((NH, HD, HD), F32)], compiler_params=_cp(1))(do, q, k, u, w, gb, qk, s_all)


def dn_intra_bwd(du, dw, dqk, dq_in, dk_in, dg_in, q, k, v, gb, grow3, *, name):
    T = q.shape[0]
    N = T // CH
    cb = 2 if N % 2 == 0 else 1

    def body(du_ref, dw_ref, dqk_ref, dqi_ref, dki_ref, dgi_ref, q_ref, k_ref, v_ref, gb_ref, gr_ref,
             dq_ref, dk_ref, dv_ref, dgb_ref, dgr_ref):
        for c in range(cb):
            rows = pl.ds(c * CH, CH)
            qv, kv, vv, gbv = q_ref[rows, :], k_ref[rows, :], v_ref[rows, :], gb_ref[rows, :]
            duv, dwv = du_ref[rows, :], dw_ref[rows, :]
            dqs, dks, dvs, dgrs = [], [], [], []
            dgb = dgi_ref[rows, :]
            for h in range(NH):
                _, vjp = jax.vjp(_dn_intra, _head(qv, h), _head(kv, h), _head(vv, h), gbv[:, NH + h:NH + h + 1], gbv[:, h:h + 1],
                                 gr_ref[c, h:h + 1, :])
                dq, dk, dv, dbeta, dgc, dgr = vjp((_head(duv, h), _head(dwv, h), dqk_ref[c, h]))
                dqs.append(dq)
                dks.append(dk)
                dvs.append(dv)
                dgrs.append(dgr)
                dgb = dgb + dgc * _lane_onehot(h) + dbeta * _lane_onehot(NH + h)
            dq_ref[rows, :] = dqi_ref[rows, :] + jnp.concatenate(dqs, axis=1)
            dk_ref[rows, :] = dki_ref[rows, :] + jnp.concatenate(dks, axis=1)
            dv_ref[rows, :] = jnp.concatenate(dvs, axis=1)
            dgb_ref[rows, :] = dgb
            dgr_ref[c] = jnp.concatenate(dgrs + [jnp.zeros((8 - NH, CH), F32)], axis=0)

    t512 = pl.BlockSpec((cb * CH, DDN), lambda i: (i, 0))
    t128 = pl.BlockSpec((cb * CH, HD), lambda i: (i, 0))
    qkb = pl.BlockSpec((cb, NH, CH, CH), lambda i: (i, 0, 0, 0))
    grb = pl.BlockSpec((cb, 8, CH), lambda i: (i, 0, 0))
    return pl.pallas_call(
        body, grid=(N // cb,), name=name,
        in_specs=[t512, t512, qkb, t512, t512, t128, t512, t512, t512, t128, grb],
        out_specs=[t512, t512, t512, t128, grb],
        out_shape=[_sds((T, DDN))] * 3 + [_sds((T, HD)), _sds((N, 8, CH))],
        compiler_params=_cp(1))(du, dw, dqk, dq_in, dk_in, dg_in, q, k, v, gb, grow3)


def _z_specs(tt):
    return [pl.BlockSpec((tt, DC), lambda i: (i, O_Z // DC)), pl.BlockSpec((tt, DC), lambda i: (i, O_Z // DC + 1))]


def mixout_fwd(x, yab, o, proj, gdn, g1, w_out, *, name):
    T = x.shape[0]
    tt = _tile(T, 256)

    def body(x_ref, yab_ref, o_ref, z0_ref, z1_ref, gdn_ref, g1_ref, w_ref, ycat_ref, mix_ref, xo_ref):
        ov = o_ref[...]
        z = jnp.concatenate([z0_ref[...], z1_ref[...]], axis=1)
        yc = [_yc(_head(ov, h), _head(z, h), gdn_ref[...]) for h in range(NH)]
        ycat = jnp.concatenate([yab_ref[...]] + yc, axis=1).astype(BF16)
        ycat_ref[...] = ycat
        mix = _dot(ycat, w_ref[...])
        mix_ref[...] = mix
        xo_ref[...] = x_ref[...] + g1_ref[...] * mix

    tile = pl.BlockSpec((tt, D), lambda i: (i, 0))
    t512 = pl.BlockSpec((tt, DDN), lambda i: (i, 0))
    return pl.pallas_call(
        body, grid=(T // tt,), name=name,
        in_specs=[tile, t512, t512] + _z_specs(tt) + [_full((1, HD)), _full((1, D)), _full((D, D))],
        out_specs=[tile, tile, tile], out_shape=[_sds((T, D), BF16), _sds((T, D)), _sds((T, D))],
        compiler_params=_cp(1))(x, yab, o, proj, proj, gdn, g1, w_out)


def mixout_bwd(dx, mix, o, proj, gdn, g1, w_out, *, name):
    T = dx.shape[0]
    tt = _tile(T, 256)

    def body(dx_ref, mix_ref, o_ref, z0_ref, z1_ref, gdn_ref, g1_ref, w_ref, dmix_ref, dyab_ref, do_ref, dz_ref, dg1_ref, dgdn_ref):
        i = pl.program_id(0)
        dxv = dx_ref[...]
        dmix = (dxv * g1_ref[...]).astype(BF16)
        dmix_ref[...] = dmix
        dycat = _dot_nt(dmix, w_ref[...])
        dyab_ref[...] = dycat[:, :2 * DC]
        ov = o_ref[...]
        z = jnp.concatenate([z0_ref[...], z1_ref[...]], axis=1)
        dos, dzs = [], []
        dgdn = jnp.zeros((1, HD), F32)
        for h in range(NH):
            _, vjp = jax.vjp(_yc, _head(ov, h), _head(z, h), gdn_ref[...])
            do, dz, dg = vjp(dycat[:, 2 * DC + h * HD:2 * DC + (h + 1) * HD])
            dos.append(do)
            dzs.append(dz)
            dgdn = dgdn + dg
        do_ref[...] = jnp.concatenate(dos, axis=1)
        dz_ref[...] = jnp.concatenate(dzs, axis=1)

        @pl.when(i == 0)
        def _():
            dg1_ref[...] = jnp.zeros_like(dg1_ref)
            dgdn_ref[...] = jnp.zeros_like(dgdn_ref)
        dg1_ref[...] += _colsum(dxv * mix_ref[...])
        dgdn_ref[...] += dgdn

    tile = pl.BlockSpec((tt, D), lambda i: (i, 0))
    t512 = pl.BlockSpec((tt, DDN), lambda i: (i, 0))
    return pl.pallas_call(
        body, grid=(T // tt,), name=name,
        in_specs=[tile, tile, t512] + _z_specs(tt) + [_full((1, HD)), _full((1, D)), _full((D, D))],
        out_specs=[tile, t512, t512, t512, _full((1, D)), _full((1, HD))],
        out_shape=[_sds((T, D), BF16), _sds((T, DDN)), _sds((T, DDN)), _sds((T, DDN)), _sds((1, D)), _sds((1, HD))],
        compiler_params=_cp(1))(dx, mix, o, proj, proj, gdn, g1, w_out)


FK = DFF // 2


def ffnout_fwd(x, gu, g2, w, *, name):
    T = x.shape[0]
    tt = _tile(T, 256)

    def body(x_ref, gate_ref, up_ref, g2_ref, w_ref, act_ref, f_ref, xo_ref):
        kk = pl.program_id(1)
        act = (_silu(gate_ref[...]) * up_ref[...]).astype(BF16)
        act_ref[...] = act
        part = _dot(act, w_ref[...])

        @pl.when(kk == 0)
        def _():
            f_ref[...] = part

        @pl.when(kk == 1)
        def _():
            f = f_ref[...] + part
            f_ref[...] = f
            xo_ref[...] = x_ref[...] + g2_ref[...] * f

    tile = pl.BlockSpec((tt, D), lambda i, kk: (i, 0))
    return pl.pallas_call(
        body, grid=(T // tt, 2), name=name,
        in_specs=[tile, pl.BlockSpec((tt, FK), lambda i, kk: (i, kk)), pl.BlockSpec((tt, FK), lambda i, kk: (i, 2 + kk)),
                  pl.BlockSpec((1, D), lambda i, kk: (0, 0)), pl.BlockSpec((FK, D), lambda i, kk: (kk, 0))],
        out_specs=[pl.BlockSpec((tt, FK), lambda i, kk: (i, kk)), tile, tile],
        out_shape=[_sds((T, DFF), BF16), _sds((T, D)), _sds((T, D))], compiler_params=_cp(2))(x, gu, gu, g2, w)


def ffnout_bwd(dx, f, gu, g2, w, *, name):
    T = dx.shape[0]
    tt = _tile(T, 256)

    def body(dx_ref, f_ref, gate_ref, up_ref, g2_ref, w_ref, df_ref, dgate_ref, dup_ref, dg2_ref):
        i, kk = pl.program_id(0), pl.program_id(1)
        dxv = dx_ref[...]
        df = (dxv * g2_ref[...]).astype(BF16)
        dact = _dot_nt(df, w_ref[...])
        gate, up = gate_ref[...], up_ref[...]
        sg = jax.nn.sigmoid(gate)
        dgate_ref[...] = (dact * up * (sg * (1.0 + gate * (1.0 - sg)))).astype(BF16)
        dup_ref[...] = (dact * (gate * sg)).astype(BF16)

        @pl.when(kk == 0)
        def _():
            df_ref[...] = df

        @pl.when((i == 0) & (kk == 0))
        def _():
            dg2_ref[...] = jnp.zeros_like(dg2_ref)

        @pl.when(kk == 0)
        def _():
            dg2_ref[...] += _colsum(dxv * f_ref[...])

    tile = pl.BlockSpec((tt, D), lambda i, kk: (i, 0))
    return pl.pallas_call(
        body, grid=(T // tt, 2), name=name,
        in_specs=[tile, tile, pl.BlockSpec((tt, FK), lambda i, kk: (i, kk)), pl.BlockSpec((tt, FK), lambda i, kk: (i, 2 + kk)),
                  pl.BlockSpec((1, D), lambda i, kk: (0, 0)), pl.BlockSpec((FK, D), lambda i, kk: (kk, 0))],
        out_specs=[tile, pl.BlockSpec((tt, FK), lambda i, kk: (i, kk)), pl.BlockSpec((tt, FK), lambda i, kk: (i, kk)),
                   pl.BlockSpec((1, D), lambda i, kk: (0, 0))],
        out_shape=[_sds((T, D), BF16), _sds((T, DFF), BF16), _sds((T, DFF), BF16), _sds((1, D))],
        compiler_params=_cp(2))(dx, f, gu, gu, g2, w)


def loss_head(x, target, gf, *, name):
    T = x.shape[0]
    tt = _tile(T, 256)

    def body(x_ref, t_ref, g_ref, loss_ref, dx_ref, dg_ref):
        i = pl.program_id(0)
        tv = t_ref[...]
        y, vjp = jax.vjp(_rms, x_ref[...], g_ref[...])
        err = y - tv
        dx, dg = vjp(err * (1.0 / D))
        dx_ref[...] = dx

        @pl.when(i == 0)
        def _():
            loss_ref[...] = jnp.zeros_like(loss_ref)
            dg_ref[...] = jnp.zeros_like(dg_ref)
        loss_ref[...] += 0.5 * jnp.sum(jnp.mean(err * err, axis=-1, keepdims=True), axis=0, keepdims=True)
        dg_ref[...] += dg

    tile = pl.BlockSpec((tt, D), lambda i: (i, 0))
    return pl.pallas_call(
        body, grid=(T // tt,), name=name, in_specs=[tile, tile, _full((1, D))],
        out_specs=[_full((1, HD)), tile, _full((1, D))], out_shape=[_sds((1, HD)), _sds((T, D)), _sds((1, D))],
        compiler_params=_cp(1))(x, target, gf)


def _pad_rows(a, rows):
    return jnp.pad(a, ((0, rows - a.shape[0]), (0, 0)))


def _row128(v):
    return jnp.pad(v, (0, HD - v.shape[0]))[None, :]


def _grow3(grow):
    return grow.reshape(8, -1, CH).transpose(1, 0, 2)


def _grow2(grow3):
    return grow3.transpose(1, 0, 2).reshape(8, -1)


def layer_params(l, mod, p):
    m = mod[l].reshape(6, 1, D)
    return dict(
        sh1=m[0], sc1=m[1], g1=m[2], sh2=m[3], sc2=m[4], g2=m[5],
        gn1=p["norm_mix_g"][l][None, :], gn2=p["norm_ffn_g"][l][None, :],
        wa=_pad_rows(p["conv_a_w"][l], 8), wb=_pad_rows(p["conf_dw_w"][l], 32), wc=_pad_rows(p["dn_conv_w"][l], 8),
        bb=p["conf_dw_b"][l][None, :], ln_g=p["conf_ln_g"][l][None, :], ln_b=p["conf_ln_b"][l][None, :],
        alog=_row128(p["dn_a_log"][l]), dt=_row128(p["dn_dt_bias"][l]), gdn=p["dn_norm_g"][l][None, :])


def layer_fwd(l, x, lp, w_in, w_out, w_ffn_in, w_ffn_out):
    s = {"x0": x}
    s["h1"], s["proj"] = nm_fwd(x, lp["gn1"], lp["sc1"], lp["sh1"], w_in, tn=NP // 3, name=f"proj_fwd_{l}")
    yab, s["q"], s["k"], s["v"], s["gb"], grow = mix_fwd(
        s["proj"], lp["wa"], lp["wb"], lp["bb"], lp["ln_g"], lp["ln_b"], lp["wc"], lp["alog"], lp["dt"], name=f"mix_fwd_{l}")
    s["grow3"] = _grow3(grow)
    s["u"], s["w"], s["qk"] = dn_intra_fwd(s["q"], s["k"], s["v"], s["gb"], s["grow3"], name=f"dn_intra_fwd_{l}")
    s["o"], s["s_all"] = dn_inter_fwd(s["q"], s["k"], s["u"], s["w"], s["gb"], s["qk"], name=f"dn_inter_fwd_{l}")
    s["ycat"], s["mix"], s["x1"] = mixout_fwd(x, yab, s["o"], s["proj"], lp["gdn"], lp["g1"], w_out, name=f"mixout_fwd_{l}")
    s["h2"], s["gu"] = nm_fwd(s["x1"], lp["gn2"], lp["sc2"], lp["sh2"], w_ffn_in, tn=2 * DFF // 4, name=f"ffnin_fwd_{l}")
    s["act"], s["f"], x2 = ffnout_fwd(s["x1"], s["gu"], lp["g2"], w_ffn_out, name=f"ffnout_fwd_{l}")
    return x2, s


def layer_bwd(l, dx2, s, lp, w_in, w_out, w_ffn_in, w_ffn_out):
    g = {}
    df, dgate, dup, dg2 = ffnout_bwd(dx2, s["f"], s["gu"], lp["g2"], w_ffn_out, name=f"ffnout_bwd_{l}")
    g["w_ffn_out"] = mm_tn(s["act"], df, tn=D // 2, name=f"ffnout_dw_{l}")
    dx1, g["norm_ffn_g"], dsc2, dsh2, dgu = nm_bwd([dgate, dup], w_ffn_in, s["x1"], lp["gn2"], lp["sc2"], lp["sh2"], dx2,
                                                   name=f"ffnin_bwd_{l}")
    g["w_ffn_in"] = mm_tn(s["h2"], dgu, tn=2 * DFF // 4, name=f"ffnin_dw_{l}")
    dmix, dyab, do, dz, dg1, g["dn_norm_g"] = mixout_bwd(dx1, s["mix"], s["o"], s["proj"], lp["gdn"], lp["g1"], w_out,
                                                         name=f"mixout_bwd_{l}")
    g["w_out"] = mm_tn(s["ycat"], dmix, tn=D // 2, name=f"mixout_dw_{l}")
    dq_i, dk_i, du, dw, dg_i, dqk = dn_inter_bwd(do, s["q"], s["k"], s["u"], s["w"], s["gb"], s["qk"], s["s_all"],
                                                 name=f"dn_inter_bwd_{l}")
    dq, dk, dv, dgb, dgrow3 = dn_intra_bwd(du, dw, dqk, dq_i, dk_i, dg_i, s["q"], s["k"], s["v"], s["gb"], s["grow3"],
                                           name=f"dn_intra_bwd_{l}")
    dab, dconv, dblk, g["conf_dw_b"], g["conf_ln_g"], g["conf_ln_b"], dal, ddt = mix_bwd_point(
        s["proj"], dyab, dq, dk, dv, dgb, _grow2(dgrow3), lp["wa"], lp["wb"], lp["bb"], lp["ln_g"], lp["ln_b"], lp["wc"],
        lp["alog"], lp["dt"], name=f"mix_bwd_point_{l}")
    dpb, dwa, dwb, dwc = mix_bwd_conv(s["proj"], dconv, lp["wa"], lp["wb"], lp["wc"], name=f"mix_bwd_conv_{l}")
    dx0, g["norm_mix_g"], dsc1, dsh1, dproj = nm_bwd([dab, dpb, dz, dblk], w_in, s["x0"], lp["gn1"], lp["sc1"], lp["sh1"], dx1,
                                                     name=f"proj_bwd_{l}")
    g["w_in"] = mm_tn(s["h1"], dproj, tn=NP // 3, name=f"proj_dw_{l}")
    g["conv_a_w"], g["conf_dw_w"], g["dn_conv_w"] = dwa[:KA], dwb[:KB], dwc[:KC]
    g["dn_a_log"], g["dn_dt_bias"] = dal[0, :NH], ddt[0, :NH]
    g["mod"] = jnp.concatenate([dsh1, dsc1, dg1, dsh2, dsc2, dg2], axis=1)
    return dx0, g


EW_BLOCK_BYTES = 1 << 20


def _row_tile(R, C):
    best = None
    for rt in range(8, R + 1, 8):
        if R % rt == 0 and rt * C * 4 <= EW_BLOCK_BYTES:
            best = rt
    return best if best is not None else R


def ew_call(fn, ins, n_out, *, name):
    B, R, C = ins[0].shape
    rt = _row_tile(R, C)
    n = len(ins)

    def body(*refs):
        outs = fn(*[r[...] for r in refs[:n]])
        for r, o in zip(refs[n:], outs):
            r[...] = o

    spec = pl.BlockSpec((1, rt, C), lambda b, r: (b, r, 0))
    return pl.pallas_call(body, grid=(B, R // rt), name=name, in_specs=[spec] * n, out_specs=[spec] * n_out,
                          out_shape=[_sds((B, R, C))] * n_out, compiler_params=_cp(2))(*ins)


def _adamw_math(w, g, m, v):
    m = ADAM_B1 * m + (1.0 - ADAM_B1) * g
    v = ADAM_B2 * v + (1.0 - ADAM_B2) * jnp.square(g)
    m_hat = m / (1.0 - ADAM_B1 ** ADAM_STEP)
    v_hat = v / (1.0 - ADAM_B2 ** ADAM_STEP)
    return -ADAM_LR * (m_hat / (jnp.sqrt(v_hat) + ADAM_EPS) + ADAM_WD * w), m, v


def adamw(w, g, m, v, *, name):
    shape = w.shape
    r3 = lambda a: a.reshape((-1,) + shape[-2:])
    return [o.reshape(shape) for o in ew_call(_adamw_math, [r3(w), r3(g), r3(m), r3(v)], 3, name=name)]


def sum_slots(a, *, name):
    S, B, R, C = a.shape
    rt = _row_tile(R, C)

    def body(*refs):
        acc = refs[0][0, 0]
        for r in refs[1:S]:
            acc = acc + r[0, 0]
        refs[S][0] = acc

    def spec(s):
        return pl.BlockSpec((1, 1, rt, C), lambda b, r: (s, b, r, 0))
    return pl.pallas_call(body, grid=(B, R // rt), name=name, in_specs=[spec(s) for s in range(S)],
                          out_specs=pl.BlockSpec((1, rt, C), lambda b, r: (b, r, 0)), out_shape=_sds((B, R, C)),
                          compiler_params=_cp(2))(*([a] * S))


ADA_SH = 6 * D // 4
ADA_TN = 512


def mod_fwd(c_all, w_ada, b_my, *, name):
    L = w_ada.shape[0]

    def body(c_ref, w_ref, b_ref, o_ref):
        o_ref[0] = _dot(_silu(c_ref[...]).astype(BF16), w_ref[0].astype(BF16)) + b_ref[0]

    return pl.pallas_call(
        body, grid=(L, ADA_SH // ADA_TN), name=name,
        in_specs=[pl.BlockSpec((8, D), lambda l, j: (0, 0)), pl.BlockSpec((1, D, ADA_TN), lambda l, j: (l, 0, j)),
                  pl.BlockSpec((1, 1, ADA_TN), lambda l, j: (l, 0, j))],
        out_specs=pl.BlockSpec((1, 8, ADA_TN), lambda l, j: (l, 0, j)), out_shape=_sds((L, 8, ADA_SH)),
        compiler_params=_cp(2))(c_all, w_ada, b_my)


def wada_grad(c_all, dmod, *, name):
    L = dmod.shape[0]

    def body(c_ref, d_ref, o_ref):
        o_ref[0] = _dot_tn(_silu(c_ref[...]), d_ref[0], HI)

    return pl.pallas_call(
        body, grid=(L, ADA_SH // ADA_TN), name=name,
        in_specs=[pl.BlockSpec((8, D), lambda l, j: (0, 0)), pl.BlockSpec((1, 8, ADA_TN), lambda l, j: (l, 0, j))],
        out_specs=pl.BlockSpec((1, D, ADA_TN), lambda l, j: (l, 0, j)), out_shape=_sds((L, D, ADA_SH)),
        compiler_params=_cp(2))(c_all, dmod)


def _place():
    return lax.axis_index("x"), lax.axis_index("y"), lax.axis_index("c")


def _other_chips(x, y):
    return [(1 - x, y), (x, 1 - y), (1 - x, 1 - y)]


def allgather8(blocks, *, space, name):
    n = len(blocks)

    def body(*refs):
        ins, outs = refs[:n], refs[n:2 * n]
        send_sems, recv_sems, local_sems = refs[2 * n:]
        x, y, c = _place()
        me, sibling = (x, y, c), (x, y, 1 - c)
        chips = _other_chips(x, y)

        def slot(p):
            return 4 * p[0] + 2 * p[1] + p[2]

        def copy(a, k, block, to, src=None):
            dst = outs[a].at[slot(block)]
            return pltpu.make_async_remote_copy(src_ref=dst if src is None else src, dst_ref=dst, send_sem=send_sems.at[a, k],
                                                recv_sem=recv_sems.at[a, k], device_id=to, device_id_type=MESH)

        mine = [pltpu.make_async_copy(ins[a], outs[a].at[slot(me)], local_sems.at[a]) for a in range(n)]
        for cp in mine:
            cp.start()
        first = []
        for a in range(n):
            first.append(copy(a, 0, me, sibling, src=ins[a]))
            first += [copy(a, 1 + j, me, (*chip, c), src=ins[a]) for j, chip in enumerate(chips)]
        for cp in first:
            cp.start()
        passed = []
        for j, chip in enumerate(chips):
            for a in range(n):
                copy(a, 1 + j, (*chip, c), me).wait_recv()
                cp = copy(a, 4 + j, (*chip, c), sibling)
                cp.start()
                passed.append(cp)
        for a in range(n):
            copy(a, 0, sibling, me).wait_recv()
            for j, chip in enumerate(chips):
                copy(a, 4 + j, (*chip, 1 - c), me).wait_recv()
        for cp in first + passed:
            cp.wait_send()
        for cp in mine:
            cp.wait()

    spec = pl.BlockSpec(memory_space=space)
    return pl.pallas_call(
        body, name=name, in_specs=[spec] * n, out_specs=[spec] * n,
        out_shape=[_sds((8,) + b.shape, b.dtype) for b in blocks],
        scratch_shapes=[pltpu.SemaphoreType.DMA((n, 7)), pltpu.SemaphoreType.DMA((n, 7)), pltpu.SemaphoreType.DMA((n,))],
        compiler_params=pltpu.CompilerParams(vmem_limit_bytes=VMEM_LIMIT))(*blocks)


def sibling_swap(blocks, *, name):
    n = len(blocks)

    def body(*refs):
        ins, outs = refs[:n], refs[n:2 * n]
        send_sems, recv_sems = refs[2 * n:]
        x, y, c = _place()
        cps = [pltpu.make_async_remote_copy(src_ref=ins[a], dst_ref=outs[a], send_sem=send_sems.at[a], recv_sem=recv_sems.at[a],
                                            device_id=(x, y, 1 - c), device_id_type=MESH) for a in range(n)]
        for cp in cps:
            cp.start()
        for cp in cps:
            cp.wait()

    spec = pl.BlockSpec(memory_space=pl.ANY)
    return pl.pallas_call(
        body, name=name, in_specs=[spec] * n, out_specs=[spec] * n, out_shape=[_sds(b.shape, b.dtype) for b in blocks],
        scratch_shapes=[pltpu.SemaphoreType.DMA((n,)), pltpu.SemaphoreType.DMA((n,))])(*blocks)


def chip_exchange(blocks, *, name):
    n = len(blocks)

    def body(*refs):
        ins, outs = refs[:n], refs[n:2 * n]
        send_sems, recv_sems, local_sems = refs[2 * n:]
        x, y, c = _place()
        me = 2 * x + y
        chips = _other_chips(x, y)
        mine = [pltpu.make_async_copy(ins[a].at[me], outs[a].at[me], local_sems.at[a]) for a in range(n)]
        for cp in mine:
            cp.start()
        cps = []
        for a in range(n):
            for j, chip in enumerate(chips):
                cps.append(pltpu.make_async_remote_copy(
                    src_ref=ins[a].at[2 * chip[0] + chip[1]], dst_ref=outs[a].at[me], send_sem=send_sems.at[a, j],
                    recv_sem=recv_sems.at[a, j], device_id=(*chip, c), device_id_type=MESH))
        for cp in cps:
            cp.start()
        for a in range(n):
            for j, chip in enumerate(chips):
                s = 2 * chip[0] + chip[1]
                pltpu.make_async_remote_copy(src_ref=ins[a].at[s], dst_ref=outs[a].at[s], send_sem=send_sems.at[a, j],
                                             recv_sem=recv_sems.at[a, j], device_id=(*chip, c), device_id_type=MESH).wait_recv()
        for cp in cps:
            cp.wait_send()
        for cp in mine:
            cp.wait()

    spec = pl.BlockSpec(memory_space=pl.ANY)
    return pl.pallas_call(
        body, name=name, in_specs=[spec] * n, out_specs=[spec] * n, out_shape=[_sds(b.shape, b.dtype) for b in blocks],
        scratch_shapes=[pltpu.SemaphoreType.DMA((n, 3)), pltpu.SemaphoreType.DMA((n, 3)), pltpu.SemaphoreType.DMA((n,))])(*blocks)


def halves_join(halves, *, name):
    n = len(halves)

    def body(*refs):
        ins, outs = refs[:n], refs[n:2 * n]
        send_sems, recv_sems, local_sems = refs[2 * n:]
        x, y, c = _place()
        cps, mine = [], []
        for a in range(n):
            rh = halves[a].shape[1]
            here = outs[a].at[:, pl.ds(pl.multiple_of(c * rh, 8), rh), :]
            mine.append(pltpu.make_async_copy(ins[a], here, local_sems.at[a]))
            cps.append(pltpu.make_async_remote_copy(src_ref=ins[a], dst_ref=here, send_sem=send_sems.at[a], recv_sem=recv_sems.at[a],
                                                    device_id=(x, y, 1 - c), device_id_type=MESH))
        for cp in mine + cps:
            cp.start()
        for a in range(n):
            rh = halves[a].shape[1]
            there = outs[a].at[:, pl.ds(pl.multiple_of((1 - c) * rh, 8), rh), :]
            pltpu.make_async_remote_copy(src_ref=ins[a], dst_ref=there, send_sem=send_sems.at[a], recv_sem=recv_sems.at[a],
                                         device_id=(x, y, 1 - c), device_id_type=MESH).wait_recv()
        for cp in cps:
            cp.wait_send()
        for cp in mine:
            cp.wait()

    spec = pl.BlockSpec(memory_space=pl.ANY)
    return pl.pallas_call(
        body, name=name, in_specs=[spec] * n, out_specs=[spec] * n,
        out_shape=[_sds((h.shape[0], 2 * h.shape[1], h.shape[2]), h.dtype) for h in halves],
        scratch_shapes=[pltpu.SemaphoreType.DMA((n,)), pltpu.SemaphoreType.DMA((n,)), pltpu.SemaphoreType.DMA((n,))])(*halves)


BIG = ("w_in", "w_out", "w_ffn_in", "w_ffn_out")
COL_SHARDED = {"w_in": True, "w_out": False, "w_ffn_in": True, "w_ffn_out": False}
SMALL = ("norm_mix_g", "norm_ffn_g", "conv_a_w", "conf_dw_w", "conf_dw_b", "conf_ln_g", "conf_ln_b", "dn_conv_w",
         "dn_a_log", "dn_dt_bias", "dn_norm_g")
SMALL_SHARDED = ("conv_a_w", "conf_dw_w", "dn_conv_w")


def _half_rows(a, c):
    rh = a.shape[1] // 2
    return lax.dynamic_slice_in_dim(a, c * rh, rh, axis=1)


def _assemble(name, g):
    _, L, rh, C = g.shape
    g = g.reshape(4, 2, L, rh, C)
    if COL_SHARDED[name]:
        return g.transpose(2, 1, 3, 0, 4).reshape(L, 2 * rh, 4 * C)
    return g.transpose(2, 0, 1, 3, 4).reshape(L, 8 * rh, C)


def _split_for_reduce(name, g, c):
    L, R, C = g.shape
    if COL_SHARDED[name]:
        t = g.reshape(L, 2, R // 2, 4, C // 4).transpose(1, 3, 0, 2, 4)
    else:
        t = g.reshape(L, 4, 2, R // 8, C).transpose(2, 1, 0, 3, 4)
    mine = lax.dynamic_index_in_dim(t, c, 0, keepdims=False)
    other = lax.dynamic_index_in_dim(t, 1 - c, 0, keepdims=False)
    return mine, other


def _pack(parts):
    flat = jnp.concatenate([p.reshape(-1) for p in parts])
    n = flat.shape[0]
    rows = -(-n // (8 * HD)) * 8
    return jnp.pad(flat, (0, rows * HD - n)).reshape(rows, HD)


def _unpack(buf, shapes):
    flat = buf.reshape(-1)
    out, off = [], 0
    for s in shapes:
        n = 1
        for d in s:
            n *= d
        out.append(flat[off:off + n].reshape(s))
        off += n
    return out


def kernel(x, c, w_ada, b_ada, norm_mix_g, norm_ffn_g, w_in, conv_a_w, conf_dw_w, conf_dw_b, conf_ln_g, conf_ln_b, dn_conv_w, dn_a_log, dn_dt_bias, dn_norm_g, w_out, w_ffn_in, w_ffn_out, final_norm_g, loss_target, m_w_ada, m_b_ada, m_norm_mix_g, m_norm_ffn_g, m_w_in, m_conv_a_w, m_conf_dw_w, m_conf_dw_b, m_conf_ln_g, m_conf_ln_b, m_dn_conv_w, m_dn_a_log, m_dn_dt_bias, m_dn_norm_g, m_w_out, m_w_ffn_in, m_w_ffn_out, m_final_norm_g, v_w_ada, v_b_ada, v_norm_mix_g, v_norm_ffn_g, v_w_in, v_conv_a_w, v_conf_dw_w, v_conf_dw_b, v_conf_ln_g, v_conf_ln_b, v_dn_conv_w, v_dn_a_log, v_dn_dt_bias, v_dn_norm_g, v_w_out, v_w_ffn_in, v_w_ffn_out, v_final_norm_g):
    W = dict(w_ada=w_ada, b_ada=b_ada, norm_mix_g=norm_mix_g, norm_ffn_g=norm_ffn_g, w_in=w_in, conv_a_w=conv_a_w,
             conf_dw_w=conf_dw_w, conf_dw_b=conf_dw_b, conf_ln_g=conf_ln_g, conf_ln_b=conf_ln_b, dn_conv_w=dn_conv_w,
             dn_a_log=dn_a_log, dn_dt_bias=dn_dt_bias, dn_norm_g=dn_norm_g, w_out=w_out, w_ffn_in=w_ffn_in,
             w_ffn_out=w_ffn_out, final_norm_g=final_norm_g)
    M = dict(w_ada=m_w_ada, b_ada=m_b_ada, norm_mix_g=m_norm_mix_g, norm_ffn_g=m_norm_ffn_g, w_in=m_w_in, conv_a_w=m_conv_a_w,
             conf_dw_w=m_conf_dw_w, conf_dw_b=m_conf_dw_b, conf_ln_g=m_conf_ln_g, conf_ln_b=m_conf_ln_b, dn_conv_w=m_dn_conv_w,
             dn_a_log=m_dn_a_log, dn_dt_bias=m_dn_dt_bias, dn_norm_g=m_dn_norm_g, w_out=m_w_out, w_ffn_in=m_w_ffn_in,
             w_ffn_out=m_w_ffn_out, final_norm_g=m_final_norm_g)
    V = dict(w_ada=v_w_ada, b_ada=v_b_ada, norm_mix_g=v_norm_mix_g, norm_ffn_g=v_norm_ffn_g, w_in=v_w_in, conv_a_w=v_conv_a_w,
             conf_dw_w=v_conf_dw_w, conf_dw_b=v_conf_dw_b, conf_ln_g=v_conf_ln_g, conf_ln_b=v_conf_ln_b, dn_conv_w=v_dn_conv_w,
             dn_a_log=v_dn_a_log, dn_dt_bias=v_dn_dt_bias, dn_norm_g=v_dn_norm_g, w_out=v_w_out, w_ffn_in=v_w_ffn_in,
             w_ffn_out=v_w_ffn_out, final_norm_g=v_final_norm_g)
    L = w_ada.shape[0]
    ax, ay, ac = _place()
    chip = 2 * ax + ay
    dev = 4 * ax + 2 * ay + ac

    c_all = allgather8([jnp.pad(c, ((0, 7), (0, 0)))], space=pltpu.VMEM, name="gather_c")[0][:, 0, :]
    b_my = lax.dynamic_slice_in_dim(b_ada, chip * ADA_SH, ADA_SH, axis=1)[:, None, :]
    mod_sh = mod_fwd(c_all, w_ada, b_my, name="mod_fwd")
    lh = L // 2
    mod_g = allgather8([lax.dynamic_slice_in_dim(mod_sh, ac * lh, lh, axis=0).reshape(lh * 8, ADA_SH)], space=pltpu.VMEM,
                       name="gather_mod")[0]
    mod_all = mod_g.reshape(4, 2, lh, 8, ADA_SH).transpose(1, 2, 3, 0, 4).reshape(L, 8, 6 * D)
    mod = lax.dynamic_index_in_dim(mod_all, dev, 1, keepdims=False)

    gathered = allgather8([_half_rows(W[n], ac).astype(BF16) for n in BIG], space=pl.ANY, name="gather_w")
    full = {n: _assemble(n, g) for n, g in zip(BIG, gathered)}
    full["w_in"] = jnp.pad(full["w_in"], ((0, 0), (0, 0), (0, NP - IN_COLS)))

    p_full = dict(W)
    sm = allgather8([_pack([W[n] for n in SMALL_SHARDED])], space=pltpu.VMEM, name="gather_convw")[0]
    per_chip = [_unpack(sm[4 * (s // 2) + 2 * (s % 2)], [W[n].shape for n in SMALL_SHARDED]) for s in range(4)]
    for i, n in enumerate(SMALL_SHARDED):
        p_full[n] = jnp.concatenate([per_chip[s][i] for s in range(4)], axis=-1)

    xs = x[0]
    saves, lps = [], []
    for l in range(L):
        lp = layer_params(l, mod, p_full)
        xs, s = layer_fwd(l, xs, lp, full["w_in"][l], full["w_out"][l], full["w_ffn_in"][l], full["w_ffn_out"][l])
        saves.append(s)
        lps.append(lp)
    loss_p, dx, dgf = loss_head(xs, loss_target[0], final_norm_g[None, :], name="loss_head")
    grads = [None] * L
    for l in reversed(range(L)):
        dx, grads[l] = layer_bwd(l, dx, saves[l], lps[l], full["w_in"][l], full["w_out"][l], full["w_ffn_in"][l], full["w_ffn_out"][l])
    loss = lax.psum(loss_p[0, 0], ("x", "y", "c"))
    grad_x = dx[None]

    small_shapes = [(L,) + p_full[n].shape[1:] for n in SMALL]
    parts = [jnp.stack([grads[l][n].reshape(sh[1:]) for l in range(L)]) for n, sh in zip(SMALL, small_shapes)]
    parts += [dgf.reshape(D), jnp.concatenate([grads[l]["mod"] for l in range(L)], axis=0)]
    small_shapes += [(D,), (L, 6 * D)]
    packed = _pack(parts)
    gathered_small = allgather8([packed], space=pltpu.VMEM, name="gather_small")[0]
    summed = sum_slots(gathered_small[:, None], name="sum_small")[0]
    g_small = dict(zip(SMALL + ("final_norm_g", "b_ada"), _unpack(summed, small_shapes)))
    for n in SMALL_SHARDED:
        sw = W[n].shape[-1]
        g_small[n] = lax.dynamic_slice_in_dim(g_small[n], chip * sw, sw, axis=g_small[n].ndim - 1)
    dmod_all = jnp.stack([_unpack(gathered_small[d], small_shapes)[-1] for d in range(8)], axis=1)
    dmod_my = lax.dynamic_slice_in_dim(dmod_all, chip * ADA_SH, ADA_SH, axis=2)
    g_w_ada = wada_grad(c_all, dmod_my, name="wada_grad")

    mine, other = zip(*[_split_for_reduce(n, jnp.stack([grads[l][n] for l in range(L)])[..., :W[n].shape[-1] * (4 if COL_SHARDED[n] else 1)], ac)
                        for n in BIG])
    from_sib = sibling_swap(list(other), name="reduce_sibling")
    chip_sum = [ew_call(lambda a, b: (a + b,), [a.reshape((-1,) + a.shape[-2:]), b.reshape((-1,) + b.shape[-2:])], 1,
                        name=f"reduce_add2_{n}")[0].reshape(a.shape) for n, a, b in zip(BIG, mine, from_sib)]
    from_chips = chip_exchange(chip_sum, name="reduce_chips")
    halves = [sum_slots(a, name=f"reduce_add4_{n}") for n, a in zip(BIG, from_chips)]
    g_big = dict(zip(BIG, halves_join(halves, name="reduce_join")))

    out_g, out_d, out_m, out_v = {}, {}, {}, {}
    for n in ("w_ada",) + BIG:
        g = g_w_ada if n == "w_ada" else g_big[n]
        out_g[n] = g
        out_d[n], out_m[n], out_v[n] = adamw(W[n], g, M[n], V[n], name=f"adamw_{n}")
    names_small = SMALL + ("final_norm_g", "b_ada")
    pk = lambda d: _pack([d[n] for n in names_small])[None]
    d_s, m_s, v_s = ew_call(_adamw_math, [pk(W), pk(g_small), pk(M), pk(V)], 3, name="adamw_small")
    shapes_s = [W[n].shape for n in names_small]
    for d, o in ((out_d, d_s), (out_m, m_s), (out_v, v_s)):
        d.update(zip(names_small, _unpack(o[0], shapes_s)))
    for n in names_small:
        out_g[n] = g_small[n].reshape(W[n].shape)

    order = ("w_ada", "b_ada", "norm_mix_g", "norm_ffn_g", "w_in", "conv_a_w", "conf_dw_w", "conf_dw_b", "conf_ln_g", "conf_ln_b",
             "dn_conv_w", "dn_a_log", "dn_dt_bias", "dn_norm_g", "w_out", "w_ffn_in", "w_ffn_out", "final_norm_g")
    return (loss, grad_x, *[out_g[n] for n in order], *[out_d[n] for n in order], *[out_m[n] for n in order],
            *[out_v[n] for n in order])
```

```python
import functools

import jax
import jax.numpy as jnp
from jax import lax
from jax.experimental import pallas as pl
from jax.experimental.pallas import tpu as pltpu

F32 = jnp.float32
BF16 = jnp.bfloat16
HI = lax.Precision.HIGHEST
MESH = pl.DeviceIdType.MESH

D = 1024
DEPTH = 4
DC = 256
DDN = 512
NH = 4
HD = 128
CH = 64
DFF = 2816
IN_COLS = 3336
NP = 3456
KA, KB, KC = 3, 31, 4
HALO = 32
EPS = 1e-6
O_AB, O_AC, O_AV, O_BA, O_BG, O_Q, O_K, O_V, O_Z, O_GB = 0, 256, 512, 768, 1024, 1280, 1792, 2304, 2816, 3328

ADAM_LR, ADAM_B1, ADAM_B2, ADAM_EPS, ADAM_WD, ADAM_STEP = 0.001, 0.9, 0.999, 1e-08, 0.01, 10

VMEM_LIMIT = 56 * 1024 * 1024


def _cp(n_grid):
    return pltpu.CompilerParams(dimension_semantics=("arbitrary",) * n_grid, vmem_limit_bytes=VMEM_LIMIT)


def _sds(shape, dtype=F32):
    return jax.ShapeDtypeStruct(tuple(shape), dtype)


def _dot(a, b, prec=None):
    return jnp.dot(a, b, preferred_element_type=F32, precision=prec)


def _dot_nt(a, b, prec=None):
    return lax.dot_general(a, b, (((1,), (1,)), ((), ())), preferred_element_type=F32, precision=prec)


def _dot_tn(a, b, prec=None):
    return lax.dot_general(a, b, (((0,), (0,)), ((), ())), preferred_element_type=F32, precision=prec)


def _split_bf16(a):
    hi = a.astype(BF16)
    return hi, (a - hi.astype(F32)).astype(BF16)


_DIMS = {"nn": (((1,), (0,)), ((), ())), "nt": (((1,), (1,)), ((), ())), "tn": (((0,), (0,)), ((), ()))}


def _mm3_raw(a, b, kind):
    ah, al = _split_bf16(a)
    bh, bl = _split_bf16(b)
    d = _DIMS[kind]
    dg = lambda u, v: lax.dot_general(u, v, d, preferred_element_type=F32)
    return dg(ah, bh) + (dg(ah, bl) + dg(al, bh))


@jax.custom_vjp
def _mm_nn(a, b):
    return _mm3_raw(a, b, "nn")


@jax.custom_vjp
def _mm_nt(a, b):
    return _mm3_raw(a, b, "nt")


@jax.custom_vjp
def _mm_tn(a, b):
    return _mm3_raw(a, b, "tn")


_mm_nn.defvjp(lambda a, b: (_mm_nn(a, b), (a, b)), lambda r, g: (_mm_nt(g, r[1]), _mm_tn(r[0], g)))
_mm_nt.defvjp(lambda a, b: (_mm_nt(a, b), (a, b)), lambda r, g: (_mm_nn(g, r[1]), _mm_tn(g, r[0])))
_mm_tn.defvjp(lambda a, b: (_mm_tn(a, b), (a, b)), lambda r, g: (_mm_nt(r[1], g), _mm_nn(r[0], g)))


@jax.custom_vjp
def _inv_given(x, p):
    return p


_inv_given.defvjp(lambda x, p: (p, p), lambda p, g: (_mm_tn(p, _mm_nt(g, p)), jnp.zeros_like(p)))


def _silu(x):
    return x * jax.nn.sigmoid(x)


def _colsum(x):
    return jnp.sum(x, axis=0, keepdims=True)


TILE_CAP = 512


def _w3(w):
    return w if w.ndim == 3 else w[None]


def _wspec(l, block, index):
    return pl.BlockSpec((None,) + block, lambda *g: (l,) + index(*g))


def _tile(T, want):
    t = min(T, want, TILE_CAP)
    assert T % t == 0
    return t


def _normmod(x, gn, sc, sh):
    r = lax.rsqrt(jnp.mean(x * x, axis=-1, keepdims=True) + EPS)
    return ((x * r) * gn) * (1.0 + sc) + sh


def _rms(x, g):
    return (x * lax.rsqrt(jnp.mean(x * x, axis=-1, keepdims=True) + EPS)) * g


def _mix_b_post(u, ln_g, ln_b):
    mu = jnp.mean(u, axis=-1, keepdims=True)
    var = jnp.mean(jnp.square(u - mu), axis=-1, keepdims=True)
    return _silu(((u - mu) * lax.rsqrt(var + 1e-5)) * ln_g + ln_b)


def _softplus(z):
    return jnp.where(z > 0, z, 0.0) + jnp.log(1.0 + jnp.exp(-jnp.where(z > 0, z, -z)))


def _chunk_tril(tt):
    r = lax.broadcasted_iota(jnp.int32, (tt, tt), 0)
    c = lax.broadcasted_iota(jnp.int32, (tt, tt), 1)
    return ((r // CH == c // CH) & (c <= r)).astype(F32)


def _eye8():
    return (lax.broadcasted_iota(jnp.int32, (8, HD), 0) == lax.broadcasted_iota(jnp.int32, (8, HD), 1)).astype(F32)


def _dn_post(pre_q, pre_k, pre_v, blk, alog_row, dt_row):
    q = [s * lax.rsqrt(jnp.sum(s * s, -1, keepdims=True) + EPS) * (HD ** -0.5) for s in map(_silu, pre_q)]
    k = [s * lax.rsqrt(jnp.sum(s * s, -1, keepdims=True) + EPS) for s in map(_silu, pre_k)]
    v = [_silu(p) for p in pre_v]
    lane = lax.broadcasted_iota(jnp.int32, (1, HD), 1)
    g = -jnp.exp(alog_row) * _softplus(blk + dt_row)
    beta = jax.nn.sigmoid(blk)
    gc = _dot(_chunk_tril(blk.shape[0]), jnp.where(lane < NH, g, 0.0), HI)
    gb = jnp.where(lane < NH, gc, jnp.where(lane < 2 * NH, beta, 0.0))
    grow = _dot_nt(_eye8(), gc, HI)
    return q, k, v, gb, grow


def _dn_intra(q, k, v, beta, gcol, grow, p_known=None):
    r = lax.broadcasted_iota(jnp.int32, (CH, CH), 0)
    c = lax.broadcasted_iota(jnp.int32, (CH, CH), 1)
    causal, strict = c <= r, c < r
    decay = jnp.where(causal, jnp.exp(jnp.where(causal, gcol - grow, 0.0)), 0.0)
    kb = k * beta
    x = -jnp.where(strict, _mm_nt(kb, k) * decay, 0.0)
    if p_known is None:
        p = (r == c).astype(F32) + x
        y = x
        for _ in range(5):
            y = _mm_nn(y, y)
            p = p + _mm_nn(p, y)
    else:
        p = _inv_given(x, p_known)
    u = _mm_nn(p, v * beta)
    w = _mm_nn(p, kb * jnp.exp(gcol))
    qk = jnp.where(causal, _mm_nt(q, k) * decay, 0.0)
    return u, w, qk, p


def _dn_inter(s, q, k, u, w, gcol, qk):
    last = (lax.broadcasted_iota(jnp.int32, (CH, 1), 0) == CH - 1).astype(F32)
    g_last = jnp.sum(gcol * last, axis=0, keepdims=True)
    v_new = u - _mm_nn(w, s)
    o = _mm_nn(q * jnp.exp(gcol), s) + _mm_nn(qk, v_new)
    s_new = s * jnp.exp(g_last) + _mm_tn(k * jnp.exp(g_last - gcol), v_new)
    return s_new, o


def _yc(o, z, gdn):
    return _rms(o, gdn) * _silu(z)


def nm_fwd(x, gn, sc, sh, w, *, tn, name, wl=0):
    T, N = x.shape[0], w.shape[-1]
    tt = _tile(T, 256)

    def body(x_ref, gn_ref, sc_ref, sh_ref, w_ref, h_ref, o_ref):
        @pl.when(pl.program_id(1) == 0)
        def _():
            h_ref[...] = _normmod(x_ref[...], gn_ref[...], sc_ref[...], sh_ref[...]).astype(BF16)
        o_ref[...] = _dot(h_ref[...], w_ref[...])

    row = pl.BlockSpec((1, D), lambda i, j: (0, 0))
    return pl.pallas_call(
        body, grid=(T // tt, N // tn), name=name,
        in_specs=[pl.BlockSpec((tt, D), lambda i, j: (i, 0)), row, row, row, _wspec(wl, (D, tn), lambda i, j: (0, j))],
        out_specs=[pl.BlockSpec((tt, D), lambda i, j: (i, 0)), pl.BlockSpec((tt, tn), lambda i, j: (i, j))],
        out_shape=[_sds((T, D), BF16), _sds((T, N))], compiler_params=_cp(2))(x, gn, sc, sh, _w3(w))


def nm_bwd(dys, w, x, gn, sc, sh, dres, *, name, wl=0):
    T = x.shape[0]
    N = w.shape[-1]
    tt = _tile(T, 256)
    widths = [a.shape[1] for a in dys]
    assert sum(widths) == N
    n = len(dys)

    def body(*refs):
        dy_refs, (w_ref, x_ref, gn_ref, sc_ref, sh_ref, dres_ref) = refs[:n], refs[n:n + 6]
        dx_ref, dgn_ref, dsc_ref, dsh_ref, dyb_ref = refs[n + 6:]
        i = pl.program_id(0)
        dyb = jnp.concatenate([r[...].astype(BF16) for r in dy_refs], axis=1) if n > 1 else dy_refs[0][...].astype(BF16)
        dyb_ref[...] = dyb
        dh = _dot_nt(dyb, w_ref[...])
        _, vjp = jax.vjp(_normmod, x_ref[...], gn_ref[...], sc_ref[...], sh_ref[...])
        dx, dgn, dsc, dsh = vjp(dh)
        dx_ref[...] = dres_ref[...] + dx

        @pl.when(i == 0)
        def _():
            dgn_ref[...] = jnp.zeros_like(dgn_ref)
            dsc_ref[...] = jnp.zeros_like(dsc_ref)
            dsh_ref[...] = jnp.zeros_like(dsh_ref)
        dgn_ref[...] += dgn
        dsc_ref[...] += dsc
        dsh_ref[...] += dsh

    row = pl.BlockSpec((1, D), lambda i: (0, 0))
    tile = pl.BlockSpec((tt, D), lambda i: (i, 0))
    return pl.pallas_call(
        body, grid=(T // tt,), name=name,
        in_specs=[pl.BlockSpec((tt, wd), lambda i: (i, 0)) for wd in widths]
        + [_wspec(wl, (D, N), lambda i: (0, 0)), tile, row, row, row, tile],
        out_specs=[tile, row, row, row, pl.BlockSpec((tt, N), lambda i: (i, 0))],
        out_shape=[_sds((T, D)), _sds((1, D)), _sds((1, D)), _sds((1, D)), _sds((T, N), BF16)],
        compiler_params=_cp(1))(*dys, _w3(w), x, gn, sc, sh, dres)


def mm_tn(a, b, *, tn, name):
    T, K = a.shape
    N = b.shape[1]
    tt = _tile(T, 512)

    def body(a_ref, b_ref, o_ref):
        @pl.when(pl.program_id(1) == 0)
        def _():
            o_ref[...] = jnp.zeros_like(o_ref)
        o_ref[...] += _dot_tn(a_ref[...], b_ref[...])

    return pl.pallas_call(
        body, grid=(N // tn, T // tt), name=name,
        in_specs=[pl.BlockSpec((tt, K), lambda j, t: (t, 0)), pl.BlockSpec((tt, tn), lambda j, t: (t, j))],
        out_specs=pl.BlockSpec((K, tn), lambda j, t: (0, j)), out_shape=_sds((K, N)), compiler_params=_cp(2))(a, b)


def mm_tn_split(a, b, *, by_cols, l, L, prev, name):
    T, K = a.shape
    N = b.shape[1]
    tt = _tile(T, 512)
    tn = N // 4 if by_cols else N // 2
    rh, C = (K // 2, N // 4) if by_cols else (K // 8, N)
    nt = T // tt

    def body(a_ref, b_ref, *rest):
        o_ref, acc = rest[-2:]
        t = pl.program_id(1)

        @pl.when(t == 0)
        def _():
            acc[...] = jnp.zeros_like(acc)
        acc[...] += _dot_tn(a_ref[...], b_ref[...])

        @pl.when(t == nt - 1)
        def _():
            if by_cols:
                for h in range(2):
                    o_ref[h] = acc[pl.ds(h * rh, rh), :]
            else:
                for s in range(4):
                    for h in range(2):
                        o_ref[h, s] = acc[pl.ds((2 * s + h) * rh, rh), :]

    if by_cols:
        out_spec = pl.BlockSpec((2, None, None, rh, tn), lambda j, t: (0, j, l, 0, 0))
    else:
        out_spec = pl.BlockSpec((2, 4, None, rh, tn), lambda j, t: (0, 0, l, 0, j))
    in_specs = [pl.BlockSpec((tt, K), lambda j, t: (t, 0)), pl.BlockSpec((tt, tn), lambda j, t: (t, j))]
    args = [a, b]
    aliases = {}
    if prev is not None:
        in_specs.append(pl.BlockSpec(memory_space=pl.ANY))
        args.append(prev)
        aliases = {2: 0}
    return pl.pallas_call(
        body, grid=(N // tn, nt), name=name, in_specs=in_specs, out_specs=out_spec, out_shape=_sds((2, 4, L, rh, C)),
        scratch_shapes=[pltpu.VMEM((K, tn), F32)], input_output_aliases=aliases, compiler_params=_cp(2))(*args)


def _fill_pad(pad_ref, prev, cur, first):
    pad_ref[pl.ds(0, HALO), :] = jnp.where(first, 0.0, prev)
    pad_ref[pl.ds(HALO, cur.shape[0]), :] = cur


def _causal_conv(pad_ref, w, K, tt):
    acc = None
    for k in range(K):
        term = pad_ref[pl.ds(HALO - (K - 1) + k, tt), :] * w[k:k + 1, :]
        acc = term if acc is None else acc + term
    return acc


def _conv_inputs(pc, pp, first, pad_a, pad_b, pad_c):
    def s_of(p):
        return p[:, O_AC:O_AV] * p[:, O_AV:O_BA]

    def u0_of(p):
        return p[:, O_BA:O_BG] * jax.nn.sigmoid(p[:, O_BG:O_Q])

    _fill_pad(pad_a, s_of(pp), s_of(pc), first)
    _fill_pad(pad_b, u0_of(pp), u0_of(pc), first)
    _fill_pad(pad_c, pp[:, O_Q:O_Z], pc[:, O_Q:O_Z], first)


def _mix_specs(T, tt):
    cur = pl.BlockSpec((tt, O_Z), lambda i: (i, 0))
    prev = pl.BlockSpec((HALO, O_Z), lambda i: (jnp.maximum(i * (tt // HALO) - 1, 0), 0))
    gbb = pl.BlockSpec((tt, HD), lambda i: (i, O_GB // HD))
    return cur, prev, gbb


def _full(shape):
    return pl.BlockSpec(shape, lambda i: (0,) * len(shape))


def mix_fwd(proj, wa, wb, bb, ln_g, ln_b, wc, alog_row, dt_row, *, name):
    T = proj.shape[0]
    tt = _tile(T, 256)

    def body(pc_ref, pp_ref, blk_ref, wa_ref, wb_ref, bb_ref, lg_ref, lb_ref, wc_ref, al_ref, dt_ref,
             yab_ref, q_ref, k_ref, v_ref, gb_ref, grow_ref, pad_a, pad_b, pad_c):
        first = pl.program_id(0) == 0
        pc = pc_ref[...]
        _conv_inputs(pc, pp_ref[...], first, pad_a, pad_b, pad_c)
        ya = pc[:, O_AB:O_AC] * _causal_conv(pad_a, wa_ref[...], KA, tt)
        yb = _mix_b_post(_causal_conv(pad_b, wb_ref[...], KB, tt) + bb_ref[...], lg_ref[...], lb_ref[...])
        yab_ref[...] = jnp.concatenate([ya, yb], axis=1)
        pre = _causal_conv(pad_c, wc_ref[...], KC, tt)
        blocks = [pre[:, j * HD:(j + 1) * HD] for j in range(3 * NH)]
        q, k, v, gb, grow = _dn_post(blocks[:NH], blocks[NH:2 * NH], blocks[2 * NH:], blk_ref[...], al_ref[...], dt_ref[...])
        q_ref[...] = jnp.concatenate(q, axis=1)
        k_ref[...] = jnp.concatenate(k, axis=1)
        v_ref[...] = jnp.concatenate(v, axis=1)
        gb_ref[...] = gb
        grow_ref[...] = grow

    cur, prev, gbb = _mix_specs(T, tt)
    t512 = pl.BlockSpec((tt, DDN), lambda i: (i, 0))
    return pl.pallas_call(
        body, grid=(T // tt,), name=name,
        in_specs=[cur, prev, gbb, _full((8, DC)), _full((32, DC)), _full((1, DC)), _full((1, DC)), _full((1, DC)),
                  _full((8, 3 * DDN)), _full((1, HD)), _full((1, HD))],
        out_specs=[t512, t512, t512, t512, pl.BlockSpec((tt, HD), lambda i: (i, 0)), pl.BlockSpec((8, tt), lambda i: (0, i))],
        out_shape=[_sds((T, 2 * DC)), _sds((T, DDN)), _sds((T, DDN)), _sds((T, DDN)), _sds((T, HD)), _sds((8, T))],
        scratch_shapes=[pltpu.VMEM((HALO + tt, DC), F32), pltpu.VMEM((HALO + tt, DC), F32), pltpu.VMEM((HALO + tt, 3 * DDN), F32)],
        compiler_params=_cp(1))(proj, proj, proj, wa, wb, bb, ln_g, ln_b, wc, alog_row, dt_row)


def mix_bwd_point(proj, dyab, dq, dk, dv, dgb, dgrow, wa, wb, bb, ln_g, ln_b, wc, alog_row, dt_row, *, name):
    T = proj.shape[0]
    tt = _tile(T, 256)
    CW = 2 * DC + 3 * DDN

    def body(pc_ref, pp_ref, blk_ref, dyab_ref, dq_ref, dk_ref, dv_ref, dgb_ref, dgrow_ref,
             wa_ref, wb_ref, bb_ref, lg_ref, lb_ref, wc_ref, al_ref, dt_ref,
             dab_ref, dconv_ref, dblk_ref, dbb_ref, dlg_ref, dlb_ref, dal_ref, ddt_ref, pad_a, pad_b, pad_c):
        i = pl.program_id(0)
        pc = pc_ref[...]
        _conv_inputs(pc, pp_ref[...], i == 0, pad_a, pad_b, pad_c)
        ca = _causal_conv(pad_a, wa_ref[...], KA, tt)
        u = _causal_conv(pad_b, wb_ref[...], KB, tt) + bb_ref[...]
        pre = _causal_conv(pad_c, wc_ref[...], KC, tt)
        dyab_v = dyab_ref[...]
        dya, dyb = dyab_v[:, :DC], dyab_v[:, DC:]
        dab_ref[...] = dya * ca
        dca = dya * pc[:, O_AB:O_AC]
        _, vjp_b = jax.vjp(_mix_b_post, u, lg_ref[...], lb_ref[...])
        du, dlg, dlb = vjp_b(dyb)
        blocks = [pre[:, j * HD:(j + 1) * HD] for j in range(3 * NH)]
        _, vjp_c = jax.vjp(_dn_post, blocks[:NH], blocks[NH:2 * NH], blocks[2 * NH:], blk_ref[...], al_ref[...], dt_ref[...])

        def heads(r):
            vv = r[...]
            return [vv[:, h * HD:(h + 1) * HD] for h in range(NH)]
        dpq, dpk, dpv, dblk, dal, ddt = vjp_c((heads(dq_ref), heads(dk_ref), heads(dv_ref), dgb_ref[...], dgrow_ref[...]))
        dconv_ref[...] = jnp.concatenate([dca, du] + dpq + dpk + dpv, axis=1)
        dblk_ref[...] = dblk

        @pl.when(i == 0)
        def _():
            for r in (dbb_ref, dlg_ref, dlb_ref, dal_ref, ddt_ref):
                r[...] = jnp.zeros_like(r)
        dbb_ref[...] += _colsum(du)
        dlg_ref[...] += dlg
        dlb_ref[...] += dlb
        dal_ref[...] += dal
        ddt_ref[...] += ddt

    cur, prev, gbb = _mix_specs(T, tt)
    t512 = pl.BlockSpec((tt, DDN), lambda i: (i, 0))
    t128 = pl.BlockSpec((tt, HD), lambda i: (i, 0))
    return pl.pallas_call(
        body, grid=(T // tt,), name=name,
        in_specs=[cur, prev, gbb, t512, t512, t512, t512, t128, pl.BlockSpec((8, tt), lambda i: (0, i)),
                  _full((8, DC)), _full((32, DC)), _full((1, DC)), _full((1, DC)), _full((1, DC)),
                  _full((8, 3 * DDN)), _full((1, HD)), _full((1, HD))],
        out_specs=[pl.BlockSpec((tt, DC), lambda i: (i, 0)), pl.BlockSpec((tt, CW), lambda i: (i, 0)), t128,
                   _full((1, DC)), _full((1, DC)), _full((1, DC)), _full((1, HD)), _full((1, HD))],
        out_shape=[_sds((T, DC)), _sds((T, CW)), _sds((T, HD)), _sds((1, DC)), _sds((1, DC)), _sds((1, DC)),
                   _sds((1, HD)), _sds((1, HD))],
        scratch_shapes=[pltpu.VMEM((HALO + tt, DC), F32), pltpu.VMEM((HALO + tt, DC), F32), pltpu.VMEM((HALO + tt, 3 * DDN), F32)],
        compiler_params=_cp(1))(proj, proj, proj, dyab, dq, dk, dv, dgb, dgrow, wa, wb, bb, ln_g, ln_b, wc, alog_row, dt_row)


def mix_bwd_conv(proj, dconv, wa, wb, wc, *, name):
    T = proj.shape[0]
    tt = _tile(T, 256)
    CW = 2 * DC + 3 * DDN
    nblk = T // HALO

    def body(pc_ref, pp_ref, dc_ref, dn_ref, wa_ref, wb_ref, wc_ref, dp_ref, dwa_ref, dwb_ref, dwc_ref,
             pad_a, pad_b, pad_c, dpad):
        i = pl.program_id(0)
        last = i == pl.num_programs(0) - 1
        pc = pc_ref[...]
        _conv_inputs(pc, pp_ref[...], i == 0, pad_a, pad_b, pad_c)
        dcur = dc_ref[...]
        dpad[pl.ds(0, tt), :] = dcur
        dpad[pl.ds(tt, HALO), :] = jnp.where(last, 0.0, dn_ref[...])

        @pl.when(i == 0)
        def _():
            for r in (dwa_ref, dwb_ref, dwc_ref):
                r[...] = jnp.zeros_like(r)

        def tconv(lo, hi, w, K, pad_ref, dw_ref):
            dy = dcur[:, lo:hi]
            acc = None
            for k in range(K):
                term = dpad[pl.ds(K - 1 - k, tt), lo:hi] * w[k:k + 1, :]
                acc = term if acc is None else acc + term
                dw_ref[pl.ds(k, 1), :] += _colsum(dy * pad_ref[pl.ds(HALO - (K - 1) + k, tt), :])
            return acc

        ds = tconv(0, DC, wa_ref[...], KA, pad_a, dwa_ref)
        du0 = tconv(DC, 2 * DC, wb_ref[...], KB, pad_b, dwb_ref)
        dqkv = tconv(2 * DC, CW, wc_ref[...], KC, pad_c, dwc_ref)
        a_c, a_v, b_a, b_g = pc[:, O_AC:O_AV], pc[:, O_AV:O_BA], pc[:, O_BA:O_BG], pc[:, O_BG:O_Q]
        sg = jax.nn.sigmoid(b_g)
        dp_ref[...] = jnp.concatenate([ds * a_v, ds * a_c, du0 * sg, du0 * b_a * sg * (1.0 - sg), dqkv], axis=1)

    cur, prev, _ = _mix_specs(T, tt)
    return pl.pallas_call(
        body, grid=(T // tt,), name=name,
        in_specs=[cur, prev, pl.BlockSpec((tt, CW), lambda i: (i, 0)),
                  pl.BlockSpec((HALO, CW), lambda i: (jnp.minimum((i + 1) * (tt // HALO), nblk - 1), 0)),
                  _full((8, DC)), _full((32, DC)), _full((8, 3 * DDN))],
        out_specs=[pl.BlockSpec((tt, O_Z - O_AC), lambda i: (i, 0)), _full((8, DC)), _full((32, DC)), _full((8, 3 * DDN))],
        out_shape=[_sds((T, O_Z - O_AC)), _sds((8, DC)), _sds((32, DC)), _sds((8, 3 * DDN))],
        scratch_shapes=[pltpu.VMEM((HALO + tt, DC), F32), pltpu.VMEM((HALO + tt, DC), F32), pltpu.VMEM((HALO + tt, 3 * DDN), F32),
                        pltpu.VMEM((tt + HALO, CW), F32)],
        compiler_params=_cp(1))(proj, proj, dconv, dconv, wa, wb, wc)


def _head(v, h):
    return v[:, h * HD:(h + 1) * HD]


def dn_intra_fwd(q, k, v, gb, grow3, *, name):
    T = q.shape[0]
    N = T // CH
    cb = 4 if N % 4 == 0 else 1

    def body(q_ref, k_ref, v_ref, gb_ref, gr_ref, u_ref, w_ref, qk_ref, p_ref):
        for c in range(cb):
            rows = pl.ds(c * CH, CH)
            qv, kv, vv, gbv = q_ref[rows, :], k_ref[rows, :], v_ref[rows, :], gb_ref[rows, :]
            us, ws = [], []
            for h in range(NH):
                u, w, qk, p = _dn_intra(_head(qv, h), _head(kv, h), _head(vv, h), gbv[:, NH + h:NH + h + 1], gbv[:, h:h + 1],
                                        gr_ref[c, h:h + 1, :])
                us.append(u)
                ws.append(w)
                qk_ref[c, h] = qk
                p_ref[c, h] = p
            u_ref[rows, :] = jnp.concatenate(us, axis=1)
            w_ref[rows, :] = jnp.concatenate(ws, axis=1)

    t512 = pl.BlockSpec((cb * CH, DDN), lambda i: (i, 0))
    sq = pl.BlockSpec((cb, NH, CH, CH), lambda i: (i, 0, 0, 0))
    return pl.pallas_call(
        body, grid=(N // cb,), name=name,
        in_specs=[t512, t512, t512, pl.BlockSpec((cb * CH, HD), lambda i: (i, 0)), pl.BlockSpec((cb, 8, CH), lambda i: (i, 0, 0))],
        out_specs=[t512, t512, sq, sq],
        out_shape=[_sds((T, DDN)), _sds((T, DDN)), _sds((N, NH, CH, CH)), _sds((N, NH, CH, CH))],
        compiler_params=_cp(1))(q, k, v, gb, grow3)


def dn_inter_fwd(q, k, u, w, gb, qk, *, name):
    T = q.shape[0]
    N = T // CH
    cb = 4 if N % 4 == 0 else 1

    def body(q_ref, k_ref, u_ref, w_ref, gb_ref, qk_ref, o_ref, s_ref, state):
        @pl.when(pl.program_id(0) == 0)
        def _():
            state[...] = jnp.zeros_like(state)
        for c in range(cb):
            rows = pl.ds(c * CH, CH)
            qv, kv, uv, wv, gbv = q_ref[rows, :], k_ref[rows, :], u_ref[rows, :], w_ref[rows, :], gb_ref[rows, :]
            os_ = []
            for h in range(NH):
                s = state[h]
                s_ref[c, h] = s
                s_new, o = _dn_inter(s, _head(qv, h), _head(kv, h), _head(uv, h), _head(wv, h), gbv[:, h:h + 1], qk_ref[c, h])
                state[h] = s_new
                os_.append(o)
            o_ref[rows, :] = jnp.concatenate(os_, axis=1)

    t512 = pl.BlockSpec((cb * CH, DDN), lambda i: (i, 0))
    return pl.pallas_call(
        body, grid=(N // cb,), name=name,
        in_specs=[t512, t512, t512, t512, pl.BlockSpec((cb * CH, HD), lambda i: (i, 0)),
                  pl.BlockSpec((cb, NH, CH, CH), lambda i: (i, 0, 0, 0))],
        out_specs=[t512, pl.BlockSpec((cb, NH, HD, HD), lambda i: (i, 0, 0, 0))],
        out_shape=[_sds((T, DDN)), _sds((N, NH, HD, HD))],
        scratch_shapes=[pltpu.VMEM((NH, HD, HD), F32)], compiler_params=_cp(1))(q, k, u, w, gb, qk)


def _lane_onehot(h):
    return (lax.broadcasted_iota(jnp.int32, (1, HD), 1) == h).astype(F32)


def dn_inter_bwd(do, q, k, u, w, gb, qk, s_all, *, name):
    T = q.shape[0]
    N = T // CH
    cb = 2 if N % 2 == 0 else 1
    G = N // cb

    def body(do_ref, q_ref, k_ref, u_ref, w_ref, gb_ref, qk_ref, s_ref, dq_ref, dk_ref, du_ref, dw_ref, dg_ref, dqk_ref, dstate):
        @pl.when(pl.program_id(0) == 0)
        def _():
            dstate[...] = jnp.zeros_like(dstate)
        for c in reversed(range(cb)):
            rows = pl.ds(c * CH, CH)
            dov, qv, kv, uv, wv, gbv = do_ref[rows, :], q_ref[rows, :], k_ref[rows, :], u_ref[rows, :], w_ref[rows, :], gb_ref[rows, :]
            dqs, dks, dus, dws = [], [], [], []
            dg = jnp.zeros((CH, HD), F32)
            for h in range(NH):
                _, vjp = jax.vjp(_dn_inter, s_ref[c, h], _head(qv, h), _head(kv, h), _head(uv, h), _head(wv, h),
                                 gbv[:, h:h + 1], qk_ref[c, h])
                ds, dq, dk, du, dw, dgc, dqk = vjp((dstate[h], _head(dov, h)))
                dstate[h] = ds
                dqs.append(dq)
                dks.append(dk)
                dus.append(du)
                dws.append(dw)
                dg = dg + dgc * _lane_onehot(h)
                dqk_ref[c, h] = dqk
            dq_ref[rows, :] = jnp.concatenate(dqs, axis=1)
            dk_ref[rows, :] = jnp.concatenate(dks, axis=1)
            du_ref[rows, :] = jnp.concatenate(dus, axis=1)
            dw_ref[rows, :] = jnp.concatenate(dws, axis=1)
            dg_ref[rows, :] = dg

    t512 = pl.BlockSpec((cb * CH, DDN), lambda i: (G - 1 - i, 0))
    t128 = pl.BlockSpec((cb * CH, HD), lambda i: (G - 1 - i, 0))
    qkb = pl.BlockSpec((cb, NH, CH, CH), lambda i: (G - 1 - i, 0, 0, 0))
    return pl.pallas_call(
        body, grid=(G,), name=name,
        in_specs=[t512, t512, t512, t512, t512, t128, qkb, pl.BlockSpec((cb, NH, HD, HD), lambda i: (G - 1 - i, 0, 0, 0))],
        out_specs=[t512, t512, t512, t512, t128, qkb],
        out_shape=[_sds((T, DDN))] * 4 + [_sds((T, HD)), _sds((N, NH, CH, CH))],
        scratch_shapes=[pltpu.VMEM((NH, HD, HD), F32)], compiler_params=_cp(1))(do, q, k, u, w, gb, qk, s_all)


def dn_intra_bwd(du, dw, dqk, dq_in, dk_in, dg_in, q, k, v, gb, grow3, p_all, *, name):
    T = q.shape[0]
    N = T // CH
    cb = 4 if N % 4 == 0 else 1

    def body(du_ref, dw_ref, dqk_ref, dqi_ref, dki_ref, dgi_ref, q_ref, k_ref, v_ref, gb_ref, gr_ref, p_ref,
             dq_ref, dk_ref, dv_ref, dgb_ref, dgr_ref):
        for c in range(cb):
            rows = pl.ds(c * CH, CH)
            qv, kv, vv, gbv = q_ref[rows, :], k_ref[rows, :], v_ref[rows, :], gb_ref[rows, :]
            duv, dwv = du_ref[rows, :], dw_ref[rows, :]
            dqs, dks, dvs, dgrs = [], [], [], []
            dgb = dgi_ref[rows, :]
            for h in range(NH):
                _, vjp = jax.vjp(functools.partial(_dn_intra, p_known=p_ref[c, h]), _head(qv, h), _head(kv, h), _head(vv, h),
                                 gbv[:, NH + h:NH + h + 1], gbv[:, h:h + 1], gr_ref[c, h:h + 1, :])
                dq, dk, dv, dbeta, dgc, dgr = vjp((_head(duv, h), _head(dwv, h), dqk_ref[c, h], jnp.zeros((CH, CH), F32)))
                dqs.append(dq)
                dks.append(dk)
                dvs.append(dv)
                dgrs.append(dgr)
                dgb = dgb + dgc * _lane_onehot(h) + dbeta * _lane_onehot(NH + h)
            dq_ref[rows, :] = dqi_ref[rows, :] + jnp.concatenate(dqs, axis=1)
            dk_ref[rows, :] = dki_ref[rows, :] + jnp.concatenate(dks, axis=1)
            dv_ref[rows, :] = jnp.concatenate(dvs, axis=1)
            dgb_ref[rows, :] = dgb
            dgr_ref[c] = jnp.concatenate(dgrs + [jnp.zeros((8 - NH, CH), F32)], axis=0)

    t512 = pl.BlockSpec((cb * CH, DDN), lambda i: (i, 0))
    t128 = pl.BlockSpec((cb * CH, HD), lambda i: (i, 0))
    qkb = pl.BlockSpec((cb, NH, CH, CH), lambda i: (i, 0, 0, 0))
    grb = pl.BlockSpec((cb, 8, CH), lambda i: (i, 0, 0))
    return pl.pallas_call(
        body, grid=(N // cb,), name=name,
        in_specs=[t512, t512, qkb, t512, t512, t128, t512, t512, t512, t128, grb, qkb],
        out_specs=[t512, t512, t512, t128, grb],
        out_shape=[_sds((T, DDN))] * 3 + [_sds((T, HD)), _sds((N, 8, CH))],
        compiler_params=_cp(1))(du, dw, dqk, dq_in, dk_in, dg_in, q, k, v, gb, grow3, p_all)


def _z_specs(tt):
    return [pl.BlockSpec((tt, DC), lambda i: (i, O_Z // DC)), pl.BlockSpec((tt, DC), lambda i: (i, O_Z // DC + 1))]


def mixout_fwd(x, yab, o, proj, gdn, g1, w_out, *, name, wl=0):
    T = x.shape[0]
    tt = _tile(T, 256)

    def body(x_ref, yab_ref, o_ref, z0_ref, z1_ref, gdn_ref, g1_ref, w_ref, ycat_ref, mix_ref, xo_ref):
        ov = o_ref[...]
        z = jnp.concatenate([z0_ref[...], z1_ref[...]], axis=1)
        yc = [_yc(_head(ov, h), _head(z, h), gdn_ref[...]) for h in range(NH)]
        ycat = jnp.concatenate([yab_ref[...]] + yc, axis=1).astype(BF16)
        ycat_ref[...] = ycat
        mix = _dot(ycat, w_ref[...])
        mix_ref[...] = mix
        xo_ref[...] = x_ref[...] + g1_ref[...] * mix

    tile = pl.BlockSpec((tt, D), lambda i: (i, 0))
    t512 = pl.BlockSpec((tt, DDN), lambda i: (i, 0))
    return pl.pallas_call(
        body, grid=(T // tt,), name=name,
        in_specs=[tile, t512, t512] + _z_specs(tt) + [_full((1, HD)), _full((1, D)), _wspec(wl, (D, D), lambda i: (0, 0))],
        out_specs=[tile, tile, tile], out_shape=[_sds((T, D), BF16), _sds((T, D)), _sds((T, D))],
        compiler_params=_cp(1))(x, yab, o, proj, proj, gdn, g1, _w3(w_out))


def mixout_bwd(dx, mix, o, proj, gdn, g1, w_out, *, name, wl=0):
    T = dx.shape[0]
    tt = _tile(T, 256)

    def body(dx_ref, mix_ref, o_ref, z0_ref, z1_ref, gdn_ref, g1_ref, w_ref, dmix_ref, dyab_ref, do_ref, dz_ref, dg1_ref, dgdn_ref):
        i = pl.program_id(0)
        dxv = dx_ref[...]
        dmix = (dxv * g1_ref[...]).astype(BF16)
        dmix_ref[...] = dmix
        dycat = _dot_nt(dmix, w_ref[...])
        dyab_ref[...] = dycat[:, :2 * DC]
        ov = o_ref[...]
        z = jnp.concatenate([z0_ref[...], z1_ref[...]], axis=1)
        dos, dzs = [], []
        dgdn = jnp.zeros((1, HD), F32)
        for h in range(NH):
            _, vjp = jax.vjp(_yc, _head(ov, h), _head(z, h), gdn_ref[...])
            do, dz, dg = vjp(dycat[:, 2 * DC + h * HD:2 * DC + (h + 1) * HD])
            dos.append(do)
            dzs.append(dz)
            dgdn = dgdn + dg
        do_ref[...] = jnp.concatenate(dos, axis=1)
        dz_ref[...] = jnp.concatenate(dzs, axis=1)

        @pl.when(i == 0)
        def _():
            dg1_ref[...] = jnp.zeros_like(dg1_ref)
            dgdn_ref[...] = jnp.zeros_like(dgdn_ref)
        dg1_ref[...] += _colsum(dxv * mix_ref[...])
        dgdn_ref[...] += dgdn

    tile = pl.BlockSpec((tt, D), lambda i: (i, 0))
    t512 = pl.BlockSpec((tt, DDN), lambda i: (i, 0))
    return pl.pallas_call(
        body, grid=(T // tt,), name=name,
        in_specs=[tile, tile, t512] + _z_specs(tt) + [_full((1, HD)), _full((1, D)), _wspec(wl, (D, D), lambda i: (0, 0))],
        out_specs=[tile, t512, t512, t512, _full((1, D)), _full((1, HD))],
        out_shape=[_sds((T, D), BF16), _sds((T, DDN)), _sds((T, DDN)), _sds((T, DDN)), _sds((1, D)), _sds((1, HD))],
        compiler_params=_cp(1))(dx, mix, o, proj, proj, gdn, g1, _w3(w_out))


FK = DFF // 2


def ffnout_fwd(x, gu, g2, w, *, name, wl=0):
    T = x.shape[0]
    tt = _tile(T, 256)

    def body(x_ref, gate_ref, up_ref, g2_ref, w_ref, act_ref, f_ref, xo_ref):
        kk = pl.program_id(1)
        act = (_silu(gate_ref[...]) * up_ref[...]).astype(BF16)
        act_ref[...] = act
        part = _dot(act, w_ref[...])

        @pl.when(kk == 0)
        def _():
            f_ref[...] = part

        @pl.when(kk == 1)
        def _():
            f = f_ref[...] + part
            f_ref[...] = f
            xo_ref[...] = x_ref[...] + g2_ref[...] * f

    tile = pl.BlockSpec((tt, D), lambda i, kk: (i, 0))
    return pl.pallas_call(
        body, grid=(T // tt, 2), name=name,
        in_specs=[tile, pl.BlockSpec((tt, FK), lambda i, kk: (i, kk)), pl.BlockSpec((tt, FK), lambda i, kk: (i, 2 + kk)),
                  pl.BlockSpec((1, D), lambda i, kk: (0, 0)), _wspec(wl, (FK, D), lambda i, kk: (kk, 0))],
        out_specs=[pl.BlockSpec((tt, FK), lambda i, kk: (i, kk)), tile, tile],
        out_shape=[_sds((T, DFF), BF16), _sds((T, D)), _sds((T, D))], compiler_params=_cp(2))(x, gu, gu, g2, _w3(w))


def ffnout_bwd(dx, f, gu, g2, w, *, name, wl=0):
    T = dx.shape[0]
    tt = _tile(T, 256)

    def body(dx_ref, f_ref, gate_ref, up_ref, g2_ref, w_ref, df_ref, dgate_ref, dup_ref, dg2_ref):
        i, kk = pl.program_id(0), pl.program_id(1)
        dxv = dx_ref[...]
        df = (dxv * g2_ref[...]).astype(BF16)
        dact = _dot_nt(df, w_ref[...])
        gate, up = gate_ref[...], up_ref[...]
        sg = jax.nn.sigmoid(gate)
        dgate_ref[...] = (dact * up * (sg * (1.0 + gate * (1.0 - sg)))).astype(BF16)
        dup_ref[...] = (dact * (gate * sg)).astype(BF16)

        @pl.when(kk == 0)
        def _():
            df_ref[...] = df

        @pl.when((i == 0) & (kk == 0))
        def _():
            dg2_ref[...] = jnp.zeros_like(dg2_ref)

        @pl.when(kk == 0)
        def _():
            dg2_ref[...] += _colsum(dxv * f_ref[...])

    tile = pl.BlockSpec((tt, D), lambda i, kk: (i, 0))
    return pl.pallas_call(
        body, grid=(T // tt, 2), name=name,
        in_specs=[tile, tile, pl.BlockSpec((tt, FK), lambda i, kk: (i, kk)), pl.BlockSpec((tt, FK), lambda i, kk: (i, 2 + kk)),
                  pl.BlockSpec((1, D), lambda i, kk: (0, 0)), _wspec(wl, (FK, D), lambda i, kk: (kk, 0))],
        out_specs=[tile, pl.BlockSpec((tt, FK), lambda i, kk: (i, kk)), pl.BlockSpec((tt, FK), lambda i, kk: (i, kk)),
                   pl.BlockSpec((1, D), lambda i, kk: (0, 0))],
        out_shape=[_sds((T, D), BF16), _sds((T, DFF), BF16), _sds((T, DFF), BF16), _sds((1, D))],
        compiler_params=_cp(2))(dx, f, gu, gu, g2, _w3(w))


def loss_head(x, target, gf, *, name):
    T = x.shape[0]
    tt = _tile(T, 256)

    def body(x_ref, t_ref, g_ref, loss_ref, dx_ref, dg_ref):
        i = pl.program_id(0)
        tv = t_ref[...]
        y, vjp = jax.vjp(_rms, x_ref[...], g_ref[...])
        err = y - tv
        dx, dg = vjp(err * (1.0 / D))
        dx_ref[...] = dx

        @pl.when(i == 0)
        def _():
            loss_ref[...] = jnp.zeros_like(loss_ref)
            dg_ref[...] = jnp.zeros_like(dg_ref)
        loss_ref[...] += 0.5 * jnp.sum(jnp.mean(err * err, axis=-1, keepdims=True), axis=0, keepdims=True)
        dg_ref[...] += dg

    tile = pl.BlockSpec((tt, D), lambda i: (i, 0))
    return pl.pallas_call(
        body, grid=(T // tt,), name=name, in_specs=[tile, tile, _full((1, D))],
        out_specs=[_full((1, HD)), tile, _full((1, D))], out_shape=[_sds((1, HD)), _sds((T, D)), _sds((1, D))],
        compiler_params=_cp(1))(x, target, gf)


def _pad_rows(a, rows):
    return jnp.pad(a, ((0, rows - a.shape[0]), (0, 0)))


def _row128(v):
    return jnp.pad(v, (0, HD - v.shape[0]))[None, :]


def _grow3(grow):
    return grow.reshape(8, -1, CH).transpose(1, 0, 2)


def _grow2(grow3):
    return grow3.transpose(1, 0, 2).reshape(8, -1)


def layer_params(l, mod, p):
    m = mod[l].reshape(6, 1, D)
    return dict(
        sh1=m[0], sc1=m[1], g1=m[2], sh2=m[3], sc2=m[4], g2=m[5],
        gn1=p["norm_mix_g"][l][None, :], gn2=p["norm_ffn_g"][l][None, :],
        wa=_pad_rows(p["conv_a_w"][l], 8), wb=_pad_rows(p["conf_dw_w"][l], 32), wc=_pad_rows(p["dn_conv_w"][l], 8),
        bb=p["conf_dw_b"][l][None, :], ln_g=p["conf_ln_g"][l][None, :], ln_b=p["conf_ln_b"][l][None, :],
        alog=_row128(p["dn_a_log"][l]), dt=_row128(p["dn_dt_bias"][l]), gdn=p["dn_norm_g"][l][None, :])


def layer_fwd(l, x, lp, w_in, w_out, w_ffn_in, w_ffn_out, wl=0):
    s = {"x0": x}
    s["h1"], s["proj"] = nm_fwd(x, lp["gn1"], lp["sc1"], lp["sh1"], w_in, tn=NP // 3, name=f"proj_fwd_{l}", wl=wl)
    yab, s["q"], s["k"], s["v"], s["gb"], grow = mix_fwd(
        s["proj"], lp["wa"], lp["wb"], lp["bb"], lp["ln_g"], lp["ln_b"], lp["wc"], lp["alog"], lp["dt"], name=f"mix_fwd_{l}")
    s["grow3"] = _grow3(grow)
    s["u"], s["w"], s["qk"], s["p"] = dn_intra_fwd(s["q"], s["k"], s["v"], s["gb"], s["grow3"], name=f"dn_intra_fwd_{l}")
    s["o"], s["s_all"] = dn_inter_fwd(s["q"], s["k"], s["u"], s["w"], s["gb"], s["qk"], name=f"dn_inter_fwd_{l}")
    s["ycat"], s["mix"], s["x1"] = mixout_fwd(x, yab, s["o"], s["proj"], lp["gdn"], lp["g1"], w_out, name=f"mixout_fwd_{l}",
                                              wl=wl)
    s["h2"], s["gu"] = nm_fwd(s["x1"], lp["gn2"], lp["sc2"], lp["sh2"], w_ffn_in, tn=2 * DFF // 4, name=f"ffnin_fwd_{l}", wl=wl)
    s["act"], s["f"], x2 = ffnout_fwd(s["x1"], s["gu"], lp["g2"], w_ffn_out, name=f"ffnout_fwd_{l}", wl=wl)
    return x2, s


def layer_bwd(l, dx2, s, lp, w_in, w_out, w_ffn_in, w_ffn_out, wl=0, L=1, gacc=None):
    g = {}
    prev = (lambda n: None) if gacc is None else gacc.get
    df, dgate, dup, dg2 = ffnout_bwd(dx2, s["f"], s["gu"], lp["g2"], w_ffn_out, name=f"ffnout_bwd_{l}", wl=wl)
    g["w_ffn_out"] = mm_tn_split(s["act"], df, by_cols=False, l=wl, L=L, prev=prev("w_ffn_out"), name=f"ffnout_dw_{l}")
    dx1, g["norm_ffn_g"], dsc2, dsh2, dgu = nm_bwd([dgate, dup], w_ffn_in, s["x1"], lp["gn2"], lp["sc2"], lp["sh2"], dx2,
                                                   name=f"ffnin_bwd_{l}", wl=wl)
    g["w_ffn_in"] = mm_tn_split(s["h2"], dgu, by_cols=True, l=wl, L=L, prev=prev("w_ffn_in"), name=f"ffnin_dw_{l}")
    dmix, dyab, do, dz, dg1, g["dn_norm_g"] = mixout_bwd(dx1, s["mix"], s["o"], s["proj"], lp["gdn"], lp["g1"], w_out,
                                                         name=f"mixout_bwd_{l}", wl=wl)
    g["w_out"] = mm_tn_split(s["ycat"], dmix, by_cols=False, l=wl, L=L, prev=prev("w_out"), name=f"mixout_dw_{l}")
    dq_i, dk_i, du, dw, dg_i, dqk = dn_inter_bwd(do, s["q"], s["k"], s["u"], s["w"], s["gb"], s["qk"], s["s_all"],
                                                 name=f"dn_inter_bwd_{l}")
    dq, dk, dv, dgb, dgrow3 = dn_intra_bwd(du, dw, dqk, dq_i, dk_i, dg_i, s["q"], s["k"], s["v"], s["gb"], s["grow3"],
                                           s["p"], name=f"dn_intra_bwd_{l}")
    dab, dconv, dblk, g["conf_dw_b"], g["conf_ln_g"], g["conf_ln_b"], dal, ddt = mix_bwd_point(
        s["proj"], dyab, dq, dk, dv, dgb, _grow2(dgrow3), lp["wa"], lp["wb"], lp["bb"], lp["ln_g"], lp["ln_b"], lp["wc"],
        lp["alog"], lp["dt"], name=f"mix_bwd_point_{l}")
    dpb, dwa, dwb, dwc = mix_bwd_conv(s["proj"], dconv, lp["wa"], lp["wb"], lp["wc"], name=f"mix_bwd_conv_{l}")
    dx0, g["norm_mix_g"], dsc1, dsh1, dproj = nm_bwd([dab, dpb, dz, dblk], w_in, s["x0"], lp["gn1"], lp["sc1"], lp["sh1"], dx1,
                                                     name=f"proj_bwd_{l}", wl=wl)
    g["w_in"] = mm_tn(s["h1"], dproj, tn=NP // 3, name=f"proj_dw_{l}")
    g["conv_a_w"], g["conf_dw_w"], g["dn_conv_w"] = dwa[:KA], dwb[:KB], dwc[:KC]
    g["dn_a_log"], g["dn_dt_bias"] = dal[0, :NH], ddt[0, :NH]
    g["mod"] = jnp.concatenate([dsh1, dsc1, dg1, dsh2, dsc2, dg2], axis=1)
    return dx0, g


EW_BLOCK_BYTES = 1 << 20


def _row_tile(R, C, mult=8):
    best = None
    for rt in range(mult, R + 1, mult):
        if R % rt == 0 and rt * C * 4 <= EW_BLOCK_BYTES:
            best = rt
    return best if best is not None else R


def add_half_bf16(g2, recv, core, *, name):
    _, B, R, C = g2.shape
    rt = _row_tile(R, C, 16)

    def body(core_ref, a_ref, b_ref, o_ref):
        o_ref[...] = (a_ref[...] + b_ref[...]).astype(BF16)

    spec = pl.BlockSpec((1, rt, C), lambda b, r, core_ref: (b, r, 0))
    return pl.pallas_call(
        body, name=name, out_shape=_sds((B, R, C), BF16), compiler_params=_cp(2),
        grid_spec=pltpu.PrefetchScalarGridSpec(
            num_scalar_prefetch=1, grid=(B, R // rt),
            in_specs=[pl.BlockSpec((None, 1, rt, C), lambda b, r, core_ref: (core_ref[0], b, r, 0)), spec],
            out_specs=spec))(core, g2, recv)


def adamw_halves(w, g2, m, v, *, name):
    L, R, C = w.shape
    rh = R // 2
    rt = _row_tile(rh, C)
    nr = rh // rt

    def body(w_ref, g_ref, m_ref, v_ref, go_ref, d_ref, mo_ref, vo_ref):
        g = g_ref[0]
        go_ref[...] = g
        d_ref[...], mo_ref[...], vo_ref[...] = _adamw_math(w_ref[...], g, m_ref[...], v_ref[...])

    spec = pl.BlockSpec((1, rt, C), lambda l, h, r: (l, h * nr + r, 0))
    return pl.pallas_call(
        body, grid=(L, 2, nr), name=name,
        in_specs=[spec, pl.BlockSpec((1, 1, rt, C), lambda l, h, r: (h, l, r, 0)), spec, spec],
        out_specs=[spec] * 4, out_shape=[_sds((L, R, C))] * 4, compiler_params=_cp(3))(w, g2, m, v)


def ew_call(fn, ins, n_out, *, name):
    B, R, C = ins[0].shape
    rt = _row_tile(R, C)
    n = len(ins)

    def body(*refs):
        outs = fn(*[r[...] for r in refs[:n]])
        for r, o in zip(refs[n:], outs):
            r[...] = o

    spec = pl.BlockSpec((1, rt, C), lambda b, r: (b, r, 0))
    return pl.pallas_call(body, grid=(B, R // rt), name=name, in_specs=[spec] * n, out_specs=[spec] * n_out,
                          out_shape=[_sds((B, R, C))] * n_out, compiler_params=_cp(2))(*ins)


def _adamw_math(w, g, m, v):
    m = ADAM_B1 * m + (1.0 - ADAM_B1) * g
    v = ADAM_B2 * v + (1.0 - ADAM_B2) * jnp.square(g)
    m_hat = m / (1.0 - ADAM_B1 ** ADAM_STEP)
    v_hat = v / (1.0 - ADAM_B2 ** ADAM_STEP)
    return -ADAM_LR * (m_hat / (jnp.sqrt(v_hat) + ADAM_EPS) + ADAM_WD * w), m, v


def adamw(w, g, m, v, *, name):
    shape = w.shape
    r3 = lambda a: a.reshape((-1,) + shape[-2:])
    return [o.reshape(shape) for o in ew_call(_adamw_math, [r3(w), r3(g), r3(m), r3(v)], 3, name=name)]


def sum_slots(a, *, name):
    S, B, R, C = a.shape
    rt = _row_tile(R, C, 16)

    def body(*refs):
        acc = refs[0][0, 0].astype(F32)
        for r in refs[1:S]:
            acc = acc + r[0, 0].astype(F32)
        refs[S][0] = acc

    def spec(s):
        return pl.BlockSpec((1, 1, rt, C), lambda b, r: (s, b, r, 0))
    return pl.pallas_call(body, grid=(B, R // rt), name=name, in_specs=[spec(s) for s in range(S)],
                          out_specs=pl.BlockSpec((1, rt, C), lambda b, r: (b, r, 0)), out_shape=_sds((B, R, C)),
                          compiler_params=_cp(2))(*([a] * S))


ADA_SH = 6 * D // 4
ADA_TN = 512


def mod_fwd(c_all, w_ada, b_my, *, name):
    L = w_ada.shape[0]

    def body(c_ref, w_ref, b_ref, o_ref):
        o_ref[0] = _dot(_silu(c_ref[...]).astype(BF16), w_ref[0].astype(BF16)) + b_ref[0]

    return pl.pallas_call(
        body, grid=(L, ADA_SH // ADA_TN), name=name,
        in_specs=[pl.BlockSpec((8, D), lambda l, j: (0, 0)), pl.BlockSpec((1, D, ADA_TN), lambda l, j: (l, 0, j)),
                  pl.BlockSpec((1, 1, ADA_TN), lambda l, j: (l, 0, j))],
        out_specs=pl.BlockSpec((1, 8, ADA_TN), lambda l, j: (l, 0, j)), out_shape=_sds((L, 8, ADA_SH)),
        compiler_params=_cp(2))(c_all, w_ada, b_my)


def wada_grad(c_all, dmod, *, name):
    L = dmod.shape[0]

    def body(c_ref, d_ref, o_ref):
        o_ref[0] = _dot_tn(_silu(c_ref[...]), d_ref[0], HI)

    return pl.pallas_call(
        body, grid=(L, ADA_SH // ADA_TN), name=name,
        in_specs=[pl.BlockSpec((8, D), lambda l, j: (0, 0)), pl.BlockSpec((1, 8, ADA_TN), lambda l, j: (l, 0, j))],
        out_specs=pl.BlockSpec((1, D, ADA_TN), lambda l, j: (l, 0, j)), out_shape=_sds((L, D, ADA_SH)),
        compiler_params=_cp(2))(c_all, dmod)


def _place():
    return lax.axis_index("x"), lax.axis_index("y"), lax.axis_index("c")


def _other_chips(x, y):
    return [(1 - x, y), (x, 1 - y), (1 - x, 1 - y)]


def allgather8(blocks, *, space, name):
    n = len(blocks)

    def body(*refs):
        ins, outs = refs[:n], refs[n:2 * n]
        send_sems, recv_sems, local_sems = refs[2 * n:]
        x, y, c = _place()
        me, sibling = (x, y, c), (x, y, 1 - c)
        chips = _other_chips(x, y)

        def slot(p):
            return 4 * p[0] + 2 * p[1] + p[2]

        def copy(a, k, block, to, src=None):
            dst = outs[a].at[slot(block)]
            return pltpu.make_async_remote_copy(src_ref=dst if src is None else src, dst_ref=dst, send_sem=send_sems.at[a, k],
                                                recv_sem=recv_sems.at[a, k], device_id=to, device_id_type=MESH)

        mine = [pltpu.make_async_copy(ins[a], outs[a].at[slot(me)], local_sems.at[a]) for a in range(n)]
        for cp in mine:
            cp.start()
        first = []
        for a in range(n):
            first.append(copy(a, 0, me, sibling, src=ins[a]))
            first += [copy(a, 1 + j, me, (*chip, c), src=ins[a]) for j, chip in enumerate(chips)]
        for cp in first:
            cp.start()
        passed = []
        for j, chip in enumerate(chips):
            for a in range(n):
                copy(a, 1 + j, (*chip, c), me).wait_recv()
                cp = copy(a, 4 + j, (*chip, c), sibling)
                cp.start()
                passed.append(cp)
        for a in range(n):
            copy(a, 0, sibling, me).wait_recv()
            for j, chip in enumerate(chips):
                copy(a, 4 + j, (*chip, 1 - c), me).wait_recv()
        for cp in first + passed:
            cp.wait_send()
        for cp in mine:
            cp.wait()

    spec = pl.BlockSpec(memory_space=space)
    return pl.pallas_call(
        body, name=name, in_specs=[spec] * n, out_specs=[spec] * n,
        out_shape=[_sds((8,) + b.shape, b.dtype) for b in blocks],
        scratch_shapes=[pltpu.SemaphoreType.DMA((n, 7)), pltpu.SemaphoreType.DMA((n, 7)), pltpu.SemaphoreType.DMA((n,))],
        compiler_params=pltpu.CompilerParams(vmem_limit_bytes=VMEM_LIMIT))(*blocks)


def sibling_swap(blocks, *, name):
    n = len(blocks)

    def body(*refs):
        ins, outs = refs[:n], refs[n:2 * n]
        send_sems, recv_sems = refs[2 * n:]
        x, y, c = _place()
        cps = [pltpu.make_async_remote_copy(src_ref=ins[a].at[1 - c], dst_ref=outs[a], send_sem=send_sems.at[a],
                                            recv_sem=recv_sems.at[a], device_id=(x, y, 1 - c), device_id_type=MESH)
               for a in range(n)]
        for cp in cps:
            cp.start()
        for cp in cps:
            cp.wait()

    spec = pl.BlockSpec(memory_space=pl.ANY)
    return pl.pallas_call(
        body, name=name, in_specs=[spec] * n, out_specs=[spec] * n, out_shape=[_sds(b.shape[1:], b.dtype) for b in blocks],
        scratch_shapes=[pltpu.SemaphoreType.DMA((n,)), pltpu.SemaphoreType.DMA((n,))])(*blocks)


def chip_exchange(blocks, *, name):
    n = len(blocks)

    def body(*refs):
        ins, outs = refs[:n], refs[n:2 * n]
        send_sems, recv_sems, local_sems = refs[2 * n:]
        x, y, c = _place()
        me = 2 * x + y
        chips = _other_chips(x, y)
        mine = [pltpu.make_async_copy(ins[a].at[me], outs[a].at[me], local_sems.at[a]) for a in range(n)]
        for cp in mine:
            cp.start()
        cps = []
        for a in range(n):
            for j, chip in enumerate(chips):
                cps.append(pltpu.make_async_remote_copy(
                    src_ref=ins[a].at[2 * chip[0] + chip[1]], dst_ref=outs[a].at[me], send_sem=send_sems.at[a, j],
                    recv_sem=recv_sems.at[a, j], device_id=(*chip, c), device_id_type=MESH))
        for cp in cps:
            cp.start()
        for a in range(n):
            for j, chip in enumerate(chips):
                s = 2 * chip[0] + chip[1]
                pltpu.make_async_remote_copy(src_ref=ins[a].at[s], dst_ref=outs[a].at[s], send_sem=send_sems.at[a, j],
                                             recv_sem=recv_sems.at[a, j], device_id=(*chip, c), device_id_type=MESH).wait_recv()
        for cp in cps:
            cp.wait_send()
        for cp in mine:
            cp.wait()

    spec = pl.BlockSpec(memory_space=pl.ANY)
    return pl.pallas_call(
        body, name=name, in_specs=[spec] * n, out_specs=[spec] * n, out_shape=[_sds(b.shape, b.dtype) for b in blocks],
        scratch_shapes=[pltpu.SemaphoreType.DMA((n, 3)), pltpu.SemaphoreType.DMA((n, 3)), pltpu.SemaphoreType.DMA((n,))])(*blocks)


def halves_join(halves, *, name):
    n = len(halves)

    def body(*refs):
        ins, outs = refs[:n], refs[n:2 * n]
        send_sems, recv_sems, local_sems = refs[2 * n:]
        x, y, c = _place()
        cps, mine = [], []
        for a in range(n):
            mine.append(pltpu.make_async_copy(ins[a], outs[a].at[c], local_sems.at[a]))
            cps.append(pltpu.make_async_remote_copy(src_ref=ins[a], dst_ref=outs[a].at[c], send_sem=send_sems.at[a],
                                                    recv_sem=recv_sems.at[a], device_id=(x, y, 1 - c), device_id_type=MESH))
        for cp in mine + cps:
            cp.start()
        for a in range(n):
            pltpu.make_async_remote_copy(src_ref=ins[a], dst_ref=outs[a].at[1 - c], send_sem=send_sems.at[a], recv_sem=recv_sems.at[a],
                                         device_id=(x, y, 1 - c), device_id_type=MESH).wait_recv()
        for cp in cps:
            cp.wait_send()
        for cp in mine:
            cp.wait()

    spec = pl.BlockSpec(memory_space=pl.ANY)
    return pl.pallas_call(
        body, name=name, in_specs=[spec] * n, out_specs=[spec] * n,
        out_shape=[_sds((2,) + h.shape, h.dtype) for h in halves],
        scratch_shapes=[pltpu.SemaphoreType.DMA((n,)), pltpu.SemaphoreType.DMA((n,)), pltpu.SemaphoreType.DMA((n,))])(*halves)


BIG = ("w_in", "w_out", "w_ffn_in", "w_ffn_out")
COL_SHARDED = {"w_in": True, "w_out": False, "w_ffn_in": True, "w_ffn_out": False}
SMALL = ("norm_mix_g", "norm_ffn_g", "conv_a_w", "conf_dw_w", "conf_dw_b", "conf_ln_g", "conf_ln_b", "dn_conv_w",
         "dn_a_log", "dn_dt_bias", "dn_norm_g")
SMALL_SHARDED = ("conv_a_w", "conf_dw_w", "dn_conv_w")


def _half_rows(a, c):
    rh = a.shape[1] // 2
    return lax.dynamic_slice_in_dim(a, c * rh, rh, axis=1)


def _assemble(name, g):
    _, L, rh, C = g.shape
    g = g.reshape(4, 2, L, rh, C)
    if COL_SHARDED[name]:
        return g.transpose(2, 1, 3, 0, 4).reshape(L, 2 * rh, 4 * C)
    return g.transpose(2, 0, 1, 3, 4).reshape(L, 8 * rh, C)


def _split_for_reduce(name, g, c):
    L, R, C = g.shape
    if COL_SHARDED[name]:
        t = g.reshape(L, 2, R // 2, 4, C // 4).transpose(1, 3, 0, 2, 4)
    else:
        t = g.reshape(L, 4, 2, R // 8, C).transpose(2, 1, 0, 3, 4)
    mine = lax.dynamic_index_in_dim(t, c, 0, keepdims=False)
    other = lax.dynamic_index_in_dim(t, 1 - c, 0, keepdims=False)
    return mine, other


def _pack(parts):
    flat = jnp.concatenate([p.reshape(-1) for p in parts])
    n = flat.shape[0]
    rows = -(-n // (8 * HD)) * 8
    return jnp.pad(flat, (0, rows * HD - n)).reshape(rows, HD)


def _unpack(buf, shapes):
    flat = buf.reshape(-1)
    out, off = [], 0
    for s in shapes:
        n = 1
        for d in s:
            n *= d
        out.append(flat[off:off + n].reshape(s))
        off += n
    return out


def kernel(x, c, w_ada, b_ada, norm_mix_g, norm_ffn_g, w_in, conv_a_w, conf_dw_w, conf_dw_b, conf_ln_g, conf_ln_b, dn_conv_w, dn_a_log, dn_dt_bias, dn_norm_g, w_out, w_ffn_in, w_ffn_out, final_norm_g, loss_target, m_w_ada, m_b_ada, m_norm_mix_g, m_norm_ffn_g, m_w_in, m_conv_a_w, m_conf_dw_w, m_conf_dw_b, m_conf_ln_g, m_conf_ln_b, m_dn_conv_w, m_dn_a_log, m_dn_dt_bias, m_dn_norm_g, m_w_out, m_w_ffn_in, m_w_ffn_out, m_final_norm_g, v_w_ada, v_b_ada, v_norm_mix_g, v_norm_ffn_g, v_w_in, v_conv_a_w, v_conf_dw_w, v_conf_dw_b, v_conf_ln_g, v_conf_ln_b, v_dn_conv_w, v_dn_a_log, v_dn_dt_bias, v_dn_norm_g, v_w_out, v_w_ffn_in, v_w_ffn_out, v_final_norm_g):
    W = dict(w_ada=w_ada, b_ada=b_ada, norm_mix_g=norm_mix_g, norm_ffn_g=norm_ffn_g, w_in=w_in, conv_a_w=conv_a_w,
             conf_dw_w=conf_dw_w, conf_dw_b=conf_dw_b, conf_ln_g=conf_ln_g, conf_ln_b=conf_ln_b, dn_conv_w=dn_conv_w,
             dn_a_log=dn_a_log, dn_dt_bias=dn_dt_bias, dn_norm_g=dn_norm_g, w_out=w_out, w_ffn_in=w_ffn_in,
             w_ffn_out=w_ffn_out, final_norm_g=final_norm_g)
    M = dict(w_ada=m_w_ada, b_ada=m_b_ada, norm_mix_g=m_norm_mix_g, norm_ffn_g=m_norm_ffn_g, w_in=m_w_in, conv_a_w=m_conv_a_w,
             conf_dw_w=m_conf_dw_w, conf_dw_b=m_conf_dw_b, conf_ln_g=m_conf_ln_g, conf_ln_b=m_conf_ln_b, dn_conv_w=m_dn_conv_w,
             dn_a_log=m_dn_a_log, dn_dt_bias=m_dn_dt_bias, dn_norm_g=m_dn_norm_g, w_out=m_w_out, w_ffn_in=m_w_ffn_in,
             w_ffn_out=m_w_ffn_out, final_norm_g=m_final_norm_g)
    V = dict(w_ada=v_w_ada, b_ada=v_b_ada, norm_mix_g=v_norm_mix_g, norm_ffn_g=v_norm_ffn_g, w_in=v_w_in, conv_a_w=v_conv_a_w,
             conf_dw_w=v_conf_dw_w, conf_dw_b=v_conf_dw_b, conf_ln_g=v_conf_ln_g, conf_ln_b=v_conf_ln_b, dn_conv_w=v_dn_conv_w,
             dn_a_log=v_dn_a_log, dn_dt_bias=v_dn_dt_bias, dn_norm_g=v_dn_norm_g, w_out=v_w_out, w_ffn_in=v_w_ffn_in,
             w_ffn_out=v_w_ffn_out, final_norm_g=v_final_norm_g)
    L = w_ada.shape[0]
    ax, ay, ac = _place()
    chip = 2 * ax + ay
    dev = 4 * ax + 2 * ay + ac

    c_all = allgather8([jnp.pad(c, ((0, 7), (0, 0)))], space=pltpu.VMEM, name="gather_c")[0][:, 0, :]
    b_my = lax.dynamic_slice_in_dim(b_ada, chip * ADA_SH, ADA_SH, axis=1)[:, None, :]
    mod_sh = mod_fwd(c_all, w_ada, b_my, name="mod_fwd")
    lh = L // 2
    mod_g = allgather8([lax.dynamic_slice_in_dim(mod_sh, ac * lh, lh, axis=0).reshape(lh * 8, ADA_SH)], space=pltpu.VMEM,
                       name="gather_mod")[0]
    mod_all = mod_g.reshape(4, 2, lh, 8, ADA_SH).transpose(1, 2, 3, 0, 4).reshape(L, 8, 6 * D)
    mod = lax.dynamic_index_in_dim(mod_all, dev, 1, keepdims=False)

    gathered = allgather8([_half_rows(W[n], ac).astype(BF16) for n in BIG], space=pl.ANY, name="gather_w")
    full = {n: _assemble(n, g) for n, g in zip(BIG, gathered)}
    full["w_in"] = jnp.pad(full["w_in"], ((0, 0), (0, 0), (0, NP - IN_COLS)))

    p_full = dict(W)
    sm = allgather8([_pack([W[n] for n in SMALL_SHARDED])], space=pltpu.VMEM, name="gather_convw")[0]
    per_chip = [_unpack(sm[4 * (s // 2) + 2 * (s % 2)], [W[n].shape for n in SMALL_SHARDED]) for s in range(4)]
    for i, n in enumerate(SMALL_SHARDED):
        p_full[n] = jnp.concatenate([per_chip[s][i] for s in range(4)], axis=-1)

    xs = x[0]
    saves, lps = [], []
    for l in range(L):
        lp = layer_params(l, mod, p_full)
        xs, s = layer_fwd(l, xs, lp, full["w_in"], full["w_out"], full["w_ffn_in"], full["w_ffn_out"], wl=l)
        saves.append(s)
        lps.append(lp)
    loss_p, dx, dgf = loss_head(xs, loss_target[0], final_norm_g[None, :], name="loss_head")
    grads = [None] * L
    gacc = None
    for l in reversed(range(L)):
        dx, grads[l] = layer_bwd(l, dx, saves[l], lps[l], full["w_in"], full["w_out"], full["w_ffn_in"], full["w_ffn_out"],
                                 wl=l, L=L, gacc=gacc)
        gacc = grads[l]
    loss = lax.psum(loss_p[0, 0], ("x", "y", "c"))
    grad_x = dx[None]

    small_shapes = [(L,) + p_full[n].shape[1:] for n in SMALL]
    parts = [jnp.stack([grads[l][n].reshape(sh[1:]) for l in range(L)]) for n, sh in zip(SMALL, small_shapes)]
    parts += [dgf.reshape(D), jnp.concatenate([grads[l]["mod"] for l in range(L)], axis=0)]
    small_shapes += [(D,), (L, 6 * D)]
    packed = _pack(parts)
    gathered_small = allgather8([packed], space=pltpu.VMEM, name="gather_small")[0]
    summed = sum_slots(gathered_small[:, None], name="sum_small")[0]
    g_small = dict(zip(SMALL + ("final_norm_g", "b_ada"), _unpack(summed, small_shapes)))
    for n in SMALL_SHARDED:
        sw = W[n].shape[-1]
        g_small[n] = lax.dynamic_slice_in_dim(g_small[n], chip * sw, sw, axis=g_small[n].ndim - 1)
    dmod_all = jnp.stack([_unpack(gathered_small[d], small_shapes)[-1] for d in range(8)], axis=1)
    dmod_my = lax.dynamic_slice_in_dim(dmod_all, chip * ADA_SH, ADA_SH, axis=2)
    g_w_ada = wada_grad(c_all, dmod_my, name="wada_grad")

    split = {n: grads[0][n] for n in BIG if n != "w_in"}
    g_in = jnp.stack([grads[l]["w_in"] for l in range(L)])[..., :IN_COLS]
    split["w_in"] = g_in.reshape(L, 2, D // 2, 4, IN_COLS // 4).transpose(1, 3, 0, 2, 4)
    core = ac.astype(jnp.int32).reshape(1)
    from_sib = sibling_swap([split[n] for n in BIG], name="reduce_sibling")
    chip_sum = [add_half_bf16(split[n].reshape((2, -1) + r.shape[-2:]), r.reshape((-1,) + r.shape[-2:]), core,
                              name=f"reduce_add2_{n}").reshape(r.shape) for n, r in zip(BIG, from_sib)]
    from_chips = chip_exchange(chip_sum, name="reduce_chips")
    halves = [sum_slots(a, name=f"reduce_add4_{n}") for n, a in zip(BIG, from_chips)]
    g_big = dict(zip(BIG, halves_join(halves, name="reduce_join")))

    out_g, out_d, out_m, out_v = {}, {}, {}, {}
    out_g["w_ada"] = g_w_ada
    out_d["w_ada"], out_m["w_ada"], out_v["w_ada"] = adamw(W["w_ada"], g_w_ada, M["w_ada"], V["w_ada"], name="adamw_w_ada")
    for n in BIG:
        out_g[n], out_d[n], out_m[n], out_v[n] = adamw_halves(W[n], g_big[n], M[n], V[n], name=f"adamw_{n}")
    names_small = SMALL + ("final_norm_g", "b_ada")
    pk = lambda d: _pack([d[n] for n in names_small])[None]
    d_s, m_s, v_s = ew_call(_adamw_math, [pk(W), pk(g_small), pk(M), pk(V)], 3, name="adamw_small")
    shapes_s = [W[n].shape for n in names_small]
    for d, o in ((out_d, d_s), (out_m, m_s), (out_v, v_s)):
        d.update(zip(names_small, _unpack(o[0], shapes_s)))
    for n in names_small:
        out_g[n] = g_small[n].reshape(W[n].shape)

    order = ("w_ada", "b_ada", "norm_mix_g", "norm_ffn_g", "w_in", "conv_a_w", "conf_dw_w", "conf_dw_b", "conf_ln_g", "conf_ln_b",
             "dn_conv_w", "dn_a_log", "dn_dt_bias", "dn_norm_g", "w_out", "w_ffn_in", "w_ffn_out", "final_norm_g")
    return (loss, grad_x, *[out_g[n] for n in order], *[out_d[n] for n in order], *[out_m[n] for n in order],
            *[out_v[n] for n in order])
```

```python
import functools

import jax
import jax.numpy as jnp
from jax import lax
from jax.experimental import pallas as pl
from jax.experimental.pallas import tpu as pltpu

F32 = jnp.float32
BF16 = jnp.bfloat16
HI = lax.Precision.HIGHEST
MESH = pl.DeviceIdType.MESH

D = 1024
DEPTH = 4
DC = 256
DDN = 512
NH = 4
HD = 128
CH = 64
DFF = 2816
IN_COLS = 3336
NP = 3456
KA, KB, KC = 3, 31, 4
HALO = 32
EPS = 1e-6
O_AB, O_AC, O_AV, O_BA, O_BG, O_Q, O_K, O_V, O_Z, O_GB = 0, 256, 512, 768, 1024, 1280, 1792, 2304, 2816, 3328

ADAM_LR, ADAM_B1, ADAM_B2, ADAM_EPS, ADAM_WD, ADAM_STEP = 0.001, 0.9, 0.999, 1e-08, 0.01, 10

VMEM_LIMIT = 56 * 1024 * 1024


def _cp(n_grid):
    return pltpu.CompilerParams(dimension_semantics=("arbitrary",) * n_grid, vmem_limit_bytes=VMEM_LIMIT)


def _sds(shape, dtype=F32):
    return jax.ShapeDtypeStruct(tuple(shape), dtype)


def _dot(a, b, prec=None):
    return jnp.dot(a, b, preferred_element_type=F32, precision=prec)


def _dot_nt(a, b, prec=None):
    return lax.dot_general(a, b, (((1,), (1,)), ((), ())), preferred_element_type=F32, precision=prec)


def _dot_tn(a, b, prec=None):
    return lax.dot_general(a, b, (((0,), (0,)), ((), ())), preferred_element_type=F32, precision=prec)


def _split_bf16(a):
    hi = a.astype(BF16)
    return hi, (a - hi.astype(F32)).astype(BF16)


_DIMS = {"nn": (((1,), (0,)), ((), ())), "nt": (((1,), (1,)), ((), ())), "tn": (((0,), (0,)), ((), ()))}
_DIMS_BATCHED = {"nn": (((2,), (1,)), ((0,), (0,))), "nt": (((2,), (2,)), ((0,), (0,))), "tn": (((1,), (1,)), ((0,), (0,)))}


def _mm3_raw(a, b, kind):
    ah, al = _split_bf16(a)
    bh, bl = _split_bf16(b)
    d = (_DIMS if a.ndim == 2 else _DIMS_BATCHED)[kind]
    dg = lambda u, v: lax.dot_general(u, v, d, preferred_element_type=F32)
    return dg(ah, bh) + (dg(ah, bl) + dg(al, bh))


@jax.custom_vjp
def _mm_nn(a, b):
    return _mm3_raw(a, b, "nn")


@jax.custom_vjp
def _mm_nt(a, b):
    return _mm3_raw(a, b, "nt")


@jax.custom_vjp
def _mm_tn(a, b):
    return _mm3_raw(a, b, "tn")


_mm_nn.defvjp(lambda a, b: (_mm_nn(a, b), (a, b)), lambda r, g: (_mm_nt(g, r[1]), _mm_tn(r[0], g)))
_mm_nt.defvjp(lambda a, b: (_mm_nt(a, b), (a, b)), lambda r, g: (_mm_nn(g, r[1]), _mm_tn(g, r[0])))
_mm_tn.defvjp(lambda a, b: (_mm_tn(a, b), (a, b)), lambda r, g: (_mm_nt(r[1], g), _mm_nn(r[0], g)))


@jax.custom_vjp
def _inv_given(x, p):
    return p


_inv_given.defvjp(lambda x, p: (p, p), lambda p, g: (_mm_tn(p, _mm_nt(g, p)), jnp.zeros_like(p)))


def _silu(x):
    return x * jax.nn.sigmoid(x)


def _colsum(x):
    return jnp.sum(x, axis=0, keepdims=True)


TILE_CAP = 512


def _w3(w):
    return w if w.ndim == 3 else w[None]


def _wspec(l, block, index):
    return pl.BlockSpec((None,) + block, lambda *g: (l,) + index(*g))


def _tile(T, want):
    t = min(T, want, TILE_CAP)
    assert T % t == 0
    return t


def _normmod(x, gn, sc, sh):
    r = lax.rsqrt(jnp.mean(x * x, axis=-1, keepdims=True) + EPS)
    return ((x * r) * gn) * (1.0 + sc) + sh


def _rms(x, g):
    return (x * lax.rsqrt(jnp.mean(x * x, axis=-1, keepdims=True) + EPS)) * g


def _mix_b_post(u, ln_g, ln_b):
    mu = jnp.mean(u, axis=-1, keepdims=True)
    var = jnp.mean(jnp.square(u - mu), axis=-1, keepdims=True)
    return _silu(((u - mu) * lax.rsqrt(var + 1e-5)) * ln_g + ln_b)


def _softplus(z):
    return jnp.where(z > 0, z, 0.0) + jnp.log(1.0 + jnp.exp(-jnp.where(z > 0, z, -z)))


def _chunk_tril(tt):
    r = lax.broadcasted_iota(jnp.int32, (tt, tt), 0)
    c = lax.broadcasted_iota(jnp.int32, (tt, tt), 1)
    return ((r // CH == c // CH) & (c <= r)).astype(F32)


def _eye8():
    return (lax.broadcasted_iota(jnp.int32, (8, HD), 0) == lax.broadcasted_iota(jnp.int32, (8, HD), 1)).astype(F32)


def _dn_post(pre_q, pre_k, pre_v, blk, alog_row, dt_row):
    q = [s * lax.rsqrt(jnp.sum(s * s, -1, keepdims=True) + EPS) * (HD ** -0.5) for s in map(_silu, pre_q)]
    k = [s * lax.rsqrt(jnp.sum(s * s, -1, keepdims=True) + EPS) for s in map(_silu, pre_k)]
    v = [_silu(p) for p in pre_v]
    lane = lax.broadcasted_iota(jnp.int32, (1, HD), 1)
    g = -jnp.exp(alog_row) * _softplus(blk + dt_row)
    beta = jax.nn.sigmoid(blk)
    gc = _dot(_chunk_tril(blk.shape[0]), jnp.where(lane < NH, g, 0.0), HI)
    gb = jnp.where(lane < NH, gc, jnp.where(lane < 2 * NH, beta, 0.0))
    grow = _dot_nt(_eye8(), gc, HI)
    return q, k, v, gb, grow


def _dn_intra(q, k, v, beta, gcol, grow, p_known=None):
    r = lax.broadcasted_iota(jnp.int32, (CH, CH), 0)
    c = lax.broadcasted_iota(jnp.int32, (CH, CH), 1)
    causal, strict = c <= r, c < r
    decay = jnp.where(causal, jnp.exp(jnp.where(causal, gcol - grow, 0.0)), 0.0)
    kb = k * beta
    x = -jnp.where(strict, _mm_nt(kb, k) * decay, 0.0)
    if p_known is None:
        p = (r == c).astype(F32) + x
        y = x
        for _ in range(5):
            y = _mm_nn(y, y)
            p = p + _mm_nn(p, y)
    else:
        p = _inv_given(x, p_known)
    u = _mm_nn(p, v * beta)
    w = _mm_nn(p, kb * jnp.exp(gcol))
    qk = jnp.where(causal, _mm_nt(q, k) * decay, 0.0)
    return u, w, qk, p


def _dn_inter(s, q, k, u, w, gcol, qk):
    last = (lax.broadcasted_iota(jnp.int32, (CH, 1), 0) == CH - 1).astype(F32)
    g_last = jnp.sum(gcol * last, axis=1, keepdims=True)
    v_new = u - _mm_nn(w, s)
    o = _mm_nn(q * jnp.exp(gcol), s) + _mm_nn(qk, v_new)
    s_new = s * jnp.exp(g_last) + _mm_tn(k * jnp.exp(g_last - gcol), v_new)
    return s_new, o


def _yc(o, z, gdn):
    return _rms(o, gdn) * _silu(z)


def nm_fwd(x, gn, sc, sh, w, *, tn, name, wl=0):
    T, N = x.shape[0], w.shape[-1]
    tt = _tile(T, 512)

    def body(x_ref, gn_ref, sc_ref, sh_ref, w_ref, h_ref, o_ref):
        @pl.when(pl.program_id(1) == 0)
        def _():
            h_ref[...] = _normmod(x_ref[...], gn_ref[...], sc_ref[...], sh_ref[...]).astype(BF16)
        o_ref[...] = _dot(h_ref[...], w_ref[...])

    row = pl.BlockSpec((1, D), lambda i, j: (0, 0))
    return pl.pallas_call(
        body, grid=(T // tt, N // tn), name=name,
        in_specs=[pl.BlockSpec((tt, D), lambda i, j: (i, 0)), row, row, row, _wspec(wl, (D, tn), lambda i, j: (0, j))],
        out_specs=[pl.BlockSpec((tt, D), lambda i, j: (i, 0)), pl.BlockSpec((tt, tn), lambda i, j: (i, j))],
        out_shape=[_sds((T, D), BF16), _sds((T, N))], compiler_params=_cp(2))(x, gn, sc, sh, _w3(w))


def nm_bwd(dys, w, x, gn, sc, sh, dres, *, name, wl=0):
    T = x.shape[0]
    N = w.shape[-1]
    tt = _tile(T, 256)
    widths = [a.shape[1] for a in dys]
    assert sum(widths) == N
    n = len(dys)

    def body(*refs):
        dy_refs, (w_ref, x_ref, gn_ref, sc_ref, sh_ref, dres_ref) = refs[:n], refs[n:n + 6]
        dx_ref, dgn_ref, dsc_ref, dsh_ref, dyb_ref = refs[n + 6:]
        i = pl.program_id(0)
        dyb = jnp.concatenate([r[...].astype(BF16) for r in dy_refs], axis=1) if n > 1 else dy_refs[0][...].astype(BF16)
        dyb_ref[...] = dyb
        dh = _dot_nt(dyb, w_ref[...])
        _, vjp = jax.vjp(_normmod, x_ref[...], gn_ref[...], sc_ref[...], sh_ref[...])
        dx, dgn, dsc, dsh = vjp(dh)
        dx_ref[...] = dres_ref[...] + dx

        @pl.when(i == 0)
        def _():
            dgn_ref[...] = jnp.zeros_like(dgn_ref)
            dsc_ref[...] = jnp.zeros_like(dsc_ref)
            dsh_ref[...] = jnp.zeros_like(dsh_ref)
        dgn_ref[...] += dgn
        dsc_ref[...] += dsc
        dsh_ref[...] += dsh

    row = pl.BlockSpec((1, D), lambda i: (0, 0))
    tile = pl.BlockSpec((tt, D), lambda i: (i, 0))
    return pl.pallas_call(
        body, grid=(T // tt,), name=name,
        in_specs=[pl.BlockSpec((tt, wd), lambda i: (i, 0)) for wd in widths]
        + [_wspec(wl, (D, N), lambda i: (0, 0)), tile, row, row, row, tile],
        out_specs=[tile, row, row, row, pl.BlockSpec((tt, N), lambda i: (i, 0))],
        out_shape=[_sds((T, D)), _sds((1, D)), _sds((1, D)), _sds((1, D)), _sds((T, N), BF16)],
        compiler_params=_cp(1))(*dys, _w3(w), x, gn, sc, sh, dres)


def mm_tn(a, b, *, tn, name):
    T, K = a.shape
    N = b.shape[1]
    tt = _tile(T, 512)

    def body(a_ref, b_ref, o_ref):
        @pl.when(pl.program_id(1) == 0)
        def _():
            o_ref[...] = jnp.zeros_like(o_ref)
        o_ref[...] += _dot_tn(a_ref[...], b_ref[...])

    return pl.pallas_call(
        body, grid=(N // tn, T // tt), name=name,
        in_specs=[pl.BlockSpec((tt, K), lambda j, t: (t, 0)), pl.BlockSpec((tt, tn), lambda j, t: (t, j))],
        out_specs=pl.BlockSpec((K, tn), lambda j, t: (0, j)), out_shape=_sds((K, N)), compiler_params=_cp(2))(a, b)


def mm_tn_split(a, b, *, by_cols, l, L, prev, name):
    T, K = a.shape
    N = b.shape[1]
    tt = _tile(T, 512)
    tn = N // 4 if by_cols else N // 2
    rh, C = (K // 2, N // 4) if by_cols else (K // 8, N)
    nt = T // tt

    def body(a_ref, b_ref, *rest):
        o_ref, acc = rest[-2:]
        t = pl.program_id(1)

        @pl.when(t == 0)
        def _():
            acc[...] = jnp.zeros_like(acc)
        acc[...] += _dot_tn(a_ref[...], b_ref[...])

        @pl.when(t == nt - 1)
        def _():
            if by_cols:
                for h in range(2):
                    o_ref[h] = acc[pl.ds(h * rh, rh), :]
            else:
                for s in range(4):
                    for h in range(2):
                        o_ref[h, s] = acc[pl.ds((2 * s + h) * rh, rh), :]

    if by_cols:
        out_spec = pl.BlockSpec((2, None, None, rh, tn), lambda j, t: (0, j, l, 0, 0))
    else:
        out_spec = pl.BlockSpec((2, 4, None, rh, tn), lambda j, t: (0, 0, l, 0, j))
    in_specs = [pl.BlockSpec((tt, K), lambda j, t: (t, 0)), pl.BlockSpec((tt, tn), lambda j, t: (t, j))]
    args = [a, b]
    aliases = {}
    if prev is not None:
        in_specs.append(pl.BlockSpec(memory_space=pl.ANY))
        args.append(prev)
        aliases = {2: 0}
    return pl.pallas_call(
        body, grid=(N // tn, nt), name=name, in_specs=in_specs, out_specs=out_spec, out_shape=_sds((2, 4, L, rh, C)),
        scratch_shapes=[pltpu.VMEM((K, tn), F32)], input_output_aliases=aliases, compiler_params=_cp(2))(*args)


def _fill_pad(pad_ref, prev, cur, first):
    pad_ref[pl.ds(0, HALO), :] = jnp.where(first, 0.0, prev)
    pad_ref[pl.ds(HALO, cur.shape[0]), :] = cur


def _causal_conv(pad_ref, w, K, tt):
    acc = None
    for k in range(K):
        term = pad_ref[pl.ds(HALO - (K - 1) + k, tt), :] * w[k:k + 1, :]
        acc = term if acc is None else acc + term
    return acc


def _conv_inputs(pc, pp, first, pad_a, pad_b, pad_c):
    def s_of(p):
        return p[:, O_AC:O_AV] * p[:, O_AV:O_BA]

    def u0_of(p):
        return p[:, O_BA:O_BG] * jax.nn.sigmoid(p[:, O_BG:O_Q])

    _fill_pad(pad_a, s_of(pp), s_of(pc), first)
    _fill_pad(pad_b, u0_of(pp), u0_of(pc), first)
    _fill_pad(pad_c, pp[:, O_Q:O_Z], pc[:, O_Q:O_Z], first)


def _mix_specs(T, tt):
    cur = pl.BlockSpec((tt, O_Z), lambda i: (i, 0))
    prev = pl.BlockSpec((HALO, O_Z), lambda i: (jnp.maximum(i * (tt // HALO) - 1, 0), 0))
    gbb = pl.BlockSpec((tt, HD), lambda i: (i, O_GB // HD))
    return cur, prev, gbb


def _full(shape):
    return pl.BlockSpec(shape, lambda i: (0,) * len(shape))


def mix_fwd(proj, wa, wb, bb, ln_g, ln_b, wc, alog_row, dt_row, *, name):
    T = proj.shape[0]
    tt = _tile(T, 256)

    def body(pc_ref, pp_ref, blk_ref, wa_ref, wb_ref, bb_ref, lg_ref, lb_ref, wc_ref, al_ref, dt_ref,
             yab_ref, q_ref, k_ref, v_ref, gb_ref, grow_ref, pad_a, pad_b, pad_c):
        first = pl.program_id(0) == 0
        pc = pc_ref[...]
        _conv_inputs(pc, pp_ref[...], first, pad_a, pad_b, pad_c)
        ya = pc[:, O_AB:O_AC] * _causal_conv(pad_a, wa_ref[...], KA, tt)
        yb = _mix_b_post(_causal_conv(pad_b, wb_ref[...], KB, tt) + bb_ref[...], lg_ref[...], lb_ref[...])
        yab_ref[...] = jnp.concatenate([ya, yb], axis=1)
        pre = _causal_conv(pad_c, wc_ref[...], KC, tt)
        blocks = [pre[:, j * HD:(j + 1) * HD] for j in range(3 * NH)]
        q, k, v, gb, grow = _dn_post(blocks[:NH], blocks[NH:2 * NH], blocks[2 * NH:], blk_ref[...], al_ref[...], dt_ref[...])
        q_ref[...] = jnp.concatenate(q, axis=1)
        k_ref[...] = jnp.concatenate(k, axis=1)
        v_ref[...] = jnp.concatenate(v, axis=1)
        gb_ref[...] = gb
        grow_ref[...] = grow

    cur, prev, gbb = _mix_specs(T, tt)
    t512 = pl.BlockSpec((tt, DDN), lambda i: (i, 0))
    return pl.pallas_call(
        body, grid=(T // tt,), name=name,
        in_specs=[cur, prev, gbb, _full((8, DC)), _full((32, DC)), _full((1, DC)), _full((1, DC)), _full((1, DC)),
                  _full((8, 3 * DDN)), _full((1, HD)), _full((1, HD))],
        out_specs=[t512, t512, t512, t512, pl.BlockSpec((tt, HD), lambda i: (i, 0)), pl.BlockSpec((8, tt), lambda i: (0, i))],
        out_shape=[_sds((T, 2 * DC)), _sds((T, DDN)), _sds((T, DDN)), _sds((T, DDN)), _sds((T, HD)), _sds((8, T))],
        scratch_shapes=[pltpu.VMEM((HALO + tt, DC), F32), pltpu.VMEM((HALO + tt, DC), F32), pltpu.VMEM((HALO + tt, 3 * DDN), F32)],
        compiler_params=_cp(1))(proj, proj, proj, wa, wb, bb, ln_g, ln_b, wc, alog_row, dt_row)


def mix_bwd_point(proj, dyab, dq, dk, dv, dgb, dgrow, wa, wb, bb, ln_g, ln_b, wc, alog_row, dt_row, *, name):
    T = proj.shape[0]
    tt = _tile(T, 256)
    CW = 2 * DC + 3 * DDN

    def body(pc_ref, pp_ref, blk_ref, dyab_ref, dq_ref, dk_ref, dv_ref, dgb_ref, dgrow_ref,
             wa_ref, wb_ref, bb_ref, lg_ref, lb_ref, wc_ref, al_ref, dt_ref,
             dab_ref, dconv_ref, dblk_ref, dbb_ref, dlg_ref, dlb_ref, dal_ref, ddt_ref, pad_a, pad_b, pad_c):
        i = pl.program_id(0)
        pc = pc_ref[...]
        _conv_inputs(pc, pp_ref[...], i == 0, pad_a, pad_b, pad_c)
        ca = _causal_conv(pad_a, wa_ref[...], KA, tt)
        u = _causal_conv(pad_b, wb_ref[...], KB, tt) + bb_ref[...]
        pre = _causal_conv(pad_c, wc_ref[...], KC, tt)
        dyab_v = dyab_ref[...]
        dya, dyb = dyab_v[:, :DC], dyab_v[:, DC:]
        dab_ref[...] = dya * ca
        dca = dya * pc[:, O_AB:O_AC]
        _, vjp_b = jax.vjp(_mix_b_post, u, lg_ref[...], lb_ref[...])
        du, dlg, dlb = vjp_b(dyb)
        blocks = [pre[:, j * HD:(j + 1) * HD] for j in range(3 * NH)]
        _, vjp_c = jax.vjp(_dn_post, blocks[:NH], blocks[NH:2 * NH], blocks[2 * NH:], blk_ref[...], al_ref[...], dt_ref[...])

        def heads(r):
            vv = r[...]
            return [vv[:, h * HD:(h + 1) * HD] for h in range(NH)]
        dpq, dpk, dpv, dblk, dal, ddt = vjp_c((heads(dq_ref), heads(dk_ref), heads(dv_ref), dgb_ref[...], dgrow_ref[...]))
        dconv_ref[...] = jnp.concatenate([dca, du] + dpq + dpk + dpv, axis=1)
        dblk_ref[...] = dblk

        @pl.when(i == 0)
        def _():
            for r in (dbb_ref, dlg_ref, dlb_ref, dal_ref, ddt_ref):
                r[...] = jnp.zeros_like(r)
        dbb_ref[...] += _colsum(du)
        dlg_ref[...] += dlg
        dlb_ref[...] += dlb
        dal_ref[...] += dal
        ddt_ref[...] += ddt

    cur, prev, gbb = _mix_specs(T, tt)
    t512 = pl.BlockSpec((tt, DDN), lambda i: (i, 0))
    t128 = pl.BlockSpec((tt, HD), lambda i: (i, 0))
    return pl.pallas_call(
        body, grid=(T // tt,), name=name,
        in_specs=[cur, prev, gbb, t512, t512, t512, t512, t128, pl.BlockSpec((8, tt), lambda i: (0, i)),
                  _full((8, DC)), _full((32, DC)), _full((1, DC)), _full((1, DC)), _full((1, DC)),
                  _full((8, 3 * DDN)), _full((1, HD)), _full((1, HD))],
        out_specs=[pl.BlockSpec((tt, DC), lambda i: (i, 0)), pl.BlockSpec((tt, CW), lambda i: (i, 0)), t128,
                   _full((1, DC)), _full((1, DC)), _full((1, DC)), _full((1, HD)), _full((1, HD))],
        out_shape=[_sds((T, DC)), _sds((T, CW)), _sds((T, HD)), _sds((1, DC)), _sds((1, DC)), _sds((1, DC)),
                   _sds((1, HD)), _sds((1, HD))],
        scratch_shapes=[pltpu.VMEM((HALO + tt, DC), F32), pltpu.VMEM((HALO + tt, DC), F32), pltpu.VMEM((HALO + tt, 3 * DDN), F32)],
        compiler_params=_cp(1))(proj, proj, proj, dyab, dq, dk, dv, dgb, dgrow, wa, wb, bb, ln_g, ln_b, wc, alog_row, dt_row)


def mix_bwd_conv(proj, dconv, wa, wb, wc, *, name):
    T = proj.shape[0]
    tt = _tile(T, 256)
    CW = 2 * DC + 3 * DDN
    nblk = T // HALO

    def body(pc_ref, pp_ref, dc_ref, dn_ref, wa_ref, wb_ref, wc_ref, dp_ref, dwa_ref, dwb_ref, dwc_ref,
             pad_a, pad_b, pad_c, dpad):
        i = pl.program_id(0)
        last = i == pl.num_programs(0) - 1
        pc = pc_ref[...]
        _conv_inputs(pc, pp_ref[...], i == 0, pad_a, pad_b, pad_c)
        dcur = dc_ref[...]
        dpad[pl.ds(0, tt), :] = dcur
        dpad[pl.ds(tt, HALO), :] = jnp.where(last, 0.0, dn_ref[...])

        @pl.when(i == 0)
        def _():
            for r in (dwa_ref, dwb_ref, dwc_ref):
                r[...] = jnp.zeros_like(r)

        def tconv(lo, hi, w, K, pad_ref, dw_ref):
            dy = dcur[:, lo:hi]
            acc = None
            for k in range(K):
                term = dpad[pl.ds(K - 1 - k, tt), lo:hi] * w[k:k + 1, :]
                acc = term if acc is None else acc + term
                dw_ref[pl.ds(k, 1), :] += _colsum(dy * pad_ref[pl.ds(HALO - (K - 1) + k, tt), :])
            return acc

        ds = tconv(0, DC, wa_ref[...], KA, pad_a, dwa_ref)
        du0 = tconv(DC, 2 * DC, wb_ref[...], KB, pad_b, dwb_ref)
        dqkv = tconv(2 * DC, CW, wc_ref[...], KC, pad_c, dwc_ref)
        a_c, a_v, b_a, b_g = pc[:, O_AC:O_AV], pc[:, O_AV:O_BA], pc[:, O_BA:O_BG], pc[:, O_BG:O_Q]
        sg = jax.nn.sigmoid(b_g)
        dp_ref[...] = jnp.concatenate([ds * a_v, ds * a_c, du0 * sg, du0 * b_a * sg * (1.0 - sg), dqkv], axis=1)

    cur, prev, _ = _mix_specs(T, tt)
    return pl.pallas_call(
        body, grid=(T // tt,), name=name,
        in_specs=[cur, prev, pl.BlockSpec((tt, CW), lambda i: (i, 0)),
                  pl.BlockSpec((HALO, CW), lambda i: (jnp.minimum((i + 1) * (tt // HALO), nblk - 1), 0)),
                  _full((8, DC)), _full((32, DC)), _full((8, 3 * DDN))],
        out_specs=[pl.BlockSpec((tt, O_Z - O_AC), lambda i: (i, 0)), _full((8, DC)), _full((32, DC)), _full((8, 3 * DDN))],
        out_shape=[_sds((T, O_Z - O_AC)), _sds((8, DC)), _sds((32, DC)), _sds((8, 3 * DDN))],
        scratch_shapes=[pltpu.VMEM((HALO + tt, DC), F32), pltpu.VMEM((HALO + tt, DC), F32), pltpu.VMEM((HALO + tt, 3 * DDN), F32),
                        pltpu.VMEM((tt + HALO, CW), F32)],
        compiler_params=_cp(1))(proj, proj, dconv, dconv, wa, wb, wc)


def _head(v, h):
    return v[:, h * HD:(h + 1) * HD]


def _to_batch(v, cb):
    return jnp.stack([v[c * CH:(c + 1) * CH, h * HD:(h + 1) * HD] for c in range(cb) for h in range(NH)])


def _from_batch(b, cb):
    return jnp.concatenate([jnp.concatenate([b[c * NH + h] for h in range(NH)], axis=1) for c in range(cb)], axis=0)


def _lanes_to_batch(v, cb, lane0):
    return jnp.stack([v[c * CH:(c + 1) * CH, lane0 + h:lane0 + h + 1] for c in range(cb) for h in range(NH)])


def _lane_onehot(h):
    return (lax.broadcasted_iota(jnp.int32, (1, HD), 1) == h).astype(F32)


def _batch_to_lanes(b, cb, lane0):
    return jnp.concatenate([sum(b[c * NH + h] * _lane_onehot(lane0 + h) for h in range(NH)) for c in range(cb)], axis=0)


def dn_intra_fwd(q, k, v, gb, grow3, *, name):
    T = q.shape[0]
    N = T // CH
    cb = 4 if N % 4 == 0 else 1

    def body(q_ref, k_ref, v_ref, gb_ref, gr_ref, u_ref, w_ref, qk_ref, p_ref):
        gbv = gb_ref[...]
        grow = jnp.stack([gr_ref[c, h:h + 1, :] for c in range(cb) for h in range(NH)])
        u, w, qk, p = _dn_intra(_to_batch(q_ref[...], cb), _to_batch(k_ref[...], cb), _to_batch(v_ref[...], cb),
                                _lanes_to_batch(gbv, cb, NH), _lanes_to_batch(gbv, cb, 0), grow)
        u_ref[...] = _from_batch(u, cb)
        w_ref[...] = _from_batch(w, cb)
        qk_ref[...] = qk.reshape(cb, NH, CH, CH)
        p_ref[...] = p.reshape(cb, NH, CH, CH)

    t512 = pl.BlockSpec((cb * CH, DDN), lambda i: (i, 0))
    sq = pl.BlockSpec((cb, NH, CH, CH), lambda i: (i, 0, 0, 0))
    return pl.pallas_call(
        body, grid=(N // cb,), name=name,
        in_specs=[t512, t512, t512, pl.BlockSpec((cb * CH, HD), lambda i: (i, 0)), pl.BlockSpec((cb, 8, CH), lambda i: (i, 0, 0))],
        out_specs=[t512, t512, sq, sq],
        out_shape=[_sds((T, DDN)), _sds((T, DDN)), _sds((N, NH, CH, CH)), _sds((N, NH, CH, CH))],
        compiler_params=_cp(1))(q, k, v, gb, grow3)


def dn_inter_fwd(q, k, u, w, gb, qk, *, name):
    T = q.shape[0]
    N = T // CH
    cb = 4 if N % 4 == 0 else 1

    def body(q_ref, k_ref, u_ref, w_ref, gb_ref, qk_ref, o_ref, s_ref, state):
        @pl.when(pl.program_id(0) == 0)
        def _():
            state[...] = jnp.zeros_like(state)
        for c in range(cb):
            rows = pl.ds(c * CH, CH)
            s = state[...]
            s_ref[c] = s
            s_new, o = _dn_inter(s, _to_batch(q_ref[rows, :], 1), _to_batch(k_ref[rows, :], 1), _to_batch(u_ref[rows, :], 1),
                                 _to_batch(w_ref[rows, :], 1), _lanes_to_batch(gb_ref[rows, :], 1, 0), qk_ref[c])
            state[...] = s_new
            o_ref[rows, :] = _from_batch(o, 1)

    t512 = pl.BlockSpec((cb * CH, DDN), lambda i: (i, 0))
    return pl.pallas_call(
        body, grid=(N // cb,), name=name,
        in_specs=[t512, t512, t512, t512, pl.BlockSpec((cb * CH, HD), lambda i: (i, 0)),
                  pl.BlockSpec((cb, NH, CH, CH), lambda i: (i, 0, 0, 0))],
        out_specs=[t512, pl.BlockSpec((cb, NH, HD, HD), lambda i: (i, 0, 0, 0))],
        out_shape=[_sds((T, DDN)), _sds((N, NH, HD, HD))],
        scratch_shapes=[pltpu.VMEM((NH, HD, HD), F32)], compiler_params=_cp(1))(q, k, u, w, gb, qk)


def dn_inter_bwd(do, q, k, u, w, gb, qk, s_all, *, name):
    T = q.shape[0]
    N = T // CH
    cb = 4 if N % 4 == 0 else 1
    G = N // cb

    def body(do_ref, q_ref, k_ref, u_ref, w_ref, gb_ref, qk_ref, s_ref, dq_ref, dk_ref, du_ref, dw_ref, dg_ref, dqk_ref, dstate):
        @pl.when(pl.program_id(0) == 0)
        def _():
            dstate[...] = jnp.zeros_like(dstate)
        for c in reversed(range(cb)):
            rows = pl.ds(c * CH, CH)
            _, vjp = jax.vjp(_dn_inter, s_ref[c], _to_batch(q_ref[rows, :], 1), _to_batch(k_ref[rows, :], 1),
                             _to_batch(u_ref[rows, :], 1), _to_batch(w_ref[rows, :], 1), _lanes_to_batch(gb_ref[rows, :], 1, 0),
                             qk_ref[c])
            ds, dq, dk, du, dw, dgc, dqk = vjp((dstate[...], _to_batch(do_ref[rows, :], 1)))
            dstate[...] = ds
            dq_ref[rows, :] = _from_batch(dq, 1)
            dk_ref[rows, :] = _from_batch(dk, 1)
            du_ref[rows, :] = _from_batch(du, 1)
            dw_ref[rows, :] = _from_batch(dw, 1)
            dg_ref[rows, :] = _batch_to_lanes(dgc, 1, 0)
            dqk_ref[c] = dqk

    t512 = pl.BlockSpec((cb * CH, DDN), lambda i: (G - 1 - i, 0))
    t128 = pl.BlockSpec((cb * CH, HD), lambda i: (G - 1 - i, 0))
    qkb = pl.BlockSpec((cb, NH, CH, CH), lambda i: (G - 1 - i, 0, 0, 0))
    return pl.pallas_call(
        body, grid=(G,), name=name,
        in_specs=[t512, t512, t512, t512, t512, t128, qkb, pl.BlockSpec((cb, NH, HD, HD), lambda i: (G - 1 - i, 0, 0, 0))],
        out_specs=[t512, t512, t512, t512, t128, qkb],
        out_shape=[_sds((T, DDN))] * 4 + [_sds((T, HD)), _sds((N, NH, CH, CH))],
        scratch_shapes=[pltpu.VMEM((NH, HD, HD), F32)], compiler_params=_cp(1))(do, q, k, u, w, gb, qk, s_all)


def dn_intra_bwd(du, dw, dqk, dq_in, dk_in, dg_in, q, k, v, gb, grow3, p_all, *, name):
    T = q.shape[0]
    N = T // CH
    cb = 4 if N % 4 == 0 else 1

    def body(du_ref, dw_ref, dqk_ref, dqi_ref, dki_ref, dgi_ref, q_ref, k_ref, v_ref, gb_ref, gr_ref, p_ref,
             dq_ref, dk_ref, dv_ref, dgb_ref, dgr_ref):
        B = cb * NH
        gbv = gb_ref[...]
        grow = jnp.stack([gr_ref[c, h:h + 1, :] for c in range(cb) for h in range(NH)])
        _, vjp = jax.vjp(functools.partial(_dn_intra, p_known=p_ref[...].reshape(B, CH, CH)),
                         _to_batch(q_ref[...], cb), _to_batch(k_ref[...], cb), _to_batch(v_ref[...], cb),
                         _lanes_to_batch(gbv, cb, NH), _lanes_to_batch(gbv, cb, 0), grow)
        dq, dk, dv, dbeta, dgc, dgr = vjp((_to_batch(du_ref[...], cb), _to_batch(dw_ref[...], cb), dqk_ref[...].reshape(B, CH, CH),
                                           jnp.zeros((B, CH, CH), F32)))
        dq_ref[...] = dqi_ref[...] + _from_batch(dq, cb)
        dk_ref[...] = dki_ref[...] + _from_batch(dk, cb)
        dv_ref[...] = _from_batch(dv, cb)
        dgb_ref[...] = dgi_ref[...] + _batch_to_lanes(dgc, cb, 0) + _batch_to_lanes(dbeta, cb, NH)
        for c in range(cb):
            dgr_ref[c] = jnp.concatenate([dgr[c * NH + h] for h in range(NH)] + [jnp.zeros((8 - NH, CH), F32)], axis=0)

    t512 = pl.BlockSpec((cb * CH, DDN), lambda i: (i, 0))
    t128 = pl.BlockSpec((cb * CH, HD), lambda i: (i, 0))
    qkb = pl.BlockSpec((cb, NH, CH, CH), lambda i: (i, 0, 0, 0))
    grb = pl.BlockSpec((cb, 8, CH), lambda i: (i, 0, 0))
    return pl.pallas_call(
        body, grid=(N // cb,), name=name,
        in_specs=[t512, t512, qkb, t512, t512, t128, t512, t512, t512, t128, grb, qkb],
        out_specs=[t512, t512, t512, t128, grb],
        out_shape=[_sds((T, DDN))] * 3 + [_sds((T, HD)), _sds((N, 8, CH))],
        compiler_params=_cp(1))(du, dw, dqk, dq_in, dk_in, dg_in, q, k, v, gb, grow3, p_all)


def _z_specs(tt):
    return [pl.BlockSpec((tt, DC), lambda i: (i, O_Z // DC)), pl.BlockSpec((tt, DC), lambda i: (i, O_Z // DC + 1))]


def mixout_fwd(x, yab, o, proj, gdn, g1, w_out, *, name, wl=0):
    T = x.shape[0]
    tt = _tile(T, 512)

    def body(x_ref, yab_ref, o_ref, z0_ref, z1_ref, gdn_ref, g1_ref, w_ref, ycat_ref, mix_ref, xo_ref):
        ov = o_ref[...]
        z = jnp.concatenate([z0_ref[...], z1_ref[...]], axis=1)
        yc = [_yc(_head(ov, h), _head(z, h), gdn_ref[...]) for h in range(NH)]
        ycat = jnp.concatenate([yab_ref[...]] + yc, axis=1).astype(BF16)
        ycat_ref[...] = ycat
        mix = _dot(ycat, w_ref[...])
        mix_ref[...] = mix
        xo_ref[...] = x_ref[...] + g1_ref[...] * mix

    tile = pl.BlockSpec((tt, D), lambda i: (i, 0))
    t512 = pl.BlockSpec((tt, DDN), lambda i: (i, 0))
    return pl.pallas_call(
        body, grid=(T // tt,), name=name,
        in_specs=[tile, t512, t512] + _z_specs(tt) + [_full((1, HD)), _full((1, D)), _wspec(wl, (D, D), lambda i: (0, 0))],
        out_specs=[tile, tile, tile], out_shape=[_sds((T, D), BF16), _sds((T, D)), _sds((T, D))],
        compiler_params=_cp(1))(x, yab, o, proj, proj, gdn, g1, _w3(w_out))


def mixout_bwd(dx, mix, o, proj, gdn, g1, w_out, *, name, wl=0):
    T = dx.shape[0]
    tt = _tile(T, 256)

    def body(dx_ref, mix_ref, o_ref, z0_ref, z1_ref, gdn_ref, g1_ref, w_ref, dmix_ref, dyab_ref, do_ref, dz_ref, dg1_ref, dgdn_ref):
        i = pl.program_id(0)
        dxv = dx_ref[...]
        dmix = (dxv * g1_ref[...]).astype(BF16)
        dmix_ref[...] = dmix
        dycat = _dot_nt(dmix, w_ref[...])
        dyab_ref[...] = dycat[:, :2 * DC]
        ov = o_ref[...]
        z = jnp.concatenate([z0_ref[...], z1_ref[...]], axis=1)
        dos, dzs = [], []
        dgdn = jnp.zeros((1, HD), F32)
        for h in range(NH):
            _, vjp = jax.vjp(_yc, _head(ov, h), _head(z, h), gdn_ref[...])
            do, dz, dg = vjp(dycat[:, 2 * DC + h * HD:2 * DC + (h + 1) * HD])
            dos.append(do)
            dzs.append(dz)
            dgdn = dgdn + dg
        do_ref[...] = jnp.concatenate(dos, axis=1)
        dz_ref[...] = jnp.concatenate(dzs, axis=1)

        @pl.when(i == 0)
        def _():
            dg1_ref[...] = jnp.zeros_like(dg1_ref)
            dgdn_ref[...] = jnp.zeros_like(dgdn_ref)
        dg1_ref[...] += _colsum(dxv * mix_ref[...])
        dgdn_ref[...] += dgdn

    tile = pl.BlockSpec((tt, D), lambda i: (i, 0))
    t512 = pl.BlockSpec((tt, DDN), lambda i: (i, 0))
    return pl.pallas_call(
        body, grid=(T // tt,), name=name,
        in_specs=[tile, tile, t512] + _z_specs(tt) + [_full((1, HD)), _full((1, D)), _wspec(wl, (D, D), lambda i: (0, 0))],
        out_specs=[tile, t512, t512, t512, _full((1, D)), _full((1, HD))],
        out_shape=[_sds((T, D), BF16), _sds((T, DDN)), _sds((T, DDN)), _sds((T, DDN)), _sds((1, D)), _sds((1, HD))],
        compiler_params=_cp(1))(dx, mix, o, proj, proj, gdn, g1, _w3(w_out))


FK = DFF // 2


def ffnout_fwd(x, gu, g2, w, *, name, wl=0):
    T = x.shape[0]
    tt = _tile(T, 512)

    def body(x_ref, gate_ref, up_ref, g2_ref, w_ref, act_ref, f_ref, xo_ref):
        kk = pl.program_id(1)
        act = (_silu(gate_ref[...]) * up_ref[...]).astype(BF16)
        act_ref[...] = act
        part = _dot(act, w_ref[...])

        @pl.when(kk == 0)
        def _():
            f_ref[...] = part

        @pl.when(kk == 1)
        def _():
            f = f_ref[...] + part
            f_ref[...] = f
            xo_ref[...] = x_ref[...] + g2_ref[...] * f

    tile = pl.BlockSpec((tt, D), lambda i, kk: (i, 0))
    return pl.pallas_call(
        body, grid=(T // tt, 2), name=name,
        in_specs=[tile, pl.BlockSpec((tt, FK), lambda i, kk: (i, kk)), pl.BlockSpec((tt, FK), lambda i, kk: (i, 2 + kk)),
                  pl.BlockSpec((1, D), lambda i, kk: (0, 0)), _wspec(wl, (FK, D), lambda i, kk: (kk, 0))],
        out_specs=[pl.BlockSpec((tt, FK), lambda i, kk: (i, kk)), tile, tile],
        out_shape=[_sds((T, DFF), BF16), _sds((T, D)), _sds((T, D))], compiler_params=_cp(2))(x, gu, gu, g2, _w3(w))


def ffnout_bwd(dx, f, gu, g2, w, *, name, wl=0):
    T = dx.shape[0]
    tt = _tile(T, 512)

    def body(dx_ref, f_ref, gate_ref, up_ref, g2_ref, w_ref, df_ref, dgate_ref, dup_ref, dg2_ref):
        i, kk = pl.program_id(0), pl.program_id(1)
        dxv = dx_ref[...]
        df = (dxv * g2_ref[...]).astype(BF16)
        dact = _dot_nt(df, w_ref[...])
        gate, up = gate_ref[...], up_ref[...]
        sg = jax.nn.sigmoid(gate)
        dgate_ref[...] = (dact * up * (sg * (1.0 + gate * (1.0 - sg)))).astype(BF16)
        dup_ref[...] = (dact * (gate * sg)).astype(BF16)

        @pl.when(kk == 0)
        def _():
            df_ref[...] = df

        @pl.when((i == 0) & (kk == 0))
        def _():
            dg2_ref[...] = jnp.zeros_like(dg2_ref)

        @pl.when(kk == 0)
        def _():
            dg2_ref[...] += _colsum(dxv * f_ref[...])

    tile = pl.BlockSpec((tt, D), lambda i, kk: (i, 0))
    return pl.pallas_call(
        body, grid=(T // tt, 2), name=name,
        in_specs=[tile, tile, pl.BlockSpec((tt, FK), lambda i, kk: (i, kk)), pl.BlockSpec((tt, FK), lambda i, kk: (i, 2 + kk)),
                  pl.BlockSpec((1, D), lambda i, kk: (0, 0)), _wspec(wl, (FK, D), lambda i, kk: (kk, 0))],
        out_specs=[tile, pl.BlockSpec((tt, FK), lambda i, kk: (i, kk)), pl.BlockSpec((tt, FK), lambda i, kk: (i, kk)),
                   pl.BlockSpec((1, D), lambda i, kk: (0, 0))],
        out_shape=[_sds((T, D), BF16), _sds((T, DFF), BF16), _sds((T, DFF), BF16), _sds((1, D))],
        compiler_params=_cp(2))(dx, f, gu, gu, g2, _w3(w))


def loss_head(x, target, gf, *, name):
    T = x.shape[0]
    tt = _tile(T, 256)

    def body(x_ref, t_ref, g_ref, loss_ref, dx_ref, dg_ref):
        i = pl.program_id(0)
        tv = t_ref[...]
        y, vjp = jax.vjp(_rms, x_ref[...], g_ref[...])
        err = y - tv
        dx, dg = vjp(err * (1.0 / D))
        dx_ref[...] = dx

        @pl.when(i == 0)
        def _():
            loss_ref[...] = jnp.zeros_like(loss_ref)
            dg_ref[...] = jnp.zeros_like(dg_ref)
        loss_ref[...] += 0.5 * jnp.sum(jnp.mean(err * err, axis=-1, keepdims=True), axis=0, keepdims=True)
        dg_ref[...] += dg

    tile = pl.BlockSpec((tt, D), lambda i: (i, 0))
    return pl.pallas_call(
        body, grid=(T // tt,), name=name, in_specs=[tile, tile, _full((1, D))],
        out_specs=[_full((1, HD)), tile, _full((1, D))], out_shape=[_sds((1, HD)), _sds((T, D)), _sds((1, D))],
        compiler_params=_cp(1))(x, target, gf)


def _pad_rows(a, rows):
    return jnp.pad(a, ((0, rows - a.shape[0]), (0, 0)))


def _row128(v):
    return jnp.pad(v, (0, HD - v.shape[0]))[None, :]


def _grow3(grow):
    return grow.reshape(8, -1, CH).transpose(1, 0, 2)


def _grow2(grow3):
    return grow3.transpose(1, 0, 2).reshape(8, -1)


def layer_params(l, mod, p):
    m = mod[l].reshape(6, 1, D)
    return dict(
        sh1=m[0], sc1=m[1], g1=m[2], sh2=m[3], sc2=m[4], g2=m[5],
        gn1=p["norm_mix_g"][l][None, :], gn2=p["norm_ffn_g"][l][None, :],
        wa=_pad_rows(p["conv_a_w"][l], 8), wb=_pad_rows(p["conf_dw_w"][l], 32), wc=_pad_rows(p["dn_conv_w"][l], 8),
        bb=p["conf_dw_b"][l][None, :], ln_g=p["conf_ln_g"][l][None, :], ln_b=p["conf_ln_b"][l][None, :],
        alog=_row128(p["dn_a_log"][l]), dt=_row128(p["dn_dt_bias"][l]), gdn=p["dn_norm_g"][l][None, :])


def layer_fwd(l, x, lp, w_in, w_out, w_ffn_in, w_ffn_out, wl=0):
    s = {"x0": x}
    s["h1"], s["proj"] = nm_fwd(x, lp["gn1"], lp["sc1"], lp["sh1"], w_in, tn=NP // 3, name=f"proj_fwd_{l}", wl=wl)
    yab, s["q"], s["k"], s["v"], s["gb"], grow = mix_fwd(
        s["proj"], lp["wa"], lp["wb"], lp["bb"], lp["ln_g"], lp["ln_b"], lp["wc"], lp["alog"], lp["dt"], name=f"mix_fwd_{l}")
    s["grow3"] = _grow3(grow)
    s["u"], s["w"], s["qk"], s["p"] = dn_intra_fwd(s["q"], s["k"], s["v"], s["gb"], s["grow3"], name=f"dn_intra_fwd_{l}")
    s["o"], s["s_all"] = dn_inter_fwd(s["q"], s["k"], s["u"], s["w"], s["gb"], s["qk"], name=f"dn_inter_fwd_{l}")
    s["ycat"], s["mix"], s["x1"] = mixout_fwd(x, yab, s["o"], s["proj"], lp["gdn"], lp["g1"], w_out, name=f"mixout_fwd_{l}",
                                              wl=wl)
    s["h2"], s["gu"] = nm_fwd(s["x1"], lp["gn2"], lp["sc2"], lp["sh2"], w_ffn_in, tn=2 * DFF // 4, name=f"ffnin_fwd_{l}", wl=wl)
    s["act"], s["f"], x2 = ffnout_fwd(s["x1"], s["gu"], lp["g2"], w_ffn_out, name=f"ffnout_fwd_{l}", wl=wl)
    return x2, s


def layer_bwd(l, dx2, s, lp, w_in, w_out, w_ffn_in, w_ffn_out, wl=0, L=1, gacc=None):
    g = {}
    prev = (lambda n: None) if gacc is None else gacc.get
    df, dgate, dup, dg2 = ffnout_bwd(dx2, s["f"], s["gu"], lp["g2"], w_ffn_out, name=f"ffnout_bwd_{l}", wl=wl)
    g["w_ffn_out"] = mm_tn_split(s["act"], df, by_cols=False, l=wl, L=L, prev=prev("w_ffn_out"), name=f"ffnout_dw_{l}")
    dx1, g["norm_ffn_g"], dsc2, dsh2, dgu = nm_bwd([dgate, dup], w_ffn_in, s["x1"], lp["gn2"], lp["sc2"], lp["sh2"], dx2,
                                                   name=f"ffnin_bwd_{l}", wl=wl)
    g["w_ffn_in"] = mm_tn_split(s["h2"], dgu, by_cols=True, l=wl, L=L, prev=prev("w_ffn_in"), name=f"ffnin_dw_{l}")
    dmix, dyab, do, dz, dg1, g["dn_norm_g"] = mixout_bwd(dx1, s["mix"], s["o"], s["proj"], lp["gdn"], lp["g1"], w_out,
                                                         name=f"mixout_bwd_{l}", wl=wl)
    g["w_out"] = mm_tn_split(s["ycat"], dmix, by_cols=False, l=wl, L=L, prev=prev("w_out"), name=f"mixout_dw_{l}")
    dq_i, dk_i, du, dw, dg_i, dqk = dn_inter_bwd(do, s["q"], s["k"], s["u"], s["w"], s["gb"], s["qk"], s["s_all"],
                                                 name=f"dn_inter_bwd_{l}")
    dq, dk, dv, dgb, dgrow3 = dn_intra_bwd(du, dw, dqk, dq_i, dk_i, dg_i, s["q"], s["k"], s["v"], s["gb"], s["grow3"],
                                           s["p"], name=f"dn_intra_bwd_{l}")
    dab, dconv, dblk, g["conf_dw_b"], g["conf_ln_g"], g["conf_ln_b"], dal, ddt = mix_bwd_point(
        s["proj"], dyab, dq, dk, dv, dgb, _grow2(dgrow3), lp["wa"], lp["wb"], lp["bb"], lp["ln_g"], lp["ln_b"], lp["wc"],
        lp["alog"], lp["dt"], name=f"mix_bwd_point_{l}")
    dpb, dwa, dwb, dwc = mix_bwd_conv(s["proj"], dconv, lp["wa"], lp["wb"], lp["wc"], name=f"mix_bwd_conv_{l}")
    dx0, g["norm_mix_g"], dsc1, dsh1, dproj = nm_bwd([dab, dpb, dz, dblk], w_in, s["x0"], lp["gn1"], lp["sc1"], lp["sh1"], dx1,
                                                     name=f"proj_bwd_{l}", wl=wl)
    g["w_in"] = mm_tn(s["h1"], dproj, tn=NP // 3, name=f"proj_dw_{l}")
    g["conv_a_w"], g["conf_dw_w"], g["dn_conv_w"] = dwa[:KA], dwb[:KB], dwc[:KC]
    g["dn_a_log"], g["dn_dt_bias"] = dal[0, :NH], ddt[0, :NH]
    g["mod"] = jnp.concatenate([dsh1, dsc1, dg1, dsh2, dsc2, dg2], axis=1)
    return dx0, g


EW_BLOCK_BYTES = 1 << 20


def _row_tile(R, C, mult=8):
    best = None
    for rt in range(mult, R + 1, mult):
        if R % rt == 0 and rt * C * 4 <= EW_BLOCK_BYTES:
            best = rt
    return best if best is not None else R


def add_half_bf16(g2, recv, core, *, name):
    _, B, R, C = g2.shape
    rt = _row_tile(R, C, 16)

    def body(core_ref, a_ref, b_ref, o_ref):
        o_ref[...] = (a_ref[...] + b_ref[...]).astype(BF16)

    spec = pl.BlockSpec((1, rt, C), lambda b, r, core_ref: (b, r, 0))
    return pl.pallas_call(
        body, name=name, out_shape=_sds((B, R, C), BF16), compiler_params=_cp(2),
        grid_spec=pltpu.PrefetchScalarGridSpec(
            num_scalar_prefetch=1, grid=(B, R // rt),
            in_specs=[pl.BlockSpec((None, 1, rt, C), lambda b, r, core_ref: (core_ref[0], b, r, 0)), spec],
            out_specs=spec))(core, g2, recv)


def adamw_halves(w, g_mine, g_theirs, core, m, v, *, name):
    L, R, C = w.shape
    rh = R // 2
    rt = _row_tile(rh, C)
    nr = rh // rt

    def body(core_ref, w_ref, gm_ref, gt_ref, m_ref, v_ref, go_ref, d_ref, mo_ref, vo_ref):
        g = jnp.where(pl.program_id(1) == core_ref[0], gm_ref[...], gt_ref[...])
        go_ref[...] = g
        d_ref[...], mo_ref[...], vo_ref[...] = _adamw_math(w_ref[...], g, m_ref[...], v_ref[...])

    spec = pl.BlockSpec((1, rt, C), lambda l, h, r, core_ref: (l, h * nr + r, 0))
    half = pl.BlockSpec((1, rt, C), lambda l, h, r, core_ref: (l, r, 0))
    return pl.pallas_call(
        body, name=name, out_shape=[_sds((L, R, C))] * 4, compiler_params=_cp(3),
        grid_spec=pltpu.PrefetchScalarGridSpec(num_scalar_prefetch=1, grid=(L, 2, nr), in_specs=[spec, half, half, spec, spec],
                                               out_specs=[spec] * 4))(core, w, g_mine, g_theirs, m, v)


def ew_call(fn, ins, n_out, *, name):
    B, R, C = ins[0].shape
    rt = _row_tile(R, C)
    n = len(ins)

    def body(*refs):
        outs = fn(*[r[...] for r in refs[:n]])
        for r, o in zip(refs[n:], outs):
            r[...] = o

    spec = pl.BlockSpec((1, rt, C), lambda b, r: (b, r, 0))
    return pl.pallas_call(body, grid=(B, R // rt), name=name, in_specs=[spec] * n, out_specs=[spec] * n_out,
                          out_shape=[_sds((B, R, C))] * n_out, compiler_params=_cp(2))(*ins)


def _adamw_math(w, g, m, v):
    m = ADAM_B1 * m + (1.0 - ADAM_B1) * g
    v = ADAM_B2 * v + (1.0 - ADAM_B2) * jnp.square(g)
    m_hat = m / (1.0 - ADAM_B1 ** ADAM_STEP)
    v_hat = v / (1.0 - ADAM_B2 ** ADAM_STEP)
    return -ADAM_LR * (m_hat / (jnp.sqrt(v_hat) + ADAM_EPS) + ADAM_WD * w), m, v


def adamw(w, g, m, v, *, name):
    shape = w.shape
    r3 = lambda a: a.reshape((-1,) + shape[-2:])
    return [o.reshape(shape) for o in ew_call(_adamw_math, [r3(w), r3(g), r3(m), r3(v)], 3, name=name)]


def sum_slots(a, *, name):
    S, B, R, C = a.shape
    rt = _row_tile(R, C, 16)

    def body(*refs):
        acc = refs[0][0, 0].astype(F32)
        for r in refs[1:S]:
            acc = acc + r[0, 0].astype(F32)
        refs[S][0] = acc

    def spec(s):
        return pl.BlockSpec((1, 1, rt, C), lambda b, r: (s, b, r, 0))
    return pl.pallas_call(body, grid=(B, R // rt), name=name, in_specs=[spec(s) for s in range(S)],
                          out_specs=pl.BlockSpec((1, rt, C), lambda b, r: (b, r, 0)), out_shape=_sds((B, R, C)),
                          compiler_params=_cp(2))(*([a] * S))


ADA_SH = 6 * D // 4
ADA_TN = 512


def mod_fwd(c_all, w_ada, b_my, *, name):
    L = w_ada.shape[0]

    def body(c_ref, w_ref, b_ref, o_ref):
        o_ref[0] = _dot(_silu(c_ref[...]).astype(BF16), w_ref[0].astype(BF16)) + b_ref[0]

    return pl.pallas_call(
        body, grid=(L, ADA_SH // ADA_TN), name=name,
        in_specs=[pl.BlockSpec((8, D), lambda l, j: (0, 0)), pl.BlockSpec((1, D, ADA_TN), lambda l, j: (l, 0, j)),
                  pl.BlockSpec((1, 1, ADA_TN), lambda l, j: (l, 0, j))],
        out_specs=pl.BlockSpec((1, 8, ADA_TN), lambda l, j: (l, 0, j)), out_shape=_sds((L, 8, ADA_SH)),
        compiler_params=_cp(2))(c_all, w_ada, b_my)


def wada_grad(c_all, dmod, *, name):
    L = dmod.shape[0]

    def body(c_ref, d_ref, o_ref):
        o_ref[0] = _dot_tn(_silu(c_ref[...]), d_ref[0], HI)

    return pl.pallas_call(
        body, grid=(L, ADA_SH // ADA_TN), name=name,
        in_specs=[pl.BlockSpec((8, D), lambda l, j: (0, 0)), pl.BlockSpec((1, 8, ADA_TN), lambda l, j: (l, 0, j))],
        out_specs=pl.BlockSpec((1, D, ADA_TN), lambda l, j: (l, 0, j)), out_shape=_sds((L, D, ADA_SH)),
        compiler_params=_cp(2))(c_all, dmod)


def _place():
    return lax.axis_index("x"), lax.axis_index("y"), lax.axis_index("c")


def _other_chips(x, y):
    return [(1 - x, y), (x, 1 - y), (1 - x, 1 - y)]


def allgather8(blocks, *, space, name):
    n = len(blocks)

    def body(*refs):
        ins, outs = refs[:n], refs[n:2 * n]
        send_sems, recv_sems, local_sems = refs[2 * n:]
        x, y, c = _place()
        me, sibling = (x, y, c), (x, y, 1 - c)
        chips = _other_chips(x, y)

        def slot(p):
            return 4 * p[0] + 2 * p[1] + p[2]

        def copy(a, k, block, to, src=None):
            dst = outs[a].at[slot(block)]
            return pltpu.make_async_remote_copy(src_ref=dst if src is None else src, dst_ref=dst, send_sem=send_sems.at[a, k],
                                                recv_sem=recv_sems.at[a, k], device_id=to, device_id_type=MESH)

        mine = [pltpu.make_async_copy(ins[a], outs[a].at[slot(me)], local_sems.at[a]) for a in range(n)]
        for cp in mine:
            cp.start()
        first = []
        for a in range(n):
            first.append(copy(a, 0, me, sibling, src=ins[a]))
            first += [copy(a, 1 + j, me, (*chip, c), src=ins[a]) for j, chip in enumerate(chips)]
        for cp in first:
            cp.start()
        passed = []
        for j, chip in enumerate(chips):
            for a in range(n):
                copy(a, 1 + j, (*chip, c), me).wait_recv()
                cp = copy(a, 4 + j, (*chip, c), sibling)
                cp.start()
                passed.append(cp)
        for a in range(n):
            copy(a, 0, sibling, me).wait_recv()
            for j, chip in enumerate(chips):
                copy(a, 4 + j, (*chip, 1 - c), me).wait_recv()
        for cp in first + passed:
            cp.wait_send()
        for cp in mine:
            cp.wait()

    spec = pl.BlockSpec(memory_space=space)
    return pl.pallas_call(
        body, name=name, in_specs=[spec] * n, out_specs=[spec] * n,
        out_shape=[_sds((8,) + b.shape, b.dtype) for b in blocks],
        scratch_shapes=[pltpu.SemaphoreType.DMA((n, 7)), pltpu.SemaphoreType.DMA((n, 7)), pltpu.SemaphoreType.DMA((n,))],
        compiler_params=pltpu.CompilerParams(vmem_limit_bytes=VMEM_LIMIT))(*blocks)


def sibling_swap(blocks, *, name, slotted=True):
    n = len(blocks)

    def body(*refs):
        ins, outs = refs[:n], refs[n:2 * n]
        send_sems, recv_sems = refs[2 * n:]
        x, y, c = _place()
        cps = [pltpu.make_async_remote_copy(src_ref=ins[a].at[1 - c] if slotted else ins[a], dst_ref=outs[a],
                                            send_sem=send_sems.at[a], recv_sem=recv_sems.at[a], device_id=(x, y, 1 - c),
                                            device_id_type=MESH)
               for a in range(n)]
        for cp in cps:
            cp.start()
        for cp in cps:
            cp.wait()

    spec = pl.BlockSpec(memory_space=pl.ANY)
    return pl.pallas_call(
        body, name=name, in_specs=[spec] * n, out_specs=[spec] * n,
        out_shape=[_sds(b.shape[1:] if slotted else b.shape, b.dtype) for b in blocks],
        scratch_shapes=[pltpu.SemaphoreType.DMA((n,)), pltpu.SemaphoreType.DMA((n,))])(*blocks)


def chip_exchange(blocks, *, name):
    n = len(blocks)

    def body(*refs):
        ins, outs = refs[:n], refs[n:2 * n]
        send_sems, recv_sems, local_sems = refs[2 * n:]
        x, y, c = _place()
        me = 2 * x + y
        chips = _other_chips(x, y)
        mine = [pltpu.make_async_copy(ins[a].at[me], outs[a].at[me], local_sems.at[a]) for a in range(n)]
        for cp in mine:
            cp.start()
        cps = []
        for a in range(n):
            for j, chip in enumerate(chips):
                cps.append(pltpu.make_async_remote_copy(
                    src_ref=ins[a].at[2 * chip[0] + chip[1]], dst_ref=outs[a].at[me], send_sem=send_sems.at[a, j],
                    recv_sem=recv_sems.at[a, j], device_id=(*chip, c), device_id_type=MESH))
        for cp in cps:
            cp.start()
        for a in range(n):
            for j, chip in enumerate(chips):
                s = 2 * chip[0] + chip[1]
                pltpu.make_async_remote_copy(src_ref=ins[a].at[s], dst_ref=outs[a].at[s], send_sem=send_sems.at[a, j],
                                             recv_sem=recv_sems.at[a, j], device_id=(*chip, c), device_id_type=MESH).wait_recv()
        for cp in cps:
            cp.wait_send()
        for cp in mine:
            cp.wait()

    spec = pl.BlockSpec(memory_space=pl.ANY)
    return pl.pallas_call(
        body, name=name, in_specs=[spec] * n, out_specs=[spec] * n, out_shape=[_sds(b.shape, b.dtype) for b in blocks],
        scratch_shapes=[pltpu.SemaphoreType.DMA((n, 3)), pltpu.SemaphoreType.DMA((n, 3)), pltpu.SemaphoreType.DMA((n,))])(*blocks)


def halves_join(halves, *, name):
    n = len(halves)

    def body(*refs):
        ins, outs = refs[:n], refs[n:2 * n]
        send_sems, recv_sems, local_sems = refs[2 * n:]
        x, y, c = _place()
        cps, mine = [], []
        for a in range(n):
            mine.append(pltpu.make_async_copy(ins[a], outs[a].at[c], local_sems.at[a]))
            cps.append(pltpu.make_async_remote_copy(src_ref=ins[a], dst_ref=outs[a].at[c], send_sem=send_sems.at[a],
                                                    recv_sem=recv_sems.at[a], device_id=(x, y, 1 - c), device_id_type=MESH))
        for cp in mine + cps:
            cp.start()
        for a in range(n):
            pltpu.make_async_remote_copy(src_ref=ins[a], dst_ref=outs[a].at[1 - c], send_sem=send_sems.at[a], recv_sem=recv_sems.at[a],
                                         device_id=(x, y, 1 - c), device_id_type=MESH).wait_recv()
        for cp in cps:
            cp.wait_send()
        for cp in mine:
            cp.wait()

    spec = pl.BlockSpec(memory_space=pl.ANY)
    return pl.pallas_call(
        body, name=name, in_specs=[spec] * n, out_specs=[spec] * n,
        out_shape=[_sds((2,) + h.shape, h.dtype) for h in halves],
        scratch_shapes=[pltpu.SemaphoreType.DMA((n,)), pltpu.SemaphoreType.DMA((n,)), pltpu.SemaphoreType.DMA((n,))])(*halves)


BIG = ("w_in", "w_out", "w_ffn_in", "w_ffn_out")
COL_SHARDED = {"w_in": True, "w_out": False, "w_ffn_in": True, "w_ffn_out": False}
SMALL = ("norm_mix_g", "norm_ffn_g", "conv_a_w", "conf_dw_w", "conf_dw_b", "conf_ln_g", "conf_ln_b", "dn_conv_w",
         "dn_a_log", "dn_dt_bias", "dn_norm_g")
SMALL_SHARDED = ("conv_a_w", "conf_dw_w", "dn_conv_w")


def _half_rows(a, c):
    rh = a.shape[1] // 2
    return lax.dynamic_slice_in_dim(a, c * rh, rh, axis=1)


def _assemble(name, g):
    _, L, rh, C = g.shape
    g = g.reshape(4, 2, L, rh, C)
    if COL_SHARDED[name]:
        return g.transpose(2, 1, 3, 0, 4).reshape(L, 2 * rh, 4 * C)
    return g.transpose(2, 0, 1, 3, 4).reshape(L, 8 * rh, C)


def _split_for_reduce(name, g, c):
    L, R, C = g.shape
    if COL_SHARDED[name]:
        t = g.reshape(L, 2, R // 2, 4, C // 4).transpose(1, 3, 0, 2, 4)
    else:
        t = g.reshape(L, 4, 2, R // 8, C).transpose(2, 1, 0, 3, 4)
    mine = lax.dynamic_index_in_dim(t, c, 0, keepdims=False)
    other = lax.dynamic_index_in_dim(t, 1 - c, 0, keepdims=False)
    return mine, other


def _pack(parts):
    flat = jnp.concatenate([p.reshape(-1) for p in parts])
    n = flat.shape[0]
    rows = -(-n // (8 * HD)) * 8
    return jnp.pad(flat, (0, rows * HD - n)).reshape(rows, HD)


def _unpack(buf, shapes):
    flat = buf.reshape(-1)
    out, off = [], 0
    for s in shapes:
        n = 1
        for d in s:
            n *= d
        out.append(flat[off:off + n].reshape(s))
        off += n
    return out


def kernel(x, c, w_ada, b_ada, norm_mix_g, norm_ffn_g, w_in, conv_a_w, conf_dw_w, conf_dw_b, conf_ln_g, conf_ln_b, dn_conv_w, dn_a_log, dn_dt_bias, dn_norm_g, w_out, w_ffn_in, w_ffn_out, final_norm_g, loss_target, m_w_ada, m_b_ada, m_norm_mix_g, m_norm_ffn_g, m_w_in, m_conv_a_w, m_conf_dw_w, m_conf_dw_b, m_conf_ln_g, m_conf_ln_b, m_dn_conv_w, m_dn_a_log, m_dn_dt_bias, m_dn_norm_g, m_w_out, m_w_ffn_in, m_w_ffn_out, m_final_norm_g, v_w_ada, v_b_ada, v_norm_mix_g, v_norm_ffn_g, v_w_in, v_conv_a_w, v_conf_dw_w, v_conf_dw_b, v_conf_ln_g, v_conf_ln_b, v_dn_conv_w, v_dn_a_log, v_dn_dt_bias, v_dn_norm_g, v_w_out, v_w_ffn_in, v_w_ffn_out, v_final_norm_g):
    W = dict(w_ada=w_ada, b_ada=b_ada, norm_mix_g=norm_mix_g, norm_ffn_g=norm_ffn_g, w_in=w_in, conv_a_w=conv_a_w,
             conf_dw_w=conf_dw_w, conf_dw_b=conf_dw_b, conf_ln_g=conf_ln_g, conf_ln_b=conf_ln_b, dn_conv_w=dn_conv_w,
             dn_a_log=dn_a_log, dn_dt_bias=dn_dt_bias, dn_norm_g=dn_norm_g, w_out=w_out, w_ffn_in=w_ffn_in,
             w_ffn_out=w_ffn_out, final_norm_g=final_norm_g)
    M = dict(w_ada=m_w_ada, b_ada=m_b_ada, norm_mix_g=m_norm_mix_g, norm_ffn_g=m_norm_ffn_g, w_in=m_w_in, conv_a_w=m_conv_a_w,
             conf_dw_w=m_conf_dw_w, conf_dw_b=m_conf_dw_b, conf_ln_g=m_conf_ln_g, conf_ln_b=m_conf_ln_b, dn_conv_w=m_dn_conv_w,
             dn_a_log=m_dn_a_log, dn_dt_bias=m_dn_dt_bias, dn_norm_g=m_dn_norm_g, w_out=m_w_out, w_ffn_in=m_w_ffn_in,
             w_ffn_out=m_w_ffn_out, final_norm_g=m_final_norm_g)
    V = dict(w_ada=v_w_ada, b_ada=v_b_ada, norm_mix_g=v_norm_mix_g, norm_ffn_g=v_norm_ffn_g, w_in=v_w_in, conv_a_w=v_conv_a_w,
             conf_dw_w=v_conf_dw_w, conf_dw_b=v_conf_dw_b, conf_ln_g=v_conf_ln_g, conf_ln_b=v_conf_ln_b, dn_conv_w=v_dn_conv_w,
             dn_a_log=v_dn_a_log, dn_dt_bias=v_dn_dt_bias, dn_norm_g=v_dn_norm_g, w_out=v_w_out, w_ffn_in=v_w_ffn_in,
             w_ffn_out=v_w_ffn_out, final_norm_g=v_final_norm_g)
    L = w_ada.shape[0]
    ax, ay, ac = _place()
    chip = 2 * ax + ay
    dev = 4 * ax + 2 * ay + ac

    c_all = allgather8([jnp.pad(c, ((0, 7), (0, 0)))], space=pltpu.VMEM, name="gather_c")[0][:, 0, :]
    b_my = lax.dynamic_slice_in_dim(b_ada, chip * ADA_SH, ADA_SH, axis=1)[:, None, :]
    mod_sh = mod_fwd(c_all, w_ada, b_my, name="mod_fwd")
    lh = L // 2
    mod_g = allgather8([lax.dynamic_slice_in_dim(mod_sh, ac * lh, lh, axis=0).reshape(lh * 8, ADA_SH)], space=pltpu.VMEM,
                       name="gather_mod")[0]
    mod_all = mod_g.reshape(4, 2, lh, 8, ADA_SH).transpose(1, 2, 3, 0, 4).reshape(L, 8, 6 * D)
    mod = lax.dynamic_index_in_dim(mod_all, dev, 1, keepdims=False)

    gathered = allgather8([_half_rows(W[n], ac).astype(BF16) for n in BIG], space=pl.ANY, name="gather_w")
    full = {n: _assemble(n, g) for n, g in zip(BIG, gathered)}
    full["w_in"] = jnp.pad(full["w_in"], ((0, 0), (0, 0), (0, NP - IN_COLS)))

    p_full = dict(W)
    sm = allgather8([_pack([W[n] for n in SMALL_SHARDED])], space=pltpu.VMEM, name="gather_convw")[0]
    per_chip = [_unpack(sm[4 * (s // 2) + 2 * (s % 2)], [W[n].shape for n in SMALL_SHARDED]) for s in range(4)]
    for i, n in enumerate(SMALL_SHARDED):
        p_full[n] = jnp.concatenate([per_chip[s][i] for s in range(4)], axis=-1)

    xs = x[0]
    saves, lps = [], []
    for l in range(L):
        lp = layer_params(l, mod, p_full)
        xs, s = layer_fwd(l, xs, lp, full["w_in"], full["w_out"], full["w_ffn_in"], full["w_ffn_out"], wl=l)
        saves.append(s)
        lps.append(lp)
    loss_p, dx, dgf = loss_head(xs, loss_target[0], final_norm_g[None, :], name="loss_head")
    grads = [None] * L
    gacc = None
    for l in reversed(range(L)):
        dx, grads[l] = layer_bwd(l, dx, saves[l], lps[l], full["w_in"], full["w_out"], full["w_ffn_in"], full["w_ffn_out"],
                                 wl=l, L=L, gacc=gacc)
        gacc = grads[l]
    loss = lax.psum(loss_p[0, 0], ("x", "y", "c"))
    grad_x = dx[None]

    small_shapes = [(L,) + p_full[n].shape[1:] for n in SMALL]
    parts = [jnp.stack([grads[l][n].reshape(sh[1:]) for l in range(L)]) for n, sh in zip(SMALL, small_shapes)]
    parts += [dgf.reshape(D), jnp.concatenate([grads[l]["mod"] for l in range(L)], axis=0)]
    small_shapes += [(D,), (L, 6 * D)]
    packed = _pack(parts)
    gathered_small = allgather8([packed], space=pltpu.VMEM, name="gather_small")[0]
    summed = sum_slots(gathered_small[:, None], name="sum_small")[0]
    g_small = dict(zip(SMALL + ("final_norm_g", "b_ada"), _unpack(summed, small_shapes)))
    for n in SMALL_SHARDED:
        sw = W[n].shape[-1]
        g_small[n] = lax.dynamic_slice_in_dim(g_small[n], chip * sw, sw, axis=g_small[n].ndim - 1)
    dmod_all = jnp.stack([_unpack(gathered_small[d], small_shapes)[-1] for d in range(8)], axis=1)
    dmod_my = lax.dynamic_slice_in_dim(dmod_all, chip * ADA_SH, ADA_SH, axis=2)
    g_w_ada = wada_grad(c_all, dmod_my, name="wada_grad")

    split = {n: grads[0][n] for n in BIG if n != "w_in"}
    g_in = jnp.stack([grads[l]["w_in"] for l in range(L)])[..., :IN_COLS]
    split["w_in"] = g_in.reshape(L, 2, D // 2, 4, IN_COLS // 4).transpose(1, 3, 0, 2, 4)
    core = ac.astype(jnp.int32).reshape(1)
    from_sib = sibling_swap([split[n] for n in BIG], name="reduce_sibling")
    chip_sum = [add_half_bf16(split[n].reshape((2, -1) + r.shape[-2:]), r.reshape((-1,) + r.shape[-2:]), core,
                              name=f"reduce_add2_{n}").reshape(r.shape) for n, r in zip(BIG, from_sib)]
    from_chips = chip_exchange(chip_sum, name="reduce_chips")
    halves = [sum_slots(a, name=f"reduce_add4_{n}") for n, a in zip(BIG, from_chips)]
    theirs = sibling_swap(halves, name="reduce_join", slotted=False)

    out_g, out_d, out_m, out_v = {}, {}, {}, {}
    out_g["w_ada"] = g_w_ada
    out_d["w_ada"], out_m["w_ada"], out_v["w_ada"] = adamw(W["w_ada"], g_w_ada, M["w_ada"], V["w_ada"], name="adamw_w_ada")
    for n, g_mine, g_theirs in zip(BIG, halves, theirs):
        out_g[n], out_d[n], out_m[n], out_v[n] = adamw_halves(W[n], g_mine, g_theirs, core, M[n], V[n], name=f"adamw_{n}")
    names_small = SMALL + ("final_norm_g", "b_ada")
    pk = lambda d: _pack([d[n] for n in names_small])[None]
    d_s, m_s, v_s = ew_call(_adamw_math, [pk(W), pk(g_small), pk(M), pk(V)], 3, name="adamw_small")
    shapes_s = [W[n].shape for n in names_small]
    for d, o in ((out_d, d_s), (out_m, m_s), (out_v, v_s)):
        d.update(zip(names_small, _unpack(o[0], shapes_s)))
    for n in names_small:
        out_g[n] = g_small[n].reshape(W[n].shape)

    order = ("w_ada", "b_ada", "norm_mix_g", "norm_ffn_g", "w_in", "conv_a_w", "conf_dw_w", "conf_dw_b", "conf_ln_g", "conf_ln_b",
             "dn_conv_w", "dn_a_log", "dn_dt_bias", "dn_norm_g", "w_out", "w_ffn_in", "w_ffn_out", "final_norm_g")
    return (loss, grad_x, *[out_g[n] for n in order], *[out_d[n] for n in order], *[out_m[n] for n in order],
            *[out_v[n] for n in order])
```

```python
import functools

import jax
import jax.numpy as jnp
from jax import lax
from jax.experimental import pallas as pl
from jax.experimental.pallas import tpu as pltpu

F32 = jnp.float32
BF16 = jnp.bfloat16
HI = lax.Precision.HIGHEST
MESH = pl.DeviceIdType.MESH

D = 1024
DEPTH = 4
DC = 256
DDN = 512
NH = 4
HD = 128
CH = 64
DFF = 2816
IN_COLS = 3336
NP = 3456
KA, KB, KC = 3, 31, 4
HALO = 32
EPS = 1e-6
O_AB, O_AC, O_AV, O_BA, O_BG, O_Q, O_K, O_V, O_Z, O_GB = 0, 256, 512, 768, 1024, 1280, 1792, 2304, 2816, 3328

ADAM_LR, ADAM_B1, ADAM_B2, ADAM_EPS, ADAM_WD, ADAM_STEP = 0.001, 0.9, 0.999, 1e-08, 0.01, 10

VMEM_LIMIT = 56 * 1024 * 1024


def _cp(n_grid):
    return pltpu.CompilerParams(dimension_semantics=("arbitrary",) * n_grid, vmem_limit_bytes=VMEM_LIMIT)


def _sds(shape, dtype=F32):
    return jax.ShapeDtypeStruct(tuple(shape), dtype)


def _dot(a, b, prec=None):
    return jnp.dot(a, b, preferred_element_type=F32, precision=prec)


def _dot_nt(a, b, prec=None):
    return lax.dot_general(a, b, (((1,), (1,)), ((), ())), preferred_element_type=F32, precision=prec)


def _dot_tn(a, b, prec=None):
    return lax.dot_general(a, b, (((0,), (0,)), ((), ())), preferred_element_type=F32, precision=prec)


def _split_bf16(a):
    hi = a.astype(BF16)
    return hi, (a - hi.astype(F32)).astype(BF16)


_DIMS = {"nn": (((1,), (0,)), ((), ())), "nt": (((1,), (1,)), ((), ())), "tn": (((0,), (0,)), ((), ()))}
_DIMS_BATCHED = {"nn": (((2,), (1,)), ((0,), (0,))), "nt": (((2,), (2,)), ((0,), (0,))), "tn": (((1,), (1,)), ((0,), (0,)))}


def _mm3_raw(a, b, kind):
    ah, al = _split_bf16(a)
    bh, bl = _split_bf16(b)
    d = (_DIMS if a.ndim == 2 else _DIMS_BATCHED)[kind]
    dg = lambda u, v: lax.dot_general(u, v, d, preferred_element_type=F32)
    return dg(ah, bh) + (dg(ah, bl) + dg(al, bh))


@jax.custom_vjp
def _mm_nn(a, b):
    return _mm3_raw(a, b, "nn")


@jax.custom_vjp
def _mm_nt(a, b):
    return _mm3_raw(a, b, "nt")


@jax.custom_vjp
def _mm_tn(a, b):
    return _mm3_raw(a, b, "tn")


_mm_nn.defvjp(lambda a, b: (_mm_nn(a, b), (a, b)), lambda r, g: (_mm_nt(g, r[1]), _mm_tn(r[0], g)))
_mm_nt.defvjp(lambda a, b: (_mm_nt(a, b), (a, b)), lambda r, g: (_mm_nn(g, r[1]), _mm_tn(g, r[0])))
_mm_tn.defvjp(lambda a, b: (_mm_tn(a, b), (a, b)), lambda r, g: (_mm_nt(r[1], g), _mm_nn(r[0], g)))


@jax.custom_vjp
def _inv_given(x, p):
    return p


_inv_given.defvjp(lambda x, p: (p, p), lambda p, g: (_mm_tn(p, _mm_nt(g, p)), jnp.zeros_like(p)))


def _silu(x):
    return x * jax.nn.sigmoid(x)


def _colsum(x):
    return jnp.sum(x, axis=0, keepdims=True)


TILE_CAP = 512


def _w3(w):
    return w if w.ndim == 3 else w[None]


def _wspec(l, block, index):
    return pl.BlockSpec((None,) + block, lambda *g: (l,) + index(*g))


def _tile(T, want):
    t = min(T, want, TILE_CAP)
    assert T % t == 0
    return t


def _normmod(x, gn, sc, sh):
    r = lax.rsqrt(jnp.mean(x * x, axis=-1, keepdims=True) + EPS)
    return ((x * r) * gn) * (1.0 + sc) + sh


def _rms(x, g):
    return (x * lax.rsqrt(jnp.mean(x * x, axis=-1, keepdims=True) + EPS)) * g


def _mix_b_post(u, ln_g, ln_b):
    mu = jnp.mean(u, axis=-1, keepdims=True)
    var = jnp.mean(jnp.square(u - mu), axis=-1, keepdims=True)
    return _silu(((u - mu) * lax.rsqrt(var + 1e-5)) * ln_g + ln_b)


def _softplus(z):
    return jnp.where(z > 0, z, 0.0) + jnp.log(1.0 + jnp.exp(-jnp.where(z > 0, z, -z)))


def _chunk_tril(tt):
    r = lax.broadcasted_iota(jnp.int32, (tt, tt), 0)
    c = lax.broadcasted_iota(jnp.int32, (tt, tt), 1)
    return ((r // CH == c // CH) & (c <= r)).astype(F32)


def _eye8():
    return (lax.broadcasted_iota(jnp.int32, (8, HD), 0) == lax.broadcasted_iota(jnp.int32, (8, HD), 1)).astype(F32)


def _dn_post(pre_q, pre_k, pre_v, blk, alog_row, dt_row):
    q = [s * lax.rsqrt(jnp.sum(s * s, -1, keepdims=True) + EPS) * (HD ** -0.5) for s in map(_silu, pre_q)]
    k = [s * lax.rsqrt(jnp.sum(s * s, -1, keepdims=True) + EPS) for s in map(_silu, pre_k)]
    v = [_silu(p) for p in pre_v]
    lane = lax.broadcasted_iota(jnp.int32, (1, HD), 1)
    g = -jnp.exp(alog_row) * _softplus(blk + dt_row)
    beta = jax.nn.sigmoid(blk)
    gc = _dot(_chunk_tril(blk.shape[0]), jnp.where(lane < NH, g, 0.0), HI)
    gb = jnp.where(lane < NH, gc, jnp.where(lane < 2 * NH, beta, 0.0))
    grow = _dot_nt(_eye8(), gc, HI)
    return q, k, v, gb, grow


def _dn_intra(q, k, v, beta, gcol, grow, p_known=None):
    r = lax.broadcasted_iota(jnp.int32, (CH, CH), 0)
    c = lax.broadcasted_iota(jnp.int32, (CH, CH), 1)
    causal, strict = c <= r, c < r
    decay = jnp.where(causal, jnp.exp(jnp.where(causal, gcol - grow, 0.0)), 0.0)
    kb = k * beta
    x = -jnp.where(strict, _mm_nt(kb, k) * decay, 0.0)
    if p_known is None:
        p = (r == c).astype(F32) + x
        y = x
        for _ in range(5):
            y = _mm_nn(y, y)
            p = p + _mm_nn(p, y)
    else:
        p = _inv_given(x, p_known)
    u = _mm_nn(p, v * beta)
    w = _mm_nn(p, kb * jnp.exp(gcol))
    qk = jnp.where(causal, _mm_nt(q, k) * decay, 0.0)
    return u, w, qk, p


def _dn_inter(s, q, k, u, w, gcol, qk):
    last = (lax.broadcasted_iota(jnp.int32, (CH, 1), 0) == CH - 1).astype(F32)
    g_last = jnp.sum(gcol * last, axis=1, keepdims=True)
    v_new = u - _mm_nn(w, s)
    o = _mm_nn(q * jnp.exp(gcol), s) + _mm_nn(qk, v_new)
    s_new = s * jnp.exp(g_last) + _mm_tn(k * jnp.exp(g_last - gcol), v_new)
    return s_new, o


def _yc(o, z, gdn):
    return _rms(o, gdn) * _silu(z)


class Cargo:
    def __init__(self, ins, outs, sems, first, last, middle=None):
        self.ins, self.outs, self.sems = list(ins), list(outs), list(sems)
        self.first, self.middle, self.last = first, middle, last
        self.results = None


def _pcall(body, *, grid, in_specs, out_specs, out_shape, args, name, scratch_shapes=(), cargo=None):
    if cargo is None:
        return pl.pallas_call(body, grid=grid, in_specs=list(in_specs), out_specs=list(out_specs), out_shape=list(out_shape),
                              scratch_shapes=list(scratch_shapes), name=name, compiler_params=_cp(len(grid)))(*args)
    n_in, n_out, n_scr = len(args), len(out_shape), len(scratch_shapes)
    k_in, k_out = len(cargo.ins), len(cargo.outs)
    total = 1
    for g in grid:
        total *= g

    def carrying(*refs):
        ins, cins = refs[:n_in], refs[n_in:n_in + k_in]
        o0 = n_in + k_in
        outs, couts = refs[o0:o0 + n_out], refs[o0 + n_out:o0 + n_out + k_out]
        rest = refs[o0 + n_out + k_out:]
        scr, csems = rest[:n_scr], rest[n_scr:]
        step = pl.program_id(0)
        for a in range(1, len(grid)):
            step = step * grid[a] + pl.program_id(a)

        @pl.when(step == 0)
        def _():
            cargo.first(cins, couts, csems)
        if cargo.middle is not None:
            @pl.when(step == total // 2)
            def _():
                cargo.middle(cins, couts, csems)
        body(*ins, *outs, *scr)

        @pl.when(step == total - 1)
        def _():
            cargo.last(cins, couts, csems)

    hbm = pl.BlockSpec(memory_space=pl.ANY)
    res = pl.pallas_call(
        carrying, grid=grid, in_specs=list(in_specs) + [hbm] * k_in, out_specs=list(out_specs) + [hbm] * k_out,
        out_shape=list(out_shape) + cargo.outs, scratch_shapes=list(scratch_shapes) + cargo.sems, name=name,
        compiler_params=_cp(len(grid)))(*args, *cargo.ins)
    cargo.results = list(res[n_out:])
    return list(res[:n_out])


def _place():
    return lax.axis_index("x"), lax.axis_index("y"), lax.axis_index("c")


def _other_chips(x, y):
    return [(1 - x, y), (x, 1 - y), (1 - x, 1 - y)]


def _gather_phases(n):
    def env(ins, outs, sems):
        send_sems, recv_sems, local_sems = sems
        x, y, c = _place()
        me, sibling = (x, y, c), (x, y, 1 - c)
        chips = _other_chips(x, y)

        def slot(p):
            return 4 * p[0] + 2 * p[1] + p[2]

        def copy(a, k, block, to, own=False):
            dst = outs[a].at[slot(block)]
            return pltpu.make_async_remote_copy(src_ref=ins[a] if own else dst, dst_ref=dst, send_sem=send_sems.at[a, k],
                                                recv_sem=recv_sems.at[a, k], device_id=to, device_id_type=MESH)

        mine = [pltpu.make_async_copy(ins[a], outs[a].at[slot(me)], local_sems.at[a]) for a in range(n)]
        own = [copy(a, 0, me, sibling, own=True) for a in range(n)]
        own += [copy(a, 1 + j, me, (*chip, c), own=True) for a in range(n) for j, chip in enumerate(chips)]
        return c, me, sibling, chips, copy, mine, own

    def first(ins, outs, sems):
        _, _, _, _, _, mine, own = env(ins, outs, sems)
        for cp in mine + own:
            cp.start()

    def middle(ins, outs, sems):
        c, me, sibling, chips, copy, _, _ = env(ins, outs, sems)
        for j, chip in enumerate(chips):
            for a in range(n):
                copy(a, 1 + j, (*chip, c), me).wait_recv()
                copy(a, 4 + j, (*chip, c), sibling).start()

    def last(ins, outs, sems):
        c, me, sibling, chips, copy, mine, own = env(ins, outs, sems)
        for a in range(n):
            copy(a, 0, sibling, me).wait_recv()
            for j, chip in enumerate(chips):
                copy(a, 4 + j, (*chip, 1 - c), me).wait_recv()
        for cp in own:
            cp.wait_send()
        for a in range(n):
            for j, chip in enumerate(chips):
                copy(a, 4 + j, (*chip, c), sibling).wait_send()
        for cp in mine:
            cp.wait()

    return first, middle, last


def _gather_sems(n):
    return [pltpu.SemaphoreType.DMA((n, 7)), pltpu.SemaphoreType.DMA((n, 7)), pltpu.SemaphoreType.DMA((n,))]


def gather_cargo(blocks):
    first, middle, last = _gather_phases(len(blocks))
    return Cargo(blocks, [_sds((8,) + b.shape, b.dtype) for b in blocks], _gather_sems(len(blocks)), first, last, middle)


def _swap_phases(n, slotted):
    def copies(ins, outs, sems):
        send_sems, recv_sems = sems
        x, y, c = _place()
        return [pltpu.make_async_remote_copy(src_ref=ins[a].at[1 - c] if slotted else ins[a], dst_ref=outs[a],
                                             send_sem=send_sems.at[a], recv_sem=recv_sems.at[a], device_id=(x, y, 1 - c),
                                             device_id_type=MESH) for a in range(n)]

    def first(ins, outs, sems):
        for cp in copies(ins, outs, sems):
            cp.start()

    def last(ins, outs, sems):
        for cp in copies(ins, outs, sems):
            cp.wait()

    return first, last


def swap_cargo(blocks, slotted):
    n = len(blocks)
    first, last = _swap_phases(n, slotted)
    return Cargo(blocks, [_sds(b.shape[1:] if slotted else b.shape, b.dtype) for b in blocks],
                 [pltpu.SemaphoreType.DMA((n,)), pltpu.SemaphoreType.DMA((n,))], first, last)


def _exchange_phases(n):
    def env(ins, outs, sems):
        send_sems, recv_sems, local_sems = sems
        x, y, c = _place()
        me = 2 * x + y
        chips = _other_chips(x, y)
        mine = [pltpu.make_async_copy(ins[a].at[me], outs[a].at[me], local_sems.at[a]) for a in range(n)]
        sends = [pltpu.make_async_remote_copy(src_ref=ins[a].at[2 * chip[0] + chip[1]], dst_ref=outs[a].at[me],
                                              send_sem=send_sems.at[a, j], recv_sem=recv_sems.at[a, j], device_id=(*chip, c),
                                              device_id_type=MESH) for a in range(n) for j, chip in enumerate(chips)]
        recvs = [pltpu.make_async_remote_copy(src_ref=ins[a].at[2 * chip[0] + chip[1]], dst_ref=outs[a].at[2 * chip[0] + chip[1]],
                                              send_sem=send_sems.at[a, j], recv_sem=recv_sems.at[a, j], device_id=(*chip, c),
                                              device_id_type=MESH) for a in range(n) for j, chip in enumerate(chips)]
        return mine, sends, recvs

    def first(ins, outs, sems):
        mine, sends, _ = env(ins, outs, sems)
        for cp in mine + sends:
            cp.start()

    def last(ins, outs, sems):
        mine, sends, recvs = env(ins, outs, sems)
        for cp in recvs:
            cp.wait_recv()
        for cp in sends:
            cp.wait_send()
        for cp in mine:
            cp.wait()

    return first, last


def exchange_cargo(blocks):
    n = len(blocks)
    first, last = _exchange_phases(n)
    return Cargo(blocks, [_sds(b.shape, b.dtype) for b in blocks],
                 [pltpu.SemaphoreType.DMA((n, 3)), pltpu.SemaphoreType.DMA((n, 3)), pltpu.SemaphoreType.DMA((n,))], first, last)


def _comm_call(cargo, name):
    def body(*refs):
        k_in, k_out = len(cargo.ins), len(cargo.outs)
        ins, outs, sems = refs[:k_in], refs[k_in:k_in + k_out], refs[k_in + k_out:]
        cargo.first(ins, outs, sems)
        if cargo.middle is not None:
            cargo.middle(ins, outs, sems)
        cargo.last(ins, outs, sems)

    hbm = pl.BlockSpec(memory_space=pl.ANY)
    return list(pl.pallas_call(body, name=name, in_specs=[hbm] * len(cargo.ins), out_specs=[hbm] * len(cargo.outs),
                               out_shape=cargo.outs, scratch_shapes=cargo.sems)(*cargo.ins))


def nm_fwd(x, gn, sc, sh, w, *, tn, name, wl=0, cargo=None):
    T, N = x.shape[0], w.shape[-1]
    tt = _tile(T, 512)

    def body(x_ref, gn_ref, sc_ref, sh_ref, w_ref, h_ref, o_ref):
        @pl.when(pl.program_id(1) == 0)
        def _():
            h_ref[...] = _normmod(x_ref[...], gn_ref[...], sc_ref[...], sh_ref[...]).astype(BF16)
        o_ref[...] = _dot(h_ref[...], w_ref[...])

    row = pl.BlockSpec((1, D), lambda i, j: (0, 0))
    return _pcall(
        body, grid=(T // tt, N // tn), name=name, cargo=cargo,
        in_specs=[pl.BlockSpec((tt, D), lambda i, j: (i, 0)), row, row, row, _wspec(wl, (D, tn), lambda i, j: (0, j))],
        out_specs=[pl.BlockSpec((tt, D), lambda i, j: (i, 0)), pl.BlockSpec((tt, tn), lambda i, j: (i, j))],
        out_shape=[_sds((T, D), BF16), _sds((T, N))], args=(x, gn, sc, sh, _w3(w)))


def nm_bwd(dys, w, x, gn, sc, sh, dres, *, name, wl=0, cargo=None):
    T = x.shape[0]
    N = w.shape[-1]
    tt = _tile(T, 256)
    widths = [a.shape[1] for a in dys]
    assert sum(widths) == N
    n = len(dys)

    def body(*refs):
        dy_refs, (w_ref, x_ref, gn_ref, sc_ref, sh_ref, dres_ref) = refs[:n], refs[n:n + 6]
        dx_ref, dgn_ref, dsc_ref, dsh_ref, dyb_ref = refs[n + 6:]
        i = pl.program_id(0)
        dyb = jnp.concatenate([r[...].astype(BF16) for r in dy_refs], axis=1) if n > 1 else dy_refs[0][...].astype(BF16)
        dyb_ref[...] = dyb
        dh = _dot_nt(dyb, w_ref[...])
        _, vjp = jax.vjp(_normmod, x_ref[...], gn_ref[...], sc_ref[...], sh_ref[...])
        dx, dgn, dsc, dsh = vjp(dh)
        dx_ref[...] = dres_ref[...] + dx

        @pl.when(i == 0)
        def _():
            dgn_ref[...] = jnp.zeros_like(dgn_ref)
            dsc_ref[...] = jnp.zeros_like(dsc_ref)
            dsh_ref[...] = jnp.zeros_like(dsh_ref)
        dgn_ref[...] += dgn
        dsc_ref[...] += dsc
        dsh_ref[...] += dsh

    row = pl.BlockSpec((1, D), lambda i: (0, 0))
    tile = pl.BlockSpec((tt, D), lambda i: (i, 0))
    return _pcall(
        body, grid=(T // tt,), name=name, cargo=cargo,
        in_specs=[pl.BlockSpec((tt, wd), lambda i: (i, 0)) for wd in widths]
        + [_wspec(wl, (D, N), lambda i: (0, 0)), tile, row, row, row, tile],
        out_specs=[tile, row, row, row, pl.BlockSpec((tt, N), lambda i: (i, 0))],
        out_shape=[_sds((T, D)), _sds((1, D)), _sds((1, D)), _sds((1, D)), _sds((T, N), BF16)],
        args=(*dys, _w3(w), x, gn, sc, sh, dres))


def mm_tn(a, b, *, tn, name):
    T, K = a.shape
    N = b.shape[1]
    tt = _tile(T, 512)

    def body(a_ref, b_ref, o_ref):
        @pl.when(pl.program_id(1) == 0)
        def _():
            o_ref[...] = jnp.zeros_like(o_ref)
        o_ref[...] += _dot_tn(a_ref[...], b_ref[...])

    return pl.pallas_call(
        body, grid=(N // tn, T // tt), name=name,
        in_specs=[pl.BlockSpec((tt, K), lambda j, t: (t, 0)), pl.BlockSpec((tt, tn), lambda j, t: (t, j))],
        out_specs=pl.BlockSpec((K, tn), lambda j, t: (0, j)), out_shape=_sds((K, N)), compiler_params=_cp(2))(a, b)


def mm_tn_split(a, b, *, by_cols, l, L, prev, name):
    T, K = a.shape
    N = b.shape[1]
    tt = _tile(T, 512)
    tn = N // 4 if by_cols else N // 2
    rh, C = (K // 2, N // 4) if by_cols else (K // 8, N)
    nt = T // tt

    def body(a_ref, b_ref, *rest):
        o_ref, acc = rest[-2:]
        t = pl.program_id(1)

        @pl.when(t == 0)
        def _():
            acc[...] = jnp.zeros_like(acc)
        acc[...] += _dot_tn(a_ref[...], b_ref[...])

        @pl.when(t == nt - 1)
        def _():
            if by_cols:
                for h in range(2):
                    o_ref[h] = acc[pl.ds(h * rh, rh), :]
            else:
                for s in range(4):
                    for h in range(2):
                        o_ref[h, s] = acc[pl.ds((2 * s + h) * rh, rh), :]

    if by_cols:
        out_spec = pl.BlockSpec((2, None, None, rh, tn), lambda j, t: (0, j, l, 0, 0))
    else:
        out_spec = pl.BlockSpec((2, 4, None, rh, tn), lambda j, t: (0, 0, l, 0, j))
    in_specs = [pl.BlockSpec((tt, K), lambda j, t: (t, 0)), pl.BlockSpec((tt, tn), lambda j, t: (t, j))]
    args = [a, b]
    aliases = {}
    if prev is not None:
        in_specs.append(pl.BlockSpec(memory_space=pl.ANY))
        args.append(prev)
        aliases = {2: 0}
    return pl.pallas_call(
        body, grid=(N // tn, nt), name=name, in_specs=in_specs, out_specs=out_spec, out_shape=_sds((2, 4, L, rh, C)),
        scratch_shapes=[pltpu.VMEM((K, tn), F32)], input_output_aliases=aliases, compiler_params=_cp(2))(*args)


def _fill_pad(pad_ref, prev, cur, first):
    pad_ref[pl.ds(0, HALO), :] = jnp.where(first, 0.0, prev)
    pad_ref[pl.ds(HALO, cur.shape[0]), :] = cur


def _causal_conv(pad_ref, w, K, tt):
    acc = None
    for k in range(K):
        term = pad_ref[pl.ds(HALO - (K - 1) + k, tt), :] * w[k:k + 1, :]
        acc = term if acc is None else acc + term
    return acc


def _conv_inputs(pc, pp, first, pad_a, pad_b, pad_c):
    def s_of(p):
        return p[:, O_AC:O_AV] * p[:, O_AV:O_BA]

    def u0_of(p):
        return p[:, O_BA:O_BG] * jax.nn.sigmoid(p[:, O_BG:O_Q])

    _fill_pad(pad_a, s_of(pp), s_of(pc), first)
    _fill_pad(pad_b, u0_of(pp), u0_of(pc), first)
    _fill_pad(pad_c, pp[:, O_Q:O_Z], pc[:, O_Q:O_Z], first)


def _mix_specs(T, tt):
    cur = pl.BlockSpec((tt, O_Z), lambda i: (i, 0))
    prev = pl.BlockSpec((HALO, O_Z), lambda i: (jnp.maximum(i * (tt // HALO) - 1, 0), 0))
    gbb = pl.BlockSpec((tt, HD), lambda i: (i, O_GB // HD))
    return cur, prev, gbb


def _full(shape):
    return pl.BlockSpec(shape, lambda i: (0,) * len(shape))


def mix_fwd(proj, wa, wb, bb, ln_g, ln_b, wc, alog_row, dt_row, *, name):
    T = proj.shape[0]
    tt = _tile(T, 256)

    def body(pc_ref, pp_ref, blk_ref, wa_ref, wb_ref, bb_ref, lg_ref, lb_ref, wc_ref, al_ref, dt_ref,
             yab_ref, q_ref, k_ref, v_ref, gb_ref, grow_ref, pad_a, pad_b, pad_c):
        first = pl.program_id(0) == 0
        pc = pc_ref[...]
        _conv_inputs(pc, pp_ref[...], first, pad_a, pad_b, pad_c)
        ya = pc[:, O_AB:O_AC] * _causal_conv(pad_a, wa_ref[...], KA, tt)
        yb = _mix_b_post(_causal_conv(pad_b, wb_ref[...], KB, tt) + bb_ref[...], lg_ref[...], lb_ref[...])
        yab_ref[...] = jnp.concatenate([ya, yb], axis=1)
        pre = _causal_conv(pad_c, wc_ref[...], KC, tt)
        blocks = [pre[:, j * HD:(j + 1) * HD] for j in range(3 * NH)]
        q, k, v, gb, grow = _dn_post(blocks[:NH], blocks[NH:2 * NH], blocks[2 * NH:], blk_ref[...], al_ref[...], dt_ref[...])
        q_ref[...] = jnp.concatenate(q, axis=1)
        k_ref[...] = jnp.concatenate(k, axis=1)
        v_ref[...] = jnp.concatenate(v, axis=1)
        gb_ref[...] = gb
        grow_ref[...] = grow

    cur, prev, gbb = _mix_specs(T, tt)
    t512 = pl.BlockSpec((tt, DDN), lambda i: (i, 0))
    return pl.pallas_call(
        body, grid=(T // tt,), name=name,
        in_specs=[cur, prev, gbb, _full((8, DC)), _full((32, DC)), _full((1, DC)), _full((1, DC)), _full((1, DC)),
                  _full((8, 3 * DDN)), _full((1, HD)), _full((1, HD))],
        out_specs=[t512, t512, t512, t512, pl.BlockSpec((tt, HD), lambda i: (i, 0)), pl.BlockSpec((8, tt), lambda i: (0, i))],
        out_shape=[_sds((T, 2 * DC)), _sds((T, DDN)), _sds((T, DDN)), _sds((T, DDN)), _sds((T, HD)), _sds((8, T))],
        scratch_shapes=[pltpu.VMEM((HALO + tt, DC), F32), pltpu.VMEM((HALO + tt, DC), F32), pltpu.VMEM((HALO + tt, 3 * DDN), F32)],
        compiler_params=_cp(1))(proj, proj, proj, wa, wb, bb, ln_g, ln_b, wc, alog_row, dt_row)


def mix_bwd_point(proj, dyab, dq, dk, dv, dgb, dgrow, wa, wb, bb, ln_g, ln_b, wc, alog_row, dt_row, *, name):
    T = proj.shape[0]
    tt = _tile(T, 256)
    CW = 2 * DC + 3 * DDN

    def body(pc_ref, pp_ref, blk_ref, dyab_ref, dq_ref, dk_ref, dv_ref, dgb_ref, dgrow_ref,
             wa_ref, wb_ref, bb_ref, lg_ref, lb_ref, wc_ref, al_ref, dt_ref,
             dab_ref, dconv_ref, dblk_ref, dbb_ref, dlg_ref, dlb_ref, dal_ref, ddt_ref, pad_a, pad_b, pad_c):
        i = pl.program_id(0)
        pc = pc_ref[...]
        _conv_inputs(pc, pp_ref[...], i == 0, pad_a, pad_b, pad_c)
        ca = _causal_conv(pad_a, wa_ref[...], KA, tt)
        u = _causal_conv(pad_b, wb_ref[...], KB, tt) + bb_ref[...]
        pre = _causal_conv(pad_c, wc_ref[...], KC, tt)
        dyab_v = dyab_ref[...]
        dya, dyb = dyab_v[:, :DC], dyab_v[:, DC:]
        dab_ref[...] = dya * ca
        dca = dya * pc[:, O_AB:O_AC]
        _, vjp_b = jax.vjp(_mix_b_post, u, lg_ref[...], lb_ref[...])
        du, dlg, dlb = vjp_b(dyb)
        blocks = [pre[:, j * HD:(j + 1) * HD] for j in range(3 * NH)]
        _, vjp_c = jax.vjp(_dn_post, blocks[:NH], blocks[NH:2 * NH], blocks[2 * NH:], blk_ref[...], al_ref[...], dt_ref[...])

        def heads(r):
            vv = r[...]
            return [vv[:, h * HD:(h + 1) * HD] for h in range(NH)]
        dpq, dpk, dpv, dblk, dal, ddt = vjp_c((heads(dq_ref), heads(dk_ref), heads(dv_ref), dgb_ref[...], dgrow_ref[...]))
        dconv_ref[...] = jnp.concatenate([dca, du] + dpq + dpk + dpv, axis=1)
        dblk_ref[...] = dblk

        @pl.when(i == 0)
        def _():
            for r in (dbb_ref, dlg_ref, dlb_ref, dal_ref, ddt_ref):
                r[...] = jnp.zeros_like(r)
        dbb_ref[...] += _colsum(du)
        dlg_ref[...] += dlg
        dlb_ref[...] += dlb
        dal_ref[...] += dal
        ddt_ref[...] += ddt

    cur, prev, gbb = _mix_specs(T, tt)
    t512 = pl.BlockSpec((tt, DDN), lambda i: (i, 0))
    t128 = pl.BlockSpec((tt, HD), lambda i: (i, 0))
    return pl.pallas_call(
        body, grid=(T // tt,), name=name,
        in_specs=[cur, prev, gbb, t512, t512, t512, t512, t128, pl.BlockSpec((8, tt), lambda i: (0, i)),
                  _full((8, DC)), _full((32, DC)), _full((1, DC)), _full((1, DC)), _full((1, DC)),
                  _full((8, 3 * DDN)), _full((1, HD)), _full((1, HD))],
        out_specs=[pl.BlockSpec((tt, DC), lambda i: (i, 0)), pl.BlockSpec((tt, CW), lambda i: (i, 0)), t128,
                   _full((1, DC)), _full((1, DC)), _full((1, DC)), _full((1, HD)), _full((1, HD))],
        out_shape=[_sds((T, DC)), _sds((T, CW)), _sds((T, HD)), _sds((1, DC)), _sds((1, DC)), _sds((1, DC)),
                   _sds((1, HD)), _sds((1, HD))],
        scratch_shapes=[pltpu.VMEM((HALO + tt, DC), F32), pltpu.VMEM((HALO + tt, DC), F32), pltpu.VMEM((HALO + tt, 3 * DDN), F32)],
        compiler_params=_cp(1))(proj, proj, proj, dyab, dq, dk, dv, dgb, dgrow, wa, wb, bb, ln_g, ln_b, wc, alog_row, dt_row)


def mix_bwd_conv(proj, dconv, wa, wb, wc, *, name, cargo=None):
    T = proj.shape[0]
    tt = _tile(T, 256)
    CW = 2 * DC + 3 * DDN
    nblk = T // HALO

    def body(pc_ref, pp_ref, dc_ref, dn_ref, wa_ref, wb_ref, wc_ref, dp_ref, dwa_ref, dwb_ref, dwc_ref,
             pad_a, pad_b, pad_c, dpad):
        i = pl.program_id(0)
        last = i == pl.num_programs(0) - 1
        pc = pc_ref[...]
        _conv_inputs(pc, pp_ref[...], i == 0, pad_a, pad_b, pad_c)
        dcur = dc_ref[...]
        dpad[pl.ds(0, tt), :] = dcur
        dpad[pl.ds(tt, HALO), :] = jnp.where(last, 0.0, dn_ref[...])

        @pl.when(i == 0)
        def _():
            for r in (dwa_ref, dwb_ref, dwc_ref):
                r[...] = jnp.zeros_like(r)

        def tconv(lo, hi, w, K, pad_ref, dw_ref):
            dy = dcur[:, lo:hi]
            acc = None
            for k in range(K):
                term = dpad[pl.ds(K - 1 - k, tt), lo:hi] * w[k:k + 1, :]
                acc = term if acc is None else acc + term
                dw_ref[pl.ds(k, 1), :] += _colsum(dy * pad_ref[pl.ds(HALO - (K - 1) + k, tt), :])
            return acc

        ds = tconv(0, DC, wa_ref[...], KA, pad_a, dwa_ref)
        du0 = tconv(DC, 2 * DC, wb_ref[...], KB, pad_b, dwb_ref)
        dqkv = tconv(2 * DC, CW, wc_ref[...], KC, pad_c, dwc_ref)
        a_c, a_v, b_a, b_g = pc[:, O_AC:O_AV], pc[:, O_AV:O_BA], pc[:, O_BA:O_BG], pc[:, O_BG:O_Q]
        sg = jax.nn.sigmoid(b_g)
        dp_ref[...] = jnp.concatenate([ds * a_v, ds * a_c, du0 * sg, du0 * b_a * sg * (1.0 - sg), dqkv], axis=1)

    cur, prev, _ = _mix_specs(T, tt)
    return _pcall(
        body, grid=(T // tt,), name=name, cargo=cargo, args=(proj, proj, dconv, dconv, wa, wb, wc),
        in_specs=[cur, prev, pl.BlockSpec((tt, CW), lambda i: (i, 0)),
                  pl.BlockSpec((HALO, CW), lambda i: (jnp.minimum((i + 1) * (tt // HALO), nblk - 1), 0)),
                  _full((8, DC)), _full((32, DC)), _full((8, 3 * DDN))],
        out_specs=[pl.BlockSpec((tt, O_Z - O_AC), lambda i: (i, 0)), _full((8, DC)), _full((32, DC)), _full((8, 3 * DDN))],
        out_shape=[_sds((T, O_Z - O_AC)), _sds((8, DC)), _sds((32, DC)), _sds((8, 3 * DDN))],
        scratch_shapes=[pltpu.VMEM((HALO + tt, DC), F32), pltpu.VMEM((HALO + tt, DC), F32), pltpu.VMEM((HALO + tt, 3 * DDN), F32),
                        pltpu.VMEM((tt + HALO, CW), F32)])


def _head(v, h):
    return v[:, h * HD:(h + 1) * HD]


def _to_batch(v, cb):
    return jnp.stack([v[c * CH:(c + 1) * CH, h * HD:(h + 1) * HD] for c in range(cb) for h in range(NH)])


def _from_batch(b, cb):
    return jnp.concatenate([jnp.concatenate([b[c * NH + h] for h in range(NH)], axis=1) for c in range(cb)], axis=0)


def _lanes_to_batch(v, cb, lane0):
    return jnp.stack([v[c * CH:(c + 1) * CH, lane0 + h:lane0 + h + 1] for c in range(cb) for h in range(NH)])


def _lane_onehot(h):
    return (lax.broadcasted_iota(jnp.int32, (1, HD), 1) == h).astype(F32)


def _batch_to_lanes(b, cb, lane0):
    return jnp.concatenate([sum(b[c * NH + h] * _lane_onehot(lane0 + h) for h in range(NH)) for c in range(cb)], axis=0)


def dn_intra_fwd(q, k, v, gb, grow3, *, name):
    T = q.shape[0]
    N = T // CH
    cb = 4 if N % 4 == 0 else 1

    def body(q_ref, k_ref, v_ref, gb_ref, gr_ref, u_ref, w_ref, qk_ref, p_ref):
        gbv = gb_ref[...]
        grow = jnp.stack([gr_ref[c, h:h + 1, :] for c in range(cb) for h in range(NH)])
        u, w, qk, p = _dn_intra(_to_batch(q_ref[...], cb), _to_batch(k_ref[...], cb), _to_batch(v_ref[...], cb),
                                _lanes_to_batch(gbv, cb, NH), _lanes_to_batch(gbv, cb, 0), grow)
        u_ref[...] = _from_batch(u, cb)
        w_ref[...] = _from_batch(w, cb)
        qk_ref[...] = qk.reshape(cb, NH, CH, CH)
        p_ref[...] = p.reshape(cb, NH, CH, CH)

    t512 = pl.BlockSpec((cb * CH, DDN), lambda i: (i, 0))
    sq = pl.BlockSpec((cb, NH, CH, CH), lambda i: (i, 0, 0, 0))
    return pl.pallas_call(
        body, grid=(N // cb,), name=name,
        in_specs=[t512, t512, t512, pl.BlockSpec((cb * CH, HD), lambda i: (i, 0)), pl.BlockSpec((cb, 8, CH), lambda i: (i, 0, 0))],
        out_specs=[t512, t512, sq, sq],
        out_shape=[_sds((T, DDN)), _sds((T, DDN)), _sds((N, NH, CH, CH)), _sds((N, NH, CH, CH))],
        compiler_params=_cp(1))(q, k, v, gb, grow3)


def dn_inter_fwd(q, k, u, w, gb, qk, *, name):
    T = q.shape[0]
    N = T // CH
    cb = 4 if N % 4 == 0 else 1

    def body(q_ref, k_ref, u_ref, w_ref, gb_ref, qk_ref, o_ref, s_ref, state):
        @pl.when(pl.program_id(0) == 0)
        def _():
            state[...] = jnp.zeros_like(state)
        for c in range(cb):
            rows = pl.ds(c * CH, CH)
            s = state[...]
            s_ref[c] = s
            s_new, o = _dn_inter(s, _to_batch(q_ref[rows, :], 1), _to_batch(k_ref[rows, :], 1), _to_batch(u_ref[rows, :], 1),
                                 _to_batch(w_ref[rows, :], 1), _lanes_to_batch(gb_ref[rows, :], 1, 0), qk_ref[c])
            state[...] = s_new
            o_ref[rows, :] = _from_batch(o, 1)

    t512 = pl.BlockSpec((cb * CH, DDN), lambda i: (i, 0))
    return pl.pallas_call(
        body, grid=(N // cb,), name=name,
        in_specs=[t512, t512, t512, t512, pl.BlockSpec((cb * CH, HD), lambda i: (i, 0)),
                  pl.BlockSpec((cb, NH, CH, CH), lambda i: (i, 0, 0, 0))],
        out_specs=[t512, pl.BlockSpec((cb, NH, HD, HD), lambda i: (i, 0, 0, 0))],
        out_shape=[_sds((T, DDN)), _sds((N, NH, HD, HD))],
        scratch_shapes=[pltpu.VMEM((NH, HD, HD), F32)], compiler_params=_cp(1))(q, k, u, w, gb, qk)


def dn_inter_bwd(do, q, k, u, w, gb, qk, s_all, *, name, cargo=None):
    T = q.shape[0]
    N = T // CH
    cb = 4 if N % 4 == 0 else 1
    G = N // cb

    def body(do_ref, q_ref, k_ref, u_ref, w_ref, gb_ref, qk_ref, s_ref, dq_ref, dk_ref, du_ref, dw_ref, dg_ref, dqk_ref, dstate):
        @pl.when(pl.program_id(0) == 0)
        def _():
            dstate[...] = jnp.zeros_like(dstate)
        for c in reversed(range(cb)):
            rows = pl.ds(c * CH, CH)
            _, vjp = jax.vjp(_dn_inter, s_ref[c], _to_batch(q_ref[rows, :], 1), _to_batch(k_ref[rows, :], 1),
                             _to_batch(u_ref[rows, :], 1), _to_batch(w_ref[rows, :], 1), _lanes_to_batch(gb_ref[rows, :], 1, 0),
                             qk_ref[c])
            ds, dq, dk, du, dw, dgc, dqk = vjp((dstate[...], _to_batch(do_ref[rows, :], 1)))
            dstate[...] = ds
            dq_ref[rows, :] = _from_batch(dq, 1)
            dk_ref[rows, :] = _from_batch(dk, 1)
            du_ref[rows, :] = _from_batch(du, 1)
            dw_ref[rows, :] = _from_batch(dw, 1)
            dg_ref[rows, :] = _batch_to_lanes(dgc, 1, 0)
            dqk_ref[c] = dqk

    t512 = pl.BlockSpec((cb * CH, DDN), lambda i: (G - 1 - i, 0))
    t128 = pl.BlockSpec((cb * CH, HD), lambda i: (G - 1 - i, 0))
    qkb = pl.BlockSpec((cb, NH, CH, CH), lambda i: (G - 1 - i, 0, 0, 0))
    return _pcall(
        body, grid=(G,), name=name, cargo=cargo, args=(do, q, k, u, w, gb, qk, s_all),
        in_specs=[t512, t512, t512, t512, t512, t128, qkb, pl.BlockSpec((cb, NH, HD, HD), lambda i: (G - 1 - i, 0, 0, 0))],
        out_specs=[t512, t512, t512, t512, t128, qkb],
        out_shape=[_sds((T, DDN))] * 4 + [_sds((T, HD)), _sds((N, NH, CH, CH))],
        scratch_shapes=[pltpu.VMEM((NH, HD, HD), F32)])


def dn_intra_bwd(du, dw, dqk, dq_in, dk_in, dg_in, q, k, v, gb, grow3, p_all, *, name):
    T = q.shape[0]
    N = T // CH
    cb = 4 if N % 4 == 0 else 1

    def body(du_ref, dw_ref, dqk_ref, dqi_ref, dki_ref, dgi_ref, q_ref, k_ref, v_ref, gb_ref, gr_ref, p_ref,
             dq_ref, dk_ref, dv_ref, dgb_ref, dgr_ref):
        B = cb * NH
        gbv = gb_ref[...]
        grow = jnp.stack([gr_ref[c, h:h + 1, :] for c in range(cb) for h in range(NH)])
        _, vjp = jax.vjp(functools.partial(_dn_intra, p_known=p_ref[...].reshape(B, CH, CH)),
                         _to_batch(q_ref[...], cb), _to_batch(k_ref[...], cb), _to_batch(v_ref[...], cb),
                         _lanes_to_batch(gbv, cb, NH), _lanes_to_batch(gbv, cb, 0), grow)
        dq, dk, dv, dbeta, dgc, dgr = vjp((_to_batch(du_ref[...], cb), _to_batch(dw_ref[...], cb), dqk_ref[...].reshape(B, CH, CH),
                                           jnp.zeros((B, CH, CH), F32)))
        dq_ref[...] = dqi_ref[...] + _from_batch(dq, cb)
        dk_ref[...] = dki_ref[...] + _from_batch(dk, cb)
        dv_ref[...] = _from_batch(dv, cb)
        dgb_ref[...] = dgi_ref[...] + _batch_to_lanes(dgc, cb, 0) + _batch_to_lanes(dbeta, cb, NH)
        for c in range(cb):
            dgr_ref[c] = jnp.concatenate([dgr[c * NH + h] for h in range(NH)] + [jnp.zeros((8 - NH, CH), F32)], axis=0)

    t512 = pl.BlockSpec((cb * CH, DDN), lambda i: (i, 0))
    t128 = pl.BlockSpec((cb * CH, HD), lambda i: (i, 0))
    qkb = pl.BlockSpec((cb, NH, CH, CH), lambda i: (i, 0, 0, 0))
    grb = pl.BlockSpec((cb, 8, CH), lambda i: (i, 0, 0))
    return pl.pallas_call(
        body, grid=(N // cb,), name=name,
        in_specs=[t512, t512, qkb, t512, t512, t128, t512, t512, t512, t128, grb, qkb],
        out_specs=[t512, t512, t512, t128, grb],
        out_shape=[_sds((T, DDN))] * 3 + [_sds((T, HD)), _sds((N, 8, CH))],
        compiler_params=_cp(1))(du, dw, dqk, dq_in, dk_in, dg_in, q, k, v, gb, grow3, p_all)


def _z_specs(tt):
    return [pl.BlockSpec((tt, DC), lambda i: (i, O_Z // DC)), pl.BlockSpec((tt, DC), lambda i: (i, O_Z // DC + 1))]


def mixout_fwd(x, yab, o, proj, gdn, g1, w_out, *, name, wl=0, cargo=None):
    T = x.shape[0]
    tt = _tile(T, 512)

    def body(x_ref, yab_ref, o_ref, z0_ref, z1_ref, gdn_ref, g1_ref, w_ref, ycat_ref, mix_ref, xo_ref):
        ov = o_ref[...]
        z = jnp.concatenate([z0_ref[...], z1_ref[...]], axis=1)
        yc = [_yc(_head(ov, h), _head(z, h), gdn_ref[...]) for h in range(NH)]
        ycat = jnp.concatenate([yab_ref[...]] + yc, axis=1).astype(BF16)
        ycat_ref[...] = ycat
        mix = _dot(ycat, w_ref[...])
        mix_ref[...] = mix
        xo_ref[...] = x_ref[...] + g1_ref[...] * mix

    tile = pl.BlockSpec((tt, D), lambda i: (i, 0))
    t512 = pl.BlockSpec((tt, DDN), lambda i: (i, 0))
    return _pcall(
        body, grid=(T // tt,), name=name, cargo=cargo, args=(x, yab, o, proj, proj, gdn, g1, _w3(w_out)),
        in_specs=[tile, t512, t512] + _z_specs(tt) + [_full((1, HD)), _full((1, D)), _wspec(wl, (D, D), lambda i: (0, 0))],
        out_specs=[tile, tile, tile], out_shape=[_sds((T, D), BF16), _sds((T, D)), _sds((T, D))])


def mixout_bwd(dx, mix, o, proj, gdn, g1, w_out, *, name, wl=0, cargo=None):
    T = dx.shape[0]
    tt = _tile(T, 256)

    def body(dx_ref, mix_ref, o_ref, z0_ref, z1_ref, gdn_ref, g1_ref, w_ref, dmix_ref, dyab_ref, do_ref, dz_ref, dg1_ref, dgdn_ref):
        i = pl.program_id(0)
        dxv = dx_ref[...]
        dmix = (dxv * g1_ref[...]).astype(BF16)
        dmix_ref[...] = dmix
        dycat = _dot_nt(dmix, w_ref[...])
        dyab_ref[...] = dycat[:, :2 * DC]
        ov = o_ref[...]
        z = jnp.concatenate([z0_ref[...], z1_ref[...]], axis=1)
        dos, dzs = [], []
        dgdn = jnp.zeros((1, HD), F32)
        for h in range(NH):
            _, vjp = jax.vjp(_yc, _head(ov, h), _head(z, h), gdn_ref[...])
            do, dz, dg = vjp(dycat[:, 2 * DC + h * HD:2 * DC + (h + 1) * HD])
            dos.append(do)
            dzs.append(dz)
            dgdn = dgdn + dg
        do_ref[...] = jnp.concatenate(dos, axis=1)
        dz_ref[...] = jnp.concatenate(dzs, axis=1)

        @pl.when(i == 0)
        def _():
            dg1_ref[...] = jnp.zeros_like(dg1_ref)
            dgdn_ref[...] = jnp.zeros_like(dgdn_ref)
        dg1_ref[...] += _colsum(dxv * mix_ref[...])
        dgdn_ref[...] += dgdn

    tile = pl.BlockSpec((tt, D), lambda i: (i, 0))
    t512 = pl.BlockSpec((tt, DDN), lambda i: (i, 0))
    return _pcall(
        body, grid=(T // tt,), name=name, cargo=cargo, args=(dx, mix, o, proj, proj, gdn, g1, _w3(w_out)),
        in_specs=[tile, tile, t512] + _z_specs(tt) + [_full((1, HD)), _full((1, D)), _wspec(wl, (D, D), lambda i: (0, 0))],
        out_specs=[tile, t512, t512, t512, _full((1, D)), _full((1, HD))],
        out_shape=[_sds((T, D), BF16), _sds((T, DDN)), _sds((T, DDN)), _sds((T, DDN)), _sds((1, D)), _sds((1, HD))])


FK = DFF // 2


def ffnout_fwd(x, gu, g2, w, *, name, wl=0, cargo=None):
    T = x.shape[0]
    tt = _tile(T, 512)

    def body(x_ref, gate_ref, up_ref, g2_ref, w_ref, act_ref, f_ref, xo_ref):
        kk = pl.program_id(1)
        act = (_silu(gate_ref[...]) * up_ref[...]).astype(BF16)
        act_ref[...] = act
        part = _dot(act, w_ref[...])

        @pl.when(kk == 0)
        def _():
            f_ref[...] = part

        @pl.when(kk == 1)
        def _():
            f = f_ref[...] + part
            f_ref[...] = f
            xo_ref[...] = x_ref[...] + g2_ref[...] * f

    tile = pl.BlockSpec((tt, D), lambda i, kk: (i, 0))
    return _pcall(
        body, grid=(T // tt, 2), name=name, cargo=cargo, args=(x, gu, gu, g2, _w3(w)),
        in_specs=[tile, pl.BlockSpec((tt, FK), lambda i, kk: (i, kk)), pl.BlockSpec((tt, FK), lambda i, kk: (i, 2 + kk)),
                  pl.BlockSpec((1, D), lambda i, kk: (0, 0)), _wspec(wl, (FK, D), lambda i, kk: (kk, 0))],
        out_specs=[pl.BlockSpec((tt, FK), lambda i, kk: (i, kk)), tile, tile],
        out_shape=[_sds((T, DFF), BF16), _sds((T, D)), _sds((T, D))])


def ffnout_bwd(dx, f, gu, g2, w, *, name, wl=0, cargo=None):
    T = dx.shape[0]
    tt = _tile(T, 512)

    def body(dx_ref, f_ref, gate_ref, up_ref, g2_ref, w_ref, df_ref, dgate_ref, dup_ref, dg2_ref):
        i, kk = pl.program_id(0), pl.program_id(1)
        dxv = dx_ref[...]
        df = (dxv * g2_ref[...]).astype(BF16)
        dact = _dot_nt(df, w_ref[...])
        gate, up = gate_ref[...], up_ref[...]
        sg = jax.nn.sigmoid(gate)
        dgate_ref[...] = (dact * up * (sg * (1.0 + gate * (1.0 - sg)))).astype(BF16)
        dup_ref[...] = (dact * (gate * sg)).astype(BF16)

        @pl.when(kk == 0)
        def _():
            df_ref[...] = df

        @pl.when((i == 0) & (kk == 0))
        def _():
            dg2_ref[...] = jnp.zeros_like(dg2_ref)

        @pl.when(kk == 0)
        def _():
            dg2_ref[...] += _colsum(dxv * f_ref[...])

    tile = pl.BlockSpec((tt, D), lambda i, kk: (i, 0))
    return _pcall(
        body, grid=(T // tt, 2), name=name, cargo=cargo, args=(dx, f, gu, gu, g2, _w3(w)),
        in_specs=[tile, tile, pl.BlockSpec((tt, FK), lambda i, kk: (i, kk)), pl.BlockSpec((tt, FK), lambda i, kk: (i, 2 + kk)),
                  pl.BlockSpec((1, D), lambda i, kk: (0, 0)), _wspec(wl, (FK, D), lambda i, kk: (kk, 0))],
        out_specs=[tile, pl.BlockSpec((tt, FK), lambda i, kk: (i, kk)), pl.BlockSpec((tt, FK), lambda i, kk: (i, kk)),
                   pl.BlockSpec((1, D), lambda i, kk: (0, 0))],
        out_shape=[_sds((T, D), BF16), _sds((T, DFF), BF16), _sds((T, DFF), BF16), _sds((1, D))])


def loss_head(x, target, gf, *, name):
    T = x.shape[0]
    tt = _tile(T, 256)

    def body(x_ref, t_ref, g_ref, loss_ref, dx_ref, dg_ref):
        i = pl.program_id(0)
        tv = t_ref[...]
        y, vjp = jax.vjp(_rms, x_ref[...], g_ref[...])
        err = y - tv
        dx, dg = vjp(err * (1.0 / D))
        dx_ref[...] = dx

        @pl.when(i == 0)
        def _():
            loss_ref[...] = jnp.zeros_like(loss_ref)
            dg_ref[...] = jnp.zeros_like(dg_ref)
        loss_ref[...] += 0.5 * jnp.sum(jnp.mean(err * err, axis=-1, keepdims=True), axis=0, keepdims=True)
        dg_ref[...] += dg

    tile = pl.BlockSpec((tt, D), lambda i: (i, 0))
    return pl.pallas_call(
        body, grid=(T // tt,), name=name, in_specs=[tile, tile, _full((1, D))],
        out_specs=[_full((1, HD)), tile, _full((1, D))], out_shape=[_sds((1, HD)), _sds((T, D)), _sds((1, D))],
        compiler_params=_cp(1))(x, target, gf)


def _pad_rows(a, rows):
    return jnp.pad(a, ((0, rows - a.shape[0]), (0, 0)))


def _row128(v):
    return jnp.pad(v, (0, HD - v.shape[0]))[None, :]


def _grow3(grow):
    return grow.reshape(8, -1, CH).transpose(1, 0, 2)


def _grow2(grow3):
    return grow3.transpose(1, 0, 2).reshape(8, -1)


def layer_params(l, mod, p):
    m = mod[l].reshape(6, 1, D)
    return dict(
        sh1=m[0], sc1=m[1], g1=m[2], sh2=m[3], sc2=m[4], g2=m[5],
        gn1=p["norm_mix_g"][l][None, :], gn2=p["norm_ffn_g"][l][None, :],
        wa=_pad_rows(p["conv_a_w"][l], 8), wb=_pad_rows(p["conf_dw_w"][l], 32), wc=_pad_rows(p["dn_conv_w"][l], 8),
        bb=p["conf_dw_b"][l][None, :], ln_g=p["conf_ln_g"][l][None, :], ln_b=p["conf_ln_b"][l][None, :],
        alog=_row128(p["dn_a_log"][l]), dt=_row128(p["dn_dt_bias"][l]), gdn=p["dn_norm_g"][l][None, :])


def layer_fwd(l, x, lp, w_in, w_out, w_ffn_in, w_ffn_out, wl=0, cargos=None):
    cg = (cargos or {}).get
    s = {"x0": x}
    s["h1"], s["proj"] = nm_fwd(x, lp["gn1"], lp["sc1"], lp["sh1"], w_in, tn=NP // 3, name=f"proj_fwd_{l}", wl=wl,
                                cargo=cg("w_in"))
    yab, s["q"], s["k"], s["v"], s["gb"], grow = mix_fwd(
        s["proj"], lp["wa"], lp["wb"], lp["bb"], lp["ln_g"], lp["ln_b"], lp["wc"], lp["alog"], lp["dt"], name=f"mix_fwd_{l}")
    s["grow3"] = _grow3(grow)
    s["u"], s["w"], s["qk"], s["p"] = dn_intra_fwd(s["q"], s["k"], s["v"], s["gb"], s["grow3"], name=f"dn_intra_fwd_{l}")
    s["o"], s["s_all"] = dn_inter_fwd(s["q"], s["k"], s["u"], s["w"], s["gb"], s["qk"], name=f"dn_inter_fwd_{l}")
    s["ycat"], s["mix"], s["x1"] = mixout_fwd(x, yab, s["o"], s["proj"], lp["gdn"], lp["g1"], w_out, name=f"mixout_fwd_{l}",
                                              wl=wl, cargo=cg("w_out"))
    s["h2"], s["gu"] = nm_fwd(s["x1"], lp["gn2"], lp["sc2"], lp["sh2"], w_ffn_in, tn=2 * DFF // 4, name=f"ffnin_fwd_{l}", wl=wl,
                              cargo=cg("w_ffn_in"))
    s["act"], s["f"], x2 = ffnout_fwd(s["x1"], s["gu"], lp["g2"], w_ffn_out, name=f"ffnout_fwd_{l}", wl=wl, cargo=cg("w_ffn_out"))
    return x2, s


def layer_bwd(l, dx2, s, lp, w_in, w_out, w_ffn_in, w_ffn_out, wl=0, carry=None):
    g = {}
    carry = carry or (lambda stage: None)
    df, dgate, dup, dg2 = ffnout_bwd(dx2, s["f"], s["gu"], lp["g2"], w_ffn_out, name=f"ffnout_bwd_{l}", wl=wl, cargo=carry("swap"))
    g["w_ffn_out"] = mm_tn_split(s["act"], df, by_cols=False, l=0, L=1, prev=None, name=f"ffnout_dw_{l}")[:, :, 0]
    dx1, g["norm_ffn_g"], dsc2, dsh2, dgu = nm_bwd([dgate, dup], w_ffn_in, s["x1"], lp["gn2"], lp["sc2"], lp["sh2"], dx2,
                                                   name=f"ffnin_bwd_{l}", wl=wl, cargo=carry("exchange_a"))
    g["w_ffn_in"] = mm_tn_split(s["h2"], dgu, by_cols=True, l=0, L=1, prev=None, name=f"ffnin_dw_{l}")[:, :, 0]
    dmix, dyab, do, dz, dg1, g["dn_norm_g"] = mixout_bwd(dx1, s["mix"], s["o"], s["proj"], lp["gdn"], lp["g1"], w_out,
                                                         name=f"mixout_bwd_{l}", wl=wl, cargo=carry("exchange_b"))
    g["w_out"] = mm_tn_split(s["ycat"], dmix, by_cols=False, l=0, L=1, prev=None, name=f"mixout_dw_{l}")[:, :, 0]
    dq_i, dk_i, du, dw, dg_i, dqk = dn_inter_bwd(do, s["q"], s["k"], s["u"], s["w"], s["gb"], s["qk"], s["s_all"],
                                                 name=f"dn_inter_bwd_{l}", cargo=carry("exchange_c"))
    dq, dk, dv, dgb, dgrow3 = dn_intra_bwd(du, dw, dqk, dq_i, dk_i, dg_i, s["q"], s["k"], s["v"], s["gb"], s["grow3"],
                                           s["p"], name=f"dn_intra_bwd_{l}")
    dab, dconv, dblk, g["conf_dw_b"], g["conf_ln_g"], g["conf_ln_b"], dal, ddt = mix_bwd_point(
        s["proj"], dyab, dq, dk, dv, dgb, _grow2(dgrow3), lp["wa"], lp["wb"], lp["bb"], lp["ln_g"], lp["ln_b"], lp["wc"],
        lp["alog"], lp["dt"], name=f"mix_bwd_point_{l}")
    dpb, dwa, dwb, dwc = mix_bwd_conv(s["proj"], dconv, lp["wa"], lp["wb"], lp["wc"], name=f"mix_bwd_conv_{l}",
                                      cargo=carry("join"))
    dx0, g["norm_mix_g"], dsc1, dsh1, dproj = nm_bwd([dab, dpb, dz, dblk], w_in, s["x0"], lp["gn1"], lp["sc1"], lp["sh1"], dx1,
                                                     name=f"proj_bwd_{l}", wl=wl)
    g_in = mm_tn(s["h1"], dproj, tn=NP // 3, name=f"proj_dw_{l}")[:, :IN_COLS]
    g["w_in"] = g_in.reshape(2, D // 2, 4, IN_COLS // 4).transpose(0, 2, 1, 3)
    g["conv_a_w"], g["conf_dw_w"], g["dn_conv_w"] = dwa[:KA], dwb[:KB], dwc[:KC]
    g["dn_a_log"], g["dn_dt_bias"] = dal[0, :NH], ddt[0, :NH]
    g["mod"] = jnp.concatenate([dsh1, dsc1, dg1, dsh2, dsc2, dg2], axis=1)
    return dx0, g


EW_BLOCK_BYTES = 1 << 20


def _row_tile(R, C, mult=8):
    best = None
    for rt in range(mult, R + 1, mult):
        if R % rt == 0 and rt * C * 4 <= EW_BLOCK_BYTES:
            best = rt
    return best if best is not None else R


def add_half_bf16(g2, recv, core, *, name):
    _, B, R, C = g2.shape
    rt = _row_tile(R, C, 16)

    def body(core_ref, a_ref, b_ref, o_ref):
        o_ref[...] = (a_ref[...] + b_ref[...]).astype(BF16)

    spec = pl.BlockSpec((1, rt, C), lambda b, r, core_ref: (b, r, 0))
    return pl.pallas_call(
        body, name=name, out_shape=_sds((B, R, C), BF16), compiler_params=_cp(2),
        grid_spec=pltpu.PrefetchScalarGridSpec(
            num_scalar_prefetch=1, grid=(B, R // rt),
            in_specs=[pl.BlockSpec((None, 1, rt, C), lambda b, r, core_ref: (core_ref[0], b, r, 0)), spec],
            out_specs=spec))(core, g2, recv)


def adamw_halves(w, g_mine, g_theirs, core, m, v, *, name):
    L, R, C = w.shape
    rh = R // 2
    rt = _row_tile(rh, C)
    nr = rh // rt

    def body(core_ref, w_ref, gm_ref, gt_ref, m_ref, v_ref, go_ref, d_ref, mo_ref, vo_ref):
        g = jnp.where(pl.program_id(1) == core_ref[0], gm_ref[...], gt_ref[...])
        go_ref[...] = g
        d_ref[...], mo_ref[...], vo_ref[...] = _adamw_math(w_ref[...], g, m_ref[...], v_ref[...])

    spec = pl.BlockSpec((1, rt, C), lambda l, h, r, core_ref: (l, h * nr + r, 0))
    half = pl.BlockSpec((1, rt, C), lambda l, h, r, core_ref: (l, r, 0))
    return pl.pallas_call(
        body, name=name, out_shape=[_sds((L, R, C))] * 4, compiler_params=_cp(3),
        grid_spec=pltpu.PrefetchScalarGridSpec(num_scalar_prefetch=1, grid=(L, 2, nr), in_specs=[spec, half, half, spec, spec],
                                               out_specs=[spec] * 4))(core, w, g_mine, g_theirs, m, v)


def ew_call(fn, ins, n_out, *, name):
    B, R, C = ins[0].shape
    rt = _row_tile(R, C)
    n = len(ins)

    def body(*refs):
        outs = fn(*[r[...] for r in refs[:n]])
        for r, o in zip(refs[n:], outs):
            r[...] = o

    spec = pl.BlockSpec((1, rt, C), lambda b, r: (b, r, 0))
    return pl.pallas_call(body, grid=(B, R // rt), name=name, in_specs=[spec] * n, out_specs=[spec] * n_out,
                          out_shape=[_sds((B, R, C))] * n_out, compiler_params=_cp(2))(*ins)


def _adamw_math(w, g, m, v):
    m = ADAM_B1 * m + (1.0 - ADAM_B1) * g
    v = ADAM_B2 * v + (1.0 - ADAM_B2) * jnp.square(g)
    m_hat = m / (1.0 - ADAM_B1 ** ADAM_STEP)
    v_hat = v / (1.0 - ADAM_B2 ** ADAM_STEP)
    return -ADAM_LR * (m_hat / (jnp.sqrt(v_hat) + ADAM_EPS) + ADAM_WD * w), m, v


def adamw(w, g, m, v, *, name):
    shape = w.shape
    r3 = lambda a: a.reshape((-1,) + shape[-2:])
    return [o.reshape(shape) for o in ew_call(_adamw_math, [r3(w), r3(g), r3(m), r3(v)], 3, name=name)]


def sum_slots(a, *, name):
    S, B, R, C = a.shape
    rt = _row_tile(R, C, 16)

    def body(*refs):
        acc = refs[0][0, 0].astype(F32)
        for r in refs[1:S]:
            acc = acc + r[0, 0].astype(F32)
        refs[S][0] = acc

    def spec(s):
        return pl.BlockSpec((1, 1, rt, C), lambda b, r: (s, b, r, 0))
    return pl.pallas_call(body, grid=(B, R // rt), name=name, in_specs=[spec(s) for s in range(S)],
                          out_specs=pl.BlockSpec((1, rt, C), lambda b, r: (b, r, 0)), out_shape=_sds((B, R, C)),
                          compiler_params=_cp(2))(*([a] * S))


ADA_SH = 6 * D // 4
ADA_TN = 512


def mod_fwd(c_all, w_ada, b_my, *, name):
    L = w_ada.shape[0]

    def body(c_ref, w_ref, b_ref, o_ref):
        o_ref[0] = _dot(_silu(c_ref[...]).astype(BF16), w_ref[0].astype(BF16)) + b_ref[0]

    return pl.pallas_call(
        body, grid=(L, ADA_SH // ADA_TN), name=name,
        in_specs=[pl.BlockSpec((8, D), lambda l, j: (0, 0)), pl.BlockSpec((1, D, ADA_TN), lambda l, j: (l, 0, j)),
                  pl.BlockSpec((1, 1, ADA_TN), lambda l, j: (l, 0, j))],
        out_specs=pl.BlockSpec((1, 8, ADA_TN), lambda l, j: (l, 0, j)), out_shape=_sds((L, 8, ADA_SH)),
        compiler_params=_cp(2))(c_all, w_ada, b_my)


def wada_grad(c_all, dmod, *, name):
    L = dmod.shape[0]

    def body(c_ref, d_ref, o_ref):
        o_ref[0] = _dot_tn(_silu(c_ref[...]), d_ref[0], HI)

    return pl.pallas_call(
        body, grid=(L, ADA_SH // ADA_TN), name=name,
        in_specs=[pl.BlockSpec((8, D), lambda l, j: (0, 0)), pl.BlockSpec((1, 8, ADA_TN), lambda l, j: (l, 0, j))],
        out_specs=pl.BlockSpec((1, D, ADA_TN), lambda l, j: (l, 0, j)), out_shape=_sds((L, D, ADA_SH)),
        compiler_params=_cp(2))(c_all, dmod)


def _place():
    return lax.axis_index("x"), lax.axis_index("y"), lax.axis_index("c")


def _other_chips(x, y):
    return [(1 - x, y), (x, 1 - y), (1 - x, 1 - y)]


def allgather8(blocks, *, space, name):
    n = len(blocks)

    def body(*refs):
        ins, outs = refs[:n], refs[n:2 * n]
        send_sems, recv_sems, local_sems = refs[2 * n:]
        x, y, c = _place()
        me, sibling = (x, y, c), (x, y, 1 - c)
        chips = _other_chips(x, y)

        def slot(p):
            return 4 * p[0] + 2 * p[1] + p[2]

        def copy(a, k, block, to, src=None):
            dst = outs[a].at[slot(block)]
            return pltpu.make_async_remote_copy(src_ref=dst if src is None else src, dst_ref=dst, send_sem=send_sems.at[a, k],
                                                recv_sem=recv_sems.at[a, k], device_id=to, device_id_type=MESH)

        mine = [pltpu.make_async_copy(ins[a], outs[a].at[slot(me)], local_sems.at[a]) for a in range(n)]
        for cp in mine:
            cp.start()
        first = []
        for a in range(n):
            first.append(copy(a, 0, me, sibling, src=ins[a]))
            first += [copy(a, 1 + j, me, (*chip, c), src=ins[a]) for j, chip in enumerate(chips)]
        for cp in first:
            cp.start()
        passed = []
        for j, chip in enumerate(chips):
            for a in range(n):
                copy(a, 1 + j, (*chip, c), me).wait_recv()
                cp = copy(a, 4 + j, (*chip, c), sibling)
                cp.start()
                passed.append(cp)
        for a in range(n):
            copy(a, 0, sibling, me).wait_recv()
            for j, chip in enumerate(chips):
                copy(a, 4 + j, (*chip, 1 - c), me).wait_recv()
        for cp in first + passed:
            cp.wait_send()
        for cp in mine:
            cp.wait()

    spec = pl.BlockSpec(memory_space=space)
    return pl.pallas_call(
        body, name=name, in_specs=[spec] * n, out_specs=[spec] * n,
        out_shape=[_sds((8,) + b.shape, b.dtype) for b in blocks],
        scratch_shapes=[pltpu.SemaphoreType.DMA((n, 7)), pltpu.SemaphoreType.DMA((n, 7)), pltpu.SemaphoreType.DMA((n,))],
        compiler_params=pltpu.CompilerParams(vmem_limit_bytes=VMEM_LIMIT))(*blocks)


def sibling_swap(blocks, *, name, slotted=True):
    n = len(blocks)

    def body(*refs):
        ins, outs = refs[:n], refs[n:2 * n]
        send_sems, recv_sems = refs[2 * n:]
        x, y, c = _place()
        cps = [pltpu.make_async_remote_copy(src_ref=ins[a].at[1 - c] if slotted else ins[a], dst_ref=outs[a],
                                            send_sem=send_sems.at[a], recv_sem=recv_sems.at[a], device_id=(x, y, 1 - c),
                                            device_id_type=MESH)
               for a in range(n)]
        for cp in cps:
            cp.start()
        for cp in cps:
            cp.wait()

    spec = pl.BlockSpec(memory_space=pl.ANY)
    return pl.pallas_call(
        body, name=name, in_specs=[spec] * n, out_specs=[spec] * n,
        out_shape=[_sds(b.shape[1:] if slotted else b.shape, b.dtype) for b in blocks],
        scratch_shapes=[pltpu.SemaphoreType.DMA((n,)), pltpu.SemaphoreType.DMA((n,))])(*blocks)


def chip_exchange(blocks, *, name):
    n = len(blocks)

    def body(*refs):
        ins, outs = refs[:n], refs[n:2 * n]
        send_sems, recv_sems, local_sems = refs[2 * n:]
        x, y, c = _place()
        me = 2 * x + y
        chips = _other_chips(x, y)
        mine = [pltpu.make_async_copy(ins[a].at[me], outs[a].at[me], local_sems.at[a]) for a in range(n)]
        for cp in mine:
            cp.start()
        cps = []
        for a in range(n):
            for j, chip in enumerate(chips):
                cps.append(pltpu.make_async_remote_copy(
                    src_ref=ins[a].at[2 * chip[0] + chip[1]], dst_ref=outs[a].at[me], send_sem=send_sems.at[a, j],
                    recv_sem=recv_sems.at[a, j], device_id=(*chip, c), device_id_type=MESH))
        for cp in cps:
            cp.start()
        for a in range(n):
            for j, chip in enumerate(chips):
                s = 2 * chip[0] + chip[1]
                pltpu.make_async_remote_copy(src_ref=ins[a].at[s], dst_ref=outs[a].at[s], send_sem=send_sems.at[a, j],
                                             recv_sem=recv_sems.at[a, j], device_id=(*chip, c), device_id_type=MESH).wait_recv()
        for cp in cps:
            cp.wait_send()
        for cp in mine:
            cp.wait()

    spec = pl.BlockSpec(memory_space=pl.ANY)
    return pl.pallas_call(
        body, name=name, in_specs=[spec] * n, out_specs=[spec] * n, out_shape=[_sds(b.shape, b.dtype) for b in blocks],
        scratch_shapes=[pltpu.SemaphoreType.DMA((n, 3)), pltpu.SemaphoreType.DMA((n, 3)), pltpu.SemaphoreType.DMA((n,))])(*blocks)


def halves_join(halves, *, name):
    n = len(halves)

    def body(*refs):
        ins, outs = refs[:n], refs[n:2 * n]
        send_sems, recv_sems, local_sems = refs[2 * n:]
        x, y, c = _place()
        cps, mine = [], []
        for a in range(n):
            mine.append(pltpu.make_async_copy(ins[a], outs[a].at[c], local_sems.at[a]))
            cps.append(pltpu.make_async_remote_copy(src_ref=ins[a], dst_ref=outs[a].at[c], send_sem=send_sems.at[a],
                                                    recv_sem=recv_sems.at[a], device_id=(x, y, 1 - c), device_id_type=MESH))
        for cp in mine + cps:
            cp.start()
        for a in range(n):
            pltpu.make_async_remote_copy(src_ref=ins[a], dst_ref=outs[a].at[1 - c], send_sem=send_sems.at[a], recv_sem=recv_sems.at[a],
                                         device_id=(x, y, 1 - c), device_id_type=MESH).wait_recv()
        for cp in cps:
            cp.wait_send()
        for cp in mine:
            cp.wait()

    spec = pl.BlockSpec(memory_space=pl.ANY)
    return pl.pallas_call(
        body, name=name, in_specs=[spec] * n, out_specs=[spec] * n,
        out_shape=[_sds((2,) + h.shape, h.dtype) for h in halves],
        scratch_shapes=[pltpu.SemaphoreType.DMA((n,)), pltpu.SemaphoreType.DMA((n,)), pltpu.SemaphoreType.DMA((n,))])(*halves)


BIG = ("w_in", "w_out", "w_ffn_in", "w_ffn_out")
COL_SHARDED = {"w_in": True, "w_out": False, "w_ffn_in": True, "w_ffn_out": False}
SMALL = ("norm_mix_g", "norm_ffn_g", "conv_a_w", "conf_dw_w", "conf_dw_b", "conf_ln_g", "conf_ln_b", "dn_conv_w",
         "dn_a_log", "dn_dt_bias", "dn_norm_g")
SMALL_SHARDED = ("conv_a_w", "conf_dw_w", "dn_conv_w")


def _half_rows(a, c):
    rh = a.shape[1] // 2
    return lax.dynamic_slice_in_dim(a, c * rh, rh, axis=1)


def _assemble(name, g):
    _, L, rh, C = g.shape
    g = g.reshape(4, 2, L, rh, C)
    if COL_SHARDED[name]:
        return g.transpose(2, 1, 3, 0, 4).reshape(L, 2 * rh, 4 * C)
    return g.transpose(2, 0, 1, 3, 4).reshape(L, 8 * rh, C)


def _split_for_reduce(name, g, c):
    L, R, C = g.shape
    if COL_SHARDED[name]:
        t = g.reshape(L, 2, R // 2, 4, C // 4).transpose(1, 3, 0, 2, 4)
    else:
        t = g.reshape(L, 4, 2, R // 8, C).transpose(2, 1, 0, 3, 4)
    mine = lax.dynamic_index_in_dim(t, c, 0, keepdims=False)
    other = lax.dynamic_index_in_dim(t, 1 - c, 0, keepdims=False)
    return mine, other


def _pack(parts):
    flat = jnp.concatenate([p.reshape(-1) for p in parts])
    n = flat.shape[0]
    rows = -(-n // (8 * HD)) * 8
    return jnp.pad(flat, (0, rows * HD - n)).reshape(rows, HD)


def _unpack(buf, shapes):
    flat = buf.reshape(-1)
    out, off = [], 0
    for s in shapes:
        n = 1
        for d in s:
            n *= d
        out.append(flat[off:off + n].reshape(s))
        off += n
    return out


class _Reduction:
    STAGES = ("swap", "exchange_a", "exchange_b", "exchange_c", "join")
    EXCHANGED = {"exchange_a": ("w_ffn_in",), "exchange_b": ("w_in",), "exchange_c": ("w_ffn_out", "w_out")}

    def __init__(self, l, split, core):
        self.l, self.split, self.core = l, split, core
        self.cargos = {}

    def stage(self, name):
        l = self.l
        if name == "swap":
            cargo = swap_cargo([self.split[n] for n in BIG], slotted=True)
        elif name in self.EXCHANGED:
            if name == "exchange_a":
                self.chip_sum = {n: add_half_bf16(self.split[n], r, self.core, name=f"reduce_add2_{n}_{l}")
                                 for n, r in zip(BIG, self.cargos["swap"].results)}
            cargo = exchange_cargo([self.chip_sum[n] for n in self.EXCHANGED[name]])
        else:
            from_chips = {n: r for st, ns in self.EXCHANGED.items() for n, r in zip(ns, self.cargos[st].results)}
            self.mine = {n: sum_slots(from_chips[n][:, None], name=f"reduce_add4_{n}_{l}")[0] for n in BIG}
            cargo = swap_cargo([self.mine[n] for n in BIG], slotted=False)
        self.cargos[name] = cargo
        return cargo

    def finish(self):
        return self.mine, dict(zip(BIG, self.cargos["join"].results))


def kernel(x, c, w_ada, b_ada, norm_mix_g, norm_ffn_g, w_in, conv_a_w, conf_dw_w, conf_dw_b, conf_ln_g, conf_ln_b, dn_conv_w, dn_a_log, dn_dt_bias, dn_norm_g, w_out, w_ffn_in, w_ffn_out, final_norm_g, loss_target, m_w_ada, m_b_ada, m_norm_mix_g, m_norm_ffn_g, m_w_in, m_conv_a_w, m_conf_dw_w, m_conf_dw_b, m_conf_ln_g, m_conf_ln_b, m_dn_conv_w, m_dn_a_log, m_dn_dt_bias, m_dn_norm_g, m_w_out, m_w_ffn_in, m_w_ffn_out, m_final_norm_g, v_w_ada, v_b_ada, v_norm_mix_g, v_norm_ffn_g, v_w_in, v_conv_a_w, v_conf_dw_w, v_conf_dw_b, v_conf_ln_g, v_conf_ln_b, v_dn_conv_w, v_dn_a_log, v_dn_dt_bias, v_dn_norm_g, v_w_out, v_w_ffn_in, v_w_ffn_out, v_final_norm_g):
    W = dict(w_ada=w_ada, b_ada=b_ada, norm_mix_g=norm_mix_g, norm_ffn_g=norm_ffn_g, w_in=w_in, conv_a_w=conv_a_w,
             conf_dw_w=conf_dw_w, conf_dw_b=conf_dw_b, conf_ln_g=conf_ln_g, conf_ln_b=conf_ln_b, dn_conv_w=dn_conv_w,
             dn_a_log=dn_a_log, dn_dt_bias=dn_dt_bias, dn_norm_g=dn_norm_g, w_out=w_out, w_ffn_in=w_ffn_in,
             w_ffn_out=w_ffn_out, final_norm_g=final_norm_g)
    M = dict(w_ada=m_w_ada, b_ada=m_b_ada, norm_mix_g=m_norm_mix_g, norm_ffn_g=m_norm_ffn_g, w_in=m_w_in, conv_a_w=m_conv_a_w,
             conf_dw_w=m_conf_dw_w, conf_dw_b=m_conf_dw_b, conf_ln_g=m_conf_ln_g, conf_ln_b=m_conf_ln_b, dn_conv_w=m_dn_conv_w,
             dn_a_log=m_dn_a_log, dn_dt_bias=m_dn_dt_bias, dn_norm_g=m_dn_norm_g, w_out=m_w_out, w_ffn_in=m_w_ffn_in,
             w_ffn_out=m_w_ffn_out, final_norm_g=m_final_norm_g)
    V = dict(w_ada=v_w_ada, b_ada=v_b_ada, norm_mix_g=v_norm_mix_g, norm_ffn_g=v_norm_ffn_g, w_in=v_w_in, conv_a_w=v_conv_a_w,
             conf_dw_w=v_conf_dw_w, conf_dw_b=v_conf_dw_b, conf_ln_g=v_conf_ln_g, conf_ln_b=v_conf_ln_b, dn_conv_w=v_dn_conv_w,
             dn_a_log=v_dn_a_log, dn_dt_bias=v_dn_dt_bias, dn_norm_g=v_dn_norm_g, w_out=v_w_out, w_ffn_in=v_w_ffn_in,
             w_ffn_out=v_w_ffn_out, final_norm_g=v_final_norm_g)
    L = w_ada.shape[0]
    ax, ay, ac = _place()
    chip = 2 * ax + ay
    dev = 4 * ax + 2 * ay + ac

    c_all = allgather8([jnp.pad(c, ((0, 7), (0, 0)))], space=pltpu.VMEM, name="gather_c")[0][:, 0, :]
    b_my = lax.dynamic_slice_in_dim(b_ada, chip * ADA_SH, ADA_SH, axis=1)[:, None, :]
    mod_sh = mod_fwd(c_all, w_ada, b_my, name="mod_fwd")
    lh = L // 2
    mod_g = allgather8([lax.dynamic_slice_in_dim(mod_sh, ac * lh, lh, axis=0).reshape(lh * 8, ADA_SH)], space=pltpu.VMEM,
                       name="gather_mod")[0]
    mod_all = mod_g.reshape(4, 2, lh, 8, ADA_SH).transpose(1, 2, 3, 0, 4).reshape(L, 8, 6 * D)
    mod = lax.dynamic_index_in_dim(mod_all, dev, 1, keepdims=False)

    src = {n: _half_rows(W[n], ac).astype(BF16) for n in BIG}

    def assembled(gathered):
        full = {n: _assemble(n, g[:, None])[0] for n, g in zip(BIG, gathered)}
        full["w_in"] = jnp.pad(full["w_in"], ((0, 0), (0, NP - IN_COLS)))
        return [full[n] for n in BIG]

    weights = [assembled(_comm_call(gather_cargo([src[n][0] for n in BIG]), "gather_w0"))]

    p_full = dict(W)
    sm = allgather8([_pack([W[n] for n in SMALL_SHARDED])], space=pltpu.VMEM, name="gather_convw")[0]
    per_chip = [_unpack(sm[4 * (s // 2) + 2 * (s % 2)], [W[n].shape for n in SMALL_SHARDED]) for s in range(4)]
    for i, n in enumerate(SMALL_SHARDED):
        p_full[n] = jnp.concatenate([per_chip[s][i] for s in range(4)], axis=-1)

    xs = x[0]
    saves, lps = [], []
    for l in range(L):
        lp = layer_params(l, mod, p_full)
        cargos = {n: gather_cargo([src[n][l + 1]]) for n in BIG} if l + 1 < L else None
        xs, s = layer_fwd(l, xs, lp, *weights[l], cargos=cargos)
        if cargos is not None:
            weights.append(assembled([cargos[n].results[0] for n in BIG]))
        saves.append(s)
        lps.append(lp)
    loss_p, dx, dgf = loss_head(xs, loss_target[0], final_norm_g[None, :], name="loss_head")
    core = ac.astype(jnp.int32).reshape(1)
    grads = [None] * L
    reduced = [None] * L
    pending = None
    for l in reversed(range(L)):
        dx, grads[l] = layer_bwd(l, dx, saves[l], lps[l], *weights[l], carry=None if pending is None else pending.stage)
        if pending is not None:
            reduced[pending.l] = pending.finish()
        pending = _Reduction(l, {n: grads[l][n] for n in BIG}, core)
    for stage in _Reduction.STAGES:
        cargo = pending.stage(stage)
        cargo.results = _comm_call(cargo, f"reduce_{stage}_{pending.l}")
    reduced[pending.l] = pending.finish()
    loss = lax.psum(loss_p[0, 0], ("x", "y", "c"))
    grad_x = dx[None]

    small_shapes = [(L,) + p_full[n].shape[1:] for n in SMALL]
    parts = [jnp.stack([grads[l][n].reshape(sh[1:]) for l in range(L)]) for n, sh in zip(SMALL, small_shapes)]
    parts += [dgf.reshape(D), jnp.concatenate([grads[l]["mod"] for l in range(L)], axis=0)]
    small_shapes += [(D,), (L, 6 * D)]
    packed = _pack(parts)
    gathered_small = allgather8([packed], space=pltpu.VMEM, name="gather_small")[0]
    summed = sum_slots(gathered_small[:, None], name="sum_small")[0]
    g_small = dict(zip(SMALL + ("final_norm_g", "b_ada"), _unpack(summed, small_shapes)))
    for n in SMALL_SHARDED:
        sw = W[n].shape[-1]
        g_small[n] = lax.dynamic_slice_in_dim(g_small[n], chip * sw, sw, axis=g_small[n].ndim - 1)
    dmod_all = jnp.stack([_unpack(gathered_small[d], small_shapes)[-1] for d in range(8)], axis=1)
    dmod_my = lax.dynamic_slice_in_dim(dmod_all, chip * ADA_SH, ADA_SH, axis=2)
    g_w_ada = wada_grad(c_all, dmod_my, name="wada_grad")

    halves = [jnp.stack([reduced[l][0][n] for l in range(L)]) for n in BIG]
    theirs = [jnp.stack([reduced[l][1][n] for l in range(L)]) for n in BIG]

    out_g, out_d, out_m, out_v = {}, {}, {}, {}
    out_g["w_ada"] = g_w_ada
    out_d["w_ada"], out_m["w_ada"], out_v["w_ada"] = adamw(W["w_ada"], g_w_ada, M["w_ada"], V["w_ada"], name="adamw_w_ada")
    for n, g_mine, g_theirs in zip(BIG, halves, theirs):
        out_g[n], out_d[n], out_m[n], out_v[n] = adamw_halves(W[n], g_mine, g_theirs, core, M[n], V[n], name=f"adamw_{n}")
    names_small = SMALL + ("final_norm_g", "b_ada")
    pk = lambda d: _pack([d[n] for n in names_small])[None]
    d_s, m_s, v_s = ew_call(_adamw_math, [pk(W), pk(g_small), pk(M), pk(V)], 3, name="adamw_small")
    shapes_s = [W[n].shape for n in names_small]
    for d, o in ((out_d, d_s), (out_m, m_s), (out_v, v_s)):
        d.update(zip(names_small, _unpack(o[0], shapes_s)))
    for n in names_small:
        out_g[n] = g_small[n].reshape(W[n].shape)

    order = ("w_ada", "b_ada", "norm_mix_g", "norm_ffn_g", "w_in", "conv_a_w", "conf_dw_w", "conf_dw_b", "conf_ln_g", "conf_ln_b",
             "dn_conv_w", "dn_a_log", "dn_dt_bias", "dn_norm_g", "w_out", "w_ffn_in", "w_ffn_out", "final_norm_g")
    return (loss, grad_x, *[out_g[n] for n in order], *[out_d[n] for n in order], *[out_m[n] for n in order],
            *[out_v[n] for n in order])
```

```python
import functools

import jax
import jax.numpy as jnp
from jax import lax
from jax.experimental import pallas as pl
from jax.experimental.pallas import tpu as pltpu

F32 = jnp.float32
BF16 = jnp.bfloat16
HI = lax.Precision.HIGHEST
MESH = pl.DeviceIdType.MESH

D = 1024
DEPTH = 4
DC = 256
DDN = 512
NH = 4
HD = 128
CH = 64
DFF = 2816
IN_COLS = 3336
NP = 3456
KA, KB, KC = 3, 31, 4
HALO = 32
EPS = 1e-6
O_AB, O_AC, O_AV, O_BA, O_BG, O_Q, O_K, O_V, O_Z, O_GB = 0, 256, 512, 768, 1024, 1280, 1792, 2304, 2816, 3328

ADAM_LR, ADAM_B1, ADAM_B2, ADAM_EPS, ADAM_WD, ADAM_STEP = 0.001, 0.9, 0.999, 1e-08, 0.01, 10

VMEM_LIMIT = 56 * 1024 * 1024


def _cp(n_grid):
    return pltpu.CompilerParams(dimension_semantics=("arbitrary",) * n_grid, vmem_limit_bytes=VMEM_LIMIT)


def _sds(shape, dtype=F32):
    return jax.ShapeDtypeStruct(tuple(shape), dtype)


def _dot(a, b, prec=None):
    return jnp.dot(a, b, preferred_element_type=F32, precision=prec)


def _dot_nt(a, b, prec=None):
    return lax.dot_general(a, b, (((1,), (1,)), ((), ())), preferred_element_type=F32, precision=prec)


def _dot_tn(a, b, prec=None):
    return lax.dot_general(a, b, (((0,), (0,)), ((), ())), preferred_element_type=F32, precision=prec)


def _split_bf16(a):
    hi = a.astype(BF16)
    return hi, (a - hi.astype(F32)).astype(BF16)


_DIMS = {"nn": (((1,), (0,)), ((), ())), "nt": (((1,), (1,)), ((), ())), "tn": (((0,), (0,)), ((), ()))}
_DIMS_BATCHED = {"nn": (((2,), (1,)), ((0,), (0,))), "nt": (((2,), (2,)), ((0,), (0,))), "tn": (((1,), (1,)), ((0,), (0,)))}


def _mm3_raw(a, b, kind):
    ah, al = _split_bf16(a)
    bh, bl = _split_bf16(b)
    d = (_DIMS if a.ndim == 2 else _DIMS_BATCHED)[kind]
    dg = lambda u, v: lax.dot_general(u, v, d, preferred_element_type=F32)
    return dg(ah, bh) + (dg(ah, bl) + dg(al, bh))


@jax.custom_vjp
def _mm_nn(a, b):
    return _mm3_raw(a, b, "nn")


@jax.custom_vjp
def _mm_nt(a, b):
    return _mm3_raw(a, b, "nt")


@jax.custom_vjp
def _mm_tn(a, b):
    return _mm3_raw(a, b, "tn")


_mm_nn.defvjp(lambda a, b: (_mm_nn(a, b), (a, b)), lambda r, g: (_mm_nt(g, r[1]), _mm_tn(r[0], g)))
_mm_nt.defvjp(lambda a, b: (_mm_nt(a, b), (a, b)), lambda r, g: (_mm_nn(g, r[1]), _mm_tn(g, r[0])))
_mm_tn.defvjp(lambda a, b: (_mm_tn(a, b), (a, b)), lambda r, g: (_mm_nt(r[1], g), _mm_nn(r[0], g)))


def _mm1_raw(a, b, kind):
    d = (_DIMS if a.ndim == 2 else _DIMS_BATCHED)[kind]
    return lax.dot_general(a.astype(BF16), b.astype(BF16), d, preferred_element_type=F32)


@jax.custom_vjp
def _mm1_nn(a, b):
    return _mm1_raw(a, b, "nn")


@jax.custom_vjp
def _mm1_nt(a, b):
    return _mm1_raw(a, b, "nt")


@jax.custom_vjp
def _mm1_tn(a, b):
    return _mm1_raw(a, b, "tn")


_mm1_nn.defvjp(lambda a, b: (_mm1_nn(a, b), (a, b)), lambda r, g: (_mm1_nt(g, r[1]), _mm1_tn(r[0], g)))
_mm1_nt.defvjp(lambda a, b: (_mm1_nt(a, b), (a, b)), lambda r, g: (_mm1_nn(g, r[1]), _mm1_tn(g, r[0])))
_mm1_tn.defvjp(lambda a, b: (_mm1_tn(a, b), (a, b)), lambda r, g: (_mm1_nt(r[1], g), _mm1_nn(r[0], g)))


@jax.custom_vjp
def _inv_given(x, p):
    return p


_inv_given.defvjp(lambda x, p: (p, p), lambda p, g: (_mm_tn(p, _mm_nt(g, p)), jnp.zeros_like(p)))


def _silu(x):
    return x * jax.nn.sigmoid(x)


def _colsum(x):
    return jnp.sum(x, axis=0, keepdims=True)


TILE_CAP = 512


def _w3(w):
    return w if w.ndim == 3 else w[None]


def _wspec(l, block, index):
    return pl.BlockSpec((None,) + block, lambda *g: (l,) + index(*g))


def _tile(T, want):
    t = min(T, want, TILE_CAP)
    assert T % t == 0
    return t


def _normmod(x, gn, sc, sh):
    r = lax.rsqrt(jnp.mean(x * x, axis=-1, keepdims=True) + EPS)
    return ((x * r) * gn) * (1.0 + sc) + sh


def _rms(x, g):
    return (x * lax.rsqrt(jnp.mean(x * x, axis=-1, keepdims=True) + EPS)) * g


def _mix_b_post(u, ln_g, ln_b):
    mu = jnp.mean(u, axis=-1, keepdims=True)
    var = jnp.mean(jnp.square(u - mu), axis=-1, keepdims=True)
    return _silu(((u - mu) * lax.rsqrt(var + 1e-5)) * ln_g + ln_b)


def _softplus(z):
    return jnp.where(z > 0, z, 0.0) + jnp.log(1.0 + jnp.exp(-jnp.where(z > 0, z, -z)))


def _chunk_tril(tt):
    r = lax.broadcasted_iota(jnp.int32, (tt, tt), 0)
    c = lax.broadcasted_iota(jnp.int32, (tt, tt), 1)
    return ((r // CH == c // CH) & (c <= r)).astype(F32)


def _eye8():
    return (lax.broadcasted_iota(jnp.int32, (8, HD), 0) == lax.broadcasted_iota(jnp.int32, (8, HD), 1)).astype(F32)


def _dn_post(pre_q, pre_k, pre_v, blk, alog_row, dt_row):
    q = [s * lax.rsqrt(jnp.sum(s * s, -1, keepdims=True) + EPS) * (HD ** -0.5) for s in map(_silu, pre_q)]
    k = [s * lax.rsqrt(jnp.sum(s * s, -1, keepdims=True) + EPS) for s in map(_silu, pre_k)]
    v = [_silu(p) for p in pre_v]
    lane = lax.broadcasted_iota(jnp.int32, (1, HD), 1)
    g = -jnp.exp(alog_row) * _softplus(blk + dt_row)
    beta = jax.nn.sigmoid(blk)
    gc = _dot(_chunk_tril(blk.shape[0]), jnp.where(lane < NH, g, 0.0), HI)
    gb = jnp.where(lane < NH, gc, jnp.where(lane < 2 * NH, beta, 0.0))
    grow = _dot_nt(_eye8(), gc, HI)
    return q, k, v, gb, grow


def _dn_intra(q, k, v, beta, gcol, grow, p_known=None):
    r = lax.broadcasted_iota(jnp.int32, (CH, CH), 0)
    c = lax.broadcasted_iota(jnp.int32, (CH, CH), 1)
    causal, strict = c <= r, c < r
    decay = jnp.where(causal, jnp.exp(jnp.where(causal, gcol - grow, 0.0)), 0.0)
    kb = k * beta
    x = -jnp.where(strict, _mm1_nt(kb, k) * decay, 0.0)
    if p_known is None:
        p = (r == c).astype(F32) + x
        y = x
        for _ in range(5):
            y = _mm_nn(y, y)
            p = p + _mm_nn(p, y)
    else:
        p = _inv_given(x, p_known)
    u = _mm1_nn(p, v * beta)
    w = _mm1_nn(p, kb * jnp.exp(gcol))
    qk = jnp.where(causal, _mm1_nt(q, k) * decay, 0.0)
    return u, w, qk, p


def _dn_inter(s, q, k, u, w, gcol, qk):
    last = (lax.broadcasted_iota(jnp.int32, (CH, 1), 0) == CH - 1).astype(F32)
    g_last = jnp.sum(gcol * last, axis=1, keepdims=True)
    v_new = u - _mm1_nn(w, s)
    o = _mm1_nn(q * jnp.exp(gcol), s) + _mm1_nn(qk, v_new)
    s_new = s * jnp.exp(g_last) + _mm1_tn(k * jnp.exp(g_last - gcol), v_new)
    return s_new, o


def _yc(o, z, gdn):
    return _rms(o, gdn) * _silu(z)


class Cargo:
    def __init__(self, ins, outs, sems, first, last, middle=None):
        self.ins, self.outs, self.sems = list(ins), list(outs), list(sems)
        self.first, self.middle, self.last = first, middle, last
        self.results = None


def _pcall(body, *, grid, in_specs, out_specs, out_shape, args, name, scratch_shapes=(), cargo=None):
    if cargo is None:
        return pl.pallas_call(body, grid=grid, in_specs=list(in_specs), out_specs=list(out_specs), out_shape=list(out_shape),
                              scratch_shapes=list(scratch_shapes), name=name, compiler_params=_cp(len(grid)))(*args)
    n_in, n_out, n_scr = len(args), len(out_shape), len(scratch_shapes)
    k_in, k_out = len(cargo.ins), len(cargo.outs)
    total = 1
    for g in grid:
        total *= g

    def carrying(*refs):
        ins, cins = refs[:n_in], refs[n_in:n_in + k_in]
        o0 = n_in + k_in
        outs, couts = refs[o0:o0 + n_out], refs[o0 + n_out:o0 + n_out + k_out]
        rest = refs[o0 + n_out + k_out:]
        scr, csems = rest[:n_scr], rest[n_scr:]
        step = pl.program_id(0)
        for a in range(1, len(grid)):
            step = step * grid[a] + pl.program_id(a)

        @pl.when(step == 0)
        def _():
            cargo.first(cins, couts, csems)
        if cargo.middle is not None:
            @pl.when(step == total // 2)
            def _():
                cargo.middle(cins, couts, csems)
        body(*ins, *outs, *scr)

        @pl.when(step == total - 1)
        def _():
            cargo.last(cins, couts, csems)

    hbm = pl.BlockSpec(memory_space=pl.ANY)
    res = pl.pallas_call(
        carrying, grid=grid, in_specs=list(in_specs) + [hbm] * k_in, out_specs=list(out_specs) + [hbm] * k_out,
        out_shape=list(out_shape) + cargo.outs, scratch_shapes=list(scratch_shapes) + cargo.sems, name=name,
        compiler_params=_cp(len(grid)))(*args, *cargo.ins)
    cargo.results = list(res[n_out:])
    return list(res[:n_out])


def _place():
    return lax.axis_index("x"), lax.axis_index("y"), lax.axis_index("c")


def _other_chips(x, y):
    return [(1 - x, y), (x, 1 - y), (1 - x, 1 - y)]


def _gather_phases(n):
    def env(ins, outs, sems):
        send_sems, recv_sems, local_sems = sems
        x, y, c = _place()
        me, sibling = (x, y, c), (x, y, 1 - c)
        chips = _other_chips(x, y)

        def slot(p):
            return 4 * p[0] + 2 * p[1] + p[2]

        def copy(a, k, block, to, own=False):
            dst = outs[a].at[slot(block)]
            return pltpu.make_async_remote_copy(src_ref=ins[a] if own else dst, dst_ref=dst, send_sem=send_sems.at[a, k],
                                                recv_sem=recv_sems.at[a, k], device_id=to, device_id_type=MESH)

        mine = [pltpu.make_async_copy(ins[a], outs[a].at[slot(me)], local_sems.at[a]) for a in range(n)]
        own = [copy(a, 0, me, sibling, own=True) for a in range(n)]
        own += [copy(a, 1 + j, me, (*chip, c), own=True) for a in range(n) for j, chip in enumerate(chips)]
        return c, me, sibling, chips, copy, mine, own

    def first(ins, outs, sems):
        _, _, _, _, _, mine, own = env(ins, outs, sems)
        for cp in mine + own:
            cp.start()

    def middle(ins, outs, sems):
        c, me, sibling, chips, copy, _, _ = env(ins, outs, sems)
        for j, chip in enumerate(chips):
            for a in range(n):
                copy(a, 1 + j, (*chip, c), me).wait_recv()
                copy(a, 4 + j, (*chip, c), sibling).start()

    def last(ins, outs, sems):
        c, me, sibling, chips, copy, mine, own = env(ins, outs, sems)
        for a in range(n):
            copy(a, 0, sibling, me).wait_recv()
            for j, chip in enumerate(chips):
                copy(a, 4 + j, (*chip, 1 - c), me).wait_recv()
        for cp in own:
            cp.wait_send()
        for a in range(n):
            for j, chip in enumerate(chips):
                copy(a, 4 + j, (*chip, c), sibling).wait_send()
        for cp in mine:
            cp.wait()

    return first, middle, last


def _gather_sems(n):
    return [pltpu.SemaphoreType.DMA((n, 7)), pltpu.SemaphoreType.DMA((n, 7)), pltpu.SemaphoreType.DMA((n,))]


def gather_cargo(blocks):
    first, middle, last = _gather_phases(len(blocks))
    return Cargo(blocks, [_sds((8,) + b.shape, b.dtype) for b in blocks], _gather_sems(len(blocks)), first, last, middle)


def _swap_phases(n, slotted):
    def copies(ins, outs, sems):
        send_sems, recv_sems = sems
        x, y, c = _place()
        return [pltpu.make_async_remote_copy(src_ref=ins[a].at[1 - c] if slotted else ins[a], dst_ref=outs[a],
                                             send_sem=send_sems.at[a], recv_sem=recv_sems.at[a], device_id=(x, y, 1 - c),
                                             device_id_type=MESH) for a in range(n)]

    def first(ins, outs, sems):
        for cp in copies(ins, outs, sems):
            cp.start()

    def last(ins, outs, sems):
        for cp in copies(ins, outs, sems):
            cp.wait()

    return first, last


def swap_cargo(blocks, slotted):
    n = len(blocks)
    first, last = _swap_phases(n, slotted)
    return Cargo(blocks, [_sds(b.shape[1:] if slotted else b.shape, b.dtype) for b in blocks],
                 [pltpu.SemaphoreType.DMA((n,)), pltpu.SemaphoreType.DMA((n,))], first, last)


def _exchange_phases(n):
    def env(ins, outs, sems):
        send_sems, recv_sems, local_sems = sems
        x, y, c = _place()
        me = 2 * x + y
        chips = _other_chips(x, y)
        mine = [pltpu.make_async_copy(ins[a].at[me], outs[a].at[me], local_sems.at[a]) for a in range(n)]
        sends = [pltpu.make_async_remote_copy(src_ref=ins[a].at[2 * chip[0] + chip[1]], dst_ref=outs[a].at[me],
                                              send_sem=send_sems.at[a, j], recv_sem=recv_sems.at[a, j], device_id=(*chip, c),
                                              device_id_type=MESH) for a in range(n) for j, chip in enumerate(chips)]
        recvs = [pltpu.make_async_remote_copy(src_ref=ins[a].at[2 * chip[0] + chip[1]], dst_ref=outs[a].at[2 * chip[0] + chip[1]],
                                              send_sem=send_sems.at[a, j], recv_sem=recv_sems.at[a, j], device_id=(*chip, c),
                                              device_id_type=MESH) for a in range(n) for j, chip in enumerate(chips)]
        return mine, sends, recvs

    def first(ins, outs, sems):
        mine, sends, _ = env(ins, outs, sems)
        for cp in mine + sends:
            cp.start()

    def last(ins, outs, sems):
        mine, sends, recvs = env(ins, outs, sems)
        for cp in recvs:
            cp.wait_recv()
        for cp in sends:
            cp.wait_send()
        for cp in mine:
            cp.wait()

    return first, last


def exchange_cargo(blocks):
    n = len(blocks)
    first, last = _exchange_phases(n)
    return Cargo(blocks, [_sds(b.shape, b.dtype) for b in blocks],
                 [pltpu.SemaphoreType.DMA((n, 3)), pltpu.SemaphoreType.DMA((n, 3)), pltpu.SemaphoreType.DMA((n,))], first, last)


def _comm_call(cargo, name):
    def body(*refs):
        k_in, k_out = len(cargo.ins), len(cargo.outs)
        ins, outs, sems = refs[:k_in], refs[k_in:k_in + k_out], refs[k_in + k_out:]
        cargo.first(ins, outs, sems)
        if cargo.middle is not None:
            cargo.middle(ins, outs, sems)
        cargo.last(ins, outs, sems)

    hbm = pl.BlockSpec(memory_space=pl.ANY)
    return list(pl.pallas_call(body, name=name, in_specs=[hbm] * len(cargo.ins), out_specs=[hbm] * len(cargo.outs),
                               out_shape=cargo.outs, scratch_shapes=cargo.sems)(*cargo.ins))


def nm_fwd(x, gn, sc, sh, w, *, tn, name, wl=0, cargo=None):
    T, N = x.shape[0], w.shape[-1]
    tt = _tile(T, 512)
    nt = T // tt

    def body(x_ref, gn_ref, sc_ref, sh_ref, w_ref, h_ref, o_ref, h_all):
        i = pl.program_id(1)

        @pl.when(pl.program_id(0) == 0)
        def _():
            h = _normmod(x_ref[...], gn_ref[...], sc_ref[...], sh_ref[...]).astype(BF16)
            h_all[i] = h
            h_ref[...] = h
        o_ref[...] = _dot(h_all[i], w_ref[...])

    def x_index(j, i):
        return (jnp.where(j == 0, i, nt - 1), 0)

    row = pl.BlockSpec((1, D), lambda j, i: (0, 0))
    return _pcall(
        body, grid=(N // tn, nt), name=name, cargo=cargo,
        in_specs=[pl.BlockSpec((tt, D), x_index), row, row, row, _wspec(wl, (D, tn), lambda j, i: (0, j))],
        out_specs=[pl.BlockSpec((tt, D), x_index), pl.BlockSpec((tt, tn), lambda j, i: (i, j))],
        out_shape=[_sds((T, D), BF16), _sds((T, N))], scratch_shapes=[pltpu.VMEM((nt, tt, D), BF16)],
        args=(x, gn, sc, sh, _w3(w)))


def nm_bwd(dys, w, x, gn, sc, sh, dres, *, name, wl=0, cargo=None):
    T = x.shape[0]
    N = w.shape[-1]
    tt = _tile(T, 256)
    widths = [a.shape[1] for a in dys]
    assert sum(widths) == N
    n = len(dys)

    def body(*refs):
        dy_refs, (w_ref, x_ref, gn_ref, sc_ref, sh_ref, dres_ref) = refs[:n], refs[n:n + 6]
        dx_ref, dgn_ref, dsc_ref, dsh_ref, dyb_ref = refs[n + 6:]
        i = pl.program_id(0)
        dyb = jnp.concatenate([r[...].astype(BF16) for r in dy_refs], axis=1) if n > 1 else dy_refs[0][...].astype(BF16)
        dyb_ref[...] = dyb
        dh = _dot_nt(dyb, w_ref[...])
        _, vjp = jax.vjp(_normmod, x_ref[...], gn_ref[...], sc_ref[...], sh_ref[...])
        dx, dgn, dsc, dsh = vjp(dh)
        dx_ref[...] = dres_ref[...] + dx

        @pl.when(i == 0)
        def _():
            dgn_ref[...] = jnp.zeros_like(dgn_ref)
            dsc_ref[...] = jnp.zeros_like(dsc_ref)
            dsh_ref[...] = jnp.zeros_like(dsh_ref)
        dgn_ref[...] += dgn
        dsc_ref[...] += dsc
        dsh_ref[...] += dsh

    row = pl.BlockSpec((1, D), lambda i: (0, 0))
    tile = pl.BlockSpec((tt, D), lambda i: (i, 0))
    return _pcall(
        body, grid=(T // tt,), name=name, cargo=cargo,
        in_specs=[pl.BlockSpec((tt, wd), lambda i: (i, 0)) for wd in widths]
        + [_wspec(wl, (D, N), lambda i: (0, 0)), tile, row, row, row, tile],
        out_specs=[tile, row, row, row, pl.BlockSpec((tt, N), lambda i: (i, 0))],
        out_shape=[_sds((T, D)), _sds((1, D)), _sds((1, D)), _sds((1, D)), _sds((T, N), BF16)],
        args=(*dys, _w3(w), x, gn, sc, sh, dres))


def mm_tn(a, b, *, tn, name):
    T, K = a.shape
    N = b.shape[1]
    tt = _tile(T, 512)

    def body(a_ref, b_ref, o_ref):
        @pl.when(pl.program_id(1) == 0)
        def _():
            o_ref[...] = jnp.zeros_like(o_ref)
        o_ref[...] += _dot_tn(a_ref[...], b_ref[...])

    return pl.pallas_call(
        body, grid=(N // tn, T // tt), name=name,
        in_specs=[pl.BlockSpec((tt, K), lambda j, t: (t, 0)), pl.BlockSpec((tt, tn), lambda j, t: (t, j))],
        out_specs=pl.BlockSpec((K, tn), lambda j, t: (0, j)), out_shape=_sds((K, N)), compiler_params=_cp(2))(a, b)


def mm_tn_split(a, b, *, by_cols, name):
    T, K = a.shape
    N = b.shape[1]
    tt = _tile(T, 512)
    tn = N // 4 if by_cols else N // 2
    rh, C = (K // 2, N // 4) if by_cols else (K // 8, N)
    nt = T // tt

    def body(a_ref, b_ref, o_ref, acc):
        t = pl.program_id(1)

        @pl.when(t == 0)
        def _():
            acc[...] = jnp.zeros_like(acc)
        acc[...] += _dot_tn(a_ref[...], b_ref[...])

        @pl.when(t == nt - 1)
        def _():
            if by_cols:
                for h in range(2):
                    o_ref[h] = acc[pl.ds(h * rh, rh), :]
            else:
                for s in range(4):
                    for h in range(2):
                        o_ref[h, s] = acc[pl.ds((2 * s + h) * rh, rh), :]

    if by_cols:
        out_spec = pl.BlockSpec((2, None, rh, tn), lambda j, t: (0, j, 0, 0))
    else:
        out_spec = pl.BlockSpec((2, 4, rh, tn), lambda j, t: (0, 0, 0, j))
    return pl.pallas_call(
        body, grid=(N // tn, nt), name=name,
        in_specs=[pl.BlockSpec((tt, K), lambda j, t: (t, 0)), pl.BlockSpec((tt, tn), lambda j, t: (t, j))],
        out_specs=out_spec, out_shape=_sds((2, 4, rh, C)), scratch_shapes=[pltpu.VMEM((K, tn), F32)],
        compiler_params=_cp(2))(a, b)


def _fill_pad(pad_ref, prev, cur, first):
    pad_ref[pl.ds(0, HALO), :] = jnp.where(first, 0.0, prev)
    pad_ref[pl.ds(HALO, cur.shape[0]), :] = cur


def _causal_conv(pad_ref, w, K, tt):
    acc = None
    for k in range(K):
        term = pad_ref[pl.ds(HALO - (K - 1) + k, tt), :] * w[k:k + 1, :]
        acc = term if acc is None else acc + term
    return acc


def _conv_inputs(pc, pp, first, pad_a, pad_b, pad_c):
    def s_of(p):
        return p[:, O_AC:O_AV] * p[:, O_AV:O_BA]

    def u0_of(p):
        return p[:, O_BA:O_BG] * jax.nn.sigmoid(p[:, O_BG:O_Q])

    _fill_pad(pad_a, s_of(pp), s_of(pc), first)
    _fill_pad(pad_b, u0_of(pp), u0_of(pc), first)
    _fill_pad(pad_c, pp[:, O_Q:O_Z], pc[:, O_Q:O_Z], first)


def _mix_specs(T, tt):
    cur = pl.BlockSpec((tt, O_Z), lambda i: (i, 0))
    prev = pl.BlockSpec((HALO, O_Z), lambda i: (jnp.maximum(i * (tt // HALO) - 1, 0), 0))
    gbb = pl.BlockSpec((tt, HD), lambda i: (i, O_GB // HD))
    return cur, prev, gbb


def _full(shape):
    return pl.BlockSpec(shape, lambda i: (0,) * len(shape))


def mix_fwd(proj, wa, wb, bb, ln_g, ln_b, wc, alog_row, dt_row, *, name):
    T = proj.shape[0]
    tt = _tile(T, 256)

    def body(pc_ref, pp_ref, blk_ref, wa_ref, wb_ref, bb_ref, lg_ref, lb_ref, wc_ref, al_ref, dt_ref,
             yab_ref, q_ref, k_ref, v_ref, gb_ref, grow_ref, pad_a, pad_b, pad_c):
        first = pl.program_id(0) == 0
        pc = pc_ref[...]
        _conv_inputs(pc, pp_ref[...], first, pad_a, pad_b, pad_c)
        ya = pc[:, O_AB:O_AC] * _causal_conv(pad_a, wa_ref[...], KA, tt)
        yb = _mix_b_post(_causal_conv(pad_b, wb_ref[...], KB, tt) + bb_ref[...], lg_ref[...], lb_ref[...])
        yab_ref[...] = jnp.concatenate([ya, yb], axis=1)
        pre = _causal_conv(pad_c, wc_ref[...], KC, tt)
        blocks = [pre[:, j * HD:(j + 1) * HD] for j in range(3 * NH)]
        q, k, v, gb, grow = _dn_post(blocks[:NH], blocks[NH:2 * NH], blocks[2 * NH:], blk_ref[...], al_ref[...], dt_ref[...])
        q_ref[...] = jnp.concatenate(q, axis=1)
        k_ref[...] = jnp.concatenate(k, axis=1)
        v_ref[...] = jnp.concatenate(v, axis=1)
        gb_ref[...] = gb
        grow_ref[...] = grow

    cur, prev, gbb = _mix_specs(T, tt)
    t512 = pl.BlockSpec((tt, DDN), lambda i: (i, 0))
    return pl.pallas_call(
        body, grid=(T // tt,), name=name,
        in_specs=[cur, prev, gbb, _full((8, DC)), _full((32, DC)), _full((1, DC)), _full((1, DC)), _full((1, DC)),
                  _full((8, 3 * DDN)), _full((1, HD)), _full((1, HD))],
        out_specs=[t512, t512, t512, t512, pl.BlockSpec((tt, HD), lambda i: (i, 0)), pl.BlockSpec((8, tt), lambda i: (0, i))],
        out_shape=[_sds((T, 2 * DC)), _sds((T, DDN)), _sds((T, DDN)), _sds((T, DDN)), _sds((T, HD)), _sds((8, T))],
        scratch_shapes=[pltpu.VMEM((HALO + tt, DC), F32), pltpu.VMEM((HALO + tt, DC), F32), pltpu.VMEM((HALO + tt, 3 * DDN), F32)],
        compiler_params=_cp(1))(proj, proj, proj, wa, wb, bb, ln_g, ln_b, wc, alog_row, dt_row)


def mix_bwd_point(proj, dyab, dq, dk, dv, dgb, dgrow, wa, wb, bb, ln_g, ln_b, wc, alog_row, dt_row, *, name):
    T = proj.shape[0]
    tt = _tile(T, 256)
    CW = 2 * DC + 3 * DDN

    def body(pc_ref, pp_ref, blk_ref, dyab_ref, dq_ref, dk_ref, dv_ref, dgb_ref, dgrow_ref,
             wa_ref, wb_ref, bb_ref, lg_ref, lb_ref, wc_ref, al_ref, dt_ref,
             dab_ref, dconv_ref, dblk_ref, dbb_ref, dlg_ref, dlb_ref, dal_ref, ddt_ref, pad_a, pad_b, pad_c):
        i = pl.program_id(0)
        pc = pc_ref[...]
        _conv_inputs(pc, pp_ref[...], i == 0, pad_a, pad_b, pad_c)
        ca = _causal_conv(pad_a, wa_ref[...], KA, tt)
        u = _causal_conv(pad_b, wb_ref[...], KB, tt) + bb_ref[...]
        pre = _causal_conv(pad_c, wc_ref[...], KC, tt)
        dyab_v = dyab_ref[...]
        dya, dyb = dyab_v[:, :DC], dyab_v[:, DC:]
        dab_ref[...] = dya * ca
        dca = dya * pc[:, O_AB:O_AC]
        _, vjp_b = jax.vjp(_mix_b_post, u, lg_ref[...], lb_ref[...])
        du, dlg, dlb = vjp_b(dyb)
        blocks = [pre[:, j * HD:(j + 1) * HD] for j in range(3 * NH)]
        _, vjp_c = jax.vjp(_dn_post, blocks[:NH], blocks[NH:2 * NH], blocks[2 * NH:], blk_ref[...], al_ref[...], dt_ref[...])

        def heads(r):
            vv = r[...]
            return [vv[:, h * HD:(h + 1) * HD] for h in range(NH)]
        dpq, dpk, dpv, dblk, dal, ddt = vjp_c((heads(dq_ref), heads(dk_ref), heads(dv_ref), dgb_ref[...], dgrow_ref[...]))
        dconv_ref[...] = jnp.concatenate([dca, du] + dpq + dpk + dpv, axis=1)
        dblk_ref[...] = dblk

        @pl.when(i == 0)
        def _():
            for r in (dbb_ref, dlg_ref, dlb_ref, dal_ref, ddt_ref):
                r[...] = jnp.zeros_like(r)
        dbb_ref[...] += _colsum(du)
        dlg_ref[...] += dlg
        dlb_ref[...] += dlb
        dal_ref[...] += dal
        ddt_ref[...] += ddt

    cur, prev, gbb = _mix_specs(T, tt)
    t512 = pl.BlockSpec((tt, DDN), lambda i: (i, 0))
    t128 = pl.BlockSpec((tt, HD), lambda i: (i, 0))
    return pl.pallas_call(
        body, grid=(T // tt,), name=name,
        in_specs=[cur, prev, gbb, t512, t512, t512, t512, t128, pl.BlockSpec((8, tt), lambda i: (0, i)),
                  _full((8, DC)), _full((32, DC)), _full((1, DC)), _full((1, DC)), _full((1, DC)),
                  _full((8, 3 * DDN)), _full((1, HD)), _full((1, HD))],
        out_specs=[pl.BlockSpec((tt, DC), lambda i: (i, 0)), pl.BlockSpec((tt, CW), lambda i: (i, 0)), t128,
                   _full((1, DC)), _full((1, DC)), _full((1, DC)), _full((1, HD)), _full((1, HD))],
        out_shape=[_sds((T, DC)), _sds((T, CW)), _sds((T, HD)), _sds((1, DC)), _sds((1, DC)), _sds((1, DC)),
                   _sds((1, HD)), _sds((1, HD))],
        scratch_shapes=[pltpu.VMEM((HALO + tt, DC), F32), pltpu.VMEM((HALO + tt, DC), F32), pltpu.VMEM((HALO + tt, 3 * DDN), F32)],
        compiler_params=_cp(1))(proj, proj, proj, dyab, dq, dk, dv, dgb, dgrow, wa, wb, bb, ln_g, ln_b, wc, alog_row, dt_row)


def mix_bwd_conv(proj, dconv, wa, wb, wc, *, name, cargo=None):
    T = proj.shape[0]
    tt = _tile(T, 256)
    CW = 2 * DC + 3 * DDN
    nblk = T // HALO

    def body(pc_ref, pp_ref, dc_ref, dn_ref, wa_ref, wb_ref, wc_ref, dp_ref, dwa_ref, dwb_ref, dwc_ref,
             pad_a, pad_b, pad_c, dpad):
        i = pl.program_id(0)
        last = i == pl.num_programs(0) - 1
        pc = pc_ref[...]
        _conv_inputs(pc, pp_ref[...], i == 0, pad_a, pad_b, pad_c)
        dcur = dc_ref[...]
        dpad[pl.ds(0, tt), :] = dcur
        dpad[pl.ds(tt, HALO), :] = jnp.where(last, 0.0, dn_ref[...])

        @pl.when(i == 0)
        def _():
            for r in (dwa_ref, dwb_ref, dwc_ref):
                r[...] = jnp.zeros_like(r)

        def tconv(lo, hi, w, K, pad_ref, dw_ref):
            dy = dcur[:, lo:hi]
            acc = None
            for k in range(K):
                term = dpad[pl.ds(K - 1 - k, tt), lo:hi] * w[k:k + 1, :]
                acc = term if acc is None else acc + term
                dw_ref[pl.ds(k, 1), :] += _colsum(dy * pad_ref[pl.ds(HALO - (K - 1) + k, tt), :])
            return acc

        ds = tconv(0, DC, wa_ref[...], KA, pad_a, dwa_ref)
        du0 = tconv(DC, 2 * DC, wb_ref[...], KB, pad_b, dwb_ref)
        dqkv = tconv(2 * DC, CW, wc_ref[...], KC, pad_c, dwc_ref)
        a_c, a_v, b_a, b_g = pc[:, O_AC:O_AV], pc[:, O_AV:O_BA], pc[:, O_BA:O_BG], pc[:, O_BG:O_Q]
        sg = jax.nn.sigmoid(b_g)
        dp_ref[...] = jnp.concatenate([ds * a_v, ds * a_c, du0 * sg, du0 * b_a * sg * (1.0 - sg), dqkv], axis=1)

    cur, prev, _ = _mix_specs(T, tt)
    return _pcall(
        body, grid=(T // tt,), name=name, cargo=cargo, args=(proj, proj, dconv, dconv, wa, wb, wc),
        in_specs=[cur, prev, pl.BlockSpec((tt, CW), lambda i: (i, 0)),
                  pl.BlockSpec((HALO, CW), lambda i: (jnp.minimum((i + 1) * (tt // HALO), nblk - 1), 0)),
                  _full((8, DC)), _full((32, DC)), _full((8, 3 * DDN))],
        out_specs=[pl.BlockSpec((tt, O_Z - O_AC), lambda i: (i, 0)), _full((8, DC)), _full((32, DC)), _full((8, 3 * DDN))],
        out_shape=[_sds((T, O_Z - O_AC)), _sds((8, DC)), _sds((32, DC)), _sds((8, 3 * DDN))],
        scratch_shapes=[pltpu.VMEM((HALO + tt, DC), F32), pltpu.VMEM((HALO + tt, DC), F32), pltpu.VMEM((HALO + tt, 3 * DDN), F32),
                        pltpu.VMEM((tt + HALO, CW), F32)])


def _head(v, h):
    return v[:, h * HD:(h + 1) * HD]


def _to_batch(v, cb):
    return jnp.stack([v[c * CH:(c + 1) * CH, h * HD:(h + 1) * HD] for c in range(cb) for h in range(NH)])


def _from_batch(b, cb):
    return jnp.concatenate([jnp.concatenate([b[c * NH + h] for h in range(NH)], axis=1) for c in range(cb)], axis=0)


def _lanes_to_batch(v, cb, lane0):
    return jnp.stack([v[c * CH:(c + 1) * CH, lane0 + h:lane0 + h + 1] for c in range(cb) for h in range(NH)])


def _lane_onehot(h):
    return (lax.broadcasted_iota(jnp.int32, (1, HD), 1) == h).astype(F32)


def _batch_to_lanes(b, cb, lane0):
    return jnp.concatenate([sum(b[c * NH + h] * _lane_onehot(lane0 + h) for h in range(NH)) for c in range(cb)], axis=0)


def dn_intra_fwd(q, k, v, gb, grow3, *, name):
    T = q.shape[0]
    N = T // CH
    cb = 4 if N % 4 == 0 else 1

    def body(q_ref, k_ref, v_ref, gb_ref, gr_ref, u_ref, w_ref, qk_ref, p_ref):
        gbv = gb_ref[...]
        grow = jnp.stack([gr_ref[c, h:h + 1, :] for c in range(cb) for h in range(NH)])
        u, w, qk, p = _dn_intra(_to_batch(q_ref[...], cb), _to_batch(k_ref[...], cb), _to_batch(v_ref[...], cb),
                                _lanes_to_batch(gbv, cb, NH), _lanes_to_batch(gbv, cb, 0), grow)
        u_ref[...] = _from_batch(u, cb)
        w_ref[...] = _from_batch(w, cb)
        qk_ref[...] = qk.reshape(cb, NH, CH, CH)
        p_ref[...] = p.reshape(cb, NH, CH, CH)

    t512 = pl.BlockSpec((cb * CH, DDN), lambda i: (i, 0))
    sq = pl.BlockSpec((cb, NH, CH, CH), lambda i: (i, 0, 0, 0))
    return pl.pallas_call(
        body, grid=(N // cb,), name=name,
        in_specs=[t512, t512, t512, pl.BlockSpec((cb * CH, HD), lambda i: (i, 0)), pl.BlockSpec((cb, 8, CH), lambda i: (i, 0, 0))],
        out_specs=[t512, t512, sq, sq],
        out_shape=[_sds((T, DDN)), _sds((T, DDN)), _sds((N, NH, CH, CH)), _sds((N, NH, CH, CH))],
        compiler_params=_cp(1))(q, k, v, gb, grow3)


def dn_inter_fwd(q, k, u, w, gb, qk, *, name):
    T = q.shape[0]
    N = T // CH
    cb = 4 if N % 4 == 0 else 1

    def body(q_ref, k_ref, u_ref, w_ref, gb_ref, qk_ref, o_ref, s_ref, state):
        @pl.when(pl.program_id(0) == 0)
        def _():
            state[...] = jnp.zeros_like(state)
        for c in range(cb):
            rows = pl.ds(c * CH, CH)
            s = state[...]
            s_ref[c] = s
            s_new, o = _dn_inter(s, _to_batch(q_ref[rows, :], 1), _to_batch(k_ref[rows, :], 1), _to_batch(u_ref[rows, :], 1),
                                 _to_batch(w_ref[rows, :], 1), _lanes_to_batch(gb_ref[rows, :], 1, 0), qk_ref[c])
            state[...] = s_new
            o_ref[rows, :] = _from_batch(o, 1)

    t512 = pl.BlockSpec((cb * CH, DDN), lambda i: (i, 0))
    return pl.pallas_call(
        body, grid=(N // cb,), name=name,
        in_specs=[t512, t512, t512, t512, pl.BlockSpec((cb * CH, HD), lambda i: (i, 0)),
                  pl.BlockSpec((cb, NH, CH, CH), lambda i: (i, 0, 0, 0))],
        out_specs=[t512, pl.BlockSpec((cb, NH, HD, HD), lambda i: (i, 0, 0, 0))],
        out_shape=[_sds((T, DDN)), _sds((N, NH, HD, HD))],
        scratch_shapes=[pltpu.VMEM((NH, HD, HD), F32)], compiler_params=_cp(1))(q, k, u, w, gb, qk)


def dn_inter_bwd(do, q, k, u, w, gb, qk, s_all, *, name, cargo=None):
    T = q.shape[0]
    N = T // CH
    cb = 4 if N % 4 == 0 else 1
    G = N // cb

    def body(do_ref, q_ref, k_ref, u_ref, w_ref, gb_ref, qk_ref, s_ref, dq_ref, dk_ref, du_ref, dw_ref, dg_ref, dqk_ref, dstate):
        @pl.when(pl.program_id(0) == 0)
        def _():
            dstate[...] = jnp.zeros_like(dstate)
        for c in reversed(range(cb)):
            rows = pl.ds(c * CH, CH)
            _, vjp = jax.vjp(_dn_inter, s_ref[c], _to_batch(q_ref[rows, :], 1), _to_batch(k_ref[rows, :], 1),
                             _to_batch(u_ref[rows, :], 1), _to_batch(w_ref[rows, :], 1), _lanes_to_batch(gb_ref[rows, :], 1, 0),
                             qk_ref[c])
            ds, dq, dk, du, dw, dgc, dqk = vjp((dstate[...], _to_batch(do_ref[rows, :], 1)))
            dstate[...] = ds
            dq_ref[rows, :] = _from_batch(dq, 1)
            dk_ref[rows, :] = _from_batch(dk, 1)
            du_ref[rows, :] = _from_batch(du, 1)
            dw_ref[rows, :] = _from_batch(dw, 1)
            dg_ref[rows, :] = _batch_to_lanes(dgc, 1, 0)
            dqk_ref[c] = dqk

    t512 = pl.BlockSpec((cb * CH, DDN), lambda i: (G - 1 - i, 0))
    t128 = pl.BlockSpec((cb * CH, HD), lambda i: (G - 1 - i, 0))
    qkb = pl.BlockSpec((cb, NH, CH, CH), lambda i: (G - 1 - i, 0, 0, 0))
    return _pcall(
        body, grid=(G,), name=name, cargo=cargo, args=(do, q, k, u, w, gb, qk, s_all),
        in_specs=[t512, t512, t512, t512, t512, t128, qkb, pl.BlockSpec((cb, NH, HD, HD), lambda i: (G - 1 - i, 0, 0, 0))],
        out_specs=[t512, t512, t512, t512, t128, qkb],
        out_shape=[_sds((T, DDN))] * 4 + [_sds((T, HD)), _sds((N, NH, CH, CH))],
        scratch_shapes=[pltpu.VMEM((NH, HD, HD), F32)])


def dn_intra_bwd(du, dw, dqk, dq_in, dk_in, dg_in, q, k, v, gb, grow3, p_all, *, name):
    T = q.shape[0]
    N = T // CH
    cb = 4 if N % 4 == 0 else 1

    def body(du_ref, dw_ref, dqk_ref, dqi_ref, dki_ref, dgi_ref, q_ref, k_ref, v_ref, gb_ref, gr_ref, p_ref,
             dq_ref, dk_ref, dv_ref, dgb_ref, dgr_ref):
        B = cb * NH
        gbv = gb_ref[...]
        grow = jnp.stack([gr_ref[c, h:h + 1, :] for c in range(cb) for h in range(NH)])
        _, vjp = jax.vjp(functools.partial(_dn_intra, p_known=p_ref[...].reshape(B, CH, CH)),
                         _to_batch(q_ref[...], cb), _to_batch(k_ref[...], cb), _to_batch(v_ref[...], cb),
                         _lanes_to_batch(gbv, cb, NH), _lanes_to_batch(gbv, cb, 0), grow)
        dq, dk, dv, dbeta, dgc, dgr = vjp((_to_batch(du_ref[...], cb), _to_batch(dw_ref[...], cb), dqk_ref[...].reshape(B, CH, CH),
                                           jnp.zeros((B, CH, CH), F32)))
        dq_ref[...] = dqi_ref[...] + _from_batch(dq, cb)
        dk_ref[...] = dki_ref[...] + _from_batch(dk, cb)
        dv_ref[...] = _from_batch(dv, cb)
        dgb_ref[...] = dgi_ref[...] + _batch_to_lanes(dgc, cb, 0) + _batch_to_lanes(dbeta, cb, NH)
        for c in range(cb):
            dgr_ref[c] = jnp.concatenate([dgr[c * NH + h] for h in range(NH)] + [jnp.zeros((8 - NH, CH), F32)], axis=0)

    t512 = pl.BlockSpec((cb * CH, DDN), lambda i: (i, 0))
    t128 = pl.BlockSpec((cb * CH, HD), lambda i: (i, 0))
    qkb = pl.BlockSpec((cb, NH, CH, CH), lambda i: (i, 0, 0, 0))
    grb = pl.BlockSpec((cb, 8, CH), lambda i: (i, 0, 0))
    return pl.pallas_call(
        body, grid=(N // cb,), name=name,
        in_specs=[t512, t512, qkb, t512, t512, t128, t512, t512, t512, t128, grb, qkb],
        out_specs=[t512, t512, t512, t128, grb],
        out_shape=[_sds((T, DDN))] * 3 + [_sds((T, HD)), _sds((N, 8, CH))],
        compiler_params=_cp(1))(du, dw, dqk, dq_in, dk_in, dg_in, q, k, v, gb, grow3, p_all)


def _z_specs(tt):
    return [pl.BlockSpec((tt, DC), lambda i: (i, O_Z // DC)), pl.BlockSpec((tt, DC), lambda i: (i, O_Z // DC + 1))]


def mixout_fwd(x, yab, o, proj, gdn, g1, w_out, *, name, wl=0, cargo=None):
    T = x.shape[0]
    tt = _tile(T, 512)

    def body(x_ref, yab_ref, o_ref, z0_ref, z1_ref, gdn_ref, g1_ref, w_ref, ycat_ref, mix_ref, xo_ref):
        ov = o_ref[...]
        z = jnp.concatenate([z0_ref[...], z1_ref[...]], axis=1)
        yc = [_yc(_head(ov, h), _head(z, h), gdn_ref[...]) for h in range(NH)]
        ycat = jnp.concatenate([yab_ref[...]] + yc, axis=1).astype(BF16)
        ycat_ref[...] = ycat
        mix = _dot(ycat, w_ref[...])
        mix_ref[...] = mix
        xo_ref[...] = x_ref[...] + g1_ref[...] * mix

    tile = pl.BlockSpec((tt, D), lambda i: (i, 0))
    t512 = pl.BlockSpec((tt, DDN), lambda i: (i, 0))
    return _pcall(
        body, grid=(T // tt,), name=name, cargo=cargo, args=(x, yab, o, proj, proj, gdn, g1, _w3(w_out)),
        in_specs=[tile, t512, t512] + _z_specs(tt) + [_full((1, HD)), _full((1, D)), _wspec(wl, (D, D), lambda i: (0, 0))],
        out_specs=[tile, tile, tile], out_shape=[_sds((T, D), BF16), _sds((T, D)), _sds((T, D))])


def mixout_bwd(dx, mix, o, proj, gdn, g1, w_out, *, name, wl=0, cargo=None):
    T = dx.shape[0]
    tt = _tile(T, 256)

    def body(dx_ref, mix_ref, o_ref, z0_ref, z1_ref, gdn_ref, g1_ref, w_ref, dmix_ref, dyab_ref, do_ref, dz_ref, dg1_ref, dgdn_ref):
        i = pl.program_id(0)
        dxv = dx_ref[...]
        dmix = (dxv * g1_ref[...]).astype(BF16)
        dmix_ref[...] = dmix
        dycat = _dot_nt(dmix, w_ref[...])
        dyab_ref[...] = dycat[:, :2 * DC]
        ov = o_ref[...]
        z = jnp.concatenate([z0_ref[...], z1_ref[...]], axis=1)
        dos, dzs = [], []
        dgdn = jnp.zeros((1, HD), F32)
        for h in range(NH):
            _, vjp = jax.vjp(_yc, _head(ov, h), _head(z, h), gdn_ref[...])
            do, dz, dg = vjp(dycat[:, 2 * DC + h * HD:2 * DC + (h + 1) * HD])
            dos.append(do)
            dzs.append(dz)
            dgdn = dgdn + dg
        do_ref[...] = jnp.concatenate(dos, axis=1)
        dz_ref[...] = jnp.concatenate(dzs, axis=1)

        @pl.when(i == 0)
        def _():
            dg1_ref[...] = jnp.zeros_like(dg1_ref)
            dgdn_ref[...] = jnp.zeros_like(dgdn_ref)
        dg1_ref[...] += _colsum(dxv * mix_ref[...])
        dgdn_ref[...] += dgdn

    tile = pl.BlockSpec((tt, D), lambda i: (i, 0))
    t512 = pl.BlockSpec((tt, DDN), lambda i: (i, 0))
    return _pcall(
        body, grid=(T // tt,), name=name, cargo=cargo, args=(dx, mix, o, proj, proj, gdn, g1, _w3(w_out)),
        in_specs=[tile, tile, t512] + _z_specs(tt) + [_full((1, HD)), _full((1, D)), _wspec(wl, (D, D), lambda i: (0, 0))],
        out_specs=[tile, t512, t512, t512, _full((1, D)), _full((1, HD))],
        out_shape=[_sds((T, D), BF16), _sds((T, DDN)), _sds((T, DDN)), _sds((T, DDN)), _sds((1, D)), _sds((1, HD))])


FK = DFF // 2


def ffnout_fwd(x, gu, g2, w, *, name, wl=0, cargo=None):
    T = x.shape[0]
    tt = _tile(T, 512)

    def body(x_ref, gate_ref, up_ref, g2_ref, w_ref, act_ref, f_ref, xo_ref):
        kk = pl.program_id(1)
        act = (_silu(gate_ref[...]) * up_ref[...]).astype(BF16)
        act_ref[...] = act
        part = _dot(act, w_ref[...])

        @pl.when(kk == 0)
        def _():
            f_ref[...] = part

        @pl.when(kk == 1)
        def _():
            f = f_ref[...] + part
            f_ref[...] = f
            xo_ref[...] = x_ref[...] + g2_ref[...] * f

    tile = pl.BlockSpec((tt, D), lambda i, kk: (i, 0))
    return _pcall(
        body, grid=(T // tt, 2), name=name, cargo=cargo, args=(x, gu, gu, g2, _w3(w)),
        in_specs=[tile, pl.BlockSpec((tt, FK), lambda i, kk: (i, kk)), pl.BlockSpec((tt, FK), lambda i, kk: (i, 2 + kk)),
                  pl.BlockSpec((1, D), lambda i, kk: (0, 0)), _wspec(wl, (FK, D), lambda i, kk: (kk, 0))],
        out_specs=[pl.BlockSpec((tt, FK), lambda i, kk: (i, kk)), tile, tile],
        out_shape=[_sds((T, DFF), BF16), _sds((T, D)), _sds((T, D))])


def ffnout_bwd(dx, f, gu, g2, w, *, name, wl=0, cargo=None):
    T = dx.shape[0]
    tt = _tile(T, 512)

    def body(dx_ref, f_ref, gate_ref, up_ref, g2_ref, w_ref, df_ref, dgate_ref, dup_ref, dg2_ref):
        i, kk = pl.program_id(0), pl.program_id(1)
        dxv = dx_ref[...]
        df = (dxv * g2_ref[...]).astype(BF16)
        dact = _dot_nt(df, w_ref[...])
        gate, up = gate_ref[...], up_ref[...]
        sg = jax.nn.sigmoid(gate)
        dgate_ref[...] = (dact * up * (sg * (1.0 + gate * (1.0 - sg)))).astype(BF16)
        dup_ref[...] = (dact * (gate * sg)).astype(BF16)

        @pl.when(kk == 0)
        def _():
            df_ref[...] = df

        @pl.when((i == 0) & (kk == 0))
        def _():
            dg2_ref[...] = jnp.zeros_like(dg2_ref)

        @pl.when(kk == 0)
        def _():
            dg2_ref[...] += _colsum(dxv * f_ref[...])

    tile = pl.BlockSpec((tt, D), lambda i, kk: (i, 0))
    return _pcall(
        body, grid=(T // tt, 2), name=name, cargo=cargo, args=(dx, f, gu, gu, g2, _w3(w)),
        in_specs=[tile, tile, pl.BlockSpec((tt, FK), lambda i, kk: (i, kk)), pl.BlockSpec((tt, FK), lambda i, kk: (i, 2 + kk)),
                  pl.BlockSpec((1, D), lambda i, kk: (0, 0)), _wspec(wl, (FK, D), lambda i, kk: (kk, 0))],
        out_specs=[tile, pl.BlockSpec((tt, FK), lambda i, kk: (i, kk)), pl.BlockSpec((tt, FK), lambda i, kk: (i, kk)),
                   pl.BlockSpec((1, D), lambda i, kk: (0, 0))],
        out_shape=[_sds((T, D), BF16), _sds((T, DFF), BF16), _sds((T, DFF), BF16), _sds((1, D))])


def loss_head(x, target, gf, *, name):
    T = x.shape[0]
    tt = _tile(T, 256)

    def body(x_ref, t_ref, g_ref, loss_ref, dx_ref, dg_ref):
        i = pl.program_id(0)
        tv = t_ref[...]
        y, vjp = jax.vjp(_rms, x_ref[...], g_ref[...])
        err = y - tv
        dx, dg = vjp(err * (1.0 / D))
        dx_ref[...] = dx

        @pl.when(i == 0)
        def _():
            loss_ref[...] = jnp.zeros_like(loss_ref)
            dg_ref[...] = jnp.zeros_like(dg_ref)
        loss_ref[...] += 0.5 * jnp.sum(jnp.mean(err * err, axis=-1, keepdims=True), axis=0, keepdims=True)
        dg_ref[...] += dg

    tile = pl.BlockSpec((tt, D), lambda i: (i, 0))
    return pl.pallas_call(
        body, grid=(T // tt,), name=name, in_specs=[tile, tile, _full((1, D))],
        out_specs=[_full((1, HD)), tile, _full((1, D))], out_shape=[_sds((1, HD)), _sds((T, D)), _sds((1, D))],
        compiler_params=_cp(1))(x, target, gf)


def _pad_rows(a, rows):
    return jnp.pad(a, ((0, rows - a.shape[0]), (0, 0)))


def _row128(v):
    return jnp.pad(v, (0, HD - v.shape[0]))[None, :]


def _grow3(grow):
    return grow.reshape(8, -1, CH).transpose(1, 0, 2)


def _grow2(grow3):
    return grow3.transpose(1, 0, 2).reshape(8, -1)


def layer_params(l, mod, p):
    m = mod[l].reshape(6, 1, D)
    return dict(
        sh1=m[0], sc1=m[1], g1=m[2], sh2=m[3], sc2=m[4], g2=m[5],
        gn1=p["norm_mix_g"][l][None, :], gn2=p["norm_ffn_g"][l][None, :],
        wa=_pad_rows(p["conv_a_w"][l], 8), wb=_pad_rows(p["conf_dw_w"][l], 32), wc=_pad_rows(p["dn_conv_w"][l], 8),
        bb=p["conf_dw_b"][l][None, :], ln_g=p["conf_ln_g"][l][None, :], ln_b=p["conf_ln_b"][l][None, :],
        alog=_row128(p["dn_a_log"][l]), dt=_row128(p["dn_dt_bias"][l]), gdn=p["dn_norm_g"][l][None, :])


def layer_fwd(l, x, lp, w_in, w_out, w_ffn_in, w_ffn_out, wl=0, cargos=None):
    cg = (cargos or {}).get
    s = {"x0": x}
    s["h1"], s["proj"] = nm_fwd(x, lp["gn1"], lp["sc1"], lp["sh1"], w_in, tn=NP // 3, name=f"proj_fwd_{l}", wl=wl,
                                cargo=cg("w_in"))
    yab, s["q"], s["k"], s["v"], s["gb"], grow = mix_fwd(
        s["proj"], lp["wa"], lp["wb"], lp["bb"], lp["ln_g"], lp["ln_b"], lp["wc"], lp["alog"], lp["dt"], name=f"mix_fwd_{l}")
    s["grow3"] = _grow3(grow)
    s["u"], s["w"], s["qk"], s["p"] = dn_intra_fwd(s["q"], s["k"], s["v"], s["gb"], s["grow3"], name=f"dn_intra_fwd_{l}")
    s["o"], s["s_all"] = dn_inter_fwd(s["q"], s["k"], s["u"], s["w"], s["gb"], s["qk"], name=f"dn_inter_fwd_{l}")
    s["ycat"], s["mix"], s["x1"] = mixout_fwd(x, yab, s["o"], s["proj"], lp["gdn"], lp["g1"], w_out, name=f"mixout_fwd_{l}",
                                              wl=wl, cargo=cg("w_out"))
    s["h2"], s["gu"] = nm_fwd(s["x1"], lp["gn2"], lp["sc2"], lp["sh2"], w_ffn_in, tn=2 * DFF // 4, name=f"ffnin_fwd_{l}", wl=wl,
                              cargo=cg("w_ffn_in"))
    s["act"], s["f"], x2 = ffnout_fwd(s["x1"], s["gu"], lp["g2"], w_ffn_out, name=f"ffnout_fwd_{l}", wl=wl, cargo=cg("w_ffn_out"))
    return x2, s


def layer_bwd(l, dx2, s, lp, w_in, w_out, w_ffn_in, w_ffn_out, wl=0, carry=None):
    g = {}
    carry = carry or (lambda stage: None)
    df, dgate, dup, dg2 = ffnout_bwd(dx2, s["f"], s["gu"], lp["g2"], w_ffn_out, name=f"ffnout_bwd_{l}", wl=wl, cargo=carry("swap"))
    g["w_ffn_out"] = mm_tn_split(s["act"], df, by_cols=False, name=f"ffnout_dw_{l}")
    dx1, g["norm_ffn_g"], dsc2, dsh2, dgu = nm_bwd([dgate, dup], w_ffn_in, s["x1"], lp["gn2"], lp["sc2"], lp["sh2"], dx2,
                                                   name=f"ffnin_bwd_{l}", wl=wl, cargo=carry("exchange_a"))
    g["w_ffn_in"] = mm_tn_split(s["h2"], dgu, by_cols=True, name=f"ffnin_dw_{l}")
    dmix, dyab, do, dz, dg1, g["dn_norm_g"] = mixout_bwd(dx1, s["mix"], s["o"], s["proj"], lp["gdn"], lp["g1"], w_out,
                                                         name=f"mixout_bwd_{l}", wl=wl, cargo=carry("exchange_b"))
    g["w_out"] = mm_tn_split(s["ycat"], dmix, by_cols=False, name=f"mixout_dw_{l}")
    dq_i, dk_i, du, dw, dg_i, dqk = dn_inter_bwd(do, s["q"], s["k"], s["u"], s["w"], s["gb"], s["qk"], s["s_all"],
                                                 name=f"dn_inter_bwd_{l}", cargo=carry("exchange_c"))
    dq, dk, dv, dgb, dgrow3 = dn_intra_bwd(du, dw, dqk, dq_i, dk_i, dg_i, s["q"], s["k"], s["v"], s["gb"], s["grow3"],
                                           s["p"], name=f"dn_intra_bwd_{l}")
    dab, dconv, dblk, g["conf_dw_b"], g["conf_ln_g"], g["conf_ln_b"], dal, ddt = mix_bwd_point(
        s["proj"], dyab, dq, dk, dv, dgb, _grow2(dgrow3), lp["wa"], lp["wb"], lp["bb"], lp["ln_g"], lp["ln_b"], lp["wc"],
        lp["alog"], lp["dt"], name=f"mix_bwd_point_{l}")
    dpb, dwa, dwb, dwc = mix_bwd_conv(s["proj"], dconv, lp["wa"], lp["wb"], lp["wc"], name=f"mix_bwd_conv_{l}",
                                      cargo=carry("join"))
    dx0, g["norm_mix_g"], dsc1, dsh1, dproj = nm_bwd([dab, dpb, dz, dblk], w_in, s["x0"], lp["gn1"], lp["sc1"], lp["sh1"], dx1,
                                                     name=f"proj_bwd_{l}", wl=wl)
    g_in = mm_tn(s["h1"], dproj, tn=NP // 3, name=f"proj_dw_{l}")[:, :IN_COLS]
    g["w_in"] = g_in.reshape(2, D // 2, 4, IN_COLS // 4).transpose(0, 2, 1, 3)
    g["conv_a_w"], g["conf_dw_w"], g["dn_conv_w"] = dwa[:KA], dwb[:KB], dwc[:KC]
    g["dn_a_log"], g["dn_dt_bias"] = dal[0, :NH], ddt[0, :NH]
    g["mod"] = jnp.concatenate([dsh1, dsc1, dg1, dsh2, dsc2, dg2], axis=1)
    return dx0, g


EW_BLOCK_BYTES = 1 << 20


def _row_tile(R, C, mult=8):
    best = None
    for rt in range(mult, R + 1, mult):
        if R % rt == 0 and rt * C * 4 <= EW_BLOCK_BYTES:
            best = rt
    return best if best is not None else R


def add_half_bf16(g2, recv, core, *, name):
    _, B, R, C = g2.shape
    rt = _row_tile(R, C, 16)

    def body(core_ref, a_ref, b_ref, o_ref):
        o_ref[...] = (a_ref[...] + b_ref[...]).astype(BF16)

    spec = pl.BlockSpec((1, rt, C), lambda b, r, core_ref: (b, r, 0))
    return pl.pallas_call(
        body, name=name, out_shape=_sds((B, R, C), BF16), compiler_params=_cp(2),
        grid_spec=pltpu.PrefetchScalarGridSpec(
            num_scalar_prefetch=1, grid=(B, R // rt),
            in_specs=[pl.BlockSpec((None, 1, rt, C), lambda b, r, core_ref: (core_ref[0], b, r, 0)), spec],
            out_specs=spec))(core, g2, recv)


def adamw_halves(w, g_mine, g_theirs, core, m, v, *, name):
    L, R, C = w.shape
    rh = R // 2
    rt = _row_tile(rh, C)
    nr = rh // rt

    def body(core_ref, w_ref, gm_ref, gt_ref, m_ref, v_ref, go_ref, d_ref, mo_ref, vo_ref):
        g = jnp.where(pl.program_id(1) == core_ref[0], gm_ref[...], gt_ref[...])
        go_ref[...] = g
        d_ref[...], mo_ref[...], vo_ref[...] = _adamw_math(w_ref[...], g, m_ref[...], v_ref[...])

    spec = pl.BlockSpec((1, rt, C), lambda l, h, r, core_ref: (l, h * nr + r, 0))
    half = pl.BlockSpec((1, rt, C), lambda l, h, r, core_ref: (l, r, 0))
    return pl.pallas_call(
        body, name=name, out_shape=[_sds((L, R, C))] * 4, compiler_params=_cp(3),
        grid_spec=pltpu.PrefetchScalarGridSpec(num_scalar_prefetch=1, grid=(L, 2, nr), in_specs=[spec, half, half, spec, spec],
                                               out_specs=[spec] * 4))(core, w, g_mine, g_theirs, m, v)


def ew_call(fn, ins, n_out, *, name):
    B, R, C = ins[0].shape
    rt = _row_tile(R, C)
    n = len(ins)

    def body(*refs):
        outs = fn(*[r[...] for r in refs[:n]])
        for r, o in zip(refs[n:], outs):
            r[...] = o

    spec = pl.BlockSpec((1, rt, C), lambda b, r: (b, r, 0))
    return pl.pallas_call(body, grid=(B, R // rt), name=name, in_specs=[spec] * n, out_specs=[spec] * n_out,
                          out_shape=[_sds((B, R, C))] * n_out, compiler_params=_cp(2))(*ins)


def _adamw_math(w, g, m, v):
    m = ADAM_B1 * m + (1.0 - ADAM_B1) * g
    v = ADAM_B2 * v + (1.0 - ADAM_B2) * jnp.square(g)
    m_hat = m / (1.0 - ADAM_B1 ** ADAM_STEP)
    v_hat = v / (1.0 - ADAM_B2 ** ADAM_STEP)
    return -ADAM_LR * (m_hat / (jnp.sqrt(v_hat) + ADAM_EPS) + ADAM_WD * w), m, v


def adamw(w, g, m, v, *, name):
    shape = w.shape
    r3 = lambda a: a.reshape((-1,) + shape[-2:])
    return [o.reshape(shape) for o in ew_call(_adamw_math, [r3(w), r3(g), r3(m), r3(v)], 3, name=name)]


def sum_slots(a, *, name):
    S, B, R, C = a.shape
    rt = _row_tile(R, C, 16)

    def body(*refs):
        acc = refs[0][0, 0].astype(F32)
        for r in refs[1:S]:
            acc = acc + r[0, 0].astype(F32)
        refs[S][0] = acc

    def spec(s):
        return pl.BlockSpec((1, 1, rt, C), lambda b, r: (s, b, r, 0))
    return pl.pallas_call(body, grid=(B, R // rt), name=name, in_specs=[spec(s) for s in range(S)],
                          out_specs=pl.BlockSpec((1, rt, C), lambda b, r: (b, r, 0)), out_shape=_sds((B, R, C)),
                          compiler_params=_cp(2))(*([a] * S))


ADA_SH = 6 * D // 4
ADA_TN = 512


def mod_fwd(c_all, w_ada, b_my, *, name):
    L = w_ada.shape[0]

    def body(c_ref, w_ref, b_ref, o_ref):
        o_ref[0] = _dot(_silu(c_ref[...]).astype(BF16), w_ref[0].astype(BF16)) + b_ref[0]

    return pl.pallas_call(
        body, grid=(L, ADA_SH // ADA_TN), name=name,
        in_specs=[pl.BlockSpec((8, D), lambda l, j: (0, 0)), pl.BlockSpec((1, D, ADA_TN), lambda l, j: (l, 0, j)),
                  pl.BlockSpec((1, 1, ADA_TN), lambda l, j: (l, 0, j))],
        out_specs=pl.BlockSpec((1, 8, ADA_TN), lambda l, j: (l, 0, j)), out_shape=_sds((L, 8, ADA_SH)),
        compiler_params=_cp(2))(c_all, w_ada, b_my)


def wada_grad(c_all, dmod, *, name):
    L = dmod.shape[0]

    def body(c_ref, d_ref, o_ref):
        o_ref[0] = _dot_tn(_silu(c_ref[...]), d_ref[0], HI)

    return pl.pallas_call(
        body, grid=(L, ADA_SH // ADA_TN), name=name,
        in_specs=[pl.BlockSpec((8, D), lambda l, j: (0, 0)), pl.BlockSpec((1, 8, ADA_TN), lambda l, j: (l, 0, j))],
        out_specs=pl.BlockSpec((1, D, ADA_TN), lambda l, j: (l, 0, j)), out_shape=_sds((L, D, ADA_SH)),
        compiler_params=_cp(2))(c_all, dmod)


def _place():
    return lax.axis_index("x"), lax.axis_index("y"), lax.axis_index("c")


def _other_chips(x, y):
    return [(1 - x, y), (x, 1 - y), (1 - x, 1 - y)]


def allgather8(blocks, *, space, name):
    n = len(blocks)

    def body(*refs):
        ins, outs = refs[:n], refs[n:2 * n]
        send_sems, recv_sems, local_sems = refs[2 * n:]
        x, y, c = _place()
        me, sibling = (x, y, c), (x, y, 1 - c)
        chips = _other_chips(x, y)

        def slot(p):
            return 4 * p[0] + 2 * p[1] + p[2]

        def copy(a, k, block, to, src=None):
            dst = outs[a].at[slot(block)]
            return pltpu.make_async_remote_copy(src_ref=dst if src is None else src, dst_ref=dst, send_sem=send_sems.at[a, k],
                                                recv_sem=recv_sems.at[a, k], device_id=to, device_id_type=MESH)

        mine = [pltpu.make_async_copy(ins[a], outs[a].at[slot(me)], local_sems.at[a]) for a in range(n)]
        for cp in mine:
            cp.start()
        first = []
        for a in range(n):
            first.append(copy(a, 0, me, sibling, src=ins[a]))
            first += [copy(a, 1 + j, me, (*chip, c), src=ins[a]) for j, chip in enumerate(chips)]
        for cp in first:
            cp.start()
        passed = []
        for j, chip in enumerate(chips):
            for a in range(n):
                copy(a, 1 + j, (*chip, c), me).wait_recv()
                cp = copy(a, 4 + j, (*chip, c), sibling)
                cp.start()
                passed.append(cp)
        for a in range(n):
            copy(a, 0, sibling, me).wait_recv()
            for j, chip in enumerate(chips):
                copy(a, 4 + j, (*chip, 1 - c), me).wait_recv()
        for cp in first + passed:
            cp.wait_send()
        for cp in mine:
            cp.wait()

    spec = pl.BlockSpec(memory_space=space)
    return pl.pallas_call(
        body, name=name, in_specs=[spec] * n, out_specs=[spec] * n,
        out_shape=[_sds((8,) + b.shape, b.dtype) for b in blocks],
        scratch_shapes=[pltpu.SemaphoreType.DMA((n, 7)), pltpu.SemaphoreType.DMA((n, 7)), pltpu.SemaphoreType.DMA((n,))],
        compiler_params=pltpu.CompilerParams(vmem_limit_bytes=VMEM_LIMIT))(*blocks)


def sibling_swap(blocks, *, name, slotted=True):
    n = len(blocks)

    def body(*refs):
        ins, outs = refs[:n], refs[n:2 * n]
        send_sems, recv_sems = refs[2 * n:]
        x, y, c = _place()
        cps = [pltpu.make_async_remote_copy(src_ref=ins[a].at[1 - c] if slotted else ins[a], dst_ref=outs[a],
                                            send_sem=send_sems.at[a], recv_sem=recv_sems.at[a], device_id=(x, y, 1 - c),
                                            device_id_type=MESH)
               for a in range(n)]
        for cp in cps:
            cp.start()
        for cp in cps:
            cp.wait()

    spec = pl.BlockSpec(memory_space=pl.ANY)
    return pl.pallas_call(
        body, name=name, in_specs=[spec] * n, out_specs=[spec] * n,
        out_shape=[_sds(b.shape[1:] if slotted else b.shape, b.dtype) for b in blocks],
        scratch_shapes=[pltpu.SemaphoreType.DMA((n,)), pltpu.SemaphoreType.DMA((n,))])(*blocks)


def chip_exchange(blocks, *, name):
    n = len(blocks)

    def body(*refs):
        ins, outs = refs[:n], refs[n:2 * n]
        send_sems, recv_sems, local_sems = refs[2 * n:]
        x, y, c = _place()
        me = 2 * x + y
        chips = _other_chips(x, y)
        mine = [pltpu.make_async_copy(ins[a].at[me], outs[a].at[me], local_sems.at[a]) for a in range(n)]
        for cp in mine:
            cp.start()
        cps = []
        for a in range(n):
            for j, chip in enumerate(chips):
                cps.append(pltpu.make_async_remote_copy(
                    src_ref=ins[a].at[2 * chip[0] + chip[1]], dst_ref=outs[a].at[me], send_sem=send_sems.at[a, j],
                    recv_sem=recv_sems.at[a, j], device_id=(*chip, c), device_id_type=MESH))
        for cp in cps:
            cp.start()
        for a in range(n):
            for j, chip in enumerate(chips):
                s = 2 * chip[0] + chip[1]
                pltpu.make_async_remote_copy(src_ref=ins[a].at[s], dst_ref=outs[a].at[s], send_sem=send_sems.at[a, j],
                                             recv_sem=recv_sems.at[a, j], device_id=(*chip, c), device_id_type=MESH).wait_recv()
        for cp in cps:
            cp.wait_send()
        for cp in mine:
            cp.wait()

    spec = pl.BlockSpec(memory_space=pl.ANY)
    return pl.pallas_call(
        body, name=name, in_specs=[spec] * n, out_specs=[spec] * n, out_shape=[_sds(b.shape, b.dtype) for b in blocks],
        scratch_shapes=[pltpu.SemaphoreType.DMA((n, 3)), pltpu.SemaphoreType.DMA((n, 3)), pltpu.SemaphoreType.DMA((n,))])(*blocks)


def halves_join(halves, *, name):
    n = len(halves)

    def body(*refs):
        ins, outs = refs[:n], refs[n:2 * n]
        send_sems, recv_sems, local_sems = refs[2 * n:]
        x, y, c = _place()
        cps, mine = [], []
        for a in range(n):
            mine.append(pltpu.make_async_copy(ins[a], outs[a].at[c], local_sems.at[a]))
            cps.append(pltpu.make_async_remote_copy(src_ref=ins[a], dst_ref=outs[a].at[c], send_sem=send_sems.at[a],
                                                    recv_sem=recv_sems.at[a], device_id=(x, y, 1 - c), device_id_type=MESH))
        for cp in mine + cps:
            cp.start()
        for a in range(n):
            pltpu.make_async_remote_copy(src_ref=ins[a], dst_ref=outs[a].at[1 - c], send_sem=send_sems.at[a], recv_sem=recv_sems.at[a],
                                         device_id=(x, y, 1 - c), device_id_type=MESH).wait_recv()
        for cp in cps:
            cp.wait_send()
        for cp in mine:
            cp.wait()

    spec = pl.BlockSpec(memory_space=pl.ANY)
    return pl.pallas_call(
        body, name=name, in_specs=[spec] * n, out_specs=[spec] * n,
        out_shape=[_sds((2,) + h.shape, h.dtype) for h in halves],
        scratch_shapes=[pltpu.SemaphoreType.DMA((n,)), pltpu.SemaphoreType.DMA((n,)), pltpu.SemaphoreType.DMA((n,))])(*halves)


BIG = ("w_in", "w_out", "w_ffn_in", "w_ffn_out")
COL_SHARDED = {"w_in": True, "w_out": False, "w_ffn_in": True, "w_ffn_out": False}
SMALL = ("norm_mix_g", "norm_ffn_g", "conv_a_w", "conf_dw_w", "conf_dw_b", "conf_ln_g", "conf_ln_b", "dn_conv_w",
         "dn_a_log", "dn_dt_bias", "dn_norm_g")
SMALL_SHARDED = ("conv_a_w", "conf_dw_w", "dn_conv_w")


def _half_rows(a, c):
    rh = a.shape[1] // 2
    return lax.dynamic_slice_in_dim(a, c * rh, rh, axis=1)


def _assemble(name, g):
    _, L, rh, C = g.shape
    g = g.reshape(4, 2, L, rh, C)
    if COL_SHARDED[name]:
        return g.transpose(2, 1, 3, 0, 4).reshape(L, 2 * rh, 4 * C)
    return g.transpose(2, 0, 1, 3, 4).reshape(L, 8 * rh, C)


def _split_for_reduce(name, g, c):
    L, R, C = g.shape
    if COL_SHARDED[name]:
        t = g.reshape(L, 2, R // 2, 4, C // 4).transpose(1, 3, 0, 2, 4)
    else:
        t = g.reshape(L, 4, 2, R // 8, C).transpose(2, 1, 0, 3, 4)
    mine = lax.dynamic_index_in_dim(t, c, 0, keepdims=False)
    other = lax.dynamic_index_in_dim(t, 1 - c, 0, keepdims=False)
    return mine, other


def _pack(parts):
    flat = jnp.concatenate([p.reshape(-1) for p in parts])
    n = flat.shape[0]
    rows = -(-n // (8 * HD)) * 8
    return jnp.pad(flat, (0, rows * HD - n)).reshape(rows, HD)


def _unpack(buf, shapes):
    flat = buf.reshape(-1)
    out, off = [], 0
    for s in shapes:
        n = 1
        for d in s:
            n *= d
        out.append(flat[off:off + n].reshape(s))
        off += n
    return out


class _Reduction:
    STAGES = ("swap", "exchange_a", "exchange_b", "exchange_c", "join")
    EXCHANGED = {"exchange_a": ("w_ffn_in",), "exchange_b": ("w_in",), "exchange_c": ("w_ffn_out", "w_out")}

    def __init__(self, l, split, core):
        self.l, self.split, self.core = l, split, core
        self.cargos = {}

    def stage(self, name):
        l = self.l
        if name == "swap":
            cargo = swap_cargo([self.split[n] for n in BIG], slotted=True)
        elif name in self.EXCHANGED:
            if name == "exchange_a":
                self.chip_sum = {n: add_half_bf16(self.split[n], r, self.core, name=f"reduce_add2_{n}_{l}")
                                 for n, r in zip(BIG, self.cargos["swap"].results)}
            cargo = exchange_cargo([self.chip_sum[n] for n in self.EXCHANGED[name]])
        else:
            from_chips = {n: r for st, ns in self.EXCHANGED.items() for n, r in zip(ns, self.cargos[st].results)}
            self.mine = {n: sum_slots(from_chips[n][:, None], name=f"reduce_add4_{n}_{l}")[0] for n in BIG}
            cargo = swap_cargo([self.mine[n] for n in BIG], slotted=False)
        self.cargos[name] = cargo
        return cargo

    def finish(self):
        return self.mine, dict(zip(BIG, self.cargos["join"].results))


def kernel(x, c, w_ada, b_ada, norm_mix_g, norm_ffn_g, w_in, conv_a_w, conf_dw_w, conf_dw_b, conf_ln_g, conf_ln_b, dn_conv_w, dn_a_log, dn_dt_bias, dn_norm_g, w_out, w_ffn_in, w_ffn_out, final_norm_g, loss_target, m_w_ada, m_b_ada, m_norm_mix_g, m_norm_ffn_g, m_w_in, m_conv_a_w, m_conf_dw_w, m_conf_dw_b, m_conf_ln_g, m_conf_ln_b, m_dn_conv_w, m_dn_a_log, m_dn_dt_bias, m_dn_norm_g, m_w_out, m_w_ffn_in, m_w_ffn_out, m_final_norm_g, v_w_ada, v_b_ada, v_norm_mix_g, v_norm_ffn_g, v_w_in, v_conv_a_w, v_conf_dw_w, v_conf_dw_b, v_conf_ln_g, v_conf_ln_b, v_dn_conv_w, v_dn_a_log, v_dn_dt_bias, v_dn_norm_g, v_w_out, v_w_ffn_in, v_w_ffn_out, v_final_norm_g):
    W = dict(w_ada=w_ada, b_ada=b_ada, norm_mix_g=norm_mix_g, norm_ffn_g=norm_ffn_g, w_in=w_in, conv_a_w=conv_a_w,
             conf_dw_w=conf_dw_w, conf_dw_b=conf_dw_b, conf_ln_g=conf_ln_g, conf_ln_b=conf_ln_b, dn_conv_w=dn_conv_w,
             dn_a_log=dn_a_log, dn_dt_bias=dn_dt_bias, dn_norm_g=dn_norm_g, w_out=w_out, w_ffn_in=w_ffn_in,
             w_ffn_out=w_ffn_out, final_norm_g=final_norm_g)
    M = dict(w_ada=m_w_ada, b_ada=m_b_ada, norm_mix_g=m_norm_mix_g, norm_ffn_g=m_norm_ffn_g, w_in=m_w_in, conv_a_w=m_conv_a_w,
             conf_dw_w=m_conf_dw_w, conf_dw_b=m_conf_dw_b, conf_ln_g=m_conf_ln_g, conf_ln_b=m_conf_ln_b, dn_conv_w=m_dn_conv_w,
             dn_a_log=m_dn_a_log, dn_dt_bias=m_dn_dt_bias, dn_norm_g=m_dn_norm_g, w_out=m_w_out, w_ffn_in=m_w_ffn_in,
             w_ffn_out=m_w_ffn_out, final_norm_g=m_final_norm_g)
    V = dict(w_ada=v_w_ada, b_ada=v_b_ada, norm_mix_g=v_norm_mix_g, norm_ffn_g=v_norm_ffn_g, w_in=v_w_in, conv_a_w=v_conv_a_w,
             conf_dw_w=v_conf_dw_w, conf_dw_b=v_conf_dw_b, conf_ln_g=v_conf_ln_g, conf_ln_b=v_conf_ln_b, dn_conv_w=v_dn_conv_w,
             dn_a_log=v_dn_a_log, dn_dt_bias=v_dn_dt_bias, dn_norm_g=v_dn_norm_g, w_out=v_w_out, w_ffn_in=v_w_ffn_in,
             w_ffn_out=v_w_ffn_out, final_norm_g=v_final_norm_g)
    L = w_ada.shape[0]
    ax, ay, ac = _place()
    chip = 2 * ax + ay
    dev = 4 * ax + 2 * ay + ac

    c_all = allgather8([jnp.pad(c, ((0, 7), (0, 0)))], space=pltpu.VMEM, name="gather_c")[0][:, 0, :]
    b_my = lax.dynamic_slice_in_dim(b_ada, chip * ADA_SH, ADA_SH, axis=1)[:, None, :]
    mod_sh = mod_fwd(c_all, w_ada, b_my, name="mod_fwd")
    lh = L // 2
    mod_g = allgather8([lax.dynamic_slice_in_dim(mod_sh, ac * lh, lh, axis=0).reshape(lh * 8, ADA_SH)], space=pltpu.VMEM,
                       name="gather_mod")[0]
    mod_all = mod_g.reshape(4, 2, lh, 8, ADA_SH).transpose(1, 2, 3, 0, 4).reshape(L, 8, 6 * D)
    mod = lax.dynamic_index_in_dim(mod_all, dev, 1, keepdims=False)

    src = {n: _half_rows(W[n], ac).astype(BF16) for n in BIG}

    def assembled(gathered):
        full = {n: _assemble(n, g[:, None])[0] for n, g in zip(BIG, gathered)}
        full["w_in"] = jnp.pad(full["w_in"], ((0, 0), (0, NP - IN_COLS)))
        return [full[n] for n in BIG]

    weights = [assembled(_comm_call(gather_cargo([src[n][0] for n in BIG]), "gather_w0"))]

    p_full = dict(W)
    sm = allgather8([_pack([W[n] for n in SMALL_SHARDED])], space=pltpu.VMEM, name="gather_convw")[0]
    per_chip = [_unpack(sm[4 * (s // 2) + 2 * (s % 2)], [W[n].shape for n in SMALL_SHARDED]) for s in range(4)]
    for i, n in enumerate(SMALL_SHARDED):
        p_full[n] = jnp.concatenate([per_chip[s][i] for s in range(4)], axis=-1)

    xs = x[0]
    saves, lps = [], []
    for l in range(L):
        lp = layer_params(l, mod, p_full)
        cargos = {n: gather_cargo([src[n][l + 1]]) for n in BIG} if l + 1 < L else None
        xs, s = layer_fwd(l, xs, lp, *weights[l], cargos=cargos)
        if cargos is not None:
            weights.append(assembled([cargos[n].results[0] for n in BIG]))
        saves.append(s)
        lps.append(lp)
    loss_p, dx, dgf = loss_head(xs, loss_target[0], final_norm_g[None, :], name="loss_head")
    core = ac.astype(jnp.int32).reshape(1)
    grads = [None] * L
    reduced = [None] * L
    pending = None
    for l in reversed(range(L)):
        dx, grads[l] = layer_bwd(l, dx, saves[l], lps[l], *weights[l], carry=None if pending is None else pending.stage)
        if pending is not None:
            reduced[pending.l] = pending.finish()
        pending = _Reduction(l, {n: grads[l][n] for n in BIG}, core)
    for stage in _Reduction.STAGES:
        cargo = pending.stage(stage)
        cargo.results = _comm_call(cargo, f"reduce_{stage}_{pending.l}")
    reduced[pending.l] = pending.finish()
    loss = lax.psum(loss_p[0, 0], ("x", "y", "c"))
    grad_x = dx[None]

    small_shapes = [(L,) + p_full[n].shape[1:] for n in SMALL]
    parts = [jnp.stack([grads[l][n].reshape(sh[1:]) for l in range(L)]) for n, sh in zip(SMALL, small_shapes)]
    parts += [dgf.reshape(D), jnp.concatenate([grads[l]["mod"] for l in range(L)], axis=0)]
    small_shapes += [(D,), (L, 6 * D)]
    packed = _pack(parts)
    gathered_small = allgather8([packed], space=pltpu.VMEM, name="gather_small")[0]
    summed = sum_slots(gathered_small[:, None], name="sum_small")[0]
    g_small = dict(zip(SMALL + ("final_norm_g", "b_ada"), _unpack(summed, small_shapes)))
    for n in SMALL_SHARDED:
        sw = W[n].shape[-1]
        g_small[n] = lax.dynamic_slice_in_dim(g_small[n], chip * sw, sw, axis=g_small[n].ndim - 1)
    dmod_all = jnp.stack([_unpack(gathered_small[d], small_shapes)[-1] for d in range(8)], axis=1)
    dmod_my = lax.dynamic_slice_in_dim(dmod_all, chip * ADA_SH, ADA_SH, axis=2)
    g_w_ada = wada_grad(c_all, dmod_my, name="wada_grad")

    halves = [jnp.stack([reduced[l][0][n] for l in range(L)]) for n in BIG]
    theirs = [jnp.stack([reduced[l][1][n] for l in range(L)]) for n in BIG]

    out_g, out_d, out_m, out_v = {}, {}, {}, {}
    out_g["w_ada"] = g_w_ada
    out_d["w_ada"], out_m["w_ada"], out_v["w_ada"] = adamw(W["w_ada"], g_w_ada, M["w_ada"], V["w_ada"], name="adamw_w_ada")
    for n, g_mine, g_theirs in zip(BIG, halves, theirs):
        out_g[n], out_d[n], out_m[n], out_v[n] = adamw_halves(W[n], g_mine, g_theirs, core, M[n], V[n], name=f"adamw_{n}")
    names_small = SMALL + ("final_norm_g", "b_ada")
    pk = lambda d: _pack([d[n] for n in names_small])[None]
    d_s, m_s, v_s = ew_call(_adamw_math, [pk(W), pk(g_small), pk(M), pk(V)], 3, name="adamw_small")
    shapes_s = [W[n].shape for n in names_small]
    for d, o in ((out_d, d_s), (out_m, m_s), (out_v, v_s)):
        d.update(zip(names_small, _unpack(o[0], shapes_s)))
    for n in names_small:
        out_g[n] = g_small[n].reshape(W[n].shape)

    order = ("w_ada", "b_ada", "norm_mix_g", "norm_ffn_g", "w_in", "conv_a_w", "conf_dw_w", "conf_dw_b", "conf_ln_g", "conf_ln_b",
             "dn_conv_w", "dn_a_log", "dn_dt_bias", "dn_norm_g", "w_out", "w_ffn_in", "w_ffn_out", "final_norm_g")
    return (loss, grad_x, *[out_g[n] for n in order], *[out_d[n] for n in order], *[out_m[n] for n in order],
            *[out_v[n] for n in order])
```

```python
import functools

import jax
import jax.numpy as jnp
from jax import lax
from jax.experimental import pallas as pl
from jax.experimental.pallas import tpu as pltpu

F32 = jnp.float32
BF16 = jnp.bfloat16
HI = lax.Precision.HIGHEST
MESH = pl.DeviceIdType.MESH

D = 1024
DEPTH = 4
DC = 256
DDN = 512
NH = 4
HD = 128
CH = 64
DFF = 2816
IN_COLS = 3336
NP = 3456
KA, KB, KC = 3, 31, 4
HALO = 32
EPS = 1e-6
O_AB, O_AC, O_AV, O_BA, O_BG, O_Q, O_K, O_V, O_Z, O_GB = 0, 256, 512, 768, 1024, 1280, 1792, 2304, 2816, 3328

ADAM_LR, ADAM_B1, ADAM_B2, ADAM_EPS, ADAM_WD, ADAM_STEP = 0.001, 0.9, 0.999, 1e-08, 0.01, 10

VMEM_LIMIT = 56 * 1024 * 1024


def _cp(n_grid):
    return pltpu.CompilerParams(dimension_semantics=("arbitrary",) * n_grid, vmem_limit_bytes=VMEM_LIMIT)


def _sds(shape, dtype=F32):
    return jax.ShapeDtypeStruct(tuple(shape), dtype)


def _dot(a, b, prec=None):
    return jnp.dot(a, b, preferred_element_type=F32, precision=prec)


def _dot_nt(a, b, prec=None):
    return lax.dot_general(a, b, (((1,), (1,)), ((), ())), preferred_element_type=F32, precision=prec)


def _dot_tn(a, b, prec=None):
    return lax.dot_general(a, b, (((0,), (0,)), ((), ())), preferred_element_type=F32, precision=prec)


def _split_bf16(a):
    hi = a.astype(BF16)
    return hi, (a - hi.astype(F32)).astype(BF16)


_DIMS = {"nn": (((1,), (0,)), ((), ())), "nt": (((1,), (1,)), ((), ())), "tn": (((0,), (0,)), ((), ()))}
_DIMS_BATCHED = {"nn": (((2,), (1,)), ((0,), (0,))), "nt": (((2,), (2,)), ((0,), (0,))), "tn": (((1,), (1,)), ((0,), (0,)))}


def _mm3_raw(a, b, kind):
    ah, al = _split_bf16(a)
    bh, bl = _split_bf16(b)
    d = (_DIMS if a.ndim == 2 else _DIMS_BATCHED)[kind]
    dg = lambda u, v: lax.dot_general(u, v, d, preferred_element_type=F32)
    return dg(ah, bh) + (dg(ah, bl) + dg(al, bh))


@jax.custom_vjp
def _mm_nn(a, b):
    return _mm3_raw(a, b, "nn")


@jax.custom_vjp
def _mm_nt(a, b):
    return _mm3_raw(a, b, "nt")


@jax.custom_vjp
def _mm_tn(a, b):
    return _mm3_raw(a, b, "tn")


_mm_nn.defvjp(lambda a, b: (_mm_nn(a, b), (a, b)), lambda r, g: (_mm_nt(g, r[1]), _mm_tn(r[0], g)))
_mm_nt.defvjp(lambda a, b: (_mm_nt(a, b), (a, b)), lambda r, g: (_mm_nn(g, r[1]), _mm_tn(g, r[0])))
_mm_tn.defvjp(lambda a, b: (_mm_tn(a, b), (a, b)), lambda r, g: (_mm_nt(r[1], g), _mm_nn(r[0], g)))


def _mm1_raw(a, b, kind):
    d = (_DIMS if a.ndim == 2 else _DIMS_BATCHED)[kind]
    return lax.dot_general(a.astype(BF16), b.astype(BF16), d, preferred_element_type=F32)


@jax.custom_vjp
def _mm1_nn(a, b):
    return _mm1_raw(a, b, "nn")


@jax.custom_vjp
def _mm1_nt(a, b):
    return _mm1_raw(a, b, "nt")


@jax.custom_vjp
def _mm1_tn(a, b):
    return _mm1_raw(a, b, "tn")


_mm1_nn.defvjp(lambda a, b: (_mm1_nn(a, b), (a, b)), lambda r, g: (_mm1_nt(g, r[1]), _mm1_tn(r[0], g)))
_mm1_nt.defvjp(lambda a, b: (_mm1_nt(a, b), (a, b)), lambda r, g: (_mm1_nn(g, r[1]), _mm1_tn(g, r[0])))
_mm1_tn.defvjp(lambda a, b: (_mm1_tn(a, b), (a, b)), lambda r, g: (_mm1_nt(r[1], g), _mm1_nn(r[0], g)))


@jax.custom_vjp
def _inv_given(x, p):
    return p


_inv_given.defvjp(lambda x, p: (p, p), lambda p, g: (_mm_tn(p, _mm_nt(g, p)), jnp.zeros_like(p)))


def _silu(x):
    return x * jax.nn.sigmoid(x)


def _colsum(x):
    return jnp.sum(x, axis=0, keepdims=True)


TILE_CAP = 512


def _w3(w):
    return w if w.ndim == 3 else w[None]


def _wspec(l, block, index):
    return pl.BlockSpec((None,) + block, lambda *g: (l,) + index(*g))


def _tile(T, want):
    t = min(T, want, TILE_CAP)
    assert T % t == 0
    return t


def _normmod(x, gn, sc, sh):
    r = lax.rsqrt(jnp.mean(x * x, axis=-1, keepdims=True) + EPS)
    return ((x * r) * gn) * (1.0 + sc) + sh


def _rms(x, g):
    return (x * lax.rsqrt(jnp.mean(x * x, axis=-1, keepdims=True) + EPS)) * g


def _mix_b_post(u, ln_g, ln_b):
    mu = jnp.mean(u, axis=-1, keepdims=True)
    var = jnp.mean(jnp.square(u - mu), axis=-1, keepdims=True)
    return _silu(((u - mu) * lax.rsqrt(var + 1e-5)) * ln_g + ln_b)


def _softplus(z):
    return jnp.where(z > 0, z, 0.0) + jnp.log(1.0 + jnp.exp(-jnp.where(z > 0, z, -z)))


def _chunk_tril(tt):
    r = lax.broadcasted_iota(jnp.int32, (tt, tt), 0)
    c = lax.broadcasted_iota(jnp.int32, (tt, tt), 1)
    return ((r // CH == c // CH) & (c <= r)).astype(F32)


def _eye8():
    return (lax.broadcasted_iota(jnp.int32, (8, HD), 0) == lax.broadcasted_iota(jnp.int32, (8, HD), 1)).astype(F32)


def _dn_post(pre_q, pre_k, pre_v, blk, alog_row, dt_row):
    q = [s * lax.rsqrt(jnp.sum(s * s, -1, keepdims=True) + EPS) * (HD ** -0.5) for s in map(_silu, pre_q)]
    k = [s * lax.rsqrt(jnp.sum(s * s, -1, keepdims=True) + EPS) for s in map(_silu, pre_k)]
    v = [_silu(p) for p in pre_v]
    lane = lax.broadcasted_iota(jnp.int32, (1, HD), 1)
    g = -jnp.exp(alog_row) * _softplus(blk + dt_row)
    beta = jax.nn.sigmoid(blk)
    gc = _dot(_chunk_tril(blk.shape[0]), jnp.where(lane < NH, g, 0.0), HI)
    gb = jnp.where(lane < NH, gc, jnp.where(lane < 2 * NH, beta, 0.0))
    grow = _dot_nt(_eye8(), gc, HI)
    return q, k, v, gb, grow


def _dn_intra(q, k, v, beta, gcol, grow, p_known=None):
    r = lax.broadcasted_iota(jnp.int32, (CH, CH), 0)
    c = lax.broadcasted_iota(jnp.int32, (CH, CH), 1)
    causal, strict = c <= r, c < r
    decay = jnp.where(causal, jnp.exp(jnp.where(causal, gcol - grow, 0.0)), 0.0)
    kb = k * beta
    x = -jnp.where(strict, _mm1_nt(kb, k) * decay, 0.0)
    if p_known is None:
        p = (r == c).astype(F32) + x
        y = x
        for _ in range(5):
            y = _mm_nn(y, y)
            p = p + _mm_nn(p, y)
    else:
        p = _inv_given(x, p_known)
    u = _mm1_nn(p, v * beta)
    w = _mm1_nn(p, kb * jnp.exp(gcol))
    qk = jnp.where(causal, _mm1_nt(q, k) * decay, 0.0)
    return u, w, qk, p


def _dn_inter(s, q, k, u, w, gcol, qk):
    last = (lax.broadcasted_iota(jnp.int32, (CH, 1), 0) == CH - 1).astype(F32)
    g_last = jnp.sum(gcol * last, axis=1, keepdims=True)
    v_new = u - _mm1_nn(w, s)
    o = _mm1_nn(q * jnp.exp(gcol), s) + _mm1_nn(qk, v_new)
    s_new = s * jnp.exp(g_last) + _mm1_tn(k * jnp.exp(g_last - gcol), v_new)
    return s_new, o


def _yc(o, z, gdn):
    return _rms(o, gdn) * _silu(z)


class Cargo:
    def __init__(self, ins, outs, sems, first, last, middle=None):
        self.ins, self.outs, self.sems = list(ins), list(outs), list(sems)
        self.first, self.middle, self.last = first, middle, last
        self.results = None


def _pcall(body, *, grid, in_specs, out_specs, out_shape, args, name, scratch_shapes=(), cargo=None):
    if cargo is None:
        return pl.pallas_call(body, grid=grid, in_specs=list(in_specs), out_specs=list(out_specs), out_shape=list(out_shape),
                              scratch_shapes=list(scratch_shapes), name=name, compiler_params=_cp(len(grid)))(*args)
    n_in, n_out, n_scr = len(args), len(out_shape), len(scratch_shapes)
    k_in, k_out = len(cargo.ins), len(cargo.outs)
    total = 1
    for g in grid:
        total *= g

    def carrying(*refs):
        ins, cins = refs[:n_in], refs[n_in:n_in + k_in]
        o0 = n_in + k_in
        outs, couts = refs[o0:o0 + n_out], refs[o0 + n_out:o0 + n_out + k_out]
        rest = refs[o0 + n_out + k_out:]
        scr, csems = rest[:n_scr], rest[n_scr:]
        step = pl.program_id(0)
        for a in range(1, len(grid)):
            step = step * grid[a] + pl.program_id(a)

        @pl.when(step == 0)
        def _():
            cargo.first(cins, couts, csems)
        if cargo.middle is not None:
            @pl.when(step == (total * 7) // 8)
            def _():
                cargo.middle(cins, couts, csems)
        body(*ins, *outs, *scr)

        @pl.when(step == total - 1)
        def _():
            cargo.last(cins, couts, csems)

    hbm = pl.BlockSpec(memory_space=pl.ANY)
    res = pl.pallas_call(
        carrying, grid=grid, in_specs=list(in_specs) + [hbm] * k_in, out_specs=list(out_specs) + [hbm] * k_out,
        out_shape=list(out_shape) + cargo.outs, scratch_shapes=list(scratch_shapes) + cargo.sems, name=name,
        compiler_params=_cp(len(grid)))(*args, *cargo.ins)
    cargo.results = list(res[n_out:])
    return list(res[:n_out])


def _place():
    return lax.axis_index("x"), lax.axis_index("y"), lax.axis_index("c")


def _other_chips(x, y):
    return [(1 - x, y), (x, 1 - y), (1 - x, 1 - y)]


def _gather_phases(n):
    def env(ins, outs, sems):
        send_sems, recv_sems, local_sems = sems
        x, y, c = _place()
        me, sibling = (x, y, c), (x, y, 1 - c)
        chips = _other_chips(x, y)

        def slot(p):
            return 4 * p[0] + 2 * p[1] + p[2]

        def copy(a, k, block, to, own=False):
            dst = outs[a].at[slot(block)]
            return pltpu.make_async_remote_copy(src_ref=ins[a] if own else dst, dst_ref=dst, send_sem=send_sems.at[a, k],
                                                recv_sem=recv_sems.at[a, k], device_id=to, device_id_type=MESH)

        mine = [pltpu.make_async_copy(ins[a], outs[a].at[slot(me)], local_sems.at[a]) for a in range(n)]
        own = [copy(a, 0, me, sibling, own=True) for a in range(n)]
        own += [copy(a, 1 + j, me, (*chip, c), own=True) for a in range(n) for j, chip in enumerate(chips)]
        return c, me, sibling, chips, copy, mine, own

    def first(ins, outs, sems):
        _, _, _, _, _, mine, own = env(ins, outs, sems)
        for cp in mine + own:
            cp.start()

    def middle(ins, outs, sems):
        c, me, sibling, chips, copy, _, _ = env(ins, outs, sems)
        for j, chip in enumerate(chips):
            for a in range(n):
                copy(a, 1 + j, (*chip, c), me).wait_recv()
                copy(a, 4 + j, (*chip, c), sibling).start()

    def last(ins, outs, sems):
        c, me, sibling, chips, copy, mine, own = env(ins, outs, sems)
        for a in range(n):
            copy(a, 0, sibling, me).wait_recv()
            for j, chip in enumerate(chips):
                copy(a, 4 + j, (*chip, 1 - c), me).wait_recv()
        for cp in own:
            cp.wait_send()
        for a in range(n):
            for j, chip in enumerate(chips):
                copy(a, 4 + j, (*chip, c), sibling).wait_send()
        for cp in mine:
            cp.wait()

    return first, middle, last


def _gather_sems(n):
    return [pltpu.SemaphoreType.DMA((n, 7)), pltpu.SemaphoreType.DMA((n, 7)), pltpu.SemaphoreType.DMA((n,))]


def gather_cargo(blocks):
    first, middle, last = _gather_phases(len(blocks))
    return Cargo(blocks, [_sds((8,) + b.shape, b.dtype) for b in blocks], _gather_sems(len(blocks)), first, last, middle)


def _swap_phases(n, slotted):
    def copies(ins, outs, sems):
        send_sems, recv_sems = sems
        x, y, c = _place()
        return [pltpu.make_async_remote_copy(src_ref=ins[a].at[1 - c] if slotted else ins[a], dst_ref=outs[a],
                                             send_sem=send_sems.at[a], recv_sem=recv_sems.at[a], device_id=(x, y, 1 - c),
                                             device_id_type=MESH) for a in range(n)]

    def first(ins, outs, sems):
        for cp in copies(ins, outs, sems):
            cp.start()

    def last(ins, outs, sems):
        for cp in copies(ins, outs, sems):
            cp.wait()

    return first, last


def swap_cargo(blocks, slotted):
    n = len(blocks)
    first, last = _swap_phases(n, slotted)
    return Cargo(blocks, [_sds(b.shape[1:] if slotted else b.shape, b.dtype) for b in blocks],
                 [pltpu.SemaphoreType.DMA((n,)), pltpu.SemaphoreType.DMA((n,))], first, last)


def _exchange_phases(n):
    def env(ins, outs, sems):
        send_sems, recv_sems, local_sems = sems
        x, y, c = _place()
        me = 2 * x + y
        chips = _other_chips(x, y)
        mine = [pltpu.make_async_copy(ins[a].at[me], outs[a].at[me], local_sems.at[a]) for a in range(n)]
        sends = [pltpu.make_async_remote_copy(src_ref=ins[a].at[2 * chip[0] + chip[1]], dst_ref=outs[a].at[me],
                                              send_sem=send_sems.at[a, j], recv_sem=recv_sems.at[a, j], device_id=(*chip, c),
                                              device_id_type=MESH) for a in range(n) for j, chip in enumerate(chips)]
        recvs = [pltpu.make_async_remote_copy(src_ref=ins[a].at[2 * chip[0] + chip[1]], dst_ref=outs[a].at[2 * chip[0] + chip[1]],
                                              send_sem=send_sems.at[a, j], recv_sem=recv_sems.at[a, j], device_id=(*chip, c),
                                              device_id_type=MESH) for a in range(n) for j, chip in enumerate(chips)]
        return mine, sends, recvs

    def first(ins, outs, sems):
        mine, sends, _ = env(ins, outs, sems)
        for cp in mine + sends:
            cp.start()

    def last(ins, outs, sems):
        mine, sends, recvs = env(ins, outs, sems)
        for cp in recvs:
            cp.wait_recv()
        for cp in sends:
            cp.wait_send()
        for cp in mine:
            cp.wait()

    return first, last


def exchange_cargo(blocks):
    n = len(blocks)
    first, last = _exchange_phases(n)
    return Cargo(blocks, [_sds(b.shape, b.dtype) for b in blocks],
                 [pltpu.SemaphoreType.DMA((n, 3)), pltpu.SemaphoreType.DMA((n, 3)), pltpu.SemaphoreType.DMA((n,))], first, last)


def _comm_call(cargo, name):
    def body(*refs):
        k_in, k_out = len(cargo.ins), len(cargo.outs)
        ins, outs, sems = refs[:k_in], refs[k_in:k_in + k_out], refs[k_in + k_out:]
        cargo.first(ins, outs, sems)
        if cargo.middle is not None:
            cargo.middle(ins, outs, sems)
        cargo.last(ins, outs, sems)

    hbm = pl.BlockSpec(memory_space=pl.ANY)
    return list(pl.pallas_call(body, name=name, in_specs=[hbm] * len(cargo.ins), out_specs=[hbm] * len(cargo.outs),
                               out_shape=cargo.outs, scratch_shapes=cargo.sems)(*cargo.ins))


def nm_fwd(x, gn, sc, sh, w, *, tn, name, wl=0, cargo=None):
    T, N = x.shape[0], w.shape[-1]
    tt = _tile(T, 512)
    nt = T // tt

    def body(x_ref, gn_ref, sc_ref, sh_ref, w_ref, h_ref, o_ref, h_all):
        i = pl.program_id(1)

        @pl.when(pl.program_id(0) == 0)
        def _():
            h = _normmod(x_ref[...], gn_ref[...], sc_ref[...], sh_ref[...]).astype(BF16)
            h_all[i] = h
            h_ref[...] = h
        o_ref[...] = _dot(h_all[i], w_ref[...])

    def x_index(j, i):
        return (jnp.where(j == 0, i, nt - 1), 0)

    row = pl.BlockSpec((1, D), lambda j, i: (0, 0))
    return _pcall(
        body, grid=(N // tn, nt), name=name, cargo=cargo,
        in_specs=[pl.BlockSpec((tt, D), x_index), row, row, row, _wspec(wl, (D, tn), lambda j, i: (0, j))],
        out_specs=[pl.BlockSpec((tt, D), x_index), pl.BlockSpec((tt, tn), lambda j, i: (i, j))],
        out_shape=[_sds((T, D), BF16), _sds((T, N))], scratch_shapes=[pltpu.VMEM((nt, tt, D), BF16)],
        args=(x, gn, sc, sh, _w3(w)))


def nm_bwd(dys, w, x, gn, sc, sh, dres, *, name, wl=0, cargo=None):
    T = x.shape[0]
    N = w.shape[-1]
    tt = _tile(T, 256)
    widths = [a.shape[1] for a in dys]
    assert sum(widths) == N
    n = len(dys)

    def body(*refs):
        dy_refs, (w_ref, x_ref, gn_ref, sc_ref, sh_ref, dres_ref) = refs[:n], refs[n:n + 6]
        dx_ref, dgn_ref, dsc_ref, dsh_ref, dyb_ref = refs[n + 6:]
        i = pl.program_id(0)
        dyb = jnp.concatenate([r[...].astype(BF16) for r in dy_refs], axis=1) if n > 1 else dy_refs[0][...].astype(BF16)
        dyb_ref[...] = dyb
        dh = _dot_nt(dyb, w_ref[...])
        _, vjp = jax.vjp(_normmod, x_ref[...], gn_ref[...], sc_ref[...], sh_ref[...])
        dx, dgn, dsc, dsh = vjp(dh)
        dx_ref[...] = dres_ref[...] + dx

        @pl.when(i == 0)
        def _():
            dgn_ref[...] = jnp.zeros_like(dgn_ref)
            dsc_ref[...] = jnp.zeros_like(dsc_ref)
            dsh_ref[...] = jnp.zeros_like(dsh_ref)
        dgn_ref[...] += dgn
        dsc_ref[...] += dsc
        dsh_ref[...] += dsh

    row = pl.BlockSpec((1, D), lambda i: (0, 0))
    tile = pl.BlockSpec((tt, D), lambda i: (i, 0))
    return _pcall(
        body, grid=(T // tt,), name=name, cargo=cargo,
        in_specs=[pl.BlockSpec((tt, wd), lambda i: (i, 0)) for wd in widths]
        + [_wspec(wl, (D, N), lambda i: (0, 0)), tile, row, row, row, tile],
        out_specs=[tile, row, row, row, pl.BlockSpec((tt, N), lambda i: (i, 0))],
        out_shape=[_sds((T, D)), _sds((1, D)), _sds((1, D)), _sds((1, D)), _sds((T, N), BF16)],
        args=(*dys, _w3(w), x, gn, sc, sh, dres))


def mm_tn(a, b, *, tn, name):
    T, K = a.shape
    N = b.shape[1]
    tt = _tile(T, 512)

    def body(a_ref, b_ref, o_ref):
        @pl.when(pl.program_id(1) == 0)
        def _():
            o_ref[...] = jnp.zeros_like(o_ref)
        o_ref[...] += _dot_tn(a_ref[...], b_ref[...])

    return pl.pallas_call(
        body, grid=(N // tn, T // tt), name=name,
        in_specs=[pl.BlockSpec((tt, K), lambda j, t: (t, 0)), pl.BlockSpec((tt, tn), lambda j, t: (t, j))],
        out_specs=pl.BlockSpec((K, tn), lambda j, t: (0, j)), out_shape=_sds((K, N)), compiler_params=_cp(2))(a, b)


def mm_tn_split(a, b, *, by_cols, name):
    T, K = a.shape
    N = b.shape[1]
    tt = _tile(T, 512)
    tn = N // 4 if by_cols else N // 2
    rh, C = (K // 2, N // 4) if by_cols else (K // 8, N)
    nt = T // tt

    def body(a_ref, b_ref, o_ref, acc):
        t = pl.program_id(1)

        @pl.when(t == 0)
        def _():
            acc[...] = jnp.zeros_like(acc)
        acc[...] += _dot_tn(a_ref[...], b_ref[...])

        @pl.when(t == nt - 1)
        def _():
            if by_cols:
                for h in range(2):
                    o_ref[h] = acc[pl.ds(h * rh, rh), :]
            else:
                for s in range(4):
                    for h in range(2):
                        o_ref[h, s] = acc[pl.ds((2 * s + h) * rh, rh), :]

    if by_cols:
        out_spec = pl.BlockSpec((2, None, rh, tn), lambda j, t: (0, j, 0, 0))
    else:
        out_spec = pl.BlockSpec((2, 4, rh, tn), lambda j, t: (0, 0, 0, j))
    return pl.pallas_call(
        body, grid=(N // tn, nt), name=name,
        in_specs=[pl.BlockSpec((tt, K), lambda j, t: (t, 0)), pl.BlockSpec((tt, tn), lambda j, t: (t, j))],
        out_specs=out_spec, out_shape=_sds((2, 4, rh, C)), scratch_shapes=[pltpu.VMEM((K, tn), F32)],
        compiler_params=_cp(2))(a, b)


def _fill_pad(pad_ref, prev, cur, first):
    pad_ref[pl.ds(0, HALO), :] = jnp.where(first, 0.0, prev)
    pad_ref[pl.ds(HALO, cur.shape[0]), :] = cur


SUBLANES = 8


def _shift_scratch(tt, C):
    return pltpu.VMEM((SUBLANES - 1, tt + HALO - SUBLANES, C), F32)


def _preshift(src_ref, sh_ref, tt, cols=slice(None)):
    for b in range(1, SUBLANES):
        sh_ref[b - 1] = src_ref[pl.ds(b, tt + HALO - SUBLANES), cols]


def _window(src_ref, sh_ref, off, tt, cols=slice(None)):
    a, b = divmod(off, SUBLANES)
    if b == 0 or sh_ref is None:
        return src_ref[pl.ds(off, tt), cols]
    return sh_ref[b - 1, pl.ds(SUBLANES * a, tt), :]


def _causal_conv(pad_ref, w, K, tt, sh_ref=None):
    acc = None
    for k in range(K):
        term = _window(pad_ref, sh_ref, HALO - (K - 1) + k, tt) * w[k:k + 1, :]
        acc = term if acc is None else acc + term
    return acc


def _conv_inputs(pc, pp, first, pad_a, pad_b, pad_c):
    def s_of(p):
        return p[:, O_AC:O_AV] * p[:, O_AV:O_BA]

    def u0_of(p):
        return p[:, O_BA:O_BG] * jax.nn.sigmoid(p[:, O_BG:O_Q])

    _fill_pad(pad_a, s_of(pp), s_of(pc), first)
    _fill_pad(pad_b, u0_of(pp), u0_of(pc), first)
    _fill_pad(pad_c, pp[:, O_Q:O_Z], pc[:, O_Q:O_Z], first)


def _mix_specs(T, tt):
    cur = pl.BlockSpec((tt, O_Z), lambda i: (i, 0))
    prev = pl.BlockSpec((HALO, O_Z), lambda i: (jnp.maximum(i * (tt // HALO) - 1, 0), 0))
    gbb = pl.BlockSpec((tt, HD), lambda i: (i, O_GB // HD))
    return cur, prev, gbb


def _full(shape):
    return pl.BlockSpec(shape, lambda i: (0,) * len(shape))


def mix_fwd(proj, wa, wb, bb, ln_g, ln_b, wc, alog_row, dt_row, *, name):
    T = proj.shape[0]
    tt = _tile(T, 256)

    def body(pc_ref, pp_ref, blk_ref, wa_ref, wb_ref, bb_ref, lg_ref, lb_ref, wc_ref, al_ref, dt_ref,
             yab_ref, q_ref, k_ref, v_ref, gb_ref, grow_ref, pad_a, pad_b, pad_c, sh_b):
        first = pl.program_id(0) == 0
        pc = pc_ref[...]
        _conv_inputs(pc, pp_ref[...], first, pad_a, pad_b, pad_c)
        _preshift(pad_b, sh_b, tt)
        ya = pc[:, O_AB:O_AC] * _causal_conv(pad_a, wa_ref[...], KA, tt)
        yb = _mix_b_post(_causal_conv(pad_b, wb_ref[...], KB, tt, sh_b) + bb_ref[...], lg_ref[...], lb_ref[...])
        yab_ref[...] = jnp.concatenate([ya, yb], axis=1)
        pre = _causal_conv(pad_c, wc_ref[...], KC, tt)
        blocks = [pre[:, j * HD:(j + 1) * HD] for j in range(3 * NH)]
        q, k, v, gb, grow = _dn_post(blocks[:NH], blocks[NH:2 * NH], blocks[2 * NH:], blk_ref[...], al_ref[...], dt_ref[...])
        q_ref[...] = jnp.concatenate(q, axis=1)
        k_ref[...] = jnp.concatenate(k, axis=1)
        v_ref[...] = jnp.concatenate(v, axis=1)
        gb_ref[...] = gb
        grow_ref[...] = grow

    cur, prev, gbb = _mix_specs(T, tt)
    t512 = pl.BlockSpec((tt, DDN), lambda i: (i, 0))
    return pl.pallas_call(
        body, grid=(T // tt,), name=name,
        in_specs=[cur, prev, gbb, _full((8, DC)), _full((32, DC)), _full((1, DC)), _full((1, DC)), _full((1, DC)),
                  _full((8, 3 * DDN)), _full((1, HD)), _full((1, HD))],
        out_specs=[t512, t512, t512, t512, pl.BlockSpec((tt, HD), lambda i: (i, 0)), pl.BlockSpec((8, tt), lambda i: (0, i))],
        out_shape=[_sds((T, 2 * DC)), _sds((T, DDN)), _sds((T, DDN)), _sds((T, DDN)), _sds((T, HD)), _sds((8, T))],
        scratch_shapes=[pltpu.VMEM((HALO + tt, DC), F32), pltpu.VMEM((HALO + tt, DC), F32), pltpu.VMEM((HALO + tt, 3 * DDN), F32),
                        _shift_scratch(tt, DC)],
        compiler_params=_cp(1))(proj, proj, proj, wa, wb, bb, ln_g, ln_b, wc, alog_row, dt_row)


def mix_bwd_point(proj, dyab, dq, dk, dv, dgb, dgrow, wa, wb, bb, ln_g, ln_b, wc, alog_row, dt_row, *, name, cargo=None):
    T = proj.shape[0]
    tt = _tile(T, 256)
    CW = 2 * DC + 3 * DDN

    def body(pc_ref, pp_ref, blk_ref, dyab_ref, dq_ref, dk_ref, dv_ref, dgb_ref, dgrow_ref,
             wa_ref, wb_ref, bb_ref, lg_ref, lb_ref, wc_ref, al_ref, dt_ref,
             dab_ref, dconv_ref, dblk_ref, dbb_ref, dlg_ref, dlb_ref, dal_ref, ddt_ref, pad_a, pad_b, pad_c, sh_b):
        i = pl.program_id(0)
        pc = pc_ref[...]
        _conv_inputs(pc, pp_ref[...], i == 0, pad_a, pad_b, pad_c)
        _preshift(pad_b, sh_b, tt)
        ca = _causal_conv(pad_a, wa_ref[...], KA, tt)
        u = _causal_conv(pad_b, wb_ref[...], KB, tt, sh_b) + bb_ref[...]
        pre = _causal_conv(pad_c, wc_ref[...], KC, tt)
        dyab_v = dyab_ref[...]
        dya, dyb = dyab_v[:, :DC], dyab_v[:, DC:]
        dab_ref[...] = dya * ca
        dca = dya * pc[:, O_AB:O_AC]
        _, vjp_b = jax.vjp(_mix_b_post, u, lg_ref[...], lb_ref[...])
        du, dlg, dlb = vjp_b(dyb)
        blocks = [pre[:, j * HD:(j + 1) * HD] for j in range(3 * NH)]
        _, vjp_c = jax.vjp(_dn_post, blocks[:NH], blocks[NH:2 * NH], blocks[2 * NH:], blk_ref[...], al_ref[...], dt_ref[...])

        def heads(r):
            vv = r[...]
            return [vv[:, h * HD:(h + 1) * HD] for h in range(NH)]
        dpq, dpk, dpv, dblk, dal, ddt = vjp_c((heads(dq_ref), heads(dk_ref), heads(dv_ref), dgb_ref[...], dgrow_ref[...]))
        dconv_ref[...] = jnp.concatenate([dca, du] + dpq + dpk + dpv, axis=1)
        dblk_ref[...] = dblk

        @pl.when(i == 0)
        def _():
            for r in (dbb_ref, dlg_ref, dlb_ref, dal_ref, ddt_ref):
                r[...] = jnp.zeros_like(r)
        dbb_ref[...] += _colsum(du)
        dlg_ref[...] += dlg
        dlb_ref[...] += dlb
        dal_ref[...] += dal
        ddt_ref[...] += ddt

    cur, prev, gbb = _mix_specs(T, tt)
    t512 = pl.BlockSpec((tt, DDN), lambda i: (i, 0))
    t128 = pl.BlockSpec((tt, HD), lambda i: (i, 0))
    return _pcall(
        body, grid=(T // tt,), name=name, cargo=cargo,
        args=(proj, proj, proj, dyab, dq, dk, dv, dgb, dgrow, wa, wb, bb, ln_g, ln_b, wc, alog_row, dt_row),
        in_specs=[cur, prev, gbb, t512, t512, t512, t512, t128, pl.BlockSpec((8, tt), lambda i: (0, i)),
                  _full((8, DC)), _full((32, DC)), _full((1, DC)), _full((1, DC)), _full((1, DC)),
                  _full((8, 3 * DDN)), _full((1, HD)), _full((1, HD))],
        out_specs=[pl.BlockSpec((tt, DC), lambda i: (i, 0)), pl.BlockSpec((tt, CW), lambda i: (i, 0)), t128,
                   _full((1, DC)), _full((1, DC)), _full((1, DC)), _full((1, HD)), _full((1, HD))],
        out_shape=[_sds((T, DC)), _sds((T, CW)), _sds((T, HD)), _sds((1, DC)), _sds((1, DC)), _sds((1, DC)),
                   _sds((1, HD)), _sds((1, HD))],
        scratch_shapes=[pltpu.VMEM((HALO + tt, DC), F32), pltpu.VMEM((HALO + tt, DC), F32), pltpu.VMEM((HALO + tt, 3 * DDN), F32),
                        _shift_scratch(tt, DC)])


def mix_bwd_conv(proj, dconv, wa, wb, wc, *, name, cargo=None):
    T = proj.shape[0]
    tt = _tile(T, 256)
    CW = 2 * DC + 3 * DDN
    nblk = T // HALO

    def body(pc_ref, pp_ref, dc_ref, dn_ref, wa_ref, wb_ref, wc_ref, dp_ref, dwa_ref, dwb_ref, dwc_ref,
             pad_a, pad_b, pad_c, dpad, sh_b, sh_d):
        i = pl.program_id(0)
        last = i == pl.num_programs(0) - 1
        pc = pc_ref[...]
        _conv_inputs(pc, pp_ref[...], i == 0, pad_a, pad_b, pad_c)
        dcur = dc_ref[...]
        dpad[pl.ds(0, tt), :] = dcur
        dpad[pl.ds(tt, HALO), :] = jnp.where(last, 0.0, dn_ref[...])
        _preshift(pad_b, sh_b, tt)
        _preshift(dpad, sh_d, tt, slice(DC, 2 * DC))

        @pl.when(i == 0)
        def _():
            for r in (dwa_ref, dwb_ref, dwc_ref):
                r[...] = jnp.zeros_like(r)

        def tconv(lo, hi, w, K, pad_ref, dw_ref, sh_x=None, sh_dy=None):
            dy = dcur[:, lo:hi]
            acc = None
            for k in range(K):
                term = _window(dpad, sh_dy, K - 1 - k, tt, slice(lo, hi)) * w[k:k + 1, :]
                acc = term if acc is None else acc + term
                dw_ref[pl.ds(k, 1), :] += _colsum(dy * _window(pad_ref, sh_x, HALO - (K - 1) + k, tt))
            return acc

        ds = tconv(0, DC, wa_ref[...], KA, pad_a, dwa_ref)
        du0 = tconv(DC, 2 * DC, wb_ref[...], KB, pad_b, dwb_ref, sh_b, sh_d)
        dqkv = tconv(2 * DC, CW, wc_ref[...], KC, pad_c, dwc_ref)
        a_c, a_v, b_a, b_g = pc[:, O_AC:O_AV], pc[:, O_AV:O_BA], pc[:, O_BA:O_BG], pc[:, O_BG:O_Q]
        sg = jax.nn.sigmoid(b_g)
        dp_ref[...] = jnp.concatenate([ds * a_v, ds * a_c, du0 * sg, du0 * b_a * sg * (1.0 - sg), dqkv], axis=1)

    cur, prev, _ = _mix_specs(T, tt)
    return _pcall(
        body, grid=(T // tt,), name=name, cargo=cargo, args=(proj, proj, dconv, dconv, wa, wb, wc),
        in_specs=[cur, prev, pl.BlockSpec((tt, CW), lambda i: (i, 0)),
                  pl.BlockSpec((HALO, CW), lambda i: (jnp.minimum((i + 1) * (tt // HALO), nblk - 1), 0)),
                  _full((8, DC)), _full((32, DC)), _full((8, 3 * DDN))],
        out_specs=[pl.BlockSpec((tt, O_Z - O_AC), lambda i: (i, 0)), _full((8, DC)), _full((32, DC)), _full((8, 3 * DDN))],
        out_shape=[_sds((T, O_Z - O_AC)), _sds((8, DC)), _sds((32, DC)), _sds((8, 3 * DDN))],
        scratch_shapes=[pltpu.VMEM((HALO + tt, DC), F32), pltpu.VMEM((HALO + tt, DC), F32), pltpu.VMEM((HALO + tt, 3 * DDN), F32),
                        pltpu.VMEM((tt + HALO, CW), F32), _shift_scratch(tt, DC), _shift_scratch(tt, DC)])


def _head(v, h):
    return v[:, h * HD:(h + 1) * HD]


def _to_batch(v, cb):
    return jnp.stack([v[c * CH:(c + 1) * CH, h * HD:(h + 1) * HD] for c in range(cb) for h in range(NH)])


def _from_batch(b, cb):
    return jnp.concatenate([jnp.concatenate([b[c * NH + h] for h in range(NH)], axis=1) for c in range(cb)], axis=0)


def _lanes_to_batch(v, cb, lane0):
    return jnp.stack([v[c * CH:(c + 1) * CH, lane0 + h:lane0 + h + 1] for c in range(cb) for h in range(NH)])


def _lane_onehot(h):
    return (lax.broadcasted_iota(jnp.int32, (1, HD), 1) == h).astype(F32)


def _batch_to_lanes(b, cb, lane0):
    return jnp.concatenate([sum(b[c * NH + h] * _lane_onehot(lane0 + h) for h in range(NH)) for c in range(cb)], axis=0)


def dn_intra_fwd(q, k, v, gb, grow3, *, name):
    T = q.shape[0]
    N = T // CH
    cb = 4 if N % 4 == 0 else 1

    def body(q_ref, k_ref, v_ref, gb_ref, gr_ref, u_ref, w_ref, qk_ref, p_ref):
        gbv = gb_ref[...]
        grow = jnp.stack([gr_ref[c, h:h + 1, :] for c in range(cb) for h in range(NH)])
        u, w, qk, p = _dn_intra(_to_batch(q_ref[...], cb), _to_batch(k_ref[...], cb), _to_batch(v_ref[...], cb),
                                _lanes_to_batch(gbv, cb, NH), _lanes_to_batch(gbv, cb, 0), grow)
        u_ref[...] = _from_batch(u, cb)
        w_ref[...] = _from_batch(w, cb)
        qk_ref[...] = qk.reshape(cb, NH, CH, CH)
        p_ref[...] = p.reshape(cb, NH, CH, CH)

    t512 = pl.BlockSpec((cb * CH, DDN), lambda i: (i, 0))
    sq = pl.BlockSpec((cb, NH, CH, CH), lambda i: (i, 0, 0, 0))
    return pl.pallas_call(
        body, grid=(N // cb,), name=name,
        in_specs=[t512, t512, t512, pl.BlockSpec((cb * CH, HD), lambda i: (i, 0)), pl.BlockSpec((cb, 8, CH), lambda i: (i, 0, 0))],
        out_specs=[t512, t512, sq, sq],
        out_shape=[_sds((T, DDN)), _sds((T, DDN)), _sds((N, NH, CH, CH)), _sds((N, NH, CH, CH))],
        compiler_params=_cp(1))(q, k, v, gb, grow3)


def dn_inter_fwd(q, k, u, w, gb, qk, *, name):
    T = q.shape[0]
    N = T // CH
    cb = 4 if N % 4 == 0 else 1

    def body(q_ref, k_ref, u_ref, w_ref, gb_ref, qk_ref, o_ref, s_ref, state):
        @pl.when(pl.program_id(0) == 0)
        def _():
            state[...] = jnp.zeros_like(state)
        for c in range(cb):
            rows = pl.ds(c * CH, CH)
            s = state[...]
            s_ref[c] = s
            s_new, o = _dn_inter(s, _to_batch(q_ref[rows, :], 1), _to_batch(k_ref[rows, :], 1), _to_batch(u_ref[rows, :], 1),
                                 _to_batch(w_ref[rows, :], 1), _lanes_to_batch(gb_ref[rows, :], 1, 0), qk_ref[c])
            state[...] = s_new
            o_ref[rows, :] = _from_batch(o, 1)

    t512 = pl.BlockSpec((cb * CH, DDN), lambda i: (i, 0))
    return pl.pallas_call(
        body, grid=(N // cb,), name=name,
        in_specs=[t512, t512, t512, t512, pl.BlockSpec((cb * CH, HD), lambda i: (i, 0)),
                  pl.BlockSpec((cb, NH, CH, CH), lambda i: (i, 0, 0, 0))],
        out_specs=[t512, pl.BlockSpec((cb, NH, HD, HD), lambda i: (i, 0, 0, 0))],
        out_shape=[_sds((T, DDN)), _sds((N, NH, HD, HD))],
        scratch_shapes=[pltpu.VMEM((NH, HD, HD), F32)], compiler_params=_cp(1))(q, k, u, w, gb, qk)


def dn_inter_bwd(do, q, k, u, w, gb, qk, s_all, *, name, cargo=None):
    T = q.shape[0]
    N = T // CH
    cb = 4 if N % 4 == 0 else 1
    G = N // cb

    def body(do_ref, q_ref, k_ref, u_ref, w_ref, gb_ref, qk_ref, s_ref, dq_ref, dk_ref, du_ref, dw_ref, dg_ref, dqk_ref, dstate):
        @pl.when(pl.program_id(0) == 0)
        def _():
            dstate[...] = jnp.zeros_like(dstate)
        for c in reversed(range(cb)):
            rows = pl.ds(c * CH, CH)
            _, vjp = jax.vjp(_dn_inter, s_ref[c], _to_batch(q_ref[rows, :], 1), _to_batch(k_ref[rows, :], 1),
                             _to_batch(u_ref[rows, :], 1), _to_batch(w_ref[rows, :], 1), _lanes_to_batch(gb_ref[rows, :], 1, 0),
                             qk_ref[c])
            ds, dq, dk, du, dw, dgc, dqk = vjp((dstate[...], _to_batch(do_ref[rows, :], 1)))
            dstate[...] = ds
            dq_ref[rows, :] = _from_batch(dq, 1)
            dk_ref[rows, :] = _from_batch(dk, 1)
            du_ref[rows, :] = _from_batch(du, 1)
            dw_ref[rows, :] = _from_batch(dw, 1)
            dg_ref[rows, :] = _batch_to_lanes(dgc, 1, 0)
            dqk_ref[c] = dqk

    t512 = pl.BlockSpec((cb * CH, DDN), lambda i: (G - 1 - i, 0))
    t128 = pl.BlockSpec((cb * CH, HD), lambda i: (G - 1 - i, 0))
    qkb = pl.BlockSpec((cb, NH, CH, CH), lambda i: (G - 1 - i, 0, 0, 0))
    return _pcall(
        body, grid=(G,), name=name, cargo=cargo, args=(do, q, k, u, w, gb, qk, s_all),
        in_specs=[t512, t512, t512, t512, t512, t128, qkb, pl.BlockSpec((cb, NH, HD, HD), lambda i: (G - 1 - i, 0, 0, 0))],
        out_specs=[t512, t512, t512, t512, t128, qkb],
        out_shape=[_sds((T, DDN))] * 4 + [_sds((T, HD)), _sds((N, NH, CH, CH))],
        scratch_shapes=[pltpu.VMEM((NH, HD, HD), F32)])


def dn_intra_bwd(du, dw, dqk, dq_in, dk_in, dg_in, q, k, v, gb, grow3, p_all, *, name, cargo=None):
    T = q.shape[0]
    N = T // CH
    cb = 4 if N % 4 == 0 else 1

    def body(du_ref, dw_ref, dqk_ref, dqi_ref, dki_ref, dgi_ref, q_ref, k_ref, v_ref, gb_ref, gr_ref, p_ref,
             dq_ref, dk_ref, dv_ref, dgb_ref, dgr_ref):
        B = cb * NH
        gbv = gb_ref[...]
        grow = jnp.stack([gr_ref[c, h:h + 1, :] for c in range(cb) for h in range(NH)])
        _, vjp = jax.vjp(functools.partial(_dn_intra, p_known=p_ref[...].reshape(B, CH, CH)),
                         _to_batch(q_ref[...], cb), _to_batch(k_ref[...], cb), _to_batch(v_ref[...], cb),
                         _lanes_to_batch(gbv, cb, NH), _lanes_to_batch(gbv, cb, 0), grow)
        dq, dk, dv, dbeta, dgc, dgr = vjp((_to_batch(du_ref[...], cb), _to_batch(dw_ref[...], cb), dqk_ref[...].reshape(B, CH, CH),
                                           jnp.zeros((B, CH, CH), F32)))
        dq_ref[...] = dqi_ref[...] + _from_batch(dq, cb)
        dk_ref[...] = dki_ref[...] + _from_batch(dk, cb)
        dv_ref[...] = _from_batch(dv, cb)
        dgb_ref[...] = dgi_ref[...] + _batch_to_lanes(dgc, cb, 0) + _batch_to_lanes(dbeta, cb, NH)
        for c in range(cb):
            dgr_ref[c] = jnp.concatenate([dgr[c * NH + h] for h in range(NH)] + [jnp.zeros((8 - NH, CH), F32)], axis=0)

    t512 = pl.BlockSpec((cb * CH, DDN), lambda i: (i, 0))
    t128 = pl.BlockSpec((cb * CH, HD), lambda i: (i, 0))
    qkb = pl.BlockSpec((cb, NH, CH, CH), lambda i: (i, 0, 0, 0))
    grb = pl.BlockSpec((cb, 8, CH), lambda i: (i, 0, 0))
    return _pcall(
        body, grid=(N // cb,), name=name, cargo=cargo, args=(du, dw, dqk, dq_in, dk_in, dg_in, q, k, v, gb, grow3, p_all),
        in_specs=[t512, t512, qkb, t512, t512, t128, t512, t512, t512, t128, grb, qkb],
        out_specs=[t512, t512, t512, t128, grb],
        out_shape=[_sds((T, DDN))] * 3 + [_sds((T, HD)), _sds((N, 8, CH))])


def _z_specs(tt):
    return [pl.BlockSpec((tt, DC), lambda i: (i, O_Z // DC)), pl.BlockSpec((tt, DC), lambda i: (i, O_Z // DC + 1))]


def mixout_fwd(x, yab, o, proj, gdn, g1, w_out, *, name, wl=0, cargo=None):
    T = x.shape[0]
    tt = _tile(T, 512)

    def body(x_ref, yab_ref, o_ref, z0_ref, z1_ref, gdn_ref, g1_ref, w_ref, ycat_ref, mix_ref, xo_ref):
        ov = o_ref[...]
        z = jnp.concatenate([z0_ref[...], z1_ref[...]], axis=1)
        yc = [_yc(_head(ov, h), _head(z, h), gdn_ref[...]) for h in range(NH)]
        ycat = jnp.concatenate([yab_ref[...]] + yc, axis=1).astype(BF16)
        ycat_ref[...] = ycat
        mix = _dot(ycat, w_ref[...])
        mix_ref[...] = mix
        xo_ref[...] = x_ref[...] + g1_ref[...] * mix

    tile = pl.BlockSpec((tt, D), lambda i: (i, 0))
    t512 = pl.BlockSpec((tt, DDN), lambda i: (i, 0))
    return _pcall(
        body, grid=(T // tt,), name=name, cargo=cargo, args=(x, yab, o, proj, proj, gdn, g1, _w3(w_out)),
        in_specs=[tile, t512, t512] + _z_specs(tt) + [_full((1, HD)), _full((1, D)), _wspec(wl, (D, D), lambda i: (0, 0))],
        out_specs=[tile, tile, tile], out_shape=[_sds((T, D), BF16), _sds((T, D)), _sds((T, D))])


def mixout_bwd(dx, mix, o, proj, gdn, g1, w_out, *, name, wl=0, cargo=None):
    T = dx.shape[0]
    tt = _tile(T, 256)

    def body(dx_ref, mix_ref, o_ref, z0_ref, z1_ref, gdn_ref, g1_ref, w_ref, dmix_ref, dyab_ref, do_ref, dz_ref, dg1_ref, dgdn_ref):
        i = pl.program_id(0)
        dxv = dx_ref[...]
        dmix = (dxv * g1_ref[...]).astype(BF16)
        dmix_ref[...] = dmix
        dycat = _dot_nt(dmix, w_ref[...])
        dyab_ref[...] = dycat[:, :2 * DC]
        ov = o_ref[...]
        z = jnp.concatenate([z0_ref[...], z1_ref[...]], axis=1)
        dos, dzs = [], []
        dgdn = jnp.zeros((1, HD), F32)
        for h in range(NH):
            _, vjp = jax.vjp(_yc, _head(ov, h), _head(z, h), gdn_ref[...])
            do, dz, dg = vjp(dycat[:, 2 * DC + h * HD:2 * DC + (h + 1) * HD])
            dos.append(do)
            dzs.append(dz)
            dgdn = dgdn + dg
        do_ref[...] = jnp.concatenate(dos, axis=1)
        dz_ref[...] = jnp.concatenate(dzs, axis=1)

        @pl.when(i == 0)
        def _():
            dg1_ref[...] = jnp.zeros_like(dg1_ref)
            dgdn_ref[...] = jnp.zeros_like(dgdn_ref)
        dg1_ref[...] += _colsum(dxv * mix_ref[...])
        dgdn_ref[...] += dgdn

    tile = pl.BlockSpec((tt, D), lambda i: (i, 0))
    t512 = pl.BlockSpec((tt, DDN), lambda i: (i, 0))
    return _pcall(
        body, grid=(T // tt,), name=name, cargo=cargo, args=(dx, mix, o, proj, proj, gdn, g1, _w3(w_out)),
        in_specs=[tile, tile, t512] + _z_specs(tt) + [_full((1, HD)), _full((1, D)), _wspec(wl, (D, D), lambda i: (0, 0))],
        out_specs=[tile, t512, t512, t512, _full((1, D)), _full((1, HD))],
        out_shape=[_sds((T, D), BF16), _sds((T, DDN)), _sds((T, DDN)), _sds((T, DDN)), _sds((1, D)), _sds((1, HD))])


FK = DFF // 2


def ffnout_fwd(x, gu, g2, w, *, name, wl=0, cargo=None):
    T = x.shape[0]
    tt = _tile(T, 512)

    def body(x_ref, gate_ref, up_ref, g2_ref, w_ref, act_ref, f_ref, xo_ref):
        kk = pl.program_id(1)
        act = (_silu(gate_ref[...]) * up_ref[...]).astype(BF16)
        act_ref[...] = act
        part = _dot(act, w_ref[...])

        @pl.when(kk == 0)
        def _():
            f_ref[...] = part

        @pl.when(kk == 1)
        def _():
            f = f_ref[...] + part
            f_ref[...] = f
            xo_ref[...] = x_ref[...] + g2_ref[...] * f

    tile = pl.BlockSpec((tt, D), lambda i, kk: (i, 0))
    return _pcall(
        body, grid=(T // tt, 2), name=name, cargo=cargo, args=(x, gu, gu, g2, _w3(w)),
        in_specs=[tile, pl.BlockSpec((tt, FK), lambda i, kk: (i, kk)), pl.BlockSpec((tt, FK), lambda i, kk: (i, 2 + kk)),
                  pl.BlockSpec((1, D), lambda i, kk: (0, 0)), _wspec(wl, (FK, D), lambda i, kk: (kk, 0))],
        out_specs=[pl.BlockSpec((tt, FK), lambda i, kk: (i, kk)), tile, tile],
        out_shape=[_sds((T, DFF), BF16), _sds((T, D)), _sds((T, D))])


def ffnout_bwd(dx, f, gu, g2, w, *, name, wl=0, cargo=None):
    T = dx.shape[0]
    tt = _tile(T, 512)

    def body(dx_ref, f_ref, gate_ref, up_ref, g2_ref, w_ref, df_ref, dgate_ref, dup_ref, dg2_ref):
        i, kk = pl.program_id(0), pl.program_id(1)
        dxv = dx_ref[...]
        df = (dxv * g2_ref[...]).astype(BF16)
        dact = _dot_nt(df, w_ref[...])
        gate, up = gate_ref[...], up_ref[...]
        sg = jax.nn.sigmoid(gate)
        dgate_ref[...] = (dact * up * (sg * (1.0 + gate * (1.0 - sg)))).astype(BF16)
        dup_ref[...] = (dact * (gate * sg)).astype(BF16)

        @pl.when(kk == 0)
        def _():
            df_ref[...] = df

        @pl.when((i == 0) & (kk == 0))
        def _():
            dg2_ref[...] = jnp.zeros_like(dg2_ref)

        @pl.when(kk == 0)
        def _():
            dg2_ref[...] += _colsum(dxv * f_ref[...])

    tile = pl.BlockSpec((tt, D), lambda i, kk: (i, 0))
    return _pcall(
        body, grid=(T // tt, 2), name=name, cargo=cargo, args=(dx, f, gu, gu, g2, _w3(w)),
        in_specs=[tile, tile, pl.BlockSpec((tt, FK), lambda i, kk: (i, kk)), pl.BlockSpec((tt, FK), lambda i, kk: (i, 2 + kk)),
                  pl.BlockSpec((1, D), lambda i, kk: (0, 0)), _wspec(wl, (FK, D), lambda i, kk: (kk, 0))],
        out_specs=[tile, pl.BlockSpec((tt, FK), lambda i, kk: (i, kk)), pl.BlockSpec((tt, FK), lambda i, kk: (i, kk)),
                   pl.BlockSpec((1, D), lambda i, kk: (0, 0))],
        out_shape=[_sds((T, D), BF16), _sds((T, DFF), BF16), _sds((T, DFF), BF16), _sds((1, D))])


def loss_head(x, target, gf, *, name):
    T = x.shape[0]
    tt = _tile(T, 256)

    def body(x_ref, t_ref, g_ref, loss_ref, dx_ref, dg_ref):
        i = pl.program_id(0)
        tv = t_ref[...]
        y, vjp = jax.vjp(_rms, x_ref[...], g_ref[...])
        err = y - tv
        dx, dg = vjp(err * (1.0 / D))
        dx_ref[...] = dx

        @pl.when(i == 0)
        def _():
            loss_ref[...] = jnp.zeros_like(loss_ref)
            dg_ref[...] = jnp.zeros_like(dg_ref)
        loss_ref[...] += 0.5 * jnp.sum(jnp.mean(err * err, axis=-1, keepdims=True), axis=0, keepdims=True)
        dg_ref[...] += dg

    tile = pl.BlockSpec((tt, D), lambda i: (i, 0))
    return pl.pallas_call(
        body, grid=(T // tt,), name=name, in_specs=[tile, tile, _full((1, D))],
        out_specs=[_full((1, HD)), tile, _full((1, D))], out_shape=[_sds((1, HD)), _sds((T, D)), _sds((1, D))],
        compiler_params=_cp(1))(x, target, gf)


def _pad_rows(a, rows):
    return jnp.pad(a, ((0, rows - a.shape[0]), (0, 0)))


def _row128(v):
    return jnp.pad(v, (0, HD - v.shape[0]))[None, :]


def _grow3(grow):
    return grow.reshape(8, -1, CH).transpose(1, 0, 2)


def _grow2(grow3):
    return grow3.transpose(1, 0, 2).reshape(8, -1)


def layer_params(l, mod, p):
    m = mod[l].reshape(6, 1, D)
    return dict(
        sh1=m[0], sc1=m[1], g1=m[2], sh2=m[3], sc2=m[4], g2=m[5],
        gn1=p["norm_mix_g"][l][None, :], gn2=p["norm_ffn_g"][l][None, :],
        wa=_pad_rows(p["conv_a_w"][l], 8), wb=_pad_rows(p["conf_dw_w"][l], 32), wc=_pad_rows(p["dn_conv_w"][l], 8),
        bb=p["conf_dw_b"][l][None, :], ln_g=p["conf_ln_g"][l][None, :], ln_b=p["conf_ln_b"][l][None, :],
        alog=_row128(p["dn_a_log"][l]), dt=_row128(p["dn_dt_bias"][l]), gdn=p["dn_norm_g"][l][None, :])


def layer_fwd(l, x, lp, w_in, w_out, w_ffn_in, w_ffn_out, wl=0, cargos=None):
    cg = (cargos or {}).get
    s = {"x0": x}
    s["h1"], s["proj"] = nm_fwd(x, lp["gn1"], lp["sc1"], lp["sh1"], w_in, tn=NP // 3, name=f"proj_fwd_{l}", wl=wl,
                                cargo=cg("w_in"))
    yab, s["q"], s["k"], s["v"], s["gb"], grow = mix_fwd(
        s["proj"], lp["wa"], lp["wb"], lp["bb"], lp["ln_g"], lp["ln_b"], lp["wc"], lp["alog"], lp["dt"], name=f"mix_fwd_{l}")
    s["grow3"] = _grow3(grow)
    s["u"], s["w"], s["qk"], s["p"] = dn_intra_fwd(s["q"], s["k"], s["v"], s["gb"], s["grow3"], name=f"dn_intra_fwd_{l}")
    s["o"], s["s_all"] = dn_inter_fwd(s["q"], s["k"], s["u"], s["w"], s["gb"], s["qk"], name=f"dn_inter_fwd_{l}")
    s["ycat"], s["mix"], s["x1"] = mixout_fwd(x, yab, s["o"], s["proj"], lp["gdn"], lp["g1"], w_out, name=f"mixout_fwd_{l}",
                                              wl=wl, cargo=cg("w_out"))
    s["h2"], s["gu"] = nm_fwd(s["x1"], lp["gn2"], lp["sc2"], lp["sh2"], w_ffn_in, tn=2 * DFF // 4, name=f"ffnin_fwd_{l}", wl=wl,
                              cargo=cg("w_ffn_in"))
    s["act"], s["f"], x2 = ffnout_fwd(s["x1"], s["gu"], lp["g2"], w_ffn_out, name=f"ffnout_fwd_{l}", wl=wl, cargo=cg("w_ffn_out"))
    return x2, s


def layer_bwd(l, dx2, s, lp, w_in, w_out, w_ffn_in, w_ffn_out, wl=0, carry=None):
    g = {}
    carry = carry or (lambda stage: None)
    df, dgate, dup, dg2 = ffnout_bwd(dx2, s["f"], s["gu"], lp["g2"], w_ffn_out, name=f"ffnout_bwd_{l}", wl=wl, cargo=carry("swap"))
    g["w_ffn_out"] = mm_tn_split(s["act"], df, by_cols=False, name=f"ffnout_dw_{l}")
    dx1, g["norm_ffn_g"], dsc2, dsh2, dgu = nm_bwd([dgate, dup], w_ffn_in, s["x1"], lp["gn2"], lp["sc2"], lp["sh2"], dx2,
                                                   name=f"ffnin_bwd_{l}", wl=wl, cargo=carry("exchange_c"))
    g["w_ffn_in"] = mm_tn_split(s["h2"], dgu, by_cols=True, name=f"ffnin_dw_{l}")
    dmix, dyab, do, dz, dg1, g["dn_norm_g"] = mixout_bwd(dx1, s["mix"], s["o"], s["proj"], lp["gdn"], lp["g1"], w_out,
                                                         name=f"mixout_bwd_{l}", wl=wl)
    g["w_out"] = mm_tn_split(s["ycat"], dmix, by_cols=False, name=f"mixout_dw_{l}")
    dq_i, dk_i, du, dw, dg_i, dqk = dn_inter_bwd(do, s["q"], s["k"], s["u"], s["w"], s["gb"], s["qk"], s["s_all"],
                                                 name=f"dn_inter_bwd_{l}")
    dq, dk, dv, dgb, dgrow3 = dn_intra_bwd(du, dw, dqk, dq_i, dk_i, dg_i, s["q"], s["k"], s["v"], s["gb"], s["grow3"],
                                           s["p"], name=f"dn_intra_bwd_{l}", cargo=carry("exchange_b"))
    dab, dconv, dblk, g["conf_dw_b"], g["conf_ln_g"], g["conf_ln_b"], dal, ddt = mix_bwd_point(
        s["proj"], dyab, dq, dk, dv, dgb, _grow2(dgrow3), lp["wa"], lp["wb"], lp["bb"], lp["ln_g"], lp["ln_b"], lp["wc"],
        lp["alog"], lp["dt"], name=f"mix_bwd_point_{l}", cargo=carry("exchange_a"))
    dpb, dwa, dwb, dwc = mix_bwd_conv(s["proj"], dconv, lp["wa"], lp["wb"], lp["wc"], name=f"mix_bwd_conv_{l}",
                                      cargo=carry("join"))
    dx0, g["norm_mix_g"], dsc1, dsh1, dproj = nm_bwd([dab, dpb, dz, dblk], w_in, s["x0"], lp["gn1"], lp["sc1"], lp["sh1"], dx1,
                                                     name=f"proj_bwd_{l}", wl=wl)
    g_in = mm_tn(s["h1"], dproj, tn=NP // 3, name=f"proj_dw_{l}")[:, :IN_COLS]
    g["w_in"] = g_in.reshape(2, D // 2, 4, IN_COLS // 4).transpose(0, 2, 1, 3)
    g["conv_a_w"], g["conf_dw_w"], g["dn_conv_w"] = dwa[:KA], dwb[:KB], dwc[:KC]
    g["dn_a_log"], g["dn_dt_bias"] = dal[0, :NH], ddt[0, :NH]
    g["mod"] = jnp.concatenate([dsh1, dsc1, dg1, dsh2, dsc2, dg2], axis=1)
    return dx0, g


EW_BLOCK_BYTES = 1 << 20


def _row_tile(R, C, mult=8):
    best = None
    for rt in range(mult, R + 1, mult):
        if R % rt == 0 and rt * C * 4 <= EW_BLOCK_BYTES:
            best = rt
    return best if best is not None else R


def add_half_bf16(g2, recv, core, *, name):
    _, B, R, C = g2.shape
    rt = _row_tile(R, C, 16)

    def body(core_ref, a_ref, b_ref, o_ref):
        o_ref[...] = (a_ref[...] + b_ref[...]).astype(BF16)

    spec = pl.BlockSpec((1, rt, C), lambda b, r, core_ref: (b, r, 0))
    return pl.pallas_call(
        body, name=name, out_shape=_sds((B, R, C), BF16), compiler_params=_cp(2),
        grid_spec=pltpu.PrefetchScalarGridSpec(
            num_scalar_prefetch=1, grid=(B, R // rt),
            in_specs=[pl.BlockSpec((None, 1, rt, C), lambda b, r, core_ref: (core_ref[0], b, r, 0)), spec],
            out_specs=spec))(core, g2, recv)


def adamw_halves(w, g_mine, g_theirs, core, m, v, *, name):
    L, R, C = w.shape
    rh = R // 2
    rt = _row_tile(rh, C)
    nr = rh // rt

    def body(core_ref, w_ref, gm_ref, gt_ref, m_ref, v_ref, go_ref, d_ref, mo_ref, vo_ref):
        g = jnp.where(pl.program_id(1) == core_ref[0], gm_ref[...], gt_ref[...])
        go_ref[...] = g
        d_ref[...], mo_ref[...], vo_ref[...] = _adamw_math(w_ref[...], g, m_ref[...], v_ref[...])

    spec = pl.BlockSpec((1, rt, C), lambda l, h, r, core_ref: (l, h * nr + r, 0))
    half = pl.BlockSpec((1, rt, C), lambda l, h, r, core_ref: (l, r, 0))
    return pl.pallas_call(
        body, name=name, out_shape=[_sds((L, R, C))] * 4, compiler_params=_cp(3),
        grid_spec=pltpu.PrefetchScalarGridSpec(num_scalar_prefetch=1, grid=(L, 2, nr), in_specs=[spec, half, half, spec, spec],
                                               out_specs=[spec] * 4))(core, w, g_mine, g_theirs, m, v)


def ew_call(fn, ins, n_out, *, name):
    B, R, C = ins[0].shape
    rt = _row_tile(R, C)
    n = len(ins)

    def body(*refs):
        outs = fn(*[r[...] for r in refs[:n]])
        for r, o in zip(refs[n:], outs):
            r[...] = o

    spec = pl.BlockSpec((1, rt, C), lambda b, r: (b, r, 0))
    return pl.pallas_call(body, grid=(B, R // rt), name=name, in_specs=[spec] * n, out_specs=[spec] * n_out,
                          out_shape=[_sds((B, R, C))] * n_out, compiler_params=_cp(2))(*ins)


def _adamw_math(w, g, m, v):
    m = ADAM_B1 * m + (1.0 - ADAM_B1) * g
    v = ADAM_B2 * v + (1.0 - ADAM_B2) * jnp.square(g)
    m_hat = m / (1.0 - ADAM_B1 ** ADAM_STEP)
    v_hat = v / (1.0 - ADAM_B2 ** ADAM_STEP)
    return -ADAM_LR * (m_hat / (jnp.sqrt(v_hat) + ADAM_EPS) + ADAM_WD * w), m, v


def adamw(w, g, m, v, *, name):
    shape = w.shape
    r3 = lambda a: a.reshape((-1,) + shape[-2:])
    return [o.reshape(shape) for o in ew_call(_adamw_math, [r3(w), r3(g), r3(m), r3(v)], 3, name=name)]


def sum_slots(a, *, name):
    S, B, R, C = a.shape
    rt = _row_tile(R, C, 16)

    def body(*refs):
        acc = refs[0][0, 0].astype(F32)
        for r in refs[1:S]:
            acc = acc + r[0, 0].astype(F32)
        refs[S][0] = acc

    def spec(s):
        return pl.BlockSpec((1, 1, rt, C), lambda b, r: (s, b, r, 0))
    return pl.pallas_call(body, grid=(B, R // rt), name=name, in_specs=[spec(s) for s in range(S)],
                          out_specs=pl.BlockSpec((1, rt, C), lambda b, r: (b, r, 0)), out_shape=_sds((B, R, C)),
                          compiler_params=_cp(2))(*([a] * S))


ADA_SH = 6 * D // 4
ADA_TN = 512


def mod_fwd(c_all, w_ada, b_my, *, name):
    L = w_ada.shape[0]

    def body(c_ref, w_ref, b_ref, o_ref):
        o_ref[0] = _dot(_silu(c_ref[...]).astype(BF16), w_ref[0].astype(BF16)) + b_ref[0]

    return pl.pallas_call(
        body, grid=(L, ADA_SH // ADA_TN), name=name,
        in_specs=[pl.BlockSpec((8, D), lambda l, j: (0, 0)), pl.BlockSpec((1, D, ADA_TN), lambda l, j: (l, 0, j)),
                  pl.BlockSpec((1, 1, ADA_TN), lambda l, j: (l, 0, j))],
        out_specs=pl.BlockSpec((1, 8, ADA_TN), lambda l, j: (l, 0, j)), out_shape=_sds((L, 8, ADA_SH)),
        compiler_params=_cp(2))(c_all, w_ada, b_my)


def wada_grad(c_all, dmod, *, name):
    L = dmod.shape[0]

    def body(c_ref, d_ref, o_ref):
        o_ref[0] = _dot_tn(_silu(c_ref[...]), d_ref[0], HI)

    return pl.pallas_call(
        body, grid=(L, ADA_SH // ADA_TN), name=name,
        in_specs=[pl.BlockSpec((8, D), lambda l, j: (0, 0)), pl.BlockSpec((1, 8, ADA_TN), lambda l, j: (l, 0, j))],
        out_specs=pl.BlockSpec((1, D, ADA_TN), lambda l, j: (l, 0, j)), out_shape=_sds((L, D, ADA_SH)),
        compiler_params=_cp(2))(c_all, dmod)


def _place():
    return lax.axis_index("x"), lax.axis_index("y"), lax.axis_index("c")


def _other_chips(x, y):
    return [(1 - x, y), (x, 1 - y), (1 - x, 1 - y)]


def allgather8(blocks, *, space, name):
    n = len(blocks)

    def body(*refs):
        ins, outs = refs[:n], refs[n:2 * n]
        send_sems, recv_sems, local_sems = refs[2 * n:]
        x, y, c = _place()
        me, sibling = (x, y, c), (x, y, 1 - c)
        chips = _other_chips(x, y)

        def slot(p):
            return 4 * p[0] + 2 * p[1] + p[2]

        def copy(a, k, block, to, src=None):
            dst = outs[a].at[slot(block)]
            return pltpu.make_async_remote_copy(src_ref=dst if src is None else src, dst_ref=dst, send_sem=send_sems.at[a, k],
                                                recv_sem=recv_sems.at[a, k], device_id=to, device_id_type=MESH)

        mine = [pltpu.make_async_copy(ins[a], outs[a].at[slot(me)], local_sems.at[a]) for a in range(n)]
        for cp in mine:
            cp.start()
        first = []
        for a in range(n):
            first.append(copy(a, 0, me, sibling, src=ins[a]))
            first += [copy(a, 1 + j, me, (*chip, c), src=ins[a]) for j, chip in enumerate(chips)]
        for cp in first:
            cp.start()
        passed = []
        for j, chip in enumerate(chips):
            for a in range(n):
                copy(a, 1 + j, (*chip, c), me).wait_recv()
                cp = copy(a, 4 + j, (*chip, c), sibling)
                cp.start()
                passed.append(cp)
        for a in range(n):
            copy(a, 0, sibling, me).wait_recv()
            for j, chip in enumerate(chips):
                copy(a, 4 + j, (*chip, 1 - c), me).wait_recv()
        for cp in first + passed:
            cp.wait_send()
        for cp in mine:
            cp.wait()

    spec = pl.BlockSpec(memory_space=space)
    return pl.pallas_call(
        body, name=name, in_specs=[spec] * n, out_specs=[spec] * n,
        out_shape=[_sds((8,) + b.shape, b.dtype) for b in blocks],
        scratch_shapes=[pltpu.SemaphoreType.DMA((n, 7)), pltpu.SemaphoreType.DMA((n, 7)), pltpu.SemaphoreType.DMA((n,))],
        compiler_params=pltpu.CompilerParams(vmem_limit_bytes=VMEM_LIMIT))(*blocks)


def sibling_swap(blocks, *, name, slotted=True):
    n = len(blocks)

    def body(*refs):
        ins, outs = refs[:n], refs[n:2 * n]
        send_sems, recv_sems = refs[2 * n:]
        x, y, c = _place()
        cps = [pltpu.make_async_remote_copy(src_ref=ins[a].at[1 - c] if slotted else ins[a], dst_ref=outs[a],
                                            send_sem=send_sems.at[a], recv_sem=recv_sems.at[a], device_id=(x, y, 1 - c),
                                            device_id_type=MESH)
               for a in range(n)]
        for cp in cps:
            cp.start()
        for cp in cps:
            cp.wait()

    spec = pl.BlockSpec(memory_space=pl.ANY)
    return pl.pallas_call(
        body, name=name, in_specs=[spec] * n, out_specs=[spec] * n,
        out_shape=[_sds(b.shape[1:] if slotted else b.shape, b.dtype) for b in blocks],
        scratch_shapes=[pltpu.SemaphoreType.DMA((n,)), pltpu.SemaphoreType.DMA((n,))])(*blocks)


def chip_exchange(blocks, *, name):
    n = len(blocks)

    def body(*refs):
        ins, outs = refs[:n], refs[n:2 * n]
        send_sems, recv_sems, local_sems = refs[2 * n:]
        x, y, c = _place()
        me = 2 * x + y
        chips = _other_chips(x, y)
        mine = [pltpu.make_async_copy(ins[a].at[me], outs[a].at[me], local_sems.at[a]) for a in range(n)]
        for cp in mine:
            cp.start()
        cps = []
        for a in range(n):
            for j, chip in enumerate(chips):
                cps.append(pltpu.make_async_remote_copy(
                    src_ref=ins[a].at[2 * chip[0] + chip[1]], dst_ref=outs[a].at[me], send_sem=send_sems.at[a, j],
                    recv_sem=recv_sems.at[a, j], device_id=(*chip, c), device_id_type=MESH))
        for cp in cps:
            cp.start()
        for a in range(n):
            for j, chip in enumerate(chips):
                s = 2 * chip[0] + chip[1]
                pltpu.make_async_remote_copy(src_ref=ins[a].at[s], dst_ref=outs[a].at[s], send_sem=send_sems.at[a, j],
                                             recv_sem=recv_sems.at[a, j], device_id=(*chip, c), device_id_type=MESH).wait_recv()
        for cp in cps:
            cp.wait_send()
        for cp in mine:
            cp.wait()

    spec = pl.BlockSpec(memory_space=pl.ANY)
    return pl.pallas_call(
        body, name=name, in_specs=[spec] * n, out_specs=[spec] * n, out_shape=[_sds(b.shape, b.dtype) for b in blocks],
        scratch_shapes=[pltpu.SemaphoreType.DMA((n, 3)), pltpu.SemaphoreType.DMA((n, 3)), pltpu.SemaphoreType.DMA((n,))])(*blocks)


def halves_join(halves, *, name):
    n = len(halves)

    def body(*refs):
        ins, outs = refs[:n], refs[n:2 * n]
        send_sems, recv_sems, local_sems = refs[2 * n:]
        x, y, c = _place()
        cps, mine = [], []
        for a in range(n):
            mine.append(pltpu.make_async_copy(ins[a], outs[a].at[c], local_sems.at[a]))
            cps.append(pltpu.make_async_remote_copy(src_ref=ins[a], dst_ref=outs[a].at[c], send_sem=send_sems.at[a],
                                                    recv_sem=recv_sems.at[a], device_id=(x, y, 1 - c), device_id_type=MESH))
        for cp in mine + cps:
            cp.start()
        for a in range(n):
            pltpu.make_async_remote_copy(src_ref=ins[a], dst_ref=outs[a].at[1 - c], send_sem=send_sems.at[a], recv_sem=recv_sems.at[a],
                                         device_id=(x, y, 1 - c), device_id_type=MESH).wait_recv()
        for cp in cps:
            cp.wait_send()
        for cp in mine:
            cp.wait()

    spec = pl.BlockSpec(memory_space=pl.ANY)
    return pl.pallas_call(
        body, name=name, in_specs=[spec] * n, out_specs=[spec] * n,
        out_shape=[_sds((2,) + h.shape, h.dtype) for h in halves],
        scratch_shapes=[pltpu.SemaphoreType.DMA((n,)), pltpu.SemaphoreType.DMA((n,)), pltpu.SemaphoreType.DMA((n,))])(*halves)


BIG = ("w_in", "w_out", "w_ffn_in", "w_ffn_out")
COL_SHARDED = {"w_in": True, "w_out": False, "w_ffn_in": True, "w_ffn_out": False}
SMALL = ("norm_mix_g", "norm_ffn_g", "conv_a_w", "conf_dw_w", "conf_dw_b", "conf_ln_g", "conf_ln_b", "dn_conv_w",
         "dn_a_log", "dn_dt_bias", "dn_norm_g")
SMALL_SHARDED = ("conv_a_w", "conf_dw_w", "dn_conv_w")


def _half_rows(a, c):
    rh = a.shape[1] // 2
    return lax.dynamic_slice_in_dim(a, c * rh, rh, axis=1)


def _assemble(name, g):
    _, L, rh, C = g.shape
    g = g.reshape(4, 2, L, rh, C)
    if COL_SHARDED[name]:
        return g.transpose(2, 1, 3, 0, 4).reshape(L, 2 * rh, 4 * C)
    return g.transpose(2, 0, 1, 3, 4).reshape(L, 8 * rh, C)


def _split_for_reduce(name, g, c):
    L, R, C = g.shape
    if COL_SHARDED[name]:
        t = g.reshape(L, 2, R // 2, 4, C // 4).transpose(1, 3, 0, 2, 4)
    else:
        t = g.reshape(L, 4, 2, R // 8, C).transpose(2, 1, 0, 3, 4)
    mine = lax.dynamic_index_in_dim(t, c, 0, keepdims=False)
    other = lax.dynamic_index_in_dim(t, 1 - c, 0, keepdims=False)
    return mine, other


def _pack(parts):
    flat = jnp.concatenate([p.reshape(-1) for p in parts])
    n = flat.shape[0]
    rows = -(-n // (8 * HD)) * 8
    return jnp.pad(flat, (0, rows * HD - n)).reshape(rows, HD)


def _unpack(buf, shapes):
    flat = buf.reshape(-1)
    out, off = [], 0
    for s in shapes:
        n = 1
        for d in s:
            n *= d
        out.append(flat[off:off + n].reshape(s))
        off += n
    return out


class _Reduction:
    STAGES = ("swap", "exchange_a", "exchange_b", "exchange_c", "join")
    EXCHANGED = {"exchange_a": ("w_ffn_in",), "exchange_b": ("w_in",), "exchange_c": ("w_ffn_out", "w_out")}

    def __init__(self, l, split, core):
        self.l, self.split, self.core = l, split, core
        self.cargos = {}
        self.chip_sum = None

    def stage(self, name):
        l = self.l
        if name == "swap":
            cargo = swap_cargo([self.split[n] for n in BIG], slotted=True)
        elif name in self.EXCHANGED:
            if self.chip_sum is None:
                self.chip_sum = {n: add_half_bf16(self.split[n], r, self.core, name=f"reduce_add2_{n}_{l}")
                                 for n, r in zip(BIG, self.cargos["swap"].results)}
            cargo = exchange_cargo([self.chip_sum[n] for n in self.EXCHANGED[name]])
        else:
            from_chips = {n: r for st, ns in self.EXCHANGED.items() for n, r in zip(ns, self.cargos[st].results)}
            self.mine = {n: sum_slots(from_chips[n][:, None], name=f"reduce_add4_{n}_{l}")[0] for n in BIG}
            cargo = swap_cargo([self.mine[n] for n in BIG], slotted=False)
        self.cargos[name] = cargo
        return cargo

    def finish(self):
        return self.mine, dict(zip(BIG, self.cargos["join"].results))


def kernel(x, c, w_ada, b_ada, norm_mix_g, norm_ffn_g, w_in, conv_a_w, conf_dw_w, conf_dw_b, conf_ln_g, conf_ln_b, dn_conv_w, dn_a_log, dn_dt_bias, dn_norm_g, w_out, w_ffn_in, w_ffn_out, final_norm_g, loss_target, m_w_ada, m_b_ada, m_norm_mix_g, m_norm_ffn_g, m_w_in, m_conv_a_w, m_conf_dw_w, m_conf_dw_b, m_conf_ln_g, m_conf_ln_b, m_dn_conv_w, m_dn_a_log, m_dn_dt_bias, m_dn_norm_g, m_w_out, m_w_ffn_in, m_w_ffn_out, m_final_norm_g, v_w_ada, v_b_ada, v_norm_mix_g, v_norm_ffn_g, v_w_in, v_conv_a_w, v_conf_dw_w, v_conf_dw_b, v_conf_ln_g, v_conf_ln_b, v_dn_conv_w, v_dn_a_log, v_dn_dt_bias, v_dn_norm_g, v_w_out, v_w_ffn_in, v_w_ffn_out, v_final_norm_g):
    W = dict(w_ada=w_ada, b_ada=b_ada, norm_mix_g=norm_mix_g, norm_ffn_g=norm_ffn_g, w_in=w_in, conv_a_w=conv_a_w,
             conf_dw_w=conf_dw_w, conf_dw_b=conf_dw_b, conf_ln_g=conf_ln_g, conf_ln_b=conf_ln_b, dn_conv_w=dn_conv_w,
             dn_a_log=dn_a_log, dn_dt_bias=dn_dt_bias, dn_norm_g=dn_norm_g, w_out=w_out, w_ffn_in=w_ffn_in,
             w_ffn_out=w_ffn_out, final_norm_g=final_norm_g)
    M = dict(w_ada=m_w_ada, b_ada=m_b_ada, norm_mix_g=m_norm_mix_g, norm_ffn_g=m_norm_ffn_g, w_in=m_w_in, conv_a_w=m_conv_a_w,
             conf_dw_w=m_conf_dw_w, conf_dw_b=m_conf_dw_b, conf_ln_g=m_conf_ln_g, conf_ln_b=m_conf_ln_b, dn_conv_w=m_dn_conv_w,
             dn_a_log=m_dn_a_log, dn_dt_bias=m_dn_dt_bias, dn_norm_g=m_dn_norm_g, w_out=m_w_out, w_ffn_in=m_w_ffn_in,
             w_ffn_out=m_w_ffn_out, final_norm_g=m_final_norm_g)
    V = dict(w_ada=v_w_ada, b_ada=v_b_ada, norm_mix_g=v_norm_mix_g, norm_ffn_g=v_norm_ffn_g, w_in=v_w_in, conv_a_w=v_conv_a_w,
             conf_dw_w=v_conf_dw_w, conf_dw_b=v_conf_dw_b, conf_ln_g=v_conf_ln_g, conf_ln_b=v_conf_ln_b, dn_conv_w=v_dn_conv_w,
             dn_a_log=v_dn_a_log, dn_dt_bias=v_dn_dt_bias, dn_norm_g=v_dn_norm_g, w_out=v_w_out, w_ffn_in=v_w_ffn_in,
             w_ffn_out=v_w_ffn_out, final_norm_g=v_final_norm_g)
    L = w_ada.shape[0]
    ax, ay, ac = _place()
    chip = 2 * ax + ay
    dev = 4 * ax + 2 * ay + ac

    c_all = allgather8([jnp.pad(c, ((0, 7), (0, 0)))], space=pltpu.VMEM, name="gather_c")[0][:, 0, :]
    b_my = lax.dynamic_slice_in_dim(b_ada, chip * ADA_SH, ADA_SH, axis=1)[:, None, :]
    mod_sh = mod_fwd(c_all, w_ada, b_my, name="mod_fwd")
    lh = L // 2
    mod_g = allgather8([lax.dynamic_slice_in_dim(mod_sh, ac * lh, lh, axis=0).reshape(lh * 8, ADA_SH)], space=pltpu.VMEM,
                       name="gather_mod")[0]
    mod_all = mod_g.reshape(4, 2, lh, 8, ADA_SH).transpose(1, 2, 3, 0, 4).reshape(L, 8, 6 * D)
    mod = lax.dynamic_index_in_dim(mod_all, dev, 1, keepdims=False)

    src = {n: _half_rows(W[n], ac).astype(BF16) for n in BIG}

    def assembled(gathered):
        full = {n: _assemble(n, g[:, None])[0] for n, g in zip(BIG, gathered)}
        full["w_in"] = jnp.pad(full["w_in"], ((0, 0), (0, NP - IN_COLS)))
        return [full[n] for n in BIG]

    weights = [assembled(_comm_call(gather_cargo([src[n][0] for n in BIG]), "gather_w0"))]

    p_full = dict(W)
    sm = allgather8([_pack([W[n] for n in SMALL_SHARDED])], space=pltpu.VMEM, name="gather_convw")[0]
    per_chip = [_unpack(sm[4 * (s // 2) + 2 * (s % 2)], [W[n].shape for n in SMALL_SHARDED]) for s in range(4)]
    for i, n in enumerate(SMALL_SHARDED):
        p_full[n] = jnp.concatenate([per_chip[s][i] for s in range(4)], axis=-1)

    xs = x[0]
    saves, lps = [], []
    for l in range(L):
        lp = layer_params(l, mod, p_full)
        cargos = {n: gather_cargo([src[n][l + 1]]) for n in BIG} if l + 1 < L else None
        xs, s = layer_fwd(l, xs, lp, *weights[l], cargos=cargos)
        if cargos is not None:
            weights.append(assembled([cargos[n].results[0] for n in BIG]))
        saves.append(s)
        lps.append(lp)
    loss_p, dx, dgf = loss_head(xs, loss_target[0], final_norm_g[None, :], name="loss_head")
    core = ac.astype(jnp.int32).reshape(1)
    grads = [None] * L
    reduced = [None] * L
    pending = None
    for l in reversed(range(L)):
        dx, grads[l] = layer_bwd(l, dx, saves[l], lps[l], *weights[l], carry=None if pending is None else pending.stage)
        if pending is not None:
            reduced[pending.l] = pending.finish()
        pending = _Reduction(l, {n: grads[l][n] for n in BIG}, core)
    for stage in _Reduction.STAGES:
        cargo = pending.stage(stage)
        cargo.results = _comm_call(cargo, f"reduce_{stage}_{pending.l}")
    reduced[pending.l] = pending.finish()
    loss = lax.psum(loss_p[0, 0], ("x", "y", "c"))
    grad_x = dx[None]

    small_shapes = [(L,) + p_full[n].shape[1:] for n in SMALL]
    parts = [jnp.stack([grads[l][n].reshape(sh[1:]) for l in range(L)]) for n, sh in zip(SMALL, small_shapes)]
    parts += [dgf.reshape(D), jnp.concatenate([grads[l]["mod"] for l in range(L)], axis=0)]
    small_shapes += [(D,), (L, 6 * D)]
    packed = _pack(parts)
    gathered_small = allgather8([packed], space=pltpu.VMEM, name="gather_small")[0]
    summed = sum_slots(gathered_small[:, None], name="sum_small")[0]
    g_small = dict(zip(SMALL + ("final_norm_g", "b_ada"), _unpack(summed, small_shapes)))
    for n in SMALL_SHARDED:
        sw = W[n].shape[-1]
        g_small[n] = lax.dynamic_slice_in_dim(g_small[n], chip * sw, sw, axis=g_small[n].ndim - 1)
    dmod_all = jnp.stack([_unpack(gathered_small[d], small_shapes)[-1] for d in range(8)], axis=1)
    dmod_my = lax.dynamic_slice_in_dim(dmod_all, chip * ADA_SH, ADA_SH, axis=2)
    g_w_ada = wada_grad(c_all, dmod_my, name="wada_grad")

    halves = [jnp.stack([reduced[l][0][n] for l in range(L)]) for n in BIG]
    theirs = [jnp.stack([reduced[l][1][n] for l in range(L)]) for n in BIG]

    out_g, out_d, out_m, out_v = {}, {}, {}, {}
    out_g["w_ada"] = g_w_ada
    out_d["w_ada"], out_m["w_ada"], out_v["w_ada"] = adamw(W["w_ada"], g_w_ada, M["w_ada"], V["w_ada"], name="adamw_w_ada")
    for n, g_mine, g_theirs in zip(BIG, halves, theirs):
        out_g[n], out_d[n], out_m[n], out_v[n] = adamw_halves(W[n], g_mine, g_theirs, core, M[n], V[n], name=f"adamw_{n}")
    names_small = SMALL + ("final_norm_g", "b_ada")
    pk = lambda d: _pack([d[n] for n in names_small])[None]
    d_s, m_s, v_s = ew_call(_adamw_math, [pk(W), pk(g_small), pk(M), pk(V)], 3, name="adamw_small")
    shapes_s = [W[n].shape for n in names_small]
    for d, o in ((out_d, d_s), (out_m, m_s), (out_v, v_s)):
        d.update(zip(names_small, _unpack(o[0], shapes_s)))
    for n in names_small:
        out_g[n] = g_small[n].reshape(W[n].shape)

    order = ("w_ada", "b_ada", "norm_mix_g", "norm_ffn_g", "w_in", "conv_a_w", "conf_dw_w", "conf_dw_b", "conf_ln_g", "conf_ln_b",
             "dn_conv_w", "dn_a_log", "dn_dt_bias", "dn_norm_g", "w_out", "w_ffn_in", "w_ffn_out", "final_norm_g")
    return (loss, grad_x, *[out_g[n] for n in order], *[out_d[n] for n in order], *[out_m[n] for n in order],
            *[out_v[n] for n in order])
```

```python
import functools

import jax
import jax.numpy as jnp
from jax import lax
from jax.experimental import pallas as pl
from jax.experimental.pallas import tpu as pltpu

F32 = jnp.float32
BF16 = jnp.bfloat16
HI = lax.Precision.HIGHEST
MESH = pl.DeviceIdType.MESH

D = 1024
DEPTH = 4
DC = 256
DDN = 512
NH = 4
HD = 128
CH = 64
DFF = 2816
IN_COLS = 3336
NP = 3456
KA, KB, KC = 3, 31, 4
HALO = 32
EPS = 1e-6
O_AB, O_AC, O_AV, O_BA, O_BG, O_Q, O_K, O_V, O_Z, O_GB = 0, 256, 512, 768, 1024, 1280, 1792, 2304, 2816, 3328

ADAM_LR, ADAM_B1, ADAM_B2, ADAM_EPS, ADAM_WD, ADAM_STEP = 0.001, 0.9, 0.999, 1e-08, 0.01, 10

VMEM_LIMIT = 56 * 1024 * 1024


def _cp(n_grid):
    return pltpu.CompilerParams(dimension_semantics=("arbitrary",) * n_grid, vmem_limit_bytes=VMEM_LIMIT)


def _sds(shape, dtype=F32):
    return jax.ShapeDtypeStruct(tuple(shape), dtype)


def _dot(a, b, prec=None):
    return jnp.dot(a, b, preferred_element_type=F32, precision=prec)


def _dot_nt(a, b, prec=None):
    return lax.dot_general(a, b, (((1,), (1,)), ((), ())), preferred_element_type=F32, precision=prec)


def _dot_tn(a, b, prec=None):
    return lax.dot_general(a, b, (((0,), (0,)), ((), ())), preferred_element_type=F32, precision=prec)


def _split_bf16(a):
    hi = a.astype(BF16)
    return hi, (a - hi.astype(F32)).astype(BF16)


_DIMS = {"nn": (((1,), (0,)), ((), ())), "nt": (((1,), (1,)), ((), ())), "tn": (((0,), (0,)), ((), ()))}
_DIMS_BATCHED = {"nn": (((2,), (1,)), ((0,), (0,))), "nt": (((2,), (2,)), ((0,), (0,))), "tn": (((1,), (1,)), ((0,), (0,)))}


def _mm3_raw(a, b, kind):
    ah, al = _split_bf16(a)
    bh, bl = _split_bf16(b)
    d = (_DIMS if a.ndim == 2 else _DIMS_BATCHED)[kind]
    dg = lambda u, v: lax.dot_general(u, v, d, preferred_element_type=F32)
    return dg(ah, bh) + (dg(ah, bl) + dg(al, bh))


@jax.custom_vjp
def _mm_nn(a, b):
    return _mm3_raw(a, b, "nn")


@jax.custom_vjp
def _mm_nt(a, b):
    return _mm3_raw(a, b, "nt")


@jax.custom_vjp
def _mm_tn(a, b):
    return _mm3_raw(a, b, "tn")


_mm_nn.defvjp(lambda a, b: (_mm_nn(a, b), (a, b)), lambda r, g: (_mm_nt(g, r[1]), _mm_tn(r[0], g)))
_mm_nt.defvjp(lambda a, b: (_mm_nt(a, b), (a, b)), lambda r, g: (_mm_nn(g, r[1]), _mm_tn(g, r[0])))
_mm_tn.defvjp(lambda a, b: (_mm_tn(a, b), (a, b)), lambda r, g: (_mm_nt(r[1], g), _mm_nn(r[0], g)))


def _mm1_raw(a, b, kind):
    d = (_DIMS if a.ndim == 2 else _DIMS_BATCHED)[kind]
    return lax.dot_general(a.astype(BF16), b.astype(BF16), d, preferred_element_type=F32)


@jax.custom_vjp
def _mm1_nn(a, b):
    return _mm1_raw(a, b, "nn")


@jax.custom_vjp
def _mm1_nt(a, b):
    return _mm1_raw(a, b, "nt")


@jax.custom_vjp
def _mm1_tn(a, b):
    return _mm1_raw(a, b, "tn")


_mm1_nn.defvjp(lambda a, b: (_mm1_nn(a, b), (a, b)), lambda r, g: (_mm1_nt(g, r[1]), _mm1_tn(r[0], g)))
_mm1_nt.defvjp(lambda a, b: (_mm1_nt(a, b), (a, b)), lambda r, g: (_mm1_nn(g, r[1]), _mm1_tn(g, r[0])))
_mm1_tn.defvjp(lambda a, b: (_mm1_tn(a, b), (a, b)), lambda r, g: (_mm1_nt(r[1], g), _mm1_nn(r[0], g)))


@jax.custom_vjp
def _inv_given(x, p):
    return p


_inv_given.defvjp(lambda x, p: (p, p), lambda p, g: (_mm_tn(p, _mm_nt(g, p)), jnp.zeros_like(p)))


def _silu(x):
    return x * jax.nn.sigmoid(x)


def _colsum(x):
    return jnp.sum(x, axis=0, keepdims=True)


TILE_CAP = 512


def _w3(w):
    return w if w.ndim == 3 else w[None]


def _wspec(l, block, index):
    return pl.BlockSpec((None,) + block, lambda *g: (l,) + index(*g))


def _tile(T, want):
    t = min(T, want, TILE_CAP)
    assert T % t == 0
    return t


def _normmod(x, gn, sc, sh):
    r = lax.rsqrt(jnp.mean(x * x, axis=-1, keepdims=True) + EPS)
    return ((x * r) * gn) * (1.0 + sc) + sh


def _rms(x, g):
    return (x * lax.rsqrt(jnp.mean(x * x, axis=-1, keepdims=True) + EPS)) * g


def _mix_b_post(u, ln_g, ln_b):
    mu = jnp.mean(u, axis=-1, keepdims=True)
    var = jnp.mean(jnp.square(u - mu), axis=-1, keepdims=True)
    return _silu(((u - mu) * lax.rsqrt(var + 1e-5)) * ln_g + ln_b)


def _softplus(z):
    return jnp.where(z > 0, z, 0.0) + jnp.log(1.0 + jnp.exp(-jnp.where(z > 0, z, -z)))


def _chunk_tril(tt):
    r = lax.broadcasted_iota(jnp.int32, (tt, tt), 0)
    c = lax.broadcasted_iota(jnp.int32, (tt, tt), 1)
    return ((r // CH == c // CH) & (c <= r)).astype(F32)


def _eye8():
    return (lax.broadcasted_iota(jnp.int32, (8, HD), 0) == lax.broadcasted_iota(jnp.int32, (8, HD), 1)).astype(F32)


def _dn_post(pre_q, pre_k, pre_v, blk, alog_row, dt_row):
    q = [s * lax.rsqrt(jnp.sum(s * s, -1, keepdims=True) + EPS) * (HD ** -0.5) for s in map(_silu, pre_q)]
    k = [s * lax.rsqrt(jnp.sum(s * s, -1, keepdims=True) + EPS) for s in map(_silu, pre_k)]
    v = [_silu(p) for p in pre_v]
    lane = lax.broadcasted_iota(jnp.int32, (1, HD), 1)
    g = -jnp.exp(alog_row) * _softplus(blk + dt_row)
    beta = jax.nn.sigmoid(blk)
    gc = _dot(_chunk_tril(blk.shape[0]), jnp.where(lane < NH, g, 0.0), HI)
    gb = jnp.where(lane < NH, gc, jnp.where(lane < 2 * NH, beta, 0.0))
    grow = _dot_nt(_eye8(), gc, HI)
    return q, k, v, gb, grow


def _dn_intra(q, k, v, beta, gcol, grow, p_known=None):
    r = lax.broadcasted_iota(jnp.int32, (CH, CH), 0)
    c = lax.broadcasted_iota(jnp.int32, (CH, CH), 1)
    causal, strict = c <= r, c < r
    decay = jnp.where(causal, jnp.exp(jnp.where(causal, gcol - grow, 0.0)), 0.0)
    kb = k * beta
    x = -jnp.where(strict, _mm1_nt(kb, k) * decay, 0.0)
    if p_known is None:
        p = (r == c).astype(F32) + x
        y = x
        for _ in range(5):
            y = _mm_nn(y, y)
            p = p + _mm_nn(p, y)
    else:
        p = _inv_given(x, p_known)
    u = _mm1_nn(p, v * beta)
    w = _mm1_nn(p, kb * jnp.exp(gcol))
    qk = jnp.where(causal, _mm1_nt(q, k) * decay, 0.0)
    return u, w, qk, p


def _dn_inter(s, q, k, u, w, gcol, qk):
    last = (lax.broadcasted_iota(jnp.int32, (CH, 1), 0) == CH - 1).astype(F32)
    g_last = jnp.sum(gcol * last, axis=1, keepdims=True)
    v_new = u - _mm1_nn(w, s)
    o = _mm1_nn(q * jnp.exp(gcol), s) + _mm1_nn(qk, v_new)
    s_new = s * jnp.exp(g_last) + _mm1_tn(k * jnp.exp(g_last - gcol), v_new)
    return s_new, o


def _yc(o, z, gdn):
    return _rms(o, gdn) * _silu(z)


class Cargo:
    def __init__(self, ins, outs, sems, first, last, middle=None):
        self.ins, self.outs, self.sems = list(ins), list(outs), list(sems)
        self.first, self.middle, self.last = first, middle, last
        self.results = None


def _pcall(body, *, grid, in_specs, out_specs, out_shape, args, name, scratch_shapes=(), cargo=None):
    if cargo is None:
        return pl.pallas_call(body, grid=grid, in_specs=list(in_specs), out_specs=list(out_specs), out_shape=list(out_shape),
                              scratch_shapes=list(scratch_shapes), name=name, compiler_params=_cp(len(grid)))(*args)
    n_in, n_out, n_scr = len(args), len(out_shape), len(scratch_shapes)
    k_in, k_out = len(cargo.ins), len(cargo.outs)
    total = 1
    for g in grid:
        total *= g

    def carrying(*refs):
        ins, cins = refs[:n_in], refs[n_in:n_in + k_in]
        o0 = n_in + k_in
        outs, couts = refs[o0:o0 + n_out], refs[o0 + n_out:o0 + n_out + k_out]
        rest = refs[o0 + n_out + k_out:]
        scr, csems = rest[:n_scr], rest[n_scr:]
        step = pl.program_id(0)
        for a in range(1, len(grid)):
            step = step * grid[a] + pl.program_id(a)

        @pl.when(step == 0)
        def _():
            cargo.first(cins, couts, csems)
        if cargo.middle is not None:
            @pl.when(step == (total * 7) // 8)
            def _():
                cargo.middle(cins, couts, csems)
        body(*ins, *outs, *scr)

        @pl.when(step == total - 1)
        def _():
            cargo.last(cins, couts, csems)

    hbm = pl.BlockSpec(memory_space=pl.ANY)
    res = pl.pallas_call(
        carrying, grid=grid, in_specs=list(in_specs) + [hbm] * k_in, out_specs=list(out_specs) + [hbm] * k_out,
        out_shape=list(out_shape) + cargo.outs, scratch_shapes=list(scratch_shapes) + cargo.sems, name=name,
        compiler_params=_cp(len(grid)))(*args, *cargo.ins)
    cargo.results = list(res[n_out:])
    return list(res[:n_out])


def _place():
    return lax.axis_index("x"), lax.axis_index("y"), lax.axis_index("c")


def _other_chips(x, y):
    return [(1 - x, y), (x, 1 - y), (1 - x, 1 - y)]


def _gather_phases(n):
    def env(ins, outs, sems):
        send_sems, recv_sems, local_sems = sems
        x, y, c = _place()
        me, sibling = (x, y, c), (x, y, 1 - c)
        chips = _other_chips(x, y)

        def slot(p):
            return 4 * p[0] + 2 * p[1] + p[2]

        def copy(a, k, block, to, own=False):
            dst = outs[a].at[slot(block)]
            return pltpu.make_async_remote_copy(src_ref=ins[a] if own else dst, dst_ref=dst, send_sem=send_sems.at[a, k],
                                                recv_sem=recv_sems.at[a, k], device_id=to, device_id_type=MESH)

        mine = [pltpu.make_async_copy(ins[a], outs[a].at[slot(me)], local_sems.at[a]) for a in range(n)]
        own = [copy(a, 0, me, sibling, own=True) for a in range(n)]
        own += [copy(a, 1 + j, me, (*chip, c), own=True) for a in range(n) for j, chip in enumerate(chips)]
        return c, me, sibling, chips, copy, mine, own

    def first(ins, outs, sems):
        _, _, _, _, _, mine, own = env(ins, outs, sems)
        for cp in mine + own:
            cp.start()

    def middle(ins, outs, sems):
        c, me, sibling, chips, copy, _, _ = env(ins, outs, sems)
        for j, chip in enumerate(chips):
            for a in range(n):
                copy(a, 1 + j, (*chip, c), me).wait_recv()
                copy(a, 4 + j, (*chip, c), sibling).start()

    def last(ins, outs, sems):
        c, me, sibling, chips, copy, mine, own = env(ins, outs, sems)
        for a in range(n):
            copy(a, 0, sibling, me).wait_recv()
            for j, chip in enumerate(chips):
                copy(a, 4 + j, (*chip, 1 - c), me).wait_recv()
        for cp in own:
            cp.wait_send()
        for a in range(n):
            for j, chip in enumerate(chips):
                copy(a, 4 + j, (*chip, c), sibling).wait_send()
        for cp in mine:
            cp.wait()

    return first, middle, last


def _gather_sems(n):
    return [pltpu.SemaphoreType.DMA((n, 7)), pltpu.SemaphoreType.DMA((n, 7)), pltpu.SemaphoreType.DMA((n,))]


def gather_cargo(blocks):
    first, middle, last = _gather_phases(len(blocks))
    return Cargo(blocks, [_sds((8,) + b.shape, b.dtype) for b in blocks], _gather_sems(len(blocks)), first, last, middle)


def _swap_phases(n, slotted):
    def copies(ins, outs, sems):
        send_sems, recv_sems = sems
        x, y, c = _place()
        return [pltpu.make_async_remote_copy(src_ref=ins[a].at[1 - c] if slotted else ins[a], dst_ref=outs[a],
                                             send_sem=send_sems.at[a], recv_sem=recv_sems.at[a], device_id=(x, y, 1 - c),
                                             device_id_type=MESH) for a in range(n)]

    def first(ins, outs, sems):
        for cp in copies(ins, outs, sems):
            cp.start()

    def last(ins, outs, sems):
        for cp in copies(ins, outs, sems):
            cp.wait()

    return first, last


def swap_cargo(blocks, slotted):
    n = len(blocks)
    first, last = _swap_phases(n, slotted)
    return Cargo(blocks, [_sds(b.shape[1:] if slotted else b.shape, b.dtype) for b in blocks],
                 [pltpu.SemaphoreType.DMA((n,)), pltpu.SemaphoreType.DMA((n,))], first, last)


def _exchange_phases(n):
    def env(ins, outs, sems):
        send_sems, recv_sems, local_sems = sems
        x, y, c = _place()
        me = 2 * x + y
        chips = _other_chips(x, y)
        mine = [pltpu.make_async_copy(ins[a].at[me], outs[a].at[me], local_sems.at[a]) for a in range(n)]
        sends = [pltpu.make_async_remote_copy(src_ref=ins[a].at[2 * chip[0] + chip[1]], dst_ref=outs[a].at[me],
                                              send_sem=send_sems.at[a, j], recv_sem=recv_sems.at[a, j], device_id=(*chip, c),
                                              device_id_type=MESH) for a in range(n) for j, chip in enumerate(chips)]
        recvs = [pltpu.make_async_remote_copy(src_ref=ins[a].at[2 * chip[0] + chip[1]], dst_ref=outs[a].at[2 * chip[0] + chip[1]],
                                              send_sem=send_sems.at[a, j], recv_sem=recv_sems.at[a, j], device_id=(*chip, c),
                                              device_id_type=MESH) for a in range(n) for j, chip in enumerate(chips)]
        return mine, sends, recvs

    def first(ins, outs, sems):
        mine, sends, _ = env(ins, outs, sems)
        for cp in mine + sends:
            cp.start()

    def last(ins, outs, sems):
        mine, sends, recvs = env(ins, outs, sems)
        for cp in recvs:
            cp.wait_recv()
        for cp in sends:
            cp.wait_send()
        for cp in mine:
            cp.wait()

    return first, last


def exchange_cargo(blocks):
    n = len(blocks)
    first, last = _exchange_phases(n)
    return Cargo(blocks, [_sds(b.shape, b.dtype) for b in blocks],
                 [pltpu.SemaphoreType.DMA((n, 3)), pltpu.SemaphoreType.DMA((n, 3)), pltpu.SemaphoreType.DMA((n,))], first, last)


def _comm_call(cargo, name):
    def body(*refs):
        k_in, k_out = len(cargo.ins), len(cargo.outs)
        ins, outs, sems = refs[:k_in], refs[k_in:k_in + k_out], refs[k_in + k_out:]
        cargo.first(ins, outs, sems)
        if cargo.middle is not None:
            cargo.middle(ins, outs, sems)
        cargo.last(ins, outs, sems)

    hbm = pl.BlockSpec(memory_space=pl.ANY)
    return list(pl.pallas_call(body, name=name, in_specs=[hbm] * len(cargo.ins), out_specs=[hbm] * len(cargo.outs),
                               out_shape=cargo.outs, scratch_shapes=cargo.sems)(*cargo.ins))


def nm_fwd(x, gn, sc, sh, w, *, tn, name, wl=0, cargo=None):
    T, N = x.shape[0], w.shape[-1]
    tt = _tile(T, 512)
    nt = T // tt

    def body(x_ref, gn_ref, sc_ref, sh_ref, w_ref, h_ref, o_ref, h_all):
        i = pl.program_id(1)

        @pl.when(pl.program_id(0) == 0)
        def _():
            h = _normmod(x_ref[...], gn_ref[...], sc_ref[...], sh_ref[...]).astype(BF16)
            h_all[i] = h
            h_ref[...] = h
        o_ref[...] = _dot(h_all[i], w_ref[...])

    def x_index(j, i):
        return (jnp.where(j == 0, i, nt - 1), 0)

    row = pl.BlockSpec((1, D), lambda j, i: (0, 0))
    return _pcall(
        body, grid=(N // tn, nt), name=name, cargo=cargo,
        in_specs=[pl.BlockSpec((tt, D), x_index), row, row, row, _wspec(wl, (D, tn), lambda j, i: (0, j))],
        out_specs=[pl.BlockSpec((tt, D), x_index), pl.BlockSpec((tt, tn), lambda j, i: (i, j))],
        out_shape=[_sds((T, D), BF16), _sds((T, N))], scratch_shapes=[pltpu.VMEM((nt, tt, D), BF16)],
        args=(x, gn, sc, sh, _w3(w)))


def nm_bwd(dys, w, x, gn, sc, sh, dres, *, name, wl=0, cargo=None):
    T = x.shape[0]
    N = w.shape[-1]
    tt = _tile(T, 256)
    widths = [a.shape[1] for a in dys]
    assert sum(widths) == N
    n = len(dys)

    def body(*refs):
        dy_refs, (w_ref, x_ref, gn_ref, sc_ref, sh_ref, dres_ref) = refs[:n], refs[n:n + 6]
        dx_ref, dgn_ref, dsc_ref, dsh_ref, dyb_ref = refs[n + 6:]
        i = pl.program_id(0)
        dyb = jnp.concatenate([r[...].astype(BF16) for r in dy_refs], axis=1) if n > 1 else dy_refs[0][...].astype(BF16)
        dyb_ref[...] = dyb
        dh = _dot_nt(dyb, w_ref[...])
        _, vjp = jax.vjp(_normmod, x_ref[...], gn_ref[...], sc_ref[...], sh_ref[...])
        dx, dgn, dsc, dsh = vjp(dh)
        dx_ref[...] = dres_ref[...] + dx

        @pl.when(i == 0)
        def _():
            dgn_ref[...] = jnp.zeros_like(dgn_ref)
            dsc_ref[...] = jnp.zeros_like(dsc_ref)
            dsh_ref[...] = jnp.zeros_like(dsh_ref)
        dgn_ref[...] += dgn
        dsc_ref[...] += dsc
        dsh_ref[...] += dsh

    row = pl.BlockSpec((1, D), lambda i: (0, 0))
    tile = pl.BlockSpec((tt, D), lambda i: (i, 0))
    return _pcall(
        body, grid=(T // tt,), name=name, cargo=cargo,
        in_specs=[pl.BlockSpec((tt, wd), lambda i: (i, 0)) for wd in widths]
        + [_wspec(wl, (D, N), lambda i: (0, 0)), tile, row, row, row, tile],
        out_specs=[tile, row, row, row, pl.BlockSpec((tt, N), lambda i: (i, 0))],
        out_shape=[_sds((T, D)), _sds((1, D)), _sds((1, D)), _sds((1, D)), _sds((T, N), BF16)],
        args=(*dys, _w3(w), x, gn, sc, sh, dres))


def mm_tn(a, b, *, tn, name):
    T, K = a.shape
    N = b.shape[1]
    tt = _tile(T, 512)

    def body(a_ref, b_ref, o_ref):
        @pl.when(pl.program_id(1) == 0)
        def _():
            o_ref[...] = jnp.zeros_like(o_ref)
        o_ref[...] += _dot_tn(a_ref[...], b_ref[...])

    return pl.pallas_call(
        body, grid=(N // tn, T // tt), name=name,
        in_specs=[pl.BlockSpec((tt, K), lambda j, t: (t, 0)), pl.BlockSpec((tt, tn), lambda j, t: (t, j))],
        out_specs=pl.BlockSpec((K, tn), lambda j, t: (0, j)), out_shape=_sds((K, N)), compiler_params=_cp(2))(a, b)


def mm_tn_split(a, b, *, by_cols, name):
    T, K = a.shape
    N = b.shape[1]
    tt = _tile(T, 512)
    tn = N // 4 if by_cols else N // 2
    rh, C = (K // 2, N // 4) if by_cols else (K // 8, N)
    nt = T // tt

    def body(a_ref, b_ref, o_ref, acc):
        t = pl.program_id(1)

        @pl.when(t == 0)
        def _():
            acc[...] = jnp.zeros_like(acc)
        acc[...] += _dot_tn(a_ref[...], b_ref[...])

        @pl.when(t == nt - 1)
        def _():
            if by_cols:
                for h in range(2):
                    o_ref[h] = acc[pl.ds(h * rh, rh), :]
            else:
                for s in range(4):
                    for h in range(2):
                        o_ref[h, s] = acc[pl.ds((2 * s + h) * rh, rh), :]

    if by_cols:
        out_spec = pl.BlockSpec((2, None, rh, tn), lambda j, t: (0, j, 0, 0))
    else:
        out_spec = pl.BlockSpec((2, 4, rh, tn), lambda j, t: (0, 0, 0, j))
    return pl.pallas_call(
        body, grid=(N // tn, nt), name=name,
        in_specs=[pl.BlockSpec((tt, K), lambda j, t: (t, 0)), pl.BlockSpec((tt, tn), lambda j, t: (t, j))],
        out_specs=out_spec, out_shape=_sds((2, 4, rh, C)), scratch_shapes=[pltpu.VMEM((K, tn), F32)],
        compiler_params=_cp(2))(a, b)


def _fill_pad(pad_ref, prev, cur, first):
    pad_ref[pl.ds(0, HALO), :] = jnp.where(first, 0.0, prev)
    pad_ref[pl.ds(HALO, cur.shape[0]), :] = cur


SUBLANES = 8


def _shift_scratch(tt, C):
    return pltpu.VMEM((SUBLANES - 1, tt + HALO - SUBLANES, C), F32)


def _preshift(src_ref, sh_ref, tt, cols=slice(None)):
    for b in range(1, SUBLANES):
        sh_ref[b - 1] = src_ref[pl.ds(b, tt + HALO - SUBLANES), cols]


def _window(src_ref, sh_ref, off, tt, cols=slice(None)):
    a, b = divmod(off, SUBLANES)
    if b == 0 or sh_ref is None:
        return src_ref[pl.ds(off, tt), cols]
    return sh_ref[b - 1, pl.ds(SUBLANES * a, tt), :]


def _causal_conv(pad_ref, w, K, tt, sh_ref=None):
    acc = None
    for k in range(K):
        term = _window(pad_ref, sh_ref, HALO - (K - 1) + k, tt) * w[k:k + 1, :]
        acc = term if acc is None else acc + term
    return acc


def _conv_inputs(pc, pp, first, pad_a, pad_b, pad_c):
    def s_of(p):
        return p[:, O_AC:O_AV] * p[:, O_AV:O_BA]

    def u0_of(p):
        return p[:, O_BA:O_BG] * jax.nn.sigmoid(p[:, O_BG:O_Q])

    _fill_pad(pad_a, s_of(pp), s_of(pc), first)
    _fill_pad(pad_b, u0_of(pp), u0_of(pc), first)
    _fill_pad(pad_c, pp[:, O_Q:O_Z], pc[:, O_Q:O_Z], first)


def _mix_specs(T, tt):
    cur = pl.BlockSpec((tt, O_Z), lambda i: (i, 0))
    prev = pl.BlockSpec((HALO, O_Z), lambda i: (jnp.maximum(i * (tt // HALO) - 1, 0), 0))
    gbb = pl.BlockSpec((tt, HD), lambda i: (i, O_GB // HD))
    return cur, prev, gbb


def _full(shape):
    return pl.BlockSpec(shape, lambda i: (0,) * len(shape))


def mix_fwd(proj, wa, wb, bb, ln_g, ln_b, wc, alog_row, dt_row, *, name, cargo=None):
    T = proj.shape[0]
    tt = _tile(T, 256)

    def body(pc_ref, pp_ref, blk_ref, wa_ref, wb_ref, bb_ref, lg_ref, lb_ref, wc_ref, al_ref, dt_ref,
             yab_ref, q_ref, k_ref, v_ref, gb_ref, grow_ref, pad_a, pad_b, pad_c, sh_b):
        first = pl.program_id(0) == 0
        pc = pc_ref[...]
        _conv_inputs(pc, pp_ref[...], first, pad_a, pad_b, pad_c)
        _preshift(pad_b, sh_b, tt)
        ya = pc[:, O_AB:O_AC] * _causal_conv(pad_a, wa_ref[...], KA, tt)
        yb = _mix_b_post(_causal_conv(pad_b, wb_ref[...], KB, tt, sh_b) + bb_ref[...], lg_ref[...], lb_ref[...])
        yab_ref[...] = jnp.concatenate([ya, yb], axis=1)
        pre = _causal_conv(pad_c, wc_ref[...], KC, tt)
        blocks = [pre[:, j * HD:(j + 1) * HD] for j in range(3 * NH)]
        q, k, v, gb, grow = _dn_post(blocks[:NH], blocks[NH:2 * NH], blocks[2 * NH:], blk_ref[...], al_ref[...], dt_ref[...])
        q_ref[...] = jnp.concatenate(q, axis=1)
        k_ref[...] = jnp.concatenate(k, axis=1)
        v_ref[...] = jnp.concatenate(v, axis=1)
        gb_ref[...] = gb
        grow_ref[...] = grow

    cur, prev, gbb = _mix_specs(T, tt)
    t512 = pl.BlockSpec((tt, DDN), lambda i: (i, 0))
    return _pcall(
        body, grid=(T // tt,), name=name, cargo=cargo, args=(proj, proj, proj, wa, wb, bb, ln_g, ln_b, wc, alog_row, dt_row),
        in_specs=[cur, prev, gbb, _full((8, DC)), _full((32, DC)), _full((1, DC)), _full((1, DC)), _full((1, DC)),
                  _full((8, 3 * DDN)), _full((1, HD)), _full((1, HD))],
        out_specs=[t512, t512, t512, t512, pl.BlockSpec((tt, HD), lambda i: (i, 0)), pl.BlockSpec((8, tt), lambda i: (0, i))],
        out_shape=[_sds((T, 2 * DC)), _sds((T, DDN)), _sds((T, DDN)), _sds((T, DDN)), _sds((T, HD)), _sds((8, T))],
        scratch_shapes=[pltpu.VMEM((HALO + tt, DC), F32), pltpu.VMEM((HALO + tt, DC), F32), pltpu.VMEM((HALO + tt, 3 * DDN), F32),
                        _shift_scratch(tt, DC)])


def mix_bwd_point(proj, dyab, dq, dk, dv, dgb, dgrow, wa, wb, bb, ln_g, ln_b, wc, alog_row, dt_row, *, name, cargo=None):
    T = proj.shape[0]
    tt = _tile(T, 256)
    CW = 2 * DC + 3 * DDN

    def body(pc_ref, pp_ref, blk_ref, dyab_ref, dq_ref, dk_ref, dv_ref, dgb_ref, dgrow_ref,
             wa_ref, wb_ref, bb_ref, lg_ref, lb_ref, wc_ref, al_ref, dt_ref,
             dab_ref, dconv_ref, dblk_ref, dbb_ref, dlg_ref, dlb_ref, dal_ref, ddt_ref, pad_a, pad_b, pad_c, sh_b):
        i = pl.program_id(0)
        pc = pc_ref[...]
        _conv_inputs(pc, pp_ref[...], i == 0, pad_a, pad_b, pad_c)
        _preshift(pad_b, sh_b, tt)
        ca = _causal_conv(pad_a, wa_ref[...], KA, tt)
        u = _causal_conv(pad_b, wb_ref[...], KB, tt, sh_b) + bb_ref[...]
        pre = _causal_conv(pad_c, wc_ref[...], KC, tt)
        dyab_v = dyab_ref[...]
        dya, dyb = dyab_v[:, :DC], dyab_v[:, DC:]
        dab_ref[...] = dya * ca
        dca = dya * pc[:, O_AB:O_AC]
        _, vjp_b = jax.vjp(_mix_b_post, u, lg_ref[...], lb_ref[...])
        du, dlg, dlb = vjp_b(dyb)
        blocks = [pre[:, j * HD:(j + 1) * HD] for j in range(3 * NH)]
        _, vjp_c = jax.vjp(_dn_post, blocks[:NH], blocks[NH:2 * NH], blocks[2 * NH:], blk_ref[...], al_ref[...], dt_ref[...])

        def heads(r):
            vv = r[...]
            return [vv[:, h * HD:(h + 1) * HD] for h in range(NH)]
        dpq, dpk, dpv, dblk, dal, ddt = vjp_c((heads(dq_ref), heads(dk_ref), heads(dv_ref), dgb_ref[...], dgrow_ref[...]))
        dconv_ref[...] = jnp.concatenate([dca, du] + dpq + dpk + dpv, axis=1)
        dblk_ref[...] = dblk

        @pl.when(i == 0)
        def _():
            for r in (dbb_ref, dlg_ref, dlb_ref, dal_ref, ddt_ref):
                r[...] = jnp.zeros_like(r)
        dbb_ref[...] += _colsum(du)
        dlg_ref[...] += dlg
        dlb_ref[...] += dlb
        dal_ref[...] += dal
        ddt_ref[...] += ddt

    cur, prev, gbb = _mix_specs(T, tt)
    t512 = pl.BlockSpec((tt, DDN), lambda i: (i, 0))
    t128 = pl.BlockSpec((tt, HD), lambda i: (i, 0))
    return _pcall(
        body, grid=(T // tt,), name=name, cargo=cargo,
        args=(proj, proj, proj, dyab, dq, dk, dv, dgb, dgrow, wa, wb, bb, ln_g, ln_b, wc, alog_row, dt_row),
        in_specs=[cur, prev, gbb, t512, t512, t512, t512, t128, pl.BlockSpec((8, tt), lambda i: (0, i)),
                  _full((8, DC)), _full((32, DC)), _full((1, DC)), _full((1, DC)), _full((1, DC)),
                  _full((8, 3 * DDN)), _full((1, HD)), _full((1, HD))],
        out_specs=[pl.BlockSpec((tt, DC), lambda i: (i, 0)), pl.BlockSpec((tt, CW), lambda i: (i, 0)), t128,
                   _full((1, DC)), _full((1, DC)), _full((1, DC)), _full((1, HD)), _full((1, HD))],
        out_shape=[_sds((T, DC)), _sds((T, CW)), _sds((T, HD)), _sds((1, DC)), _sds((1, DC)), _sds((1, DC)),
                   _sds((1, HD)), _sds((1, HD))],
        scratch_shapes=[pltpu.VMEM((HALO + tt, DC), F32), pltpu.VMEM((HALO + tt, DC), F32), pltpu.VMEM((HALO + tt, 3 * DDN), F32),
                        _shift_scratch(tt, DC)])


def mix_bwd_conv(proj, dconv, wa, wb, wc, *, name, cargo=None):
    T = proj.shape[0]
    tt = _tile(T, 256)
    CW = 2 * DC + 3 * DDN
    nblk = T // HALO

    def body(pc_ref, pp_ref, dc_ref, dn_ref, wa_ref, wb_ref, wc_ref, dp_ref, dwa_ref, dwb_ref, dwc_ref,
             pad_a, pad_b, pad_c, dpad, sh_b, sh_d):
        i = pl.program_id(0)
        last = i == pl.num_programs(0) - 1
        pc = pc_ref[...]
        _conv_inputs(pc, pp_ref[...], i == 0, pad_a, pad_b, pad_c)
        dcur = dc_ref[...]
        dpad[pl.ds(0, tt), :] = dcur
        dpad[pl.ds(tt, HALO), :] = jnp.where(last, 0.0, dn_ref[...])
        _preshift(pad_b, sh_b, tt)
        _preshift(dpad, sh_d, tt, slice(DC, 2 * DC))

        @pl.when(i == 0)
        def _():
            for r in (dwa_ref, dwb_ref, dwc_ref):
                r[...] = jnp.zeros_like(r)

        def tconv(lo, hi, w, K, pad_ref, dw_ref, sh_x=None, sh_dy=None):
            dy = dcur[:, lo:hi]
            acc = None
            for k in range(K):
                term = _window(dpad, sh_dy, K - 1 - k, tt, slice(lo, hi)) * w[k:k + 1, :]
                acc = term if acc is None else acc + term
                dw_ref[pl.ds(k, 1), :] += _colsum(dy * _window(pad_ref, sh_x, HALO - (K - 1) + k, tt))
            return acc

        ds = tconv(0, DC, wa_ref[...], KA, pad_a, dwa_ref)
        du0 = tconv(DC, 2 * DC, wb_ref[...], KB, pad_b, dwb_ref, sh_b, sh_d)
        dqkv = tconv(2 * DC, CW, wc_ref[...], KC, pad_c, dwc_ref)
        a_c, a_v, b_a, b_g = pc[:, O_AC:O_AV], pc[:, O_AV:O_BA], pc[:, O_BA:O_BG], pc[:, O_BG:O_Q]
        sg = jax.nn.sigmoid(b_g)
        dp_ref[...] = jnp.concatenate([ds * a_v, ds * a_c, du0 * sg, du0 * b_a * sg * (1.0 - sg), dqkv], axis=1)

    cur, prev, _ = _mix_specs(T, tt)
    return _pcall(
        body, grid=(T // tt,), name=name, cargo=cargo, args=(proj, proj, dconv, dconv, wa, wb, wc),
        in_specs=[cur, prev, pl.BlockSpec((tt, CW), lambda i: (i, 0)),
                  pl.BlockSpec((HALO, CW), lambda i: (jnp.minimum((i + 1) * (tt // HALO), nblk - 1), 0)),
                  _full((8, DC)), _full((32, DC)), _full((8, 3 * DDN))],
        out_specs=[pl.BlockSpec((tt, O_Z - O_AC), lambda i: (i, 0)), _full((8, DC)), _full((32, DC)), _full((8, 3 * DDN))],
        out_shape=[_sds((T, O_Z - O_AC)), _sds((8, DC)), _sds((32, DC)), _sds((8, 3 * DDN))],
        scratch_shapes=[pltpu.VMEM((HALO + tt, DC), F32), pltpu.VMEM((HALO + tt, DC), F32), pltpu.VMEM((HALO + tt, 3 * DDN), F32),
                        pltpu.VMEM((tt + HALO, CW), F32), _shift_scratch(tt, DC), _shift_scratch(tt, DC)])


def _head(v, h):
    return v[:, h * HD:(h + 1) * HD]


def _to_batch(v, cb):
    return jnp.stack([v[c * CH:(c + 1) * CH, h * HD:(h + 1) * HD] for c in range(cb) for h in range(NH)])


def _from_batch(b, cb):
    return jnp.concatenate([jnp.concatenate([b[c * NH + h] for h in range(NH)], axis=1) for c in range(cb)], axis=0)


def _lanes_to_batch(v, cb, lane0):
    return jnp.stack([v[c * CH:(c + 1) * CH, lane0 + h:lane0 + h + 1] for c in range(cb) for h in range(NH)])


def _lane_onehot(h):
    return (lax.broadcasted_iota(jnp.int32, (1, HD), 1) == h).astype(F32)


def _batch_to_lanes(b, cb, lane0):
    return jnp.concatenate([sum(b[c * NH + h] * _lane_onehot(lane0 + h) for h in range(NH)) for c in range(cb)], axis=0)


def dn_intra_fwd(q, k, v, gb, grow3, *, name, cargo=None):
    T = q.shape[0]
    N = T // CH
    cb = 4 if N % 4 == 0 else 1

    def body(q_ref, k_ref, v_ref, gb_ref, gr_ref, u_ref, w_ref, qk_ref, p_ref):
        gbv = gb_ref[...]
        grow = jnp.stack([gr_ref[c, h:h + 1, :] for c in range(cb) for h in range(NH)])
        u, w, qk, p = _dn_intra(_to_batch(q_ref[...], cb), _to_batch(k_ref[...], cb), _to_batch(v_ref[...], cb),
                                _lanes_to_batch(gbv, cb, NH), _lanes_to_batch(gbv, cb, 0), grow)
        u_ref[...] = _from_batch(u, cb)
        w_ref[...] = _from_batch(w, cb)
        qk_ref[...] = qk.reshape(cb, NH, CH, CH)
        p_ref[...] = p.reshape(cb, NH, CH, CH)

    t512 = pl.BlockSpec((cb * CH, DDN), lambda i: (i, 0))
    sq = pl.BlockSpec((cb, NH, CH, CH), lambda i: (i, 0, 0, 0))
    return _pcall(
        body, grid=(N // cb,), name=name, cargo=cargo, args=(q, k, v, gb, grow3),
        in_specs=[t512, t512, t512, pl.BlockSpec((cb * CH, HD), lambda i: (i, 0)), pl.BlockSpec((cb, 8, CH), lambda i: (i, 0, 0))],
        out_specs=[t512, t512, sq, sq],
        out_shape=[_sds((T, DDN)), _sds((T, DDN)), _sds((N, NH, CH, CH)), _sds((N, NH, CH, CH))])


def dn_inter_fwd(q, k, u, w, gb, qk, *, name):
    T = q.shape[0]
    N = T // CH
    cb = 4 if N % 4 == 0 else 1

    def body(q_ref, k_ref, u_ref, w_ref, gb_ref, qk_ref, o_ref, s_ref, state):
        @pl.when(pl.program_id(0) == 0)
        def _():
            state[...] = jnp.zeros_like(state)
        for c in range(cb):
            rows = pl.ds(c * CH, CH)
            s = state[...]
            s_ref[c] = s
            s_new, o = _dn_inter(s, _to_batch(q_ref[rows, :], 1), _to_batch(k_ref[rows, :], 1), _to_batch(u_ref[rows, :], 1),
                                 _to_batch(w_ref[rows, :], 1), _lanes_to_batch(gb_ref[rows, :], 1, 0), qk_ref[c])
            state[...] = s_new
            o_ref[rows, :] = _from_batch(o, 1)

    t512 = pl.BlockSpec((cb * CH, DDN), lambda i: (i, 0))
    return pl.pallas_call(
        body, grid=(N // cb,), name=name,
        in_specs=[t512, t512, t512, t512, pl.BlockSpec((cb * CH, HD), lambda i: (i, 0)),
                  pl.BlockSpec((cb, NH, CH, CH), lambda i: (i, 0, 0, 0))],
        out_specs=[t512, pl.BlockSpec((cb, NH, HD, HD), lambda i: (i, 0, 0, 0))],
        out_shape=[_sds((T, DDN)), _sds((N, NH, HD, HD))],
        scratch_shapes=[pltpu.VMEM((NH, HD, HD), F32)], compiler_params=_cp(1))(q, k, u, w, gb, qk)


def dn_inter_bwd(do, q, k, u, w, gb, qk, s_all, *, name, cargo=None):
    T = q.shape[0]
    N = T // CH
    cb = 4 if N % 4 == 0 else 1
    G = N // cb

    def body(do_ref, q_ref, k_ref, u_ref, w_ref, gb_ref, qk_ref, s_ref, dq_ref, dk_ref, du_ref, dw_ref, dg_ref, dqk_ref, dstate):
        @pl.when(pl.program_id(0) == 0)
        def _():
            dstate[...] = jnp.zeros_like(dstate)
        for c in reversed(range(cb)):
            rows = pl.ds(c * CH, CH)
            _, vjp = jax.vjp(_dn_inter, s_ref[c], _to_batch(q_ref[rows, :], 1), _to_batch(k_ref[rows, :], 1),
                             _to_batch(u_ref[rows, :], 1), _to_batch(w_ref[rows, :], 1), _lanes_to_batch(gb_ref[rows, :], 1, 0),
                             qk_ref[c])
            ds, dq, dk, du, dw, dgc, dqk = vjp((dstate[...], _to_batch(do_ref[rows, :], 1)))
            dstate[...] = ds
            dq_ref[rows, :] = _from_batch(dq, 1)
            dk_ref[rows, :] = _from_batch(dk, 1)
            du_ref[rows, :] = _from_batch(du, 1)
            dw_ref[rows, :] = _from_batch(dw, 1)
            dg_ref[rows, :] = _batch_to_lanes(dgc, 1, 0)
            dqk_ref[c] = dqk

    t512 = pl.BlockSpec((cb * CH, DDN), lambda i: (G - 1 - i, 0))
    t128 = pl.BlockSpec((cb * CH, HD), lambda i: (G - 1 - i, 0))
    qkb = pl.BlockSpec((cb, NH, CH, CH), lambda i: (G - 1 - i, 0, 0, 0))
    return _pcall(
        body, grid=(G,), name=name, cargo=cargo, args=(do, q, k, u, w, gb, qk, s_all),
        in_specs=[t512, t512, t512, t512, t512, t128, qkb, pl.BlockSpec((cb, NH, HD, HD), lambda i: (G - 1 - i, 0, 0, 0))],
        out_specs=[t512, t512, t512, t512, t128, qkb],
        out_shape=[_sds((T, DDN))] * 4 + [_sds((T, HD)), _sds((N, NH, CH, CH))],
        scratch_shapes=[pltpu.VMEM((NH, HD, HD), F32)])


def dn_intra_bwd(du, dw, dqk, dq_in, dk_in, dg_in, q, k, v, gb, grow3, p_all, *, name, cargo=None):
    T = q.shape[0]
    N = T // CH
    cb = 4 if N % 4 == 0 else 1

    def body(du_ref, dw_ref, dqk_ref, dqi_ref, dki_ref, dgi_ref, q_ref, k_ref, v_ref, gb_ref, gr_ref, p_ref,
             dq_ref, dk_ref, dv_ref, dgb_ref, dgr_ref):
        B = cb * NH
        gbv = gb_ref[...]
        grow = jnp.stack([gr_ref[c, h:h + 1, :] for c in range(cb) for h in range(NH)])
        _, vjp = jax.vjp(functools.partial(_dn_intra, p_known=p_ref[...].reshape(B, CH, CH)),
                         _to_batch(q_ref[...], cb), _to_batch(k_ref[...], cb), _to_batch(v_ref[...], cb),
                         _lanes_to_batch(gbv, cb, NH), _lanes_to_batch(gbv, cb, 0), grow)
        dq, dk, dv, dbeta, dgc, dgr = vjp((_to_batch(du_ref[...], cb), _to_batch(dw_ref[...], cb), dqk_ref[...].reshape(B, CH, CH),
                                           jnp.zeros((B, CH, CH), F32)))
        dq_ref[...] = dqi_ref[...] + _from_batch(dq, cb)
        dk_ref[...] = dki_ref[...] + _from_batch(dk, cb)
        dv_ref[...] = _from_batch(dv, cb)
        dgb_ref[...] = dgi_ref[...] + _batch_to_lanes(dgc, cb, 0) + _batch_to_lanes(dbeta, cb, NH)
        for c in range(cb):
            dgr_ref[c] = jnp.concatenate([dgr[c * NH + h] for h in range(NH)] + [jnp.zeros((8 - NH, CH), F32)], axis=0)

    t512 = pl.BlockSpec((cb * CH, DDN), lambda i: (i, 0))
    t128 = pl.BlockSpec((cb * CH, HD), lambda i: (i, 0))
    qkb = pl.BlockSpec((cb, NH, CH, CH), lambda i: (i, 0, 0, 0))
    grb = pl.BlockSpec((cb, 8, CH), lambda i: (i, 0, 0))
    return _pcall(
        body, grid=(N // cb,), name=name, cargo=cargo, args=(du, dw, dqk, dq_in, dk_in, dg_in, q, k, v, gb, grow3, p_all),
        in_specs=[t512, t512, qkb, t512, t512, t128, t512, t512, t512, t128, grb, qkb],
        out_specs=[t512, t512, t512, t128, grb],
        out_shape=[_sds((T, DDN))] * 3 + [_sds((T, HD)), _sds((N, 8, CH))])


def _z_specs(tt):
    return [pl.BlockSpec((tt, DC), lambda i: (i, O_Z // DC)), pl.BlockSpec((tt, DC), lambda i: (i, O_Z // DC + 1))]


def mixout_fwd(x, yab, o, proj, gdn, g1, w_out, *, name, wl=0, cargo=None):
    T = x.shape[0]
    tt = _tile(T, 512)

    def body(x_ref, yab_ref, o_ref, z0_ref, z1_ref, gdn_ref, g1_ref, w_ref, ycat_ref, mix_ref, xo_ref):
        ov = o_ref[...]
        z = jnp.concatenate([z0_ref[...], z1_ref[...]], axis=1)
        yc = [_yc(_head(ov, h), _head(z, h), gdn_ref[...]) for h in range(NH)]
        ycat = jnp.concatenate([yab_ref[...]] + yc, axis=1).astype(BF16)
        ycat_ref[...] = ycat
        mix = _dot(ycat, w_ref[...])
        mix_ref[...] = mix
        xo_ref[...] = x_ref[...] + g1_ref[...] * mix

    tile = pl.BlockSpec((tt, D), lambda i: (i, 0))
    t512 = pl.BlockSpec((tt, DDN), lambda i: (i, 0))
    return _pcall(
        body, grid=(T // tt,), name=name, cargo=cargo, args=(x, yab, o, proj, proj, gdn, g1, _w3(w_out)),
        in_specs=[tile, t512, t512] + _z_specs(tt) + [_full((1, HD)), _full((1, D)), _wspec(wl, (D, D), lambda i: (0, 0))],
        out_specs=[tile, tile, tile], out_shape=[_sds((T, D), BF16), _sds((T, D)), _sds((T, D))])


def mixout_bwd(dx, mix, o, proj, gdn, g1, w_out, *, name, wl=0, cargo=None):
    T = dx.shape[0]
    tt = _tile(T, 256)

    def body(dx_ref, mix_ref, o_ref, z0_ref, z1_ref, gdn_ref, g1_ref, w_ref, dmix_ref, dyab_ref, do_ref, dz_ref, dg1_ref, dgdn_ref):
        i = pl.program_id(0)
        dxv = dx_ref[...]
        dmix = (dxv * g1_ref[...]).astype(BF16)
        dmix_ref[...] = dmix
        dycat = _dot_nt(dmix, w_ref[...])
        dyab_ref[...] = dycat[:, :2 * DC]
        ov = o_ref[...]
        z = jnp.concatenate([z0_ref[...], z1_ref[...]], axis=1)
        dos, dzs = [], []
        dgdn = jnp.zeros((1, HD), F32)
        for h in range(NH):
            _, vjp = jax.vjp(_yc, _head(ov, h), _head(z, h), gdn_ref[...])
            do, dz, dg = vjp(dycat[:, 2 * DC + h * HD:2 * DC + (h + 1) * HD])
            dos.append(do)
            dzs.append(dz)
            dgdn = dgdn + dg
        do_ref[...] = jnp.concatenate(dos, axis=1)
        dz_ref[...] = jnp.concatenate(dzs, axis=1)

        @pl.when(i == 0)
        def _():
            dg1_ref[...] = jnp.zeros_like(dg1_ref)
            dgdn_ref[...] = jnp.zeros_like(dgdn_ref)
        dg1_ref[...] += _colsum(dxv * mix_ref[...])
        dgdn_ref[...] += dgdn

    tile = pl.BlockSpec((tt, D), lambda i: (i, 0))
    t512 = pl.BlockSpec((tt, DDN), lambda i: (i, 0))
    return _pcall(
        body, grid=(T // tt,), name=name, cargo=cargo, args=(dx, mix, o, proj, proj, gdn, g1, _w3(w_out)),
        in_specs=[tile, tile, t512] + _z_specs(tt) + [_full((1, HD)), _full((1, D)), _wspec(wl, (D, D), lambda i: (0, 0))],
        out_specs=[tile, t512, t512, t512, _full((1, D)), _full((1, HD))],
        out_shape=[_sds((T, D), BF16), _sds((T, DDN)), _sds((T, DDN)), _sds((T, DDN)), _sds((1, D)), _sds((1, HD))])


FK = DFF // 2


def ffnout_fwd(x, gu, g2, w, *, name, wl=0, cargo=None):
    T = x.shape[0]
    tt = _tile(T, 512)

    def body(x_ref, gate_ref, up_ref, g2_ref, w_ref, act_ref, f_ref, xo_ref):
        kk = pl.program_id(1)
        act = (_silu(gate_ref[...]) * up_ref[...]).astype(BF16)
        act_ref[...] = act
        part = _dot(act, w_ref[...])

        @pl.when(kk == 0)
        def _():
            f_ref[...] = part

        @pl.when(kk == 1)
        def _():
            f = f_ref[...] + part
            f_ref[...] = f
            xo_ref[...] = x_ref[...] + g2_ref[...] * f

    tile = pl.BlockSpec((tt, D), lambda i, kk: (i, 0))
    return _pcall(
        body, grid=(T // tt, 2), name=name, cargo=cargo, args=(x, gu, gu, g2, _w3(w)),
        in_specs=[tile, pl.BlockSpec((tt, FK), lambda i, kk: (i, kk)), pl.BlockSpec((tt, FK), lambda i, kk: (i, 2 + kk)),
                  pl.BlockSpec((1, D), lambda i, kk: (0, 0)), _wspec(wl, (FK, D), lambda i, kk: (kk, 0))],
        out_specs=[pl.BlockSpec((tt, FK), lambda i, kk: (i, kk)), tile, tile],
        out_shape=[_sds((T, DFF), BF16), _sds((T, D)), _sds((T, D))])


def ffnout_bwd(dx, f, gu, g2, w, *, name, wl=0, cargo=None):
    T = dx.shape[0]
    tt = _tile(T, 512)

    def body(dx_ref, f_ref, gate_ref, up_ref, g2_ref, w_ref, df_ref, dgate_ref, dup_ref, dg2_ref):
        i, kk = pl.program_id(0), pl.program_id(1)
        dxv = dx_ref[...]
        df = (dxv * g2_ref[...]).astype(BF16)
        dact = _dot_nt(df, w_ref[...])
        gate, up = gate_ref[...], up_ref[...]
        sg = jax.nn.sigmoid(gate)
        dgate_ref[...] = (dact * up * (sg * (1.0 + gate * (1.0 - sg)))).astype(BF16)
        dup_ref[...] = (dact * (gate * sg)).astype(BF16)

        @pl.when(kk == 0)
        def _():
            df_ref[...] = df

        @pl.when((i == 0) & (kk == 0))
        def _():
            dg2_ref[...] = jnp.zeros_like(dg2_ref)

        @pl.when(kk == 0)
        def _():
            dg2_ref[...] += _colsum(dxv * f_ref[...])

    tile = pl.BlockSpec((tt, D), lambda i, kk: (i, 0))
    return _pcall(
        body, grid=(T // tt, 2), name=name, cargo=cargo, args=(dx, f, gu, gu, g2, _w3(w)),
        in_specs=[tile, tile, pl.BlockSpec((tt, FK), lambda i, kk: (i, kk)), pl.BlockSpec((tt, FK), lambda i, kk: (i, 2 + kk)),
                  pl.BlockSpec((1, D), lambda i, kk: (0, 0)), _wspec(wl, (FK, D), lambda i, kk: (kk, 0))],
        out_specs=[tile, pl.BlockSpec((tt, FK), lambda i, kk: (i, kk)), pl.BlockSpec((tt, FK), lambda i, kk: (i, kk)),
                   pl.BlockSpec((1, D), lambda i, kk: (0, 0))],
        out_shape=[_sds((T, D), BF16), _sds((T, DFF), BF16), _sds((T, DFF), BF16), _sds((1, D))])


def loss_head(x, target, gf, *, name):
    T = x.shape[0]
    tt = _tile(T, 256)

    def body(x_ref, t_ref, g_ref, loss_ref, dx_ref, dg_ref):
        i = pl.program_id(0)
        tv = t_ref[...]
        y, vjp = jax.vjp(_rms, x_ref[...], g_ref[...])
        err = y - tv
        dx, dg = vjp(err * (1.0 / D))
        dx_ref[...] = dx

        @pl.when(i == 0)
        def _():
            loss_ref[...] = jnp.zeros_like(loss_ref)
            dg_ref[...] = jnp.zeros_like(dg_ref)
        loss_ref[...] += 0.5 * jnp.sum(jnp.mean(err * err, axis=-1, keepdims=True), axis=0, keepdims=True)
        dg_ref[...] += dg

    tile = pl.BlockSpec((tt, D), lambda i: (i, 0))
    return pl.pallas_call(
        body, grid=(T // tt,), name=name, in_specs=[tile, tile, _full((1, D))],
        out_specs=[_full((1, HD)), tile, _full((1, D))], out_shape=[_sds((1, HD)), _sds((T, D)), _sds((1, D))],
        compiler_params=_cp(1))(x, target, gf)


def _pad_rows(a, rows):
    return jnp.pad(a, ((0, rows - a.shape[0]), (0, 0)))


def _row128(v):
    return jnp.pad(v, (0, HD - v.shape[0]))[None, :]


def _grow3(grow):
    return grow.reshape(8, -1, CH).transpose(1, 0, 2)


def _grow2(grow3):
    return grow3.transpose(1, 0, 2).reshape(8, -1)


def layer_params(l, mod, p):
    m = mod[l].reshape(6, 1, D)
    return dict(
        sh1=m[0], sc1=m[1], g1=m[2], sh2=m[3], sc2=m[4], g2=m[5],
        gn1=p["norm_mix_g"][l][None, :], gn2=p["norm_ffn_g"][l][None, :],
        wa=_pad_rows(p["conv_a_w"][l], 8), wb=_pad_rows(p["conf_dw_w"][l], 32), wc=_pad_rows(p["dn_conv_w"][l], 8),
        bb=p["conf_dw_b"][l][None, :], ln_g=p["conf_ln_g"][l][None, :], ln_b=p["conf_ln_b"][l][None, :],
        alog=_row128(p["dn_a_log"][l]), dt=_row128(p["dn_dt_bias"][l]), gdn=p["dn_norm_g"][l][None, :])


def layer_fwd(l, x, lp, w_in, w_out, w_ffn_in, w_ffn_out, wl=0, cargos=None):
    cg = (cargos or {}).get
    get = lambda w: w() if callable(w) else w
    s = {"x0": x}
    s["h1"], s["proj"] = nm_fwd(x, lp["gn1"], lp["sc1"], lp["sh1"], get(w_in), tn=NP // 3, name=f"proj_fwd_{l}", wl=wl,
                                cargo=cg("proj_fwd"))
    yab, s["q"], s["k"], s["v"], s["gb"], grow = mix_fwd(
        s["proj"], lp["wa"], lp["wb"], lp["bb"], lp["ln_g"], lp["ln_b"], lp["wc"], lp["alog"], lp["dt"], name=f"mix_fwd_{l}",
        cargo=cg("mix_fwd"))
    s["grow3"] = _grow3(grow)
    s["u"], s["w"], s["qk"], s["p"] = dn_intra_fwd(s["q"], s["k"], s["v"], s["gb"], s["grow3"], name=f"dn_intra_fwd_{l}",
                                                   cargo=cg("dn_intra_fwd"))
    s["o"], s["s_all"] = dn_inter_fwd(s["q"], s["k"], s["u"], s["w"], s["gb"], s["qk"], name=f"dn_inter_fwd_{l}")
    s["ycat"], s["mix"], s["x1"] = mixout_fwd(x, yab, s["o"], s["proj"], lp["gdn"], lp["g1"], get(w_out), name=f"mixout_fwd_{l}",
                                              wl=wl, cargo=cg("mixout_fwd"))
    s["h2"], s["gu"] = nm_fwd(s["x1"], lp["gn2"], lp["sc2"], lp["sh2"], get(w_ffn_in), tn=2 * DFF // 4, name=f"ffnin_fwd_{l}",
                              wl=wl, cargo=cg("ffnin_fwd"))
    s["act"], s["f"], x2 = ffnout_fwd(s["x1"], s["gu"], lp["g2"], get(w_ffn_out), name=f"ffnout_fwd_{l}", wl=wl,
                                      cargo=cg("ffnout_fwd"))
    return x2, s


def layer_bwd(l, dx2, s, lp, w_in, w_out, w_ffn_in, w_ffn_out, wl=0, carry=None):
    g = {}
    carry = carry or (lambda stage: None)
    df, dgate, dup, dg2 = ffnout_bwd(dx2, s["f"], s["gu"], lp["g2"], w_ffn_out, name=f"ffnout_bwd_{l}", wl=wl, cargo=carry("swap"))
    g["w_ffn_out"] = mm_tn_split(s["act"], df, by_cols=False, name=f"ffnout_dw_{l}")
    dx1, g["norm_ffn_g"], dsc2, dsh2, dgu = nm_bwd([dgate, dup], w_ffn_in, s["x1"], lp["gn2"], lp["sc2"], lp["sh2"], dx2,
                                                   name=f"ffnin_bwd_{l}", wl=wl, cargo=carry("exchange_c"))
    g["w_ffn_in"] = mm_tn_split(s["h2"], dgu, by_cols=True, name=f"ffnin_dw_{l}")
    dmix, dyab, do, dz, dg1, g["dn_norm_g"] = mixout_bwd(dx1, s["mix"], s["o"], s["proj"], lp["gdn"], lp["g1"], w_out,
                                                         name=f"mixout_bwd_{l}", wl=wl)
    g["w_out"] = mm_tn_split(s["ycat"], dmix, by_cols=False, name=f"mixout_dw_{l}")
    dq_i, dk_i, du, dw, dg_i, dqk = dn_inter_bwd(do, s["q"], s["k"], s["u"], s["w"], s["gb"], s["qk"], s["s_all"],
                                                 name=f"dn_inter_bwd_{l}")
    dq, dk, dv, dgb, dgrow3 = dn_intra_bwd(du, dw, dqk, dq_i, dk_i, dg_i, s["q"], s["k"], s["v"], s["gb"], s["grow3"],
                                           s["p"], name=f"dn_intra_bwd_{l}", cargo=carry("exchange_b"))
    dab, dconv, dblk, g["conf_dw_b"], g["conf_ln_g"], g["conf_ln_b"], dal, ddt = mix_bwd_point(
        s["proj"], dyab, dq, dk, dv, dgb, _grow2(dgrow3), lp["wa"], lp["wb"], lp["bb"], lp["ln_g"], lp["ln_b"], lp["wc"],
        lp["alog"], lp["dt"], name=f"mix_bwd_point_{l}", cargo=carry("exchange_a"))
    dpb, dwa, dwb, dwc = mix_bwd_conv(s["proj"], dconv, lp["wa"], lp["wb"], lp["wc"], name=f"mix_bwd_conv_{l}",
                                      cargo=carry("join"))
    dx0, g["norm_mix_g"], dsc1, dsh1, dproj = nm_bwd([dab, dpb, dz, dblk], w_in, s["x0"], lp["gn1"], lp["sc1"], lp["sh1"], dx1,
                                                     name=f"proj_bwd_{l}", wl=wl)
    g_in = mm_tn(s["h1"], dproj, tn=NP // 3, name=f"proj_dw_{l}")[:, :IN_COLS]
    g["w_in"] = g_in.reshape(2, D // 2, 4, IN_COLS // 4).transpose(0, 2, 1, 3)
    g["conv_a_w"], g["conf_dw_w"], g["dn_conv_w"] = dwa[:KA], dwb[:KB], dwc[:KC]
    g["dn_a_log"], g["dn_dt_bias"] = dal[0, :NH], ddt[0, :NH]
    g["mod"] = jnp.concatenate([dsh1, dsc1, dg1, dsh2, dsc2, dg2], axis=1)
    return dx0, g


EW_BLOCK_BYTES = 1 << 20


def _row_tile(R, C, mult=8):
    best = None
    for rt in range(mult, R + 1, mult):
        if R % rt == 0 and rt * C * 4 <= EW_BLOCK_BYTES:
            best = rt
    return best if best is not None else R


def add_half_bf16(g2, recv, core, *, name):
    _, B, R, C = g2.shape
    rt = _row_tile(R, C, 16)

    def body(core_ref, a_ref, b_ref, o_ref):
        o_ref[...] = (a_ref[...] + b_ref[...]).astype(BF16)

    spec = pl.BlockSpec((1, rt, C), lambda b, r, core_ref: (b, r, 0))
    return pl.pallas_call(
        body, name=name, out_shape=_sds((B, R, C), BF16), compiler_params=_cp(2),
        grid_spec=pltpu.PrefetchScalarGridSpec(
            num_scalar_prefetch=1, grid=(B, R // rt),
            in_specs=[pl.BlockSpec((None, 1, rt, C), lambda b, r, core_ref: (core_ref[0], b, r, 0)), spec],
            out_specs=spec))(core, g2, recv)


def adamw_halves(w, g_mine, g_theirs, core, m, v, *, name):
    L, R, C = w.shape
    rh = R // 2
    rt = _row_tile(rh, C)
    nr = rh // rt

    def body(core_ref, w_ref, gm_ref, gt_ref, m_ref, v_ref, go_ref, d_ref, mo_ref, vo_ref):
        g = jnp.where(pl.program_id(1) == core_ref[0], gm_ref[...], gt_ref[...])
        go_ref[...] = g
        d_ref[...], mo_ref[...], vo_ref[...] = _adamw_math(w_ref[...], g, m_ref[...], v_ref[...])

    spec = pl.BlockSpec((1, rt, C), lambda l, h, r, core_ref: (l, h * nr + r, 0))
    half = pl.BlockSpec((1, rt, C), lambda l, h, r, core_ref: (l, r, 0))
    return pl.pallas_call(
        body, name=name, out_shape=[_sds((L, R, C))] * 4, compiler_params=_cp(3),
        grid_spec=pltpu.PrefetchScalarGridSpec(num_scalar_prefetch=1, grid=(L, 2, nr), in_specs=[spec, half, half, spec, spec],
                                               out_specs=[spec] * 4))(core, w, g_mine, g_theirs, m, v)


def ew_call(fn, ins, n_out, *, name):
    B, R, C = ins[0].shape
    rt = _row_tile(R, C)
    n = len(ins)

    def body(*refs):
        outs = fn(*[r[...] for r in refs[:n]])
        for r, o in zip(refs[n:], outs):
            r[...] = o

    spec = pl.BlockSpec((1, rt, C), lambda b, r: (b, r, 0))
    return pl.pallas_call(body, grid=(B, R // rt), name=name, in_specs=[spec] * n, out_specs=[spec] * n_out,
                          out_shape=[_sds((B, R, C))] * n_out, compiler_params=_cp(2))(*ins)


def _adamw_math(w, g, m, v):
    m = ADAM_B1 * m + (1.0 - ADAM_B1) * g
    v = ADAM_B2 * v + (1.0 - ADAM_B2) * jnp.square(g)
    m_hat = m / (1.0 - ADAM_B1 ** ADAM_STEP)
    v_hat = v / (1.0 - ADAM_B2 ** ADAM_STEP)
    return -ADAM_LR * (m_hat / (jnp.sqrt(v_hat) + ADAM_EPS) + ADAM_WD * w), m, v


def adamw(w, g, m, v, *, name):
    shape = w.shape
    r3 = lambda a: a.reshape((-1,) + shape[-2:])
    return [o.reshape(shape) for o in ew_call(_adamw_math, [r3(w), r3(g), r3(m), r3(v)], 3, name=name)]


def sum_slots(a, *, name):
    S, B, R, C = a.shape
    rt = _row_tile(R, C, 16)

    def body(*refs):
        acc = refs[0][0, 0].astype(F32)
        for r in refs[1:S]:
            acc = acc + r[0, 0].astype(F32)
        refs[S][0] = acc

    def spec(s):
        return pl.BlockSpec((1, 1, rt, C), lambda b, r: (s, b, r, 0))
    return pl.pallas_call(body, grid=(B, R // rt), name=name, in_specs=[spec(s) for s in range(S)],
                          out_specs=pl.BlockSpec((1, rt, C), lambda b, r: (b, r, 0)), out_shape=_sds((B, R, C)),
                          compiler_params=_cp(2))(*([a] * S))


ADA_SH = 6 * D // 4
ADA_TN = 512


def mod_fwd(c_all, w_ada, b_my, *, name):
    L = w_ada.shape[0]

    def body(c_ref, w_ref, b_ref, o_ref):
        o_ref[0] = _dot(_silu(c_ref[...]).astype(BF16), w_ref[0].astype(BF16)) + b_ref[0]

    return pl.pallas_call(
        body, grid=(L, ADA_SH // ADA_TN), name=name,
        in_specs=[pl.BlockSpec((8, D), lambda l, j: (0, 0)), pl.BlockSpec((1, D, ADA_TN), lambda l, j: (l, 0, j)),
                  pl.BlockSpec((1, 1, ADA_TN), lambda l, j: (l, 0, j))],
        out_specs=pl.BlockSpec((1, 8, ADA_TN), lambda l, j: (l, 0, j)), out_shape=_sds((L, 8, ADA_SH)),
        compiler_params=_cp(2))(c_all, w_ada, b_my)


def wada_grad(c_all, dmod, *, name):
    L = dmod.shape[0]

    def body(c_ref, d_ref, o_ref):
        o_ref[0] = _dot_tn(_silu(c_ref[...]), d_ref[0], HI)

    return pl.pallas_call(
        body, grid=(L, ADA_SH // ADA_TN), name=name,
        in_specs=[pl.BlockSpec((8, D), lambda l, j: (0, 0)), pl.BlockSpec((1, 8, ADA_TN), lambda l, j: (l, 0, j))],
        out_specs=pl.BlockSpec((1, D, ADA_TN), lambda l, j: (l, 0, j)), out_shape=_sds((L, D, ADA_SH)),
        compiler_params=_cp(2))(c_all, dmod)


def _place():
    return lax.axis_index("x"), lax.axis_index("y"), lax.axis_index("c")


def _other_chips(x, y):
    return [(1 - x, y), (x, 1 - y), (1 - x, 1 - y)]


def allgather8(blocks, *, space, name):
    n = len(blocks)

    def body(*refs):
        ins, outs = refs[:n], refs[n:2 * n]
        send_sems, recv_sems, local_sems = refs[2 * n:]
        x, y, c = _place()
        me, sibling = (x, y, c), (x, y, 1 - c)
        chips = _other_chips(x, y)

        def slot(p):
            return 4 * p[0] + 2 * p[1] + p[2]

        def copy(a, k, block, to, src=None):
            dst = outs[a].at[slot(block)]
            return pltpu.make_async_remote_copy(src_ref=dst if src is None else src, dst_ref=dst, send_sem=send_sems.at[a, k],
                                                recv_sem=recv_sems.at[a, k], device_id=to, device_id_type=MESH)

        mine = [pltpu.make_async_copy(ins[a], outs[a].at[slot(me)], local_sems.at[a]) for a in range(n)]
        for cp in mine:
            cp.start()
        first = []
        for a in range(n):
            first.append(copy(a, 0, me, sibling, src=ins[a]))
            first += [copy(a, 1 + j, me, (*chip, c), src=ins[a]) for j, chip in enumerate(chips)]
        for cp in first:
            cp.start()
        passed = []
        for j, chip in enumerate(chips):
            for a in range(n):
                copy(a, 1 + j, (*chip, c), me).wait_recv()
                cp = copy(a, 4 + j, (*chip, c), sibling)
                cp.start()
                passed.append(cp)
        for a in range(n):
            copy(a, 0, sibling, me).wait_recv()
            for j, chip in enumerate(chips):
                copy(a, 4 + j, (*chip, 1 - c), me).wait_recv()
        for cp in first + passed:
            cp.wait_send()
        for cp in mine:
            cp.wait()

    spec = pl.BlockSpec(memory_space=space)
    return pl.pallas_call(
        body, name=name, in_specs=[spec] * n, out_specs=[spec] * n,
        out_shape=[_sds((8,) + b.shape, b.dtype) for b in blocks],
        scratch_shapes=[pltpu.SemaphoreType.DMA((n, 7)), pltpu.SemaphoreType.DMA((n, 7)), pltpu.SemaphoreType.DMA((n,))],
        compiler_params=pltpu.CompilerParams(vmem_limit_bytes=VMEM_LIMIT))(*blocks)


def sibling_swap(blocks, *, name, slotted=True):
    n = len(blocks)

    def body(*refs):
        ins, outs = refs[:n], refs[n:2 * n]
        send_sems, recv_sems = refs[2 * n:]
        x, y, c = _place()
        cps = [pltpu.make_async_remote_copy(src_ref=ins[a].at[1 - c] if slotted else ins[a], dst_ref=outs[a],
                                            send_sem=send_sems.at[a], recv_sem=recv_sems.at[a], device_id=(x, y, 1 - c),
                                            device_id_type=MESH)
               for a in range(n)]
        for cp in cps:
            cp.start()
        for cp in cps:
            cp.wait()

    spec = pl.BlockSpec(memory_space=pl.ANY)
    return pl.pallas_call(
        body, name=name, in_specs=[spec] * n, out_specs=[spec] * n,
        out_shape=[_sds(b.shape[1:] if slotted else b.shape, b.dtype) for b in blocks],
        scratch_shapes=[pltpu.SemaphoreType.DMA((n,)), pltpu.SemaphoreType.DMA((n,))])(*blocks)


def chip_exchange(blocks, *, name):
    n = len(blocks)

    def body(*refs):
        ins, outs = refs[:n], refs[n:2 * n]
        send_sems, recv_sems, local_sems = refs[2 * n:]
        x, y, c = _place()
        me = 2 * x + y
        chips = _other_chips(x, y)
        mine = [pltpu.make_async_copy(ins[a].at[me], outs[a].at[me], local_sems.at[a]) for a in range(n)]
        for cp in mine:
            cp.start()
        cps = []
        for a in range(n):
            for j, chip in enumerate(chips):
                cps.append(pltpu.make_async_remote_copy(
                    src_ref=ins[a].at[2 * chip[0] + chip[1]], dst_ref=outs[a].at[me], send_sem=send_sems.at[a, j],
                    recv_sem=recv_sems.at[a, j], device_id=(*chip, c), device_id_type=MESH))
        for cp in cps:
            cp.start()
        for a in range(n):
            for j, chip in enumerate(chips):
                s = 2 * chip[0] + chip[1]
                pltpu.make_async_remote_copy(src_ref=ins[a].at[s], dst_ref=outs[a].at[s], send_sem=send_sems.at[a, j],
                                             recv_sem=recv_sems.at[a, j], device_id=(*chip, c), device_id_type=MESH).wait_recv()
        for cp in cps:
            cp.wait_send()
        for cp in mine:
            cp.wait()

    spec = pl.BlockSpec(memory_space=pl.ANY)
    return pl.pallas_call(
        body, name=name, in_specs=[spec] * n, out_specs=[spec] * n, out_shape=[_sds(b.shape, b.dtype) for b in blocks],
        scratch_shapes=[pltpu.SemaphoreType.DMA((n, 3)), pltpu.SemaphoreType.DMA((n, 3)), pltpu.SemaphoreType.DMA((n,))])(*blocks)


def halves_join(halves, *, name):
    n = len(halves)

    def body(*refs):
        ins, outs = refs[:n], refs[n:2 * n]
        send_sems, recv_sems, local_sems = refs[2 * n:]
        x, y, c = _place()
        cps, mine = [], []
        for a in range(n):
            mine.append(pltpu.make_async_copy(ins[a], outs[a].at[c], local_sems.at[a]))
            cps.append(pltpu.make_async_remote_copy(src_ref=ins[a], dst_ref=outs[a].at[c], send_sem=send_sems.at[a],
                                                    recv_sem=recv_sems.at[a], device_id=(x, y, 1 - c), device_id_type=MESH))
        for cp in mine + cps:
            cp.start()
        for a in range(n):
            pltpu.make_async_remote_copy(src_ref=ins[a], dst_ref=outs[a].at[1 - c], send_sem=send_sems.at[a], recv_sem=recv_sems.at[a],
                                         device_id=(x, y, 1 - c), device_id_type=MESH).wait_recv()
        for cp in cps:
            cp.wait_send()
        for cp in mine:
            cp.wait()

    spec = pl.BlockSpec(memory_space=pl.ANY)
    return pl.pallas_call(
        body, name=name, in_specs=[spec] * n, out_specs=[spec] * n,
        out_shape=[_sds((2,) + h.shape, h.dtype) for h in halves],
        scratch_shapes=[pltpu.SemaphoreType.DMA((n,)), pltpu.SemaphoreType.DMA((n,)), pltpu.SemaphoreType.DMA((n,))])(*halves)


BIG = ("w_in", "w_out", "w_ffn_in", "w_ffn_out")
COL_SHARDED = {"w_in": True, "w_out": False, "w_ffn_in": True, "w_ffn_out": False}
SMALL = ("norm_mix_g", "norm_ffn_g", "conv_a_w", "conf_dw_w", "conf_dw_b", "conf_ln_g", "conf_ln_b", "dn_conv_w",
         "dn_a_log", "dn_dt_bias", "dn_norm_g")
SMALL_SHARDED = ("conv_a_w", "conf_dw_w", "dn_conv_w")


def _half_rows(a, c):
    rh = a.shape[1] // 2
    return lax.dynamic_slice_in_dim(a, c * rh, rh, axis=1)


def _assemble(name, g):
    _, L, rh, C = g.shape
    g = g.reshape(4, 2, L, rh, C)
    if COL_SHARDED[name]:
        return g.transpose(2, 1, 3, 0, 4).reshape(L, 2 * rh, 4 * C)
    return g.transpose(2, 0, 1, 3, 4).reshape(L, 8 * rh, C)


def _split_for_reduce(name, g, c):
    L, R, C = g.shape
    if COL_SHARDED[name]:
        t = g.reshape(L, 2, R // 2, 4, C // 4).transpose(1, 3, 0, 2, 4)
    else:
        t = g.reshape(L, 4, 2, R // 8, C).transpose(2, 1, 0, 3, 4)
    mine = lax.dynamic_index_in_dim(t, c, 0, keepdims=False)
    other = lax.dynamic_index_in_dim(t, 1 - c, 0, keepdims=False)
    return mine, other


def _pack(parts):
    flat = jnp.concatenate([p.reshape(-1) for p in parts])
    n = flat.shape[0]
    rows = -(-n // (8 * HD)) * 8
    return jnp.pad(flat, (0, rows * HD - n)).reshape(rows, HD)


def _unpack(buf, shapes):
    flat = buf.reshape(-1)
    out, off = [], 0
    for s in shapes:
        n = 1
        for d in s:
            n *= d
        out.append(flat[off:off + n].reshape(s))
        off += n
    return out


class _Reduction:
    STAGES = ("swap", "exchange_a", "exchange_b", "exchange_c", "join")
    EXCHANGED = {"exchange_a": ("w_ffn_in",), "exchange_b": ("w_in",), "exchange_c": ("w_ffn_out", "w_out")}

    def __init__(self, l, split, core):
        self.l, self.split, self.core = l, split, core
        self.cargos = {}
        self.chip_sum = None

    def stage(self, name):
        l = self.l
        if name == "swap":
            cargo = swap_cargo([self.split[n] for n in BIG], slotted=True)
        elif name in self.EXCHANGED:
            if self.chip_sum is None:
                self.chip_sum = {n: add_half_bf16(self.split[n], r, self.core, name=f"reduce_add2_{n}_{l}")
                                 for n, r in zip(BIG, self.cargos["swap"].results)}
            cargo = exchange_cargo([self.chip_sum[n] for n in self.EXCHANGED[name]])
        else:
            from_chips = {n: r for st, ns in self.EXCHANGED.items() for n, r in zip(ns, self.cargos[st].results)}
            self.mine = {n: sum_slots(from_chips[n][:, None], name=f"reduce_add4_{n}_{l}")[0] for n in BIG}
            cargo = swap_cargo([self.mine[n] for n in BIG], slotted=False)
        self.cargos[name] = cargo
        return cargo

    def finish(self):
        return self.mine, dict(zip(BIG, self.cargos["join"].results))


def kernel(x, c, w_ada, b_ada, norm_mix_g, norm_ffn_g, w_in, conv_a_w, conf_dw_w, conf_dw_b, conf_ln_g, conf_ln_b, dn_conv_w, dn_a_log, dn_dt_bias, dn_norm_g, w_out, w_ffn_in, w_ffn_out, final_norm_g, loss_target, m_w_ada, m_b_ada, m_norm_mix_g, m_norm_ffn_g, m_w_in, m_conv_a_w, m_conf_dw_w, m_conf_dw_b, m_conf_ln_g, m_conf_ln_b, m_dn_conv_w, m_dn_a_log, m_dn_dt_bias, m_dn_norm_g, m_w_out, m_w_ffn_in, m_w_ffn_out, m_final_norm_g, v_w_ada, v_b_ada, v_norm_mix_g, v_norm_ffn_g, v_w_in, v_conv_a_w, v_conf_dw_w, v_conf_dw_b, v_conf_ln_g, v_conf_ln_b, v_dn_conv_w, v_dn_a_log, v_dn_dt_bias, v_dn_norm_g, v_w_out, v_w_ffn_in, v_w_ffn_out, v_final_norm_g):
    W = dict(w_ada=w_ada, b_ada=b_ada, norm_mix_g=norm_mix_g, norm_ffn_g=norm_ffn_g, w_in=w_in, conv_a_w=conv_a_w,
             conf_dw_w=conf_dw_w, conf_dw_b=conf_dw_b, conf_ln_g=conf_ln_g, conf_ln_b=conf_ln_b, dn_conv_w=dn_conv_w,
             dn_a_log=dn_a_log, dn_dt_bias=dn_dt_bias, dn_norm_g=dn_norm_g, w_out=w_out, w_ffn_in=w_ffn_in,
             w_ffn_out=w_ffn_out, final_norm_g=final_norm_g)
    M = dict(w_ada=m_w_ada, b_ada=m_b_ada, norm_mix_g=m_norm_mix_g, norm_ffn_g=m_norm_ffn_g, w_in=m_w_in, conv_a_w=m_conv_a_w,
             conf_dw_w=m_conf_dw_w, conf_dw_b=m_conf_dw_b, conf_ln_g=m_conf_ln_g, conf_ln_b=m_conf_ln_b, dn_conv_w=m_dn_conv_w,
             dn_a_log=m_dn_a_log, dn_dt_bias=m_dn_dt_bias, dn_norm_g=m_dn_norm_g, w_out=m_w_out, w_ffn_in=m_w_ffn_in,
             w_ffn_out=m_w_ffn_out, final_norm_g=m_final_norm_g)
    V = dict(w_ada=v_w_ada, b_ada=v_b_ada, norm_mix_g=v_norm_mix_g, norm_ffn_g=v_norm_ffn_g, w_in=v_w_in, conv_a_w=v_conv_a_w,
             conf_dw_w=v_conf_dw_w, conf_dw_b=v_conf_dw_b, conf_ln_g=v_conf_ln_g, conf_ln_b=v_conf_ln_b, dn_conv_w=v_dn_conv_w,
             dn_a_log=v_dn_a_log, dn_dt_bias=v_dn_dt_bias, dn_norm_g=v_dn_norm_g, w_out=v_w_out, w_ffn_in=v_w_ffn_in,
             w_ffn_out=v_w_ffn_out, final_norm_g=v_final_norm_g)
    L = w_ada.shape[0]
    ax, ay, ac = _place()
    chip = 2 * ax + ay
    dev = 4 * ax + 2 * ay + ac

    c_all = allgather8([jnp.pad(c, ((0, 7), (0, 0)))], space=pltpu.VMEM, name="gather_c")[0][:, 0, :]
    b_my = lax.dynamic_slice_in_dim(b_ada, chip * ADA_SH, ADA_SH, axis=1)[:, None, :]
    mod_sh = mod_fwd(c_all, w_ada, b_my, name="mod_fwd")
    lh = L // 2
    mod_g = allgather8([lax.dynamic_slice_in_dim(mod_sh, ac * lh, lh, axis=0).reshape(lh * 8, ADA_SH)], space=pltpu.VMEM,
                       name="gather_mod")[0]
    mod_all = mod_g.reshape(4, 2, lh, 8, ADA_SH).transpose(1, 2, 3, 0, 4).reshape(L, 8, 6 * D)
    mod = lax.dynamic_index_in_dim(mod_all, dev, 1, keepdims=False)

    src = {n: _half_rows(W[n], ac).astype(BF16) for n in BIG}
    arrived = {("w_in", 0): _comm_call(gather_cargo([src["w_in"][0]]), "gather_w0")[0]}
    carried = {}
    full = {}

    def weight(n, l):
        if (n, l) not in full:
            if (n, l) not in arrived:
                cargo, i = carried[(n, l)]
                arrived[(n, l)] = cargo.results[i]
            w = _assemble(n, arrived[(n, l)][:, None])[0]
            full[(n, l)] = jnp.pad(w, ((0, 0), (0, NP - IN_COLS))) if n == "w_in" else w
        return full[(n, l)]

    def gather_plan(l):
        plan = {}
        if l == 0:
            plan["proj_fwd"] = [("w_out", 0), ("w_ffn_out", 0)]
            plan["dn_intra_fwd"] = [("w_ffn_in", 0)]
        if l + 1 < L:
            plan["mix_fwd" if l == 0 else "proj_fwd"] = [("w_in", l + 1)]
            plan["mixout_fwd"] = [("w_out", l + 1)]
            plan["ffnin_fwd"] = [("w_ffn_in", l + 1)]
            plan["ffnout_fwd"] = [("w_ffn_out", l + 1)]
        return plan

    p_full = dict(W)
    sm = allgather8([_pack([W[n] for n in SMALL_SHARDED])], space=pltpu.VMEM, name="gather_convw")[0]
    per_chip = [_unpack(sm[4 * (s // 2) + 2 * (s % 2)], [W[n].shape for n in SMALL_SHARDED]) for s in range(4)]
    for i, n in enumerate(SMALL_SHARDED):
        p_full[n] = jnp.concatenate([per_chip[s][i] for s in range(4)], axis=-1)

    xs = x[0]
    saves, lps = [], []
    for l in range(L):
        lp = layer_params(l, mod, p_full)
        cargos = {}
        for carrier, pieces in gather_plan(l).items():
            cargos[carrier] = gather_cargo([src[n][ll] for n, ll in pieces])
            for i, piece in enumerate(pieces):
                carried[piece] = (cargos[carrier], i)
        xs, s = layer_fwd(l, xs, lp, *[functools.partial(weight, n, l) for n in BIG], cargos=cargos)
        saves.append(s)
        lps.append(lp)
    weights = [[weight(n, l) for n in BIG] for l in range(L)]
    loss_p, dx, dgf = loss_head(xs, loss_target[0], final_norm_g[None, :], name="loss_head")
    core = ac.astype(jnp.int32).reshape(1)
    grads = [None] * L
    reduced = [None] * L
    pending = None
    for l in reversed(range(L)):
        dx, grads[l] = layer_bwd(l, dx, saves[l], lps[l], *weights[l], carry=None if pending is None else pending.stage)
        if pending is not None:
            reduced[pending.l] = pending.finish()
        pending = _Reduction(l, {n: grads[l][n] for n in BIG}, core)
    for stage in _Reduction.STAGES:
        cargo = pending.stage(stage)
        cargo.results = _comm_call(cargo, f"reduce_{stage}_{pending.l}")
    reduced[pending.l] = pending.finish()
    loss = lax.psum(loss_p[0, 0], ("x", "y", "c"))
    grad_x = dx[None]

    small_shapes = [(L,) + p_full[n].shape[1:] for n in SMALL]
    parts = [jnp.stack([grads[l][n].reshape(sh[1:]) for l in range(L)]) for n, sh in zip(SMALL, small_shapes)]
    parts += [dgf.reshape(D), jnp.concatenate([grads[l]["mod"] for l in range(L)], axis=0)]
    small_shapes += [(D,), (L, 6 * D)]
    packed = _pack(parts)
    gathered_small = allgather8([packed], space=pltpu.VMEM, name="gather_small")[0]
    summed = sum_slots(gathered_small[:, None], name="sum_small")[0]
    g_small = dict(zip(SMALL + ("final_norm_g", "b_ada"), _unpack(summed, small_shapes)))
    for n in SMALL_SHARDED:
        sw = W[n].shape[-1]
        g_small[n] = lax.dynamic_slice_in_dim(g_small[n], chip * sw, sw, axis=g_small[n].ndim - 1)
    dmod_all = jnp.stack([_unpack(gathered_small[d], small_shapes)[-1] for d in range(8)], axis=1)
    dmod_my = lax.dynamic_slice_in_dim(dmod_all, chip * ADA_SH, ADA_SH, axis=2)
    g_w_ada = wada_grad(c_all, dmod_my, name="wada_grad")

    halves = [jnp.stack([reduced[l][0][n] for l in range(L)]) for n in BIG]
    theirs = [jnp.stack([reduced[l][1][n] for l in range(L)]) for n in BIG]

    out_g, out_d, out_m, out_v = {}, {}, {}, {}
    out_g["w_ada"] = g_w_ada
    out_d["w_ada"], out_m["w_ada"], out_v["w_ada"] = adamw(W["w_ada"], g_w_ada, M["w_ada"], V["w_ada"], name="adamw_w_ada")
    for n, g_mine, g_theirs in zip(BIG, halves, theirs):
        out_g[n], out_d[n], out_m[n], out_v[n] = adamw_halves(W[n], g_mine, g_theirs, core, M[n], V[n], name=f"adamw_{n}")
    names_small = SMALL + ("final_norm_g", "b_ada")
    pk = lambda d: _pack([d[n] for n in names_small])[None]
    d_s, m_s, v_s = ew_call(_adamw_math, [pk(W), pk(g_small), pk(M), pk(V)], 3, name="adamw_small")
    shapes_s = [W[n].shape for n in names_small]
    for d, o in ((out_d, d_s), (out_m, m_s), (out_v, v_s)):
        d.update(zip(names_small, _unpack(o[0], shapes_s)))
    for n in names_small:
        out_g[n] = g_small[n].reshape(W[n].shape)

    order = ("w_ada", "b_ada", "norm_mix_g", "norm_ffn_g", "w_in", "conv_a_w", "conf_dw_w", "conf_dw_b", "conf_ln_g", "conf_ln_b",
             "dn_conv_w", "dn_a_log", "dn_dt_bias", "dn_norm_g", "w_out", "w_ffn_in", "w_ffn_out", "final_norm_g")
    return (loss, grad_x, *[out_g[n] for n in order], *[out_d[n] for n in order], *[out_m[n] for n in order],
            *[out_v[n] for n in order])
```

```python
import functools

import jax
import jax.numpy as jnp
from jax import lax
from jax.experimental import pallas as pl
from jax.experimental.pallas import tpu as pltpu

F32 = jnp.float32
BF16 = jnp.bfloat16
HI = lax.Precision.HIGHEST
MESH = pl.DeviceIdType.MESH

D = 1024
DEPTH = 4
DC = 256
DDN = 512
NH = 4
HD = 128
CH = 64
DFF = 2816
IN_COLS = 3336
NP = 3456
KA, KB, KC = 3, 31, 4
HALO = 32
EPS = 1e-6
O_AB, O_AC, O_AV, O_BA, O_BG, O_Q, O_K, O_V, O_Z, O_GB = 0, 256, 512, 768, 1024, 1280, 1792, 2304, 2816, 3328

ADAM_LR, ADAM_B1, ADAM_B2, ADAM_EPS, ADAM_WD, ADAM_STEP = 0.001, 0.9, 0.999, 1e-08, 0.01, 10

VMEM_LIMIT = 56 * 1024 * 1024


def _cp(n_grid):
    return pltpu.CompilerParams(dimension_semantics=("arbitrary",) * n_grid, vmem_limit_bytes=VMEM_LIMIT)


def _sds(shape, dtype=F32):
    return jax.ShapeDtypeStruct(tuple(shape), dtype)


def _dot(a, b, prec=None):
    return jnp.dot(a, b, preferred_element_type=F32, precision=prec)


def _dot_nt(a, b, prec=None):
    return lax.dot_general(a, b, (((1,), (1,)), ((), ())), preferred_element_type=F32, precision=prec)


def _dot_tn(a, b, prec=None):
    return lax.dot_general(a, b, (((0,), (0,)), ((), ())), preferred_element_type=F32, precision=prec)


def _split_bf16(a):
    hi = a.astype(BF16)
    return hi, (a - hi.astype(F32)).astype(BF16)


_DIMS = {"nn": (((1,), (0,)), ((), ())), "nt": (((1,), (1,)), ((), ())), "tn": (((0,), (0,)), ((), ()))}
_DIMS_BATCHED = {"nn": (((2,), (1,)), ((0,), (0,))), "nt": (((2,), (2,)), ((0,), (0,))), "tn": (((1,), (1,)), ((0,), (0,)))}


def _mm3_raw(a, b, kind):
    ah, al = _split_bf16(a)
    bh, bl = _split_bf16(b)
    d = (_DIMS if a.ndim == 2 else _DIMS_BATCHED)[kind]
    dg = lambda u, v: lax.dot_general(u, v, d, preferred_element_type=F32)
    return dg(ah, bh) + (dg(ah, bl) + dg(al, bh))


@jax.custom_vjp
def _mm_nn(a, b):
    return _mm3_raw(a, b, "nn")


@jax.custom_vjp
def _mm_nt(a, b):
    return _mm3_raw(a, b, "nt")


@jax.custom_vjp
def _mm_tn(a, b):
    return _mm3_raw(a, b, "tn")


_mm_nn.defvjp(lambda a, b: (_mm_nn(a, b), (a, b)), lambda r, g: (_mm_nt(g, r[1]), _mm_tn(r[0], g)))
_mm_nt.defvjp(lambda a, b: (_mm_nt(a, b), (a, b)), lambda r, g: (_mm_nn(g, r[1]), _mm_tn(g, r[0])))
_mm_tn.defvjp(lambda a, b: (_mm_tn(a, b), (a, b)), lambda r, g: (_mm_nt(r[1], g), _mm_nn(r[0], g)))


def _mm1_raw(a, b, kind):
    d = (_DIMS if a.ndim == 2 else _DIMS_BATCHED)[kind]
    return lax.dot_general(a.astype(BF16), b.astype(BF16), d, preferred_element_type=F32)


@jax.custom_vjp
def _mm1_nn(a, b):
    return _mm1_raw(a, b, "nn")


@jax.custom_vjp
def _mm1_nt(a, b):
    return _mm1_raw(a, b, "nt")


@jax.custom_vjp
def _mm1_tn(a, b):
    return _mm1_raw(a, b, "tn")


_mm1_nn.defvjp(lambda a, b: (_mm1_nn(a, b), (a, b)), lambda r, g: (_mm1_nt(g, r[1]), _mm1_tn(r[0], g)))
_mm1_nt.defvjp(lambda a, b: (_mm1_nt(a, b), (a, b)), lambda r, g: (_mm1_nn(g, r[1]), _mm1_tn(g, r[0])))
_mm1_tn.defvjp(lambda a, b: (_mm1_tn(a, b), (a, b)), lambda r, g: (_mm1_nt(r[1], g), _mm1_nn(r[0], g)))


@jax.custom_vjp
def _inv_given(x, p):
    return p


_inv_given.defvjp(lambda x, p: (p, p), lambda p, g: (_mm_tn(p, _mm_nt(g, p)), jnp.zeros_like(p)))


def _silu(x):
    return x * jax.nn.sigmoid(x)


def _colsum(x):
    return jnp.sum(x, axis=0, keepdims=True)


TILE_CAP = 512


def _w3(w):
    return w if w.ndim == 3 else w[None]


def _wspec(l, block, index):
    return pl.BlockSpec((None,) + block, lambda *g: (l,) + index(*g))


def _tile(T, want):
    t = min(T, want, TILE_CAP)
    assert T % t == 0
    return t


def _normmod(x, gn, sc, sh):
    r = lax.rsqrt(jnp.mean(x * x, axis=-1, keepdims=True) + EPS)
    return ((x * r) * gn) * (1.0 + sc) + sh


def _rms(x, g):
    return (x * lax.rsqrt(jnp.mean(x * x, axis=-1, keepdims=True) + EPS)) * g


def _mix_b_post(u, ln_g, ln_b):
    mu = jnp.mean(u, axis=-1, keepdims=True)
    var = jnp.mean(jnp.square(u - mu), axis=-1, keepdims=True)
    return _silu(((u - mu) * lax.rsqrt(var + 1e-5)) * ln_g + ln_b)


def _softplus(z):
    return jnp.where(z > 0, z, 0.0) + jnp.log(1.0 + jnp.exp(-jnp.where(z > 0, z, -z)))


def _chunk_tril(tt):
    r = lax.broadcasted_iota(jnp.int32, (tt, tt), 0)
    c = lax.broadcasted_iota(jnp.int32, (tt, tt), 1)
    return ((r // CH == c // CH) & (c <= r)).astype(F32)


def _eye8():
    return (lax.broadcasted_iota(jnp.int32, (8, HD), 0) == lax.broadcasted_iota(jnp.int32, (8, HD), 1)).astype(F32)


def _dn_post(pre_q, pre_k, pre_v, blk, alog_row, dt_row):
    q = [s * lax.rsqrt(jnp.sum(s * s, -1, keepdims=True) + EPS) * (HD ** -0.5) for s in map(_silu, pre_q)]
    k = [s * lax.rsqrt(jnp.sum(s * s, -1, keepdims=True) + EPS) for s in map(_silu, pre_k)]
    v = [_silu(p) for p in pre_v]
    lane = lax.broadcasted_iota(jnp.int32, (1, HD), 1)
    g = -jnp.exp(alog_row) * _softplus(blk + dt_row)
    beta = jax.nn.sigmoid(blk)
    gc = _dot(_chunk_tril(blk.shape[0]), jnp.where(lane < NH, g, 0.0), HI)
    gb = jnp.where(lane < NH, gc, jnp.where(lane < 2 * NH, beta, 0.0))
    grow = _dot_nt(_eye8(), gc, HI)
    return q, k, v, gb, grow


def _dn_intra(q, k, v, beta, gcol, grow, p_known=None):
    r = lax.broadcasted_iota(jnp.int32, (CH, CH), 0)
    c = lax.broadcasted_iota(jnp.int32, (CH, CH), 1)
    causal, strict = c <= r, c < r
    decay = jnp.where(causal, jnp.exp(jnp.where(causal, gcol - grow, 0.0)), 0.0)
    kb = k * beta
    x = -jnp.where(strict, _mm1_nt(kb, k) * decay, 0.0)
    if p_known is None:
        p = (r == c).astype(F32) + x
        y = x
        for _ in range(5):
            y = _mm_nn(y, y)
            p = p + _mm_nn(p, y)
    else:
        p = _inv_given(x, p_known)
    u = _mm1_nn(p, v * beta)
    w = _mm1_nn(p, kb * jnp.exp(gcol))
    qk = jnp.where(causal, _mm1_nt(q, k) * decay, 0.0)
    return u, w, qk, p


def _dn_inter(s, q, k, u, w, gcol, qk):
    last = (lax.broadcasted_iota(jnp.int32, (CH, 1), 0) == CH - 1).astype(F32)
    g_last = jnp.sum(gcol * last, axis=1, keepdims=True)
    v_new = u - _mm1_nn(w, s)
    o = _mm1_nn(q * jnp.exp(gcol), s) + _mm1_nn(qk, v_new)
    s_new = s * jnp.exp(g_last) + _mm1_tn(k * jnp.exp(g_last - gcol), v_new)
    return s_new, o


def _yc(o, z, gdn):
    return _rms(o, gdn) * _silu(z)


class Cargo:
    def __init__(self, ins, outs, sems, first, last, middle=None):
        self.ins, self.outs, self.sems = list(ins), list(outs), list(sems)
        self.first, self.middle, self.last = first, middle, last
        self.results = None


def _pcall(body, *, grid, in_specs, out_specs, out_shape, args, name, scratch_shapes=(), cargo=None):
    if cargo is None:
        return pl.pallas_call(body, grid=grid, in_specs=list(in_specs), out_specs=list(out_specs), out_shape=list(out_shape),
                              scratch_shapes=list(scratch_shapes), name=name, compiler_params=_cp(len(grid)))(*args)
    n_in, n_out, n_scr = len(args), len(out_shape), len(scratch_shapes)
    k_in, k_out = len(cargo.ins), len(cargo.outs)
    total = 1
    for g in grid:
        total *= g

    def carrying(*refs):
        ins, cins = refs[:n_in], refs[n_in:n_in + k_in]
        o0 = n_in + k_in
        outs, couts = refs[o0:o0 + n_out], refs[o0 + n_out:o0 + n_out + k_out]
        rest = refs[o0 + n_out + k_out:]
        scr, csems = rest[:n_scr], rest[n_scr:]
        step = pl.program_id(0)
        for a in range(1, len(grid)):
            step = step * grid[a] + pl.program_id(a)

        @pl.when(step == 0)
        def _():
            cargo.first(cins, couts, csems)
        if cargo.middle is not None:
            @pl.when(step == (total * 7) // 8)
            def _():
                cargo.middle(cins, couts, csems)
        body(*ins, *outs, *scr)

        @pl.when(step == total - 1)
        def _():
            cargo.last(cins, couts, csems)

    hbm = pl.BlockSpec(memory_space=pl.ANY)
    res = pl.pallas_call(
        carrying, grid=grid, in_specs=list(in_specs) + [hbm] * k_in, out_specs=list(out_specs) + [hbm] * k_out,
        out_shape=list(out_shape) + cargo.outs, scratch_shapes=list(scratch_shapes) + cargo.sems, name=name,
        compiler_params=_cp(len(grid)))(*args, *cargo.ins)
    cargo.results = list(res[n_out:])
    return list(res[:n_out])


def _place():
    return lax.axis_index("x"), lax.axis_index("y"), lax.axis_index("c")


def _other_chips(x, y):
    return [(1 - x, y), (x, 1 - y), (1 - x, 1 - y)]


def _gather_phases(n):
    def env(ins, outs, sems):
        send_sems, recv_sems, local_sems = sems
        x, y, c = _place()
        me, sibling = (x, y, c), (x, y, 1 - c)
        chips = _other_chips(x, y)

        def slot(p):
            return 4 * p[0] + 2 * p[1] + p[2]

        def copy(a, k, block, to, own=False):
            dst = outs[a].at[slot(block)]
            return pltpu.make_async_remote_copy(src_ref=ins[a] if own else dst, dst_ref=dst, send_sem=send_sems.at[a, k],
                                                recv_sem=recv_sems.at[a, k], device_id=to, device_id_type=MESH)

        mine = [pltpu.make_async_copy(ins[a], outs[a].at[slot(me)], local_sems.at[a]) for a in range(n)]
        own = [copy(a, 0, me, sibling, own=True) for a in range(n)]
        own += [copy(a, 1 + j, me, (*chip, c), own=True) for a in range(n) for j, chip in enumerate(chips)]
        return c, me, sibling, chips, copy, mine, own

    def first(ins, outs, sems):
        _, _, _, _, _, mine, own = env(ins, outs, sems)
        for cp in mine + own:
            cp.start()

    def middle(ins, outs, sems):
        c, me, sibling, chips, copy, _, _ = env(ins, outs, sems)
        for j, chip in enumerate(chips):
            for a in range(n):
                copy(a, 1 + j, (*chip, c), me).wait_recv()
                copy(a, 4 + j, (*chip, c), sibling).start()

    def last(ins, outs, sems):
        c, me, sibling, chips, copy, mine, own = env(ins, outs, sems)
        for a in range(n):
            copy(a, 0, sibling, me).wait_recv()
            for j, chip in enumerate(chips):
                copy(a, 4 + j, (*chip, 1 - c), me).wait_recv()
        for cp in own:
            cp.wait_send()
        for a in range(n):
            for j, chip in enumerate(chips):
                copy(a, 4 + j, (*chip, c), sibling).wait_send()
        for cp in mine:
            cp.wait()

    return first, middle, last


def _gather_sems(n):
    return [pltpu.SemaphoreType.DMA((n, 7)), pltpu.SemaphoreType.DMA((n, 7)), pltpu.SemaphoreType.DMA((n,))]


def gather_cargo(blocks):
    first, middle, last = _gather_phases(len(blocks))
    return Cargo(blocks, [_sds((8,) + b.shape, b.dtype) for b in blocks], _gather_sems(len(blocks)), first, last, middle)


def _swap_phases(n, slotted):
    def copies(ins, outs, sems):
        send_sems, recv_sems = sems
        x, y, c = _place()
        return [pltpu.make_async_remote_copy(src_ref=ins[a].at[1 - c] if slotted else ins[a], dst_ref=outs[a],
                                             send_sem=send_sems.at[a], recv_sem=recv_sems.at[a], device_id=(x, y, 1 - c),
                                             device_id_type=MESH) for a in range(n)]

    def first(ins, outs, sems):
        for cp in copies(ins, outs, sems):
            cp.start()

    def last(ins, outs, sems):
        for cp in copies(ins, outs, sems):
            cp.wait()

    return first, last


def swap_cargo(blocks, slotted):
    n = len(blocks)
    first, last = _swap_phases(n, slotted)
    return Cargo(blocks, [_sds(b.shape[1:] if slotted else b.shape, b.dtype) for b in blocks],
                 [pltpu.SemaphoreType.DMA((n,)), pltpu.SemaphoreType.DMA((n,))], first, last)


def _exchange_phases(n):
    def env(ins, outs, sems):
        send_sems, recv_sems, local_sems = sems
        x, y, c = _place()
        me = 2 * x + y
        chips = _other_chips(x, y)
        mine = [pltpu.make_async_copy(ins[a].at[me], outs[a].at[me], local_sems.at[a]) for a in range(n)]
        sends = [pltpu.make_async_remote_copy(src_ref=ins[a].at[2 * chip[0] + chip[1]], dst_ref=outs[a].at[me],
                                              send_sem=send_sems.at[a, j], recv_sem=recv_sems.at[a, j], device_id=(*chip, c),
                                              device_id_type=MESH) for a in range(n) for j, chip in enumerate(chips)]
        recvs = [pltpu.make_async_remote_copy(src_ref=ins[a].at[2 * chip[0] + chip[1]], dst_ref=outs[a].at[2 * chip[0] + chip[1]],
                                              send_sem=send_sems.at[a, j], recv_sem=recv_sems.at[a, j], device_id=(*chip, c),
                                              device_id_type=MESH) for a in range(n) for j, chip in enumerate(chips)]
        return mine, sends, recvs

    def first(ins, outs, sems):
        mine, sends, _ = env(ins, outs, sems)
        for cp in mine + sends:
            cp.start()

    def last(ins, outs, sems):
        mine, sends, recvs = env(ins, outs, sems)
        for cp in recvs:
            cp.wait_recv()
        for cp in sends:
            cp.wait_send()
        for cp in mine:
            cp.wait()

    return first, last


def exchange_cargo(blocks):
    n = len(blocks)
    first, last = _exchange_phases(n)
    return Cargo(blocks, [_sds(b.shape, b.dtype) for b in blocks],
                 [pltpu.SemaphoreType.DMA((n, 3)), pltpu.SemaphoreType.DMA((n, 3)), pltpu.SemaphoreType.DMA((n,))], first, last)


def _comm_call(cargo, name):
    def body(*refs):
        k_in, k_out = len(cargo.ins), len(cargo.outs)
        ins, outs, sems = refs[:k_in], refs[k_in:k_in + k_out], refs[k_in + k_out:]
        cargo.first(ins, outs, sems)
        if cargo.middle is not None:
            cargo.middle(ins, outs, sems)
        cargo.last(ins, outs, sems)

    hbm = pl.BlockSpec(memory_space=pl.ANY)
    return list(pl.pallas_call(body, name=name, in_specs=[hbm] * len(cargo.ins), out_specs=[hbm] * len(cargo.outs),
                               out_shape=cargo.outs, scratch_shapes=cargo.sems)(*cargo.ins))


def nm_fwd(x, gn, sc, sh, w, *, tn, name, wl=0, cargo=None):
    T, N = x.shape[0], w.shape[-1]
    tt = _tile(T, 512)
    nt = T // tt

    def body(x_ref, gn_ref, sc_ref, sh_ref, w_ref, h_ref, o_ref, h_all):
        i = pl.program_id(1)

        @pl.when(pl.program_id(0) == 0)
        def _():
            h = _normmod(x_ref[...], gn_ref[...], sc_ref[...], sh_ref[...]).astype(BF16)
            h_all[i] = h
            h_ref[...] = h
        o_ref[...] = _dot(h_all[i], w_ref[...])

    def x_index(j, i):
        return (jnp.where(j == 0, i, nt - 1), 0)

    row = pl.BlockSpec((1, D), lambda j, i: (0, 0))
    return _pcall(
        body, grid=(N // tn, nt), name=name, cargo=cargo,
        in_specs=[pl.BlockSpec((tt, D), x_index), row, row, row, _wspec(wl, (D, tn), lambda j, i: (0, j))],
        out_specs=[pl.BlockSpec((tt, D), x_index), pl.BlockSpec((tt, tn), lambda j, i: (i, j))],
        out_shape=[_sds((T, D), BF16), _sds((T, N))], scratch_shapes=[pltpu.VMEM((nt, tt, D), BF16)],
        args=(x, gn, sc, sh, _w3(w)))


def nm_bwd(dys, w, x, gn, sc, sh, dres, *, name, wl=0, cargo=None):
    T = x.shape[0]
    N = w.shape[-1]
    tt = _tile(T, 256)
    widths = [a.shape[1] for a in dys]
    assert sum(widths) == N
    n = len(dys)

    def body(*refs):
        dy_refs, (w_ref, x_ref, gn_ref, sc_ref, sh_ref, dres_ref) = refs[:n], refs[n:n + 6]
        dx_ref, dgn_ref, dsc_ref, dsh_ref, dyb_ref = refs[n + 6:]
        i = pl.program_id(0)
        dyb = jnp.concatenate([r[...].astype(BF16) for r in dy_refs], axis=1) if n > 1 else dy_refs[0][...].astype(BF16)
        dyb_ref[...] = dyb
        dh = _dot_nt(dyb, w_ref[...])
        _, vjp = jax.vjp(_normmod, x_ref[...], gn_ref[...], sc_ref[...], sh_ref[...])
        dx, dgn, dsc, dsh = vjp(dh)
        dx_ref[...] = dres_ref[...] + dx

        @pl.when(i == 0)
        def _():
            dgn_ref[...] = jnp.zeros_like(dgn_ref)
            dsc_ref[...] = jnp.zeros_like(dsc_ref)
            dsh_ref[...] = jnp.zeros_like(dsh_ref)
        dgn_ref[...] += dgn
        dsc_ref[...] += dsc
        dsh_ref[...] += dsh

    row = pl.BlockSpec((1, D), lambda i: (0, 0))
    tile = pl.BlockSpec((tt, D), lambda i: (i, 0))
    return _pcall(
        body, grid=(T // tt,), name=name, cargo=cargo,
        in_specs=[pl.BlockSpec((tt, wd), lambda i: (i, 0)) for wd in widths]
        + [_wspec(wl, (D, N), lambda i: (0, 0)), tile, row, row, row, tile],
        out_specs=[tile, row, row, row, pl.BlockSpec((tt, N), lambda i: (i, 0))],
        out_shape=[_sds((T, D)), _sds((1, D)), _sds((1, D)), _sds((1, D)), _sds((T, N), BF16)],
        args=(*dys, _w3(w), x, gn, sc, sh, dres))


def mm_tn(a, b, *, tn, name):
    T, K = a.shape
    N = b.shape[1]
    tt = _tile(T, 512)

    def body(a_ref, b_ref, o_ref):
        @pl.when(pl.program_id(1) == 0)
        def _():
            o_ref[...] = jnp.zeros_like(o_ref)
        o_ref[...] += _dot_tn(a_ref[...], b_ref[...])

    return pl.pallas_call(
        body, grid=(N // tn, T // tt), name=name,
        in_specs=[pl.BlockSpec((tt, K), lambda j, t: (t, 0)), pl.BlockSpec((tt, tn), lambda j, t: (t, j))],
        out_specs=pl.BlockSpec((K, tn), lambda j, t: (0, j)), out_shape=_sds((K, N)), compiler_params=_cp(2))(a, b)


def mm_tn_split(a, b, *, by_cols, name):
    T, K = a.shape
    N = b.shape[1]
    tt = _tile(T, 512)
    tn = N // 4 if by_cols else N // 2
    rh, C = (K // 2, N // 4) if by_cols else (K // 8, N)
    nt = T // tt

    def body(a_ref, b_ref, o_ref, acc):
        t = pl.program_id(1)

        @pl.when(t == 0)
        def _():
            acc[...] = jnp.zeros_like(acc)
        acc[...] += _dot_tn(a_ref[...], b_ref[...])

        @pl.when(t == nt - 1)
        def _():
            if by_cols:
                for h in range(2):
                    o_ref[h] = acc[pl.ds(h * rh, rh), :]
            else:
                for s in range(4):
                    for h in range(2):
                        o_ref[h, s] = acc[pl.ds((2 * s + h) * rh, rh), :]

    if by_cols:
        out_spec = pl.BlockSpec((2, None, rh, tn), lambda j, t: (0, j, 0, 0))
    else:
        out_spec = pl.BlockSpec((2, 4, rh, tn), lambda j, t: (0, 0, 0, j))
    return pl.pallas_call(
        body, grid=(N // tn, nt), name=name,
        in_specs=[pl.BlockSpec((tt, K), lambda j, t: (t, 0)), pl.BlockSpec((tt, tn), lambda j, t: (t, j))],
        out_specs=out_spec, out_shape=_sds((2, 4, rh, C)), scratch_shapes=[pltpu.VMEM((K, tn), F32)],
        compiler_params=_cp(2))(a, b)


def _fill_pad(pad_ref, prev, cur, first):
    pad_ref[pl.ds(0, HALO), :] = jnp.where(first, 0.0, prev)
    pad_ref[pl.ds(HALO, cur.shape[0]), :] = cur


SUBLANES = 8


def _shift_scratch(tt, C):
    return pltpu.VMEM((SUBLANES - 1, tt + HALO - SUBLANES, C), F32)


def _preshift(src_ref, sh_ref, tt, cols=slice(None)):
    for b in range(1, SUBLANES):
        sh_ref[b - 1] = src_ref[pl.ds(b, tt + HALO - SUBLANES), cols]


def _window(src_ref, sh_ref, off, tt, cols=slice(None)):
    a, b = divmod(off, SUBLANES)
    if b == 0 or sh_ref is None:
        return src_ref[pl.ds(off, tt), cols]
    return sh_ref[b - 1, pl.ds(SUBLANES * a, tt), :]


def _causal_conv(pad_ref, w, K, tt, sh_ref=None):
    acc = None
    for k in range(K):
        term = _window(pad_ref, sh_ref, HALO - (K - 1) + k, tt) * w[k:k + 1, :]
        acc = term if acc is None else acc + term
    return acc


def _conv_inputs(pc, pp, first, pad_a, pad_b, pad_c):
    def s_of(p):
        return p[:, O_AC:O_AV] * p[:, O_AV:O_BA]

    def u0_of(p):
        return p[:, O_BA:O_BG] * jax.nn.sigmoid(p[:, O_BG:O_Q])

    _fill_pad(pad_a, s_of(pp), s_of(pc), first)
    _fill_pad(pad_b, u0_of(pp), u0_of(pc), first)
    _fill_pad(pad_c, pp[:, O_Q:O_Z], pc[:, O_Q:O_Z], first)


def _mix_specs(T, tt):
    cur = pl.BlockSpec((tt, O_Z), lambda i: (i, 0))
    prev = pl.BlockSpec((HALO, O_Z), lambda i: (jnp.maximum(i * (tt // HALO) - 1, 0), 0))
    gbb = pl.BlockSpec((tt, HD), lambda i: (i, O_GB // HD))
    return cur, prev, gbb


def _full(shape):
    return pl.BlockSpec(shape, lambda i: (0,) * len(shape))


def mix_fwd(proj, wa, wb, bb, ln_g, ln_b, wc, alog_row, dt_row, *, name, cargo=None):
    T = proj.shape[0]
    tt = _tile(T, 256)

    def body(pc_ref, pp_ref, blk_ref, wa_ref, wb_ref, bb_ref, lg_ref, lb_ref, wc_ref, al_ref, dt_ref,
             yab_ref, q_ref, k_ref, v_ref, gb_ref, grow_ref, pad_a, pad_b, pad_c, sh_b):
        first = pl.program_id(0) == 0
        pc = pc_ref[...]
        _conv_inputs(pc, pp_ref[...], first, pad_a, pad_b, pad_c)
        _preshift(pad_b, sh_b, tt)
        ya = pc[:, O_AB:O_AC] * _causal_conv(pad_a, wa_ref[...], KA, tt)
        yb = _mix_b_post(_causal_conv(pad_b, wb_ref[...], KB, tt, sh_b) + bb_ref[...], lg_ref[...], lb_ref[...])
        yab_ref[...] = jnp.concatenate([ya, yb], axis=1)
        pre = _causal_conv(pad_c, wc_ref[...], KC, tt)
        blocks = [pre[:, j * HD:(j + 1) * HD] for j in range(3 * NH)]
        q, k, v, gb, grow = _dn_post(blocks[:NH], blocks[NH:2 * NH], blocks[2 * NH:], blk_ref[...], al_ref[...], dt_ref[...])
        q_ref[...] = jnp.concatenate(q, axis=1)
        k_ref[...] = jnp.concatenate(k, axis=1)
        v_ref[...] = jnp.concatenate(v, axis=1)
        gb_ref[...] = gb
        grow_ref[...] = grow

    cur, prev, gbb = _mix_specs(T, tt)
    t512 = pl.BlockSpec((tt, DDN), lambda i: (i, 0))
    return _pcall(
        body, grid=(T // tt,), name=name, cargo=cargo, args=(proj, proj, proj, wa, wb, bb, ln_g, ln_b, wc, alog_row, dt_row),
        in_specs=[cur, prev, gbb, _full((8, DC)), _full((32, DC)), _full((1, DC)), _full((1, DC)), _full((1, DC)),
                  _full((8, 3 * DDN)), _full((1, HD)), _full((1, HD))],
        out_specs=[t512, t512, t512, t512, pl.BlockSpec((tt, HD), lambda i: (i, 0)), pl.BlockSpec((8, tt), lambda i: (0, i))],
        out_shape=[_sds((T, 2 * DC)), _sds((T, DDN)), _sds((T, DDN)), _sds((T, DDN)), _sds((T, HD)), _sds((8, T))],
        scratch_shapes=[pltpu.VMEM((HALO + tt, DC), F32), pltpu.VMEM((HALO + tt, DC), F32), pltpu.VMEM((HALO + tt, 3 * DDN), F32),
                        _shift_scratch(tt, DC)])


def mix_bwd_point(proj, dyab, dq, dk, dv, dgb, dgrow, wa, wb, bb, ln_g, ln_b, wc, alog_row, dt_row, *, name, cargo=None):
    T = proj.shape[0]
    tt = _tile(T, 256)
    CW = 2 * DC + 3 * DDN

    def body(pc_ref, pp_ref, blk_ref, dyab_ref, dq_ref, dk_ref, dv_ref, dgb_ref, dgrow_ref,
             wa_ref, wb_ref, bb_ref, lg_ref, lb_ref, wc_ref, al_ref, dt_ref,
             dab_ref, dconv_ref, dblk_ref, dbb_ref, dlg_ref, dlb_ref, dal_ref, ddt_ref, pad_a, pad_b, pad_c, sh_b):
        i = pl.program_id(0)
        pc = pc_ref[...]
        _conv_inputs(pc, pp_ref[...], i == 0, pad_a, pad_b, pad_c)
        _preshift(pad_b, sh_b, tt)
        ca = _causal_conv(pad_a, wa_ref[...], KA, tt)
        u = _causal_conv(pad_b, wb_ref[...], KB, tt, sh_b) + bb_ref[...]
        pre = _causal_conv(pad_c, wc_ref[...], KC, tt)
        dyab_v = dyab_ref[...]
        dya, dyb = dyab_v[:, :DC], dyab_v[:, DC:]
        dab_ref[...] = dya * ca
        dca = dya * pc[:, O_AB:O_AC]
        _, vjp_b = jax.vjp(_mix_b_post, u, lg_ref[...], lb_ref[...])
        du, dlg, dlb = vjp_b(dyb)
        blocks = [pre[:, j * HD:(j + 1) * HD] for j in range(3 * NH)]
        _, vjp_c = jax.vjp(_dn_post, blocks[:NH], blocks[NH:2 * NH], blocks[2 * NH:], blk_ref[...], al_ref[...], dt_ref[...])

        def heads(r):
            vv = r[...]
            return [vv[:, h * HD:(h + 1) * HD] for h in range(NH)]
        dpq, dpk, dpv, dblk, dal, ddt = vjp_c((heads(dq_ref), heads(dk_ref), heads(dv_ref), dgb_ref[...], dgrow_ref[...]))
        dconv_ref[...] = jnp.concatenate([dca, du] + dpq + dpk + dpv, axis=1)
        dblk_ref[...] = dblk

        @pl.when(i == 0)
        def _():
            for r in (dbb_ref, dlg_ref, dlb_ref, dal_ref, ddt_ref):
                r[...] = jnp.zeros_like(r)
        dbb_ref[...] += _colsum(du)
        dlg_ref[...] += dlg
        dlb_ref[...] += dlb
        dal_ref[...] += dal
        ddt_ref[...] += ddt

    cur, prev, gbb = _mix_specs(T, tt)
    t512 = pl.BlockSpec((tt, DDN), lambda i: (i, 0))
    t128 = pl.BlockSpec((tt, HD), lambda i: (i, 0))
    return _pcall(
        body, grid=(T // tt,), name=name, cargo=cargo,
        args=(proj, proj, proj, dyab, dq, dk, dv, dgb, dgrow, wa, wb, bb, ln_g, ln_b, wc, alog_row, dt_row),
        in_specs=[cur, prev, gbb, t512, t512, t512, t512, t128, pl.BlockSpec((8, tt), lambda i: (0, i)),
                  _full((8, DC)), _full((32, DC)), _full((1, DC)), _full((1, DC)), _full((1, DC)),
                  _full((8, 3 * DDN)), _full((1, HD)), _full((1, HD))],
        out_specs=[pl.BlockSpec((tt, DC), lambda i: (i, 0)), pl.BlockSpec((tt, CW), lambda i: (i, 0)), t128,
                   _full((1, DC)), _full((1, DC)), _full((1, DC)), _full((1, HD)), _full((1, HD))],
        out_shape=[_sds((T, DC)), _sds((T, CW)), _sds((T, HD)), _sds((1, DC)), _sds((1, DC)), _sds((1, DC)),
                   _sds((1, HD)), _sds((1, HD))],
        scratch_shapes=[pltpu.VMEM((HALO + tt, DC), F32), pltpu.VMEM((HALO + tt, DC), F32), pltpu.VMEM((HALO + tt, 3 * DDN), F32),
                        _shift_scratch(tt, DC)])


def mix_bwd_conv(proj, dconv, wa, wb, wc, *, name, cargo=None):
    T = proj.shape[0]
    tt = _tile(T, 256)
    CW = 2 * DC + 3 * DDN
    nblk = T // HALO

    def body(pc_ref, pp_ref, dc_ref, dn_ref, wa_ref, wb_ref, wc_ref, dp_ref, dwa_ref, dwb_ref, dwc_ref,
             pad_a, pad_b, pad_c, dpad, sh_b, sh_d):
        i = pl.program_id(0)
        last = i == pl.num_programs(0) - 1
        pc = pc_ref[...]
        _conv_inputs(pc, pp_ref[...], i == 0, pad_a, pad_b, pad_c)
        dcur = dc_ref[...]
        dpad[pl.ds(0, tt), :] = dcur
        dpad[pl.ds(tt, HALO), :] = jnp.where(last, 0.0, dn_ref[...])
        _preshift(pad_b, sh_b, tt)
        _preshift(dpad, sh_d, tt, slice(DC, 2 * DC))

        @pl.when(i == 0)
        def _():
            for r in (dwa_ref, dwb_ref, dwc_ref):
                r[...] = jnp.zeros_like(r)

        def tconv(lo, hi, w, K, pad_ref, dw_ref, sh_x=None, sh_dy=None):
            dy = dcur[:, lo:hi]
            acc = None
            for k in range(K):
                term = _window(dpad, sh_dy, K - 1 - k, tt, slice(lo, hi)) * w[k:k + 1, :]
                acc = term if acc is None else acc + term
                prod = (dy * _window(pad_ref, sh_x, HALO - (K - 1) + k, tt)).astype(BF16)
                dw_ref[pl.ds(k, 1), :] += _dot(jnp.ones((SUBLANES, tt), BF16), prod)[0:1]
            return acc

        ds = tconv(0, DC, wa_ref[...], KA, pad_a, dwa_ref)
        du0 = tconv(DC, 2 * DC, wb_ref[...], KB, pad_b, dwb_ref, sh_b, sh_d)
        dqkv = tconv(2 * DC, CW, wc_ref[...], KC, pad_c, dwc_ref)
        a_c, a_v, b_a, b_g = pc[:, O_AC:O_AV], pc[:, O_AV:O_BA], pc[:, O_BA:O_BG], pc[:, O_BG:O_Q]
        sg = jax.nn.sigmoid(b_g)
        dp_ref[...] = jnp.concatenate([ds * a_v, ds * a_c, du0 * sg, du0 * b_a * sg * (1.0 - sg), dqkv], axis=1)

    cur, prev, _ = _mix_specs(T, tt)
    return _pcall(
        body, grid=(T // tt,), name=name, cargo=cargo, args=(proj, proj, dconv, dconv, wa, wb, wc),
        in_specs=[cur, prev, pl.BlockSpec((tt, CW), lambda i: (i, 0)),
                  pl.BlockSpec((HALO, CW), lambda i: (jnp.minimum((i + 1) * (tt // HALO), nblk - 1), 0)),
                  _full((8, DC)), _full((32, DC)), _full((8, 3 * DDN))],
        out_specs=[pl.BlockSpec((tt, O_Z - O_AC), lambda i: (i, 0)), _full((8, DC)), _full((32, DC)), _full((8, 3 * DDN))],
        out_shape=[_sds((T, O_Z - O_AC)), _sds((8, DC)), _sds((32, DC)), _sds((8, 3 * DDN))],
        scratch_shapes=[pltpu.VMEM((HALO + tt, DC), F32), pltpu.VMEM((HALO + tt, DC), F32), pltpu.VMEM((HALO + tt, 3 * DDN), F32),
                        pltpu.VMEM((tt + HALO, CW), F32), _shift_scratch(tt, DC), _shift_scratch(tt, DC)])


def _head(v, h):
    return v[:, h * HD:(h + 1) * HD]


def _to_batch(v, cb):
    return jnp.stack([v[c * CH:(c + 1) * CH, h * HD:(h + 1) * HD] for c in range(cb) for h in range(NH)])


def _from_batch(b, cb):
    return jnp.concatenate([jnp.concatenate([b[c * NH + h] for h in range(NH)], axis=1) for c in range(cb)], axis=0)


def _lanes_to_batch(v, cb, lane0):
    return jnp.stack([v[c * CH:(c + 1) * CH, lane0 + h:lane0 + h + 1] for c in range(cb) for h in range(NH)])


def _lane_onehot(h):
    return (lax.broadcasted_iota(jnp.int32, (1, HD), 1) == h).astype(F32)


def _batch_to_lanes(b, cb, lane0):
    return jnp.concatenate([sum(b[c * NH + h] * _lane_onehot(lane0 + h) for h in range(NH)) for c in range(cb)], axis=0)


def dn_intra_fwd(q, k, v, gb, grow3, *, name, cargo=None):
    T = q.shape[0]
    N = T // CH
    cb = 4 if N % 4 == 0 else 1

    def body(q_ref, k_ref, v_ref, gb_ref, gr_ref, u_ref, w_ref, qk_ref, p_ref):
        gbv = gb_ref[...]
        grow = jnp.stack([gr_ref[c, h:h + 1, :] for c in range(cb) for h in range(NH)])
        u, w, qk, p = _dn_intra(_to_batch(q_ref[...], cb), _to_batch(k_ref[...], cb), _to_batch(v_ref[...], cb),
                                _lanes_to_batch(gbv, cb, NH), _lanes_to_batch(gbv, cb, 0), grow)
        u_ref[...] = _from_batch(u, cb)
        w_ref[...] = _from_batch(w, cb)
        qk_ref[...] = qk.reshape(cb, NH, CH, CH)
        p_ref[...] = p.reshape(cb, NH, CH, CH)

    t512 = pl.BlockSpec((cb * CH, DDN), lambda i: (i, 0))
    sq = pl.BlockSpec((cb, NH, CH, CH), lambda i: (i, 0, 0, 0))
    return _pcall(
        body, grid=(N // cb,), name=name, cargo=cargo, args=(q, k, v, gb, grow3),
        in_specs=[t512, t512, t512, pl.BlockSpec((cb * CH, HD), lambda i: (i, 0)), pl.BlockSpec((cb, 8, CH), lambda i: (i, 0, 0))],
        out_specs=[t512, t512, sq, sq],
        out_shape=[_sds((T, DDN)), _sds((T, DDN)), _sds((N, NH, CH, CH)), _sds((N, NH, CH, CH))])


def dn_inter_fwd(q, k, u, w, gb, qk, *, name):
    T = q.shape[0]
    N = T // CH
    cb = 4 if N % 4 == 0 else 1

    def body(q_ref, k_ref, u_ref, w_ref, gb_ref, qk_ref, o_ref, s_ref, state):
        @pl.when(pl.program_id(0) == 0)
        def _():
            state[...] = jnp.zeros_like(state)
        for c in range(cb):
            rows = pl.ds(c * CH, CH)
            s = state[...]
            s_ref[c] = s
            s_new, o = _dn_inter(s, _to_batch(q_ref[rows, :], 1), _to_batch(k_ref[rows, :], 1), _to_batch(u_ref[rows, :], 1),
                                 _to_batch(w_ref[rows, :], 1), _lanes_to_batch(gb_ref[rows, :], 1, 0), qk_ref[c])
            state[...] = s_new
            o_ref[rows, :] = _from_batch(o, 1)

    t512 = pl.BlockSpec((cb * CH, DDN), lambda i: (i, 0))
    return pl.pallas_call(
        body, grid=(N // cb,), name=name,
        in_specs=[t512, t512, t512, t512, pl.BlockSpec((cb * CH, HD), lambda i: (i, 0)),
                  pl.BlockSpec((cb, NH, CH, CH), lambda i: (i, 0, 0, 0))],
        out_specs=[t512, pl.BlockSpec((cb, NH, HD, HD), lambda i: (i, 0, 0, 0))],
        out_shape=[_sds((T, DDN)), _sds((N, NH, HD, HD))],
        scratch_shapes=[pltpu.VMEM((NH, HD, HD), F32)], compiler_params=_cp(1))(q, k, u, w, gb, qk)


def dn_inter_bwd(do, q, k, u, w, gb, qk, s_all, *, name, cargo=None):
    T = q.shape[0]
    N = T // CH
    cb = 4 if N % 4 == 0 else 1
    G = N // cb

    def body(do_ref, q_ref, k_ref, u_ref, w_ref, gb_ref, qk_ref, s_ref, dq_ref, dk_ref, du_ref, dw_ref, dg_ref, dqk_ref, dstate):
        @pl.when(pl.program_id(0) == 0)
        def _():
            dstate[...] = jnp.zeros_like(dstate)
        for c in reversed(range(cb)):
            rows = pl.ds(c * CH, CH)
            _, vjp = jax.vjp(_dn_inter, s_ref[c], _to_batch(q_ref[rows, :], 1), _to_batch(k_ref[rows, :], 1),
                             _to_batch(u_ref[rows, :], 1), _to_batch(w_ref[rows, :], 1), _lanes_to_batch(gb_ref[rows, :], 1, 0),
                             qk_ref[c])
            ds, dq, dk, du, dw, dgc, dqk = vjp((dstate[...], _to_batch(do_ref[rows, :], 1)))
            dstate[...] = ds
            dq_ref[rows, :] = _from_batch(dq, 1)
            dk_ref[rows, :] = _from_batch(dk, 1)
            du_ref[rows, :] = _from_batch(du, 1)
            dw_ref[rows, :] = _from_batch(dw, 1)
            dg_ref[rows, :] = _batch_to_lanes(dgc, 1, 0)
            dqk_ref[c] = dqk

    t512 = pl.BlockSpec((cb * CH, DDN), lambda i: (G - 1 - i, 0))
    t128 = pl.BlockSpec((cb * CH, HD), lambda i: (G - 1 - i, 0))
    qkb = pl.BlockSpec((cb, NH, CH, CH), lambda i: (G - 1 - i, 0, 0, 0))
    return _pcall(
        body, grid=(G,), name=name, cargo=cargo, args=(do, q, k, u, w, gb, qk, s_all),
        in_specs=[t512, t512, t512, t512, t512, t128, qkb, pl.BlockSpec((cb, NH, HD, HD), lambda i: (G - 1 - i, 0, 0, 0))],
        out_specs=[t512, t512, t512, t512, t128, qkb],
        out_shape=[_sds((T, DDN))] * 4 + [_sds((T, HD)), _sds((N, NH, CH, CH))],
        scratch_shapes=[pltpu.VMEM((NH, HD, HD), F32)])


def dn_intra_bwd(du, dw, dqk, dq_in, dk_in, dg_in, q, k, v, gb, grow3, p_all, *, name, cargo=None):
    T = q.shape[0]
    N = T // CH
    cb = 4 if N % 4 == 0 else 1

    def body(du_ref, dw_ref, dqk_ref, dqi_ref, dki_ref, dgi_ref, q_ref, k_ref, v_ref, gb_ref, gr_ref, p_ref,
             dq_ref, dk_ref, dv_ref, dgb_ref, dgr_ref):
        B = cb * NH
        gbv = gb_ref[...]
        grow = jnp.stack([gr_ref[c, h:h + 1, :] for c in range(cb) for h in range(NH)])
        _, vjp = jax.vjp(functools.partial(_dn_intra, p_known=p_ref[...].reshape(B, CH, CH)),
                         _to_batch(q_ref[...], cb), _to_batch(k_ref[...], cb), _to_batch(v_ref[...], cb),
                         _lanes_to_batch(gbv, cb, NH), _lanes_to_batch(gbv, cb, 0), grow)
        dq, dk, dv, dbeta, dgc, dgr = vjp((_to_batch(du_ref[...], cb), _to_batch(dw_ref[...], cb), dqk_ref[...].reshape(B, CH, CH),
                                           jnp.zeros((B, CH, CH), F32)))
        dq_ref[...] = dqi_ref[...] + _from_batch(dq, cb)
        dk_ref[...] = dki_ref[...] + _from_batch(dk, cb)
        dv_ref[...] = _from_batch(dv, cb)
        dgb_ref[...] = dgi_ref[...] + _batch_to_lanes(dgc, cb, 0) + _batch_to_lanes(dbeta, cb, NH)
        for c in range(cb):
            dgr_ref[c] = jnp.concatenate([dgr[c * NH + h] for h in range(NH)] + [jnp.zeros((8 - NH, CH), F32)], axis=0)

    t512 = pl.BlockSpec((cb * CH, DDN), lambda i: (i, 0))
    t128 = pl.BlockSpec((cb * CH, HD), lambda i: (i, 0))
    qkb = pl.BlockSpec((cb, NH, CH, CH), lambda i: (i, 0, 0, 0))
    grb = pl.BlockSpec((cb, 8, CH), lambda i: (i, 0, 0))
    return _pcall(
        body, grid=(N // cb,), name=name, cargo=cargo, args=(du, dw, dqk, dq_in, dk_in, dg_in, q, k, v, gb, grow3, p_all),
        in_specs=[t512, t512, qkb, t512, t512, t128, t512, t512, t512, t128, grb, qkb],
        out_specs=[t512, t512, t512, t128, grb],
        out_shape=[_sds((T, DDN))] * 3 + [_sds((T, HD)), _sds((N, 8, CH))])


def _z_specs(tt):
    return [pl.BlockSpec((tt, DC), lambda i: (i, O_Z // DC)), pl.BlockSpec((tt, DC), lambda i: (i, O_Z // DC + 1))]


def mixout_fwd(x, yab, o, proj, gdn, g1, w_out, *, name, wl=0, cargo=None):
    T = x.shape[0]
    tt = _tile(T, 512)

    def body(x_ref, yab_ref, o_ref, z0_ref, z1_ref, gdn_ref, g1_ref, w_ref, ycat_ref, mix_ref, xo_ref):
        ov = o_ref[...]
        z = jnp.concatenate([z0_ref[...], z1_ref[...]], axis=1)
        yc = [_yc(_head(ov, h), _head(z, h), gdn_ref[...]) for h in range(NH)]
        ycat = jnp.concatenate([yab_ref[...]] + yc, axis=1).astype(BF16)
        ycat_ref[...] = ycat
        mix = _dot(ycat, w_ref[...])
        mix_ref[...] = mix
        xo_ref[...] = x_ref[...] + g1_ref[...] * mix

    tile = pl.BlockSpec((tt, D), lambda i: (i, 0))
    t512 = pl.BlockSpec((tt, DDN), lambda i: (i, 0))
    return _pcall(
        body, grid=(T // tt,), name=name, cargo=cargo, args=(x, yab, o, proj, proj, gdn, g1, _w3(w_out)),
        in_specs=[tile, t512, t512] + _z_specs(tt) + [_full((1, HD)), _full((1, D)), _wspec(wl, (D, D), lambda i: (0, 0))],
        out_specs=[tile, tile, tile], out_shape=[_sds((T, D), BF16), _sds((T, D)), _sds((T, D))])


def mixout_bwd(dx, mix, o, proj, gdn, g1, w_out, *, name, wl=0, cargo=None):
    T = dx.shape[0]
    tt = _tile(T, 256)

    def body(dx_ref, mix_ref, o_ref, z0_ref, z1_ref, gdn_ref, g1_ref, w_ref, dmix_ref, dyab_ref, do_ref, dz_ref, dg1_ref, dgdn_ref):
        i = pl.program_id(0)
        dxv = dx_ref[...]
        dmix = (dxv * g1_ref[...]).astype(BF16)
        dmix_ref[...] = dmix
        dycat = _dot_nt(dmix, w_ref[...])
        dyab_ref[...] = dycat[:, :2 * DC]
        ov = o_ref[...]
        z = jnp.concatenate([z0_ref[...], z1_ref[...]], axis=1)
        dos, dzs = [], []
        dgdn = jnp.zeros((1, HD), F32)
        for h in range(NH):
            _, vjp = jax.vjp(_yc, _head(ov, h), _head(z, h), gdn_ref[...])
            do, dz, dg = vjp(dycat[:, 2 * DC + h * HD:2 * DC + (h + 1) * HD])
            dos.append(do)
            dzs.append(dz)
            dgdn = dgdn + dg
        do_ref[...] = jnp.concatenate(dos, axis=1)
        dz_ref[...] = jnp.concatenate(dzs, axis=1)

        @pl.when(i == 0)
        def _():
            dg1_ref[...] = jnp.zeros_like(dg1_ref)
            dgdn_ref[...] = jnp.zeros_like(dgdn_ref)
        dg1_ref[...] += _colsum(dxv * mix_ref[...])
        dgdn_ref[...] += dgdn

    tile = pl.BlockSpec((tt, D), lambda i: (i, 0))
    t512 = pl.BlockSpec((tt, DDN), lambda i: (i, 0))
    return _pcall(
        body, grid=(T // tt,), name=name, cargo=cargo, args=(dx, mix, o, proj, proj, gdn, g1, _w3(w_out)),
        in_specs=[tile, tile, t512] + _z_specs(tt) + [_full((1, HD)), _full((1, D)), _wspec(wl, (D, D), lambda i: (0, 0))],
        out_specs=[tile, t512, t512, t512, _full((1, D)), _full((1, HD))],
        out_shape=[_sds((T, D), BF16), _sds((T, DDN)), _sds((T, DDN)), _sds((T, DDN)), _sds((1, D)), _sds((1, HD))])


FK = DFF // 2


def ffnout_fwd(x, gu, g2, w, *, name, wl=0, cargo=None):
    T = x.shape[0]
    tt = _tile(T, 512)

    def body(x_ref, gate_ref, up_ref, g2_ref, w_ref, act_ref, f_ref, xo_ref):
        kk = pl.program_id(1)
        act = (_silu(gate_ref[...]) * up_ref[...]).astype(BF16)
        act_ref[...] = act
        part = _dot(act, w_ref[...])

        @pl.when(kk == 0)
        def _():
            f_ref[...] = part

        @pl.when(kk == 1)
        def _():
            f = f_ref[...] + part
            f_ref[...] = f
            xo_ref[...] = x_ref[...] + g2_ref[...] * f

    tile = pl.BlockSpec((tt, D), lambda i, kk: (i, 0))
    return _pcall(
        body, grid=(T // tt, 2), name=name, cargo=cargo, args=(x, gu, gu, g2, _w3(w)),
        in_specs=[tile, pl.BlockSpec((tt, FK), lambda i, kk: (i, kk)), pl.BlockSpec((tt, FK), lambda i, kk: (i, 2 + kk)),
                  pl.BlockSpec((1, D), lambda i, kk: (0, 0)), _wspec(wl, (FK, D), lambda i, kk: (kk, 0))],
        out_specs=[pl.BlockSpec((tt, FK), lambda i, kk: (i, kk)), tile, tile],
        out_shape=[_sds((T, DFF), BF16), _sds((T, D)), _sds((T, D))])


def ffnout_bwd(dx, f, gu, g2, w, *, name, wl=0, cargo=None):
    T = dx.shape[0]
    tt = _tile(T, 512)

    def body(dx_ref, f_ref, gate_ref, up_ref, g2_ref, w_ref, df_ref, dgate_ref, dup_ref, dg2_ref):
        i, kk = pl.program_id(0), pl.program_id(1)
        dxv = dx_ref[...]
        df = (dxv * g2_ref[...]).astype(BF16)
        dact = _dot_nt(df, w_ref[...])
        gate, up = gate_ref[...], up_ref[...]
        sg = jax.nn.sigmoid(gate)
        dgate_ref[...] = (dact * up * (sg * (1.0 + gate * (1.0 - sg)))).astype(BF16)
        dup_ref[...] = (dact * (gate * sg)).astype(BF16)

        @pl.when(kk == 0)
        def _():
            df_ref[...] = df

        @pl.when((i == 0) & (kk == 0))
        def _():
            dg2_ref[...] = jnp.zeros_like(dg2_ref)

        @pl.when(kk == 0)
        def _():
            dg2_ref[...] += _colsum(dxv * f_ref[...])

    tile = pl.BlockSpec((tt, D), lambda i, kk: (i, 0))
    return _pcall(
        body, grid=(T // tt, 2), name=name, cargo=cargo, args=(dx, f, gu, gu, g2, _w3(w)),
        in_specs=[tile, tile, pl.BlockSpec((tt, FK), lambda i, kk: (i, kk)), pl.BlockSpec((tt, FK), lambda i, kk: (i, 2 + kk)),
                  pl.BlockSpec((1, D), lambda i, kk: (0, 0)), _wspec(wl, (FK, D), lambda i, kk: (kk, 0))],
        out_specs=[tile, pl.BlockSpec((tt, FK), lambda i, kk: (i, kk)), pl.BlockSpec((tt, FK), lambda i, kk: (i, kk)),
                   pl.BlockSpec((1, D), lambda i, kk: (0, 0))],
        out_shape=[_sds((T, D), BF16), _sds((T, DFF), BF16), _sds((T, DFF), BF16), _sds((1, D))])


def loss_head(x, target, gf, *, name):
    T = x.shape[0]
    tt = _tile(T, 256)

    def body(x_ref, t_ref, g_ref, loss_ref, dx_ref, dg_ref):
        i = pl.program_id(0)
        tv = t_ref[...]
        y, vjp = jax.vjp(_rms, x_ref[...], g_ref[...])
        err = y - tv
        dx, dg = vjp(err * (1.0 / D))
        dx_ref[...] = dx

        @pl.when(i == 0)
        def _():
            loss_ref[...] = jnp.zeros_like(loss_ref)
            dg_ref[...] = jnp.zeros_like(dg_ref)
        loss_ref[...] += 0.5 * jnp.sum(jnp.mean(err * err, axis=-1, keepdims=True), axis=0, keepdims=True)
        dg_ref[...] += dg

    tile = pl.BlockSpec((tt, D), lambda i: (i, 0))
    return pl.pallas_call(
        body, grid=(T // tt,), name=name, in_specs=[tile, tile, _full((1, D))],
        out_specs=[_full((1, HD)), tile, _full((1, D))], out_shape=[_sds((1, HD)), _sds((T, D)), _sds((1, D))],
        compiler_params=_cp(1))(x, target, gf)


def _pad_rows(a, rows):
    return jnp.pad(a, ((0, rows - a.shape[0]), (0, 0)))


def _row128(v):
    return jnp.pad(v, (0, HD - v.shape[0]))[None, :]


def _grow3(grow):
    return grow.reshape(8, -1, CH).transpose(1, 0, 2)


def _grow2(grow3):
    return grow3.transpose(1, 0, 2).reshape(8, -1)


def layer_params(l, mod, p):
    m = mod[l].reshape(6, 1, D)
    return dict(
        sh1=m[0], sc1=m[1], g1=m[2], sh2=m[3], sc2=m[4], g2=m[5],
        gn1=p["norm_mix_g"][l][None, :], gn2=p["norm_ffn_g"][l][None, :],
        wa=_pad_rows(p["conv_a_w"][l], 8), wb=_pad_rows(p["conf_dw_w"][l], 32), wc=_pad_rows(p["dn_conv_w"][l], 8),
        bb=p["conf_dw_b"][l][None, :], ln_g=p["conf_ln_g"][l][None, :], ln_b=p["conf_ln_b"][l][None, :],
        alog=_row128(p["dn_a_log"][l]), dt=_row128(p["dn_dt_bias"][l]), gdn=p["dn_norm_g"][l][None, :])


def layer_fwd(l, x, lp, w_in, w_out, w_ffn_in, w_ffn_out, wl=0, cargos=None):
    cg = (cargos or {}).get
    get = lambda w: w() if callable(w) else w
    s = {"x0": x}
    s["h1"], s["proj"] = nm_fwd(x, lp["gn1"], lp["sc1"], lp["sh1"], get(w_in), tn=NP // 3, name=f"proj_fwd_{l}", wl=wl,
                                cargo=cg("proj_fwd"))
    yab, s["q"], s["k"], s["v"], s["gb"], grow = mix_fwd(
        s["proj"], lp["wa"], lp["wb"], lp["bb"], lp["ln_g"], lp["ln_b"], lp["wc"], lp["alog"], lp["dt"], name=f"mix_fwd_{l}",
        cargo=cg("mix_fwd"))
    s["grow3"] = _grow3(grow)
    s["u"], s["w"], s["qk"], s["p"] = dn_intra_fwd(s["q"], s["k"], s["v"], s["gb"], s["grow3"], name=f"dn_intra_fwd_{l}",
                                                   cargo=cg("dn_intra_fwd"))
    s["o"], s["s_all"] = dn_inter_fwd(s["q"], s["k"], s["u"], s["w"], s["gb"], s["qk"], name=f"dn_inter_fwd_{l}")
    s["ycat"], s["mix"], s["x1"] = mixout_fwd(x, yab, s["o"], s["proj"], lp["gdn"], lp["g1"], get(w_out), name=f"mixout_fwd_{l}",
                                              wl=wl, cargo=cg("mixout_fwd"))
    s["h2"], s["gu"] = nm_fwd(s["x1"], lp["gn2"], lp["sc2"], lp["sh2"], get(w_ffn_in), tn=2 * DFF // 4, name=f"ffnin_fwd_{l}",
                              wl=wl, cargo=cg("ffnin_fwd"))
    s["act"], s["f"], x2 = ffnout_fwd(s["x1"], s["gu"], lp["g2"], get(w_ffn_out), name=f"ffnout_fwd_{l}", wl=wl,
                                      cargo=cg("ffnout_fwd"))
    return x2, s


def layer_bwd(l, dx2, s, lp, w_in, w_out, w_ffn_in, w_ffn_out, wl=0, carry=None):
    g = {}
    carry = carry or (lambda stage: None)
    df, dgate, dup, dg2 = ffnout_bwd(dx2, s["f"], s["gu"], lp["g2"], w_ffn_out, name=f"ffnout_bwd_{l}", wl=wl, cargo=carry("swap"))
    g["w_ffn_out"] = mm_tn_split(s["act"], df, by_cols=False, name=f"ffnout_dw_{l}")
    dx1, g["norm_ffn_g"], dsc2, dsh2, dgu = nm_bwd([dgate, dup], w_ffn_in, s["x1"], lp["gn2"], lp["sc2"], lp["sh2"], dx2,
                                                   name=f"ffnin_bwd_{l}", wl=wl, cargo=carry("exchange_c"))
    g["w_ffn_in"] = mm_tn_split(s["h2"], dgu, by_cols=True, name=f"ffnin_dw_{l}")
    dmix, dyab, do, dz, dg1, g["dn_norm_g"] = mixout_bwd(dx1, s["mix"], s["o"], s["proj"], lp["gdn"], lp["g1"], w_out,
                                                         name=f"mixout_bwd_{l}", wl=wl)
    g["w_out"] = mm_tn_split(s["ycat"], dmix, by_cols=False, name=f"mixout_dw_{l}")
    dq_i, dk_i, du, dw, dg_i, dqk = dn_inter_bwd(do, s["q"], s["k"], s["u"], s["w"], s["gb"], s["qk"], s["s_all"],
                                                 name=f"dn_inter_bwd_{l}")
    dq, dk, dv, dgb, dgrow3 = dn_intra_bwd(du, dw, dqk, dq_i, dk_i, dg_i, s["q"], s["k"], s["v"], s["gb"], s["grow3"],
                                           s["p"], name=f"dn_intra_bwd_{l}", cargo=carry("exchange_b"))
    dab, dconv, dblk, g["conf_dw_b"], g["conf_ln_g"], g["conf_ln_b"], dal, ddt = mix_bwd_point(
        s["proj"], dyab, dq, dk, dv, dgb, _grow2(dgrow3), lp["wa"], lp["wb"], lp["bb"], lp["ln_g"], lp["ln_b"], lp["wc"],
        lp["alog"], lp["dt"], name=f"mix_bwd_point_{l}", cargo=carry("exchange_a"))
    dpb, dwa, dwb, dwc = mix_bwd_conv(s["proj"], dconv, lp["wa"], lp["wb"], lp["wc"], name=f"mix_bwd_conv_{l}",
                                      cargo=carry("join"))
    dx0, g["norm_mix_g"], dsc1, dsh1, dproj = nm_bwd([dab, dpb, dz, dblk], w_in, s["x0"], lp["gn1"], lp["sc1"], lp["sh1"], dx1,
                                                     name=f"proj_bwd_{l}", wl=wl)
    g_in = mm_tn(s["h1"], dproj, tn=NP // 3, name=f"proj_dw_{l}")[:, :IN_COLS]
    g["w_in"] = g_in.reshape(2, D // 2, 4, IN_COLS // 4).transpose(0, 2, 1, 3)
    g["conv_a_w"], g["conf_dw_w"], g["dn_conv_w"] = dwa[:KA], dwb[:KB], dwc[:KC]
    g["dn_a_log"], g["dn_dt_bias"] = dal[0, :NH], ddt[0, :NH]
    g["mod"] = jnp.concatenate([dsh1, dsc1, dg1, dsh2, dsc2, dg2], axis=1)
    return dx0, g


EW_BLOCK_BYTES = 1 << 20


def _row_tile(R, C, mult=8):
    best = None
    for rt in range(mult, R + 1, mult):
        if R % rt == 0 and rt * C * 4 <= EW_BLOCK_BYTES:
            best = rt
    return best if best is not None else R


def add_half_bf16(g2, recv, core, *, name):
    _, B, R, C = g2.shape
    rt = _row_tile(R, C, 16)

    def body(core_ref, a_ref, b_ref, o_ref):
        o_ref[...] = (a_ref[...] + b_ref[...]).astype(BF16)

    spec = pl.BlockSpec((1, rt, C), lambda b, r, core_ref: (b, r, 0))
    return pl.pallas_call(
        body, name=name, out_shape=_sds((B, R, C), BF16), compiler_params=_cp(2),
        grid_spec=pltpu.PrefetchScalarGridSpec(
            num_scalar_prefetch=1, grid=(B, R // rt),
            in_specs=[pl.BlockSpec((None, 1, rt, C), lambda b, r, core_ref: (core_ref[0], b, r, 0)), spec],
            out_specs=spec))(core, g2, recv)


def adamw_halves(w, g_mine, g_theirs, core, m, v, *, name):
    L, R, C = w.shape
    rh = R // 2
    rt = _row_tile(rh, C)
    nr = rh // rt

    def body(core_ref, w_ref, gm_ref, gt_ref, m_ref, v_ref, go_ref, d_ref, mo_ref, vo_ref):
        g = jnp.where(pl.program_id(1) == core_ref[0], gm_ref[...], gt_ref[...])
        go_ref[...] = g
        d_ref[...], mo_ref[...], vo_ref[...] = _adamw_math(w_ref[...], g, m_ref[...], v_ref[...])

    spec = pl.BlockSpec((1, rt, C), lambda l, h, r, core_ref: (l, h * nr + r, 0))
    half = pl.BlockSpec((1, rt, C), lambda l, h, r, core_ref: (l, r, 0))
    return pl.pallas_call(
        body, name=name, out_shape=[_sds((L, R, C))] * 4, compiler_params=_cp(3),
        grid_spec=pltpu.PrefetchScalarGridSpec(num_scalar_prefetch=1, grid=(L, 2, nr), in_specs=[spec, half, half, spec, spec],
                                               out_specs=[spec] * 4))(core, w, g_mine, g_theirs, m, v)


def ew_call(fn, ins, n_out, *, name):
    B, R, C = ins[0].shape
    rt = _row_tile(R, C)
    n = len(ins)

    def body(*refs):
        outs = fn(*[r[...] for r in refs[:n]])
        for r, o in zip(refs[n:], outs):
            r[...] = o

    spec = pl.BlockSpec((1, rt, C), lambda b, r: (b, r, 0))
    return pl.pallas_call(body, grid=(B, R // rt), name=name, in_specs=[spec] * n, out_specs=[spec] * n_out,
                          out_shape=[_sds((B, R, C))] * n_out, compiler_params=_cp(2))(*ins)


def _adamw_math(w, g, m, v):
    m = ADAM_B1 * m + (1.0 - ADAM_B1) * g
    v = ADAM_B2 * v + (1.0 - ADAM_B2) * jnp.square(g)
    m_hat = m / (1.0 - ADAM_B1 ** ADAM_STEP)
    v_hat = v / (1.0 - ADAM_B2 ** ADAM_STEP)
    return -ADAM_LR * (m_hat / (jnp.sqrt(v_hat) + ADAM_EPS) + ADAM_WD * w), m, v


def adamw(w, g, m, v, *, name):
    shape = w.shape
    r3 = lambda a: a.reshape((-1,) + shape[-2:])
    return [o.reshape(shape) for o in ew_call(_adamw_math, [r3(w), r3(g), r3(m), r3(v)], 3, name=name)]


def sum_slots(a, *, name):
    S, B, R, C = a.shape
    rt = _row_tile(R, C, 16)

    def body(*refs):
        acc = refs[0][0, 0].astype(F32)
        for r in refs[1:S]:
            acc = acc + r[0, 0].astype(F32)
        refs[S][0] = acc

    def spec(s):
        return pl.BlockSpec((1, 1, rt, C), lambda b, r: (s, b, r, 0))
    return pl.pallas_call(body, grid=(B, R // rt), name=name, in_specs=[spec(s) for s in range(S)],
                          out_specs=pl.BlockSpec((1, rt, C), lambda b, r: (b, r, 0)), out_shape=_sds((B, R, C)),
                          compiler_params=_cp(2))(*([a] * S))


ADA_SH = 6 * D // 4
ADA_TN = 512


def mod_fwd(c_all, w_ada, b_my, *, name):
    L = w_ada.shape[0]

    def body(c_ref, w_ref, b_ref, o_ref):
        o_ref[0] = _dot(_silu(c_ref[...]).astype(BF16), w_ref[0].astype(BF16)) + b_ref[0]

    return pl.pallas_call(
        body, grid=(L, ADA_SH // ADA_TN), name=name,
        in_specs=[pl.BlockSpec((8, D), lambda l, j: (0, 0)), pl.BlockSpec((1, D, ADA_TN), lambda l, j: (l, 0, j)),
                  pl.BlockSpec((1, 1, ADA_TN), lambda l, j: (l, 0, j))],
        out_specs=pl.BlockSpec((1, 8, ADA_TN), lambda l, j: (l, 0, j)), out_shape=_sds((L, 8, ADA_SH)),
        compiler_params=_cp(2))(c_all, w_ada, b_my)


def wada_grad(c_all, dmod, *, name):
    L = dmod.shape[0]

    def body(c_ref, d_ref, o_ref):
        o_ref[0] = _dot_tn(_silu(c_ref[...]), d_ref[0], HI)

    return pl.pallas_call(
        body, grid=(L, ADA_SH // ADA_TN), name=name,
        in_specs=[pl.BlockSpec((8, D), lambda l, j: (0, 0)), pl.BlockSpec((1, 8, ADA_TN), lambda l, j: (l, 0, j))],
        out_specs=pl.BlockSpec((1, D, ADA_TN), lambda l, j: (l, 0, j)), out_shape=_sds((L, D, ADA_SH)),
        compiler_params=_cp(2))(c_all, dmod)


def _place():
    return lax.axis_index("x"), lax.axis_index("y"), lax.axis_index("c")


def _other_chips(x, y):
    return [(1 - x, y), (x, 1 - y), (1 - x, 1 - y)]


def allgather8(blocks, *, space, name):
    n = len(blocks)

    def body(*refs):
        ins, outs = refs[:n], refs[n:2 * n]
        send_sems, recv_sems, local_sems = refs[2 * n:]
        x, y, c = _place()
        me, sibling = (x, y, c), (x, y, 1 - c)
        chips = _other_chips(x, y)

        def slot(p):
            return 4 * p[0] + 2 * p[1] + p[2]

        def copy(a, k, block, to, src=None):
            dst = outs[a].at[slot(block)]
            return pltpu.make_async_remote_copy(src_ref=dst if src is None else src, dst_ref=dst, send_sem=send_sems.at[a, k],
                                                recv_sem=recv_sems.at[a, k], device_id=to, device_id_type=MESH)

        mine = [pltpu.make_async_copy(ins[a], outs[a].at[slot(me)], local_sems.at[a]) for a in range(n)]
        for cp in mine:
            cp.start()
        first = []
        for a in range(n):
            first.append(copy(a, 0, me, sibling, src=ins[a]))
            first += [copy(a, 1 + j, me, (*chip, c), src=ins[a]) for j, chip in enumerate(chips)]
        for cp in first:
            cp.start()
        passed = []
        for j, chip in enumerate(chips):
            for a in range(n):
                copy(a, 1 + j, (*chip, c), me).wait_recv()
                cp = copy(a, 4 + j, (*chip, c), sibling)
                cp.start()
                passed.append(cp)
        for a in range(n):
            copy(a, 0, sibling, me).wait_recv()
            for j, chip in enumerate(chips):
                copy(a, 4 + j, (*chip, 1 - c), me).wait_recv()
        for cp in first + passed:
            cp.wait_send()
        for cp in mine:
            cp.wait()

    spec = pl.BlockSpec(memory_space=space)
    return pl.pallas_call(
        body, name=name, in_specs=[spec] * n, out_specs=[spec] * n,
        out_shape=[_sds((8,) + b.shape, b.dtype) for b in blocks],
        scratch_shapes=[pltpu.SemaphoreType.DMA((n, 7)), pltpu.SemaphoreType.DMA((n, 7)), pltpu.SemaphoreType.DMA((n,))],
        compiler_params=pltpu.CompilerParams(vmem_limit_bytes=VMEM_LIMIT))(*blocks)


def sibling_swap(blocks, *, name, slotted=True):
    n = len(blocks)

    def body(*refs):
        ins, outs = refs[:n], refs[n:2 * n]
        send_sems, recv_sems = refs[2 * n:]
        x, y, c = _place()
        cps = [pltpu.make_async_remote_copy(src_ref=ins[a].at[1 - c] if slotted else ins[a], dst_ref=outs[a],
                                            send_sem=send_sems.at[a], recv_sem=recv_sems.at[a], device_id=(x, y, 1 - c),
                                            device_id_type=MESH)
               for a in range(n)]
        for cp in cps:
            cp.start()
        for cp in cps:
            cp.wait()

    spec = pl.BlockSpec(memory_space=pl.ANY)
    return pl.pallas_call(
        body, name=name, in_specs=[spec] * n, out_specs=[spec] * n,
        out_shape=[_sds(b.shape[1:] if slotted else b.shape, b.dtype) for b in blocks],
        scratch_shapes=[pltpu.SemaphoreType.DMA((n,)), pltpu.SemaphoreType.DMA((n,))])(*blocks)


def chip_exchange(blocks, *, name):
    n = len(blocks)

    def body(*refs):
        ins, outs = refs[:n], refs[n:2 * n]
        send_sems, recv_sems, local_sems = refs[2 * n:]
        x, y, c = _place()
        me = 2 * x + y
        chips = _other_chips(x, y)
        mine = [pltpu.make_async_copy(ins[a].at[me], outs[a].at[me], local_sems.at[a]) for a in range(n)]
        for cp in mine:
            cp.start()
        cps = []
        for a in range(n):
            for j, chip in enumerate(chips):
                cps.append(pltpu.make_async_remote_copy(
                    src_ref=ins[a].at[2 * chip[0] + chip[1]], dst_ref=outs[a].at[me], send_sem=send_sems.at[a, j],
                    recv_sem=recv_sems.at[a, j], device_id=(*chip, c), device_id_type=MESH))
        for cp in cps:
            cp.start()
        for a in range(n):
            for j, chip in enumerate(chips):
                s = 2 * chip[0] + chip[1]
                pltpu.make_async_remote_copy(src_ref=ins[a].at[s], dst_ref=outs[a].at[s], send_sem=send_sems.at[a, j],
                                             recv_sem=recv_sems.at[a, j], device_id=(*chip, c), device_id_type=MESH).wait_recv()
        for cp in cps:
            cp.wait_send()
        for cp in mine:
            cp.wait()

    spec = pl.BlockSpec(memory_space=pl.ANY)
    return pl.pallas_call(
        body, name=name, in_specs=[spec] * n, out_specs=[spec] * n, out_shape=[_sds(b.shape, b.dtype) for b in blocks],
        scratch_shapes=[pltpu.SemaphoreType.DMA((n, 3)), pltpu.SemaphoreType.DMA((n, 3)), pltpu.SemaphoreType.DMA((n,))])(*blocks)


def halves_join(halves, *, name):
    n = len(halves)

    def body(*refs):
        ins, outs = refs[:n], refs[n:2 * n]
        send_sems, recv_sems, local_sems = refs[2 * n:]
        x, y, c = _place()
        cps, mine = [], []
        for a in range(n):
            mine.append(pltpu.make_async_copy(ins[a], outs[a].at[c], local_sems.at[a]))
            cps.append(pltpu.make_async_remote_copy(src_ref=ins[a], dst_ref=outs[a].at[c], send_sem=send_sems.at[a],
                                                    recv_sem=recv_sems.at[a], device_id=(x, y, 1 - c), device_id_type=MESH))
        for cp in mine + cps:
            cp.start()
        for a in range(n):
            pltpu.make_async_remote_copy(src_ref=ins[a], dst_ref=outs[a].at[1 - c], send_sem=send_sems.at[a], recv_sem=recv_sems.at[a],
                                         device_id=(x, y, 1 - c), device_id_type=MESH).wait_recv()
        for cp in cps:
            cp.wait_send()
        for cp in mine:
            cp.wait()

    spec = pl.BlockSpec(memory_space=pl.ANY)
    return pl.pallas_call(
        body, name=name, in_specs=[spec] * n, out_specs=[spec] * n,
        out_shape=[_sds((2,) + h.shape, h.dtype) for h in halves],
        scratch_shapes=[pltpu.SemaphoreType.DMA((n,)), pltpu.SemaphoreType.DMA((n,)), pltpu.SemaphoreType.DMA((n,))])(*halves)


BIG = ("w_in", "w_out", "w_ffn_in", "w_ffn_out")
COL_SHARDED = {"w_in": True, "w_out": False, "w_ffn_in": True, "w_ffn_out": False}
SMALL = ("norm_mix_g", "norm_ffn_g", "conv_a_w", "conf_dw_w", "conf_dw_b", "conf_ln_g", "conf_ln_b", "dn_conv_w",
         "dn_a_log", "dn_dt_bias", "dn_norm_g")
SMALL_SHARDED = ("conv_a_w", "conf_dw_w", "dn_conv_w")


def _half_rows(a, c):
    rh = a.shape[1] // 2
    return lax.dynamic_slice_in_dim(a, c * rh, rh, axis=1)


def _assemble(name, g):
    _, L, rh, C = g.shape
    g = g.reshape(4, 2, L, rh, C)
    if COL_SHARDED[name]:
        return g.transpose(2, 1, 3, 0, 4).reshape(L, 2 * rh, 4 * C)
    return g.transpose(2, 0, 1, 3, 4).reshape(L, 8 * rh, C)


def _split_for_reduce(name, g, c):
    L, R, C = g.shape
    if COL_SHARDED[name]:
        t = g.reshape(L, 2, R // 2, 4, C // 4).transpose(1, 3, 0, 2, 4)
    else:
        t = g.reshape(L, 4, 2, R // 8, C).transpose(2, 1, 0, 3, 4)
    mine = lax.dynamic_index_in_dim(t, c, 0, keepdims=False)
    other = lax.dynamic_index_in_dim(t, 1 - c, 0, keepdims=False)
    return mine, other


def _pack(parts):
    flat = jnp.concatenate([p.reshape(-1) for p in parts])
    n = flat.shape[0]
    rows = -(-n // (8 * HD)) * 8
    return jnp.pad(flat, (0, rows * HD - n)).reshape(rows, HD)


def _unpack(buf, shapes):
    flat = buf.reshape(-1)
    out, off = [], 0
    for s in shapes:
        n = 1
        for d in s:
            n *= d
        out.append(flat[off:off + n].reshape(s))
        off += n
    return out


class _Reduction:
    STAGES = ("swap", "exchange_a", "exchange_b", "exchange_c", "join")
    EXCHANGED = {"exchange_a": ("w_ffn_in",), "exchange_b": ("w_in",), "exchange_c": ("w_ffn_out", "w_out")}

    def __init__(self, l, split, core):
        self.l, self.split, self.core = l, split, core
        self.cargos = {}
        self.chip_sum = None

    def stage(self, name):
        l = self.l
        if name == "swap":
            cargo = swap_cargo([self.split[n] for n in BIG], slotted=True)
        elif name in self.EXCHANGED:
            if self.chip_sum is None:
                self.chip_sum = {n: add_half_bf16(self.split[n], r, self.core, name=f"reduce_add2_{n}_{l}")
                                 for n, r in zip(BIG, self.cargos["swap"].results)}
            cargo = exchange_cargo([self.chip_sum[n] for n in self.EXCHANGED[name]])
        else:
            from_chips = {n: r for st, ns in self.EXCHANGED.items() for n, r in zip(ns, self.cargos[st].results)}
            self.mine = {n: sum_slots(from_chips[n][:, None], name=f"reduce_add4_{n}_{l}")[0] for n in BIG}
            cargo = swap_cargo([self.mine[n] for n in BIG], slotted=False)
        self.cargos[name] = cargo
        return cargo

    def finish(self):
        return self.mine, dict(zip(BIG, self.cargos["join"].results))


def kernel(x, c, w_ada, b_ada, norm_mix_g, norm_ffn_g, w_in, conv_a_w, conf_dw_w, conf_dw_b, conf_ln_g, conf_ln_b, dn_conv_w, dn_a_log, dn_dt_bias, dn_norm_g, w_out, w_ffn_in, w_ffn_out, final_norm_g, loss_target, m_w_ada, m_b_ada, m_norm_mix_g, m_norm_ffn_g, m_w_in, m_conv_a_w, m_conf_dw_w, m_conf_dw_b, m_conf_ln_g, m_conf_ln_b, m_dn_conv_w, m_dn_a_log, m_dn_dt_bias, m_dn_norm_g, m_w_out, m_w_ffn_in, m_w_ffn_out, m_final_norm_g, v_w_ada, v_b_ada, v_norm_mix_g, v_norm_ffn_g, v_w_in, v_conv_a_w, v_conf_dw_w, v_conf_dw_b, v_conf_ln_g, v_conf_ln_b, v_dn_conv_w, v_dn_a_log, v_dn_dt_bias, v_dn_norm_g, v_w_out, v_w_ffn_in, v_w_ffn_out, v_final_norm_g):
    W = dict(w_ada=w_ada, b_ada=b_ada, norm_mix_g=norm_mix_g, norm_ffn_g=norm_ffn_g, w_in=w_in, conv_a_w=conv_a_w,
             conf_dw_w=conf_dw_w, conf_dw_b=conf_dw_b, conf_ln_g=conf_ln_g, conf_ln_b=conf_ln_b, dn_conv_w=dn_conv_w,
             dn_a_log=dn_a_log, dn_dt_bias=dn_dt_bias, dn_norm_g=dn_norm_g, w_out=w_out, w_ffn_in=w_ffn_in,
             w_ffn_out=w_ffn_out, final_norm_g=final_norm_g)
    M = dict(w_ada=m_w_ada, b_ada=m_b_ada, norm_mix_g=m_norm_mix_g, norm_ffn_g=m_norm_ffn_g, w_in=m_w_in, conv_a_w=m_conv_a_w,
             conf_dw_w=m_conf_dw_w, conf_dw_b=m_conf_dw_b, conf_ln_g=m_conf_ln_g, conf_ln_b=m_conf_ln_b, dn_conv_w=m_dn_conv_w,
             dn_a_log=m_dn_a_log, dn_dt_bias=m_dn_dt_bias, dn_norm_g=m_dn_norm_g, w_out=m_w_out, w_ffn_in=m_w_ffn_in,
             w_ffn_out=m_w_ffn_out, final_norm_g=m_final_norm_g)
    V = dict(w_ada=v_w_ada, b_ada=v_b_ada, norm_mix_g=v_norm_mix_g, norm_ffn_g=v_norm_ffn_g, w_in=v_w_in, conv_a_w=v_conv_a_w,
             conf_dw_w=v_conf_dw_w, conf_dw_b=v_conf_dw_b, conf_ln_g=v_conf_ln_g, conf_ln_b=v_conf_ln_b, dn_conv_w=v_dn_conv_w,
             dn_a_log=v_dn_a_log, dn_dt_bias=v_dn_dt_bias, dn_norm_g=v_dn_norm_g, w_out=v_w_out, w_ffn_in=v_w_ffn_in,
             w_ffn_out=v_w_ffn_out, final_norm_g=v_final_norm_g)
    L = w_ada.shape[0]
    ax, ay, ac = _place()
    chip = 2 * ax + ay
    dev = 4 * ax + 2 * ay + ac

    c_all = allgather8([jnp.pad(c, ((0, 7), (0, 0)))], space=pltpu.VMEM, name="gather_c")[0][:, 0, :]
    b_my = lax.dynamic_slice_in_dim(b_ada, chip * ADA_SH, ADA_SH, axis=1)[:, None, :]
    mod_sh = mod_fwd(c_all, w_ada, b_my, name="mod_fwd")
    lh = L // 2
    mod_g = allgather8([lax.dynamic_slice_in_dim(mod_sh, ac * lh, lh, axis=0).reshape(lh * 8, ADA_SH)], space=pltpu.VMEM,
                       name="gather_mod")[0]
    mod_all = mod_g.reshape(4, 2, lh, 8, ADA_SH).transpose(1, 2, 3, 0, 4).reshape(L, 8, 6 * D)
    mod = lax.dynamic_index_in_dim(mod_all, dev, 1, keepdims=False)

    src = {n: _half_rows(W[n], ac).astype(BF16) for n in BIG}
    arrived = {("w_in", 0): _comm_call(gather_cargo([src["w_in"][0]]), "gather_w0")[0]}
    carried = {}
    full = {}

    def weight(n, l):
        if (n, l) not in full:
            if (n, l) not in arrived:
                cargo, i = carried[(n, l)]
                arrived[(n, l)] = cargo.results[i]
            w = _assemble(n, arrived[(n, l)][:, None])[0]
            full[(n, l)] = jnp.pad(w, ((0, 0), (0, NP - IN_COLS))) if n == "w_in" else w
        return full[(n, l)]

    def gather_plan(l):
        plan = {}
        if l == 0:
            plan["proj_fwd"] = [("w_out", 0), ("w_ffn_out", 0)]
            plan["dn_intra_fwd"] = [("w_ffn_in", 0)]
        if l + 1 < L:
            plan["mix_fwd" if l == 0 else "proj_fwd"] = [("w_in", l + 1)]
            plan["mixout_fwd"] = [("w_out", l + 1)]
            plan["ffnin_fwd"] = [("w_ffn_in", l + 1)]
            plan["ffnout_fwd"] = [("w_ffn_out", l + 1)]
        return plan

    p_full = dict(W)
    sm = allgather8([_pack([W[n] for n in SMALL_SHARDED])], space=pltpu.VMEM, name="gather_convw")[0]
    per_chip = [_unpack(sm[4 * (s // 2) + 2 * (s % 2)], [W[n].shape for n in SMALL_SHARDED]) for s in range(4)]
    for i, n in enumerate(SMALL_SHARDED):
        p_full[n] = jnp.concatenate([per_chip[s][i] for s in range(4)], axis=-1)

    xs = x[0]
    saves, lps = [], []
    for l in range(L):
        lp = layer_params(l, mod, p_full)
        cargos = {}
        for carrier, pieces in gather_plan(l).items():
            cargos[carrier] = gather_cargo([src[n][ll] for n, ll in pieces])
            for i, piece in enumerate(pieces):
                carried[piece] = (cargos[carrier], i)
        xs, s = layer_fwd(l, xs, lp, *[functools.partial(weight, n, l) for n in BIG], cargos=cargos)
        saves.append(s)
        lps.append(lp)
    weights = [[weight(n, l) for n in BIG] for l in range(L)]
    loss_p, dx, dgf = loss_head(xs, loss_target[0], final_norm_g[None, :], name="loss_head")
    core = ac.astype(jnp.int32).reshape(1)
    grads = [None] * L
    reduced = [None] * L
    pending = None
    for l in reversed(range(L)):
        dx, grads[l] = layer_bwd(l, dx, saves[l], lps[l], *weights[l], carry=None if pending is None else pending.stage)
        if pending is not None:
            reduced[pending.l] = pending.finish()
        pending = _Reduction(l, {n: grads[l][n] for n in BIG}, core)
    for stage in _Reduction.STAGES:
        cargo = pending.stage(stage)
        cargo.results = _comm_call(cargo, f"reduce_{stage}_{pending.l}")
    reduced[pending.l] = pending.finish()
    loss = lax.psum(loss_p[0, 0], ("x", "y", "c"))
    grad_x = dx[None]

    small_shapes = [(L,) + p_full[n].shape[1:] for n in SMALL]
    parts = [jnp.stack([grads[l][n].reshape(sh[1:]) for l in range(L)]) for n, sh in zip(SMALL, small_shapes)]
    parts += [dgf.reshape(D), jnp.concatenate([grads[l]["mod"] for l in range(L)], axis=0)]
    small_shapes += [(D,), (L, 6 * D)]
    packed = _pack(parts)
    gathered_small = allgather8([packed], space=pltpu.VMEM, name="gather_small")[0]
    summed = sum_slots(gathered_small[:, None], name="sum_small")[0]
    g_small = dict(zip(SMALL + ("final_norm_g", "b_ada"), _unpack(summed, small_shapes)))
    for n in SMALL_SHARDED:
        sw = W[n].shape[-1]
        g_small[n] = lax.dynamic_slice_in_dim(g_small[n], chip * sw, sw, axis=g_small[n].ndim - 1)
    dmod_all = jnp.stack([_unpack(gathered_small[d], small_shapes)[-1] for d in range(8)], axis=1)
    dmod_my = lax.dynamic_slice_in_dim(dmod_all, chip * ADA_SH, ADA_SH, axis=2)
    g_w_ada = wada_grad(c_all, dmod_my, name="wada_grad")

    halves = [jnp.stack([reduced[l][0][n] for l in range(L)]) for n in BIG]
    theirs = [jnp.stack([reduced[l][1][n] for l in range(L)]) for n in BIG]

    out_g, out_d, out_m, out_v = {}, {}, {}, {}
    out_g["w_ada"] = g_w_ada
    out_d["w_ada"], out_m["w_ada"], out_v["w_ada"] = adamw(W["w_ada"], g_w_ada, M["w_ada"], V["w_ada"], name="adamw_w_ada")
    for n, g_mine, g_theirs in zip(BIG, halves, theirs):
        out_g[n], out_d[n], out_m[n], out_v[n] = adamw_halves(W[n], g_mine, g_theirs, core, M[n], V[n], name=f"adamw_{n}")
    names_small = SMALL + ("final_norm_g", "b_ada")
    pk = lambda d: _pack([d[n] for n in names_small])[None]
    d_s, m_s, v_s = ew_call(_adamw_math, [pk(W), pk(g_small), pk(M), pk(V)], 3, name="adamw_small")
    shapes_s = [W[n].shape for n in names_small]
    for d, o in ((out_d, d_s), (out_m, m_s), (out_v, v_s)):
        d.update(zip(names_small, _unpack(o[0], shapes_s)))
    for n in names_small:
        out_g[n] = g_small[n].reshape(W[n].shape)

    order = ("w_ada", "b_ada", "norm_mix_g", "norm_ffn_g", "w_in", "conv_a_w", "conf_dw_w", "conf_dw_b", "conf_ln_g", "conf_ln_b",
             "dn_conv_w", "dn_a_log", "dn_dt_bias", "dn_norm_g", "w_out", "w_ffn_in", "w_ffn_out", "final_norm_g")
    return (loss, grad_x, *[out_g[n] for n in order], *[out_d[n] for n in order], *[out_m[n] for n in order],
            *[out_v[n] for n in order])
```
